```python
import jax
import jax.numpy as jnp
from jax import lax
import numpy as np

D_MODEL = 1024
BATCH = 8
SEQ = 2048
DEPTH = 2
DEC_BATCH = 128
DEC_SEQ = 4
PAST_LEN = 16384
PAGE_SIZE = 128

N_EVEN = (DEPTH + 1) // 2
N_ODD = DEPTH // 2
N_SUB = 3
EPS = 1e-6
A_HEADS = 8
A_KV_HEADS = 2
A_HEAD_DIM = 64
A_GROUP = A_HEADS // A_KV_HEADS
WINDOW = 128
ROPE_THETA = 500000.0
ROT_DIM = A_HEAD_DIM // 4
B_WIDTH = D_MODEL // 2
B_BLOCKS = 8
B_BLOCK_DIM = B_WIDTH // B_BLOCKS
CONV_W = 4
RG_C = 8.0
C_HEADS = 8
C_KEY_DIM = D_MODEL // C_HEADS
C_VAL_DIM = D_MODEL // C_HEADS
C_CHUNK = 32
D_FF = ((8 * D_MODEL // 3 + 127) // 128) * 128
A_Q = A_HEADS * A_HEAD_DIM
A_KV = A_KV_HEADS * A_HEAD_DIM
EVEN_IN = A_Q + 2 * A_KV + 2 * B_WIDTH
EVEN_MIX = A_Q + B_WIDTH
C_HK = C_HEADS * C_KEY_DIM
ODD_MIX = C_HEADS * C_VAL_DIM
ODD_IN = 2 * C_HK + 2 * ODD_MIX

kernel_name = 'hybrid_swa_rglru_hgrn2_macaron_step'


def _rms_norm(x, gain):
    xf = x.astype(jnp.float32)
    inv = lax.rsqrt(jnp.mean(xf * xf, axis=-1, keepdims=True) + EPS)
    return (xf * inv).astype(x.dtype) * gain.astype(x.dtype)


def _pre(x, c, g_pre, ada_w, ada_b):
    mod = jax.nn.silu(c) @ ada_w + ada_b
    shift, scale, gate = jnp.split(mod[:, None, :], 3, axis=-1)
    return _rms_norm(x, g_pre) * (1 + scale) + shift, gate


def _post(x, y, gate, g_post, res_w):
    return x + res_w * (1 + gate) * _rms_norm(y, g_post)


def _swiglu(h, w_in, w_out):
    g, u = jnp.split(h @ w_in, 2, axis=-1)
    return (jax.nn.silu(g) * u) @ w_out


def _rope_partial(x, pos):
    inv_freq = ROPE_THETA ** (-jnp.arange(0, ROT_DIM, 2, dtype=jnp.float32) / ROT_DIM)
    ang = pos.astype(jnp.float32)[:, None] * inv_freq[None, :]
    cos = jnp.cos(ang)[None, :, None, :]
    sin = jnp.sin(ang)[None, :, None, :]
    half = ROT_DIM // 2
    x1 = x[..., :half].astype(jnp.float32)
    x2 = x[..., half:ROT_DIM].astype(jnp.float32)
    rot = jnp.concatenate([x1 * cos - x2 * sin, x2 * cos + x1 * sin], axis=-1).astype(x.dtype)
    return jnp.concatenate([rot, x[..., ROT_DIM:]], axis=-1)


def _sink_attention(q, k, v, q_pos, k_pos, sinks):
    B, N, Tq, H, Dh = q.shape
    qg = q.reshape(B, N, Tq, A_KV_HEADS, A_GROUP, Dh)
    s = jnp.einsum('bnqkgd,bnskd->bnkgqs', qg, k).astype(jnp.float32) / np.float32(np.sqrt(Dh))
    rel = q_pos[:, :, None] - k_pos[:, None, :]
    mask = (rel >= 0) & (rel < WINDOW) & (k_pos[:, None, :] >= 0)
    s = jnp.where(mask[None, :, None, None], s, -jnp.inf)
    sink = jnp.broadcast_to(sinks.astype(jnp.float32).reshape(1, 1, A_KV_HEADS, A_GROUP, 1, 1), s.shape[:-1] + (1,))
    p = jax.nn.softmax(jnp.concatenate([s, sink], axis=-1), axis=-1)[..., :-1].astype(v.dtype)
    o = jnp.einsum('bnkgqs,bnskd->bnqkgd', p, v)
    return o.reshape(B, N, Tq, H, Dh)


def _swa_prompt(q, k, v, sinks):
    B, T, H, Dh = q.shape
    N = T // WINDOW
    qb = q.reshape(B, N, WINDOW, H, Dh)
    kb = k.reshape(B, N, WINDOW, A_KV_HEADS, Dh)
    vb = v.reshape(B, N, WINDOW, A_KV_HEADS, Dh)
    prev = lambda a: jnp.pad(a, ((0, 0), (1, 0), (0, 0), (0, 0), (0, 0)))[:, :-1]
    kk = jnp.concatenate([prev(kb), kb], axis=2)
    vv = jnp.concatenate([prev(vb), vb], axis=2)
    pos = jnp.arange(T, dtype=jnp.int32).reshape(N, WINDOW)
    k_pos = jnp.concatenate([pos - WINDOW, pos], axis=1)
    return _sink_attention(qb, kk, vv, pos, k_pos, sinks).reshape(B, T, H, Dh)


def _swa_sample(q, k, v, prefix_k, prefix_v, pos0, sinks):
    B, T, H, Dh = q.shape
    P = prefix_k.shape[1]
    kk = jnp.concatenate([prefix_k.astype(k.dtype), k], axis=1)
    vv = jnp.concatenate([prefix_v.astype(v.dtype), v], axis=1)
    q_pos = pos0 + jnp.arange(T, dtype=jnp.int32)
    k_pos = jnp.concatenate([pos0 - P + jnp.arange(P, dtype=jnp.int32), q_pos])
    o = _sink_attention(q[:, None], kk[:, None], vv[:, None], q_pos[None], k_pos[None], sinks)
    return o[:, 0], kk[:, -P:], vv[:, -P:]


def _lin_combine(left, right):
    a1, b1 = left
    a2, b2 = right
    return a1 * a2, a2 * b1 + b2


def _rglru_branch(xr, conv_buf, h0, conv_w, conv_b, wa, ba, wx, bx, lam):
    B, T, W = xr.shape
    xp = jnp.concatenate([conv_buf.astype(xr.dtype), xr], axis=1)
    xc = conv_b
    for j in range(CONV_W):
        xc = xc + xp[:, j:j + T] * conv_w[j]
    new_buf = xp[:, T:]
    xb = xc.reshape(B, T, B_BLOCKS, B_BLOCK_DIM)
    r = jax.nn.sigmoid(jnp.einsum('btkd,kde->btke', xb, wa).reshape(B, T, W) + ba)
    i = jax.nn.sigmoid(jnp.einsum('btkd,kde->btke', xb, wx).reshape(B, T, W) + bx)
    log_a = -RG_C * r.astype(jnp.float32) * jax.nn.softplus(-lam.astype(jnp.float32))
    a = jnp.exp(log_a)
    mult = jnp.sqrt(jnp.maximum(1.0 - jnp.exp(2.0 * log_a), 0.0))
    b = mult * (i * xc).astype(jnp.float32)
    b = b.at[:, 0].add(a[:, 0] * h0.astype(jnp.float32))
    _, hs = lax.associative_scan(_lin_combine, (a, b), axis=1)
    return hs.astype(xr.dtype), new_buf, hs[:, -1]


def _even_mixer(h, pos0, prefix_k, prefix_v, conv_buf, h0, w_in, w_out, sinks,
                conv_w, conv_b, wa, ba, wx, bx, lam):
    B, T, _ = h.shape
    q, k, v, xg, xr = jnp.split(h @ w_in, [A_Q, A_Q + A_KV, A_Q + 2 * A_KV, A_Q + 2 * A_KV + B_WIDTH], axis=-1)
    pos = pos0 + jnp.arange(T, dtype=jnp.int32)
    q = _rope_partial(q.reshape(B, T, A_HEADS, A_HEAD_DIM), pos)
    k = _rope_partial(k.reshape(B, T, A_KV_HEADS, A_HEAD_DIM), pos)
    v = v.reshape(B, T, A_KV_HEADS, A_HEAD_DIM)
    if prefix_k is None:
        o_a = _swa_prompt(q, k, v, sinks)
        win = min(WINDOW, PAST_LEN)
        k_win, v_win = k[:, -win:], v[:, -win:]
    else:
        o_a, k_win, v_win = _swa_sample(q, k, v, prefix_k, prefix_v, pos0, sinks)
    o_b, new_buf, h_last = _rglru_branch(xr, conv_buf, h0, conv_w, conv_b, wa, ba, wx, bx, lam)
    o_b = jax.nn.gelu(xg, approximate=True) * o_b
    y = jnp.concatenate([o_a.reshape(B, T, A_Q), o_b], axis=-1) @ w_out
    return y, k_win, v_win, new_buf, h_last


def _hgrn2_chunked(q, log_f, k, v, s0):
    B, T, H, K = q.shape
    V = v.shape[-1]
    L = min(C_CHUNK, T)
    N = -(-T // L)
    pad = N * L - T

    def chunks(a):
        a = jnp.pad(a.astype(jnp.float32), ((0, 0), (0, pad), (0, 0), (0, 0)))
        return jnp.moveaxis(a.reshape(B, N, L, H, a.shape[-1]), 1, 0)

    qc, kc, vc = chunks(q), chunks(k), chunks(v)
    bc = jnp.cumsum(chunks(log_f), axis=2)
    causal = jnp.tril(jnp.ones((L, L), dtype=bool))[None, :, :, None, None]

    def step(S, xs):
        qi, ki, vi, bi = xs
        diff = bi[:, :, None] - bi[:, None, :]
        dec = jnp.where(causal, jnp.exp(jnp.minimum(diff, 0.0)), 0.0)
        scores = jnp.einsum('bthk,bshk,btshk->bhts', qi, ki, dec)
        o = jnp.einsum('bhts,bshv->bthv', scores, vi) + jnp.einsum('bthk,bhkv->bthv', qi * jnp.exp(bi), S)
        b_last = bi[:, -1]
        S_new = jnp.exp(b_last)[..., None] * S + jnp.einsum('bshk,bshv->bhkv', ki * jnp.exp(b_last[:, None] - bi), vi)
        return S_new, o

    s_last, o = lax.scan(step, s0.astype(jnp.float32), (qc, kc, vc, bc))
    o = jnp.moveaxis(o, 0, 1).reshape(B, N * L, H, V)[:, :T]
    return o, s_last


def _lower_bound(lb_logits, layer):
    gam = jax.nn.softmax(lb_logits.astype(jnp.float32), axis=0)
    return jnp.cumsum(gam, axis=0)[layer] - gam[0]


def _odd_mixer(h, s0, lb, w_in, w_out, gnorm):
    B, T, _ = h.shape
    q, fz, v, g = jnp.split(h @ w_in, [C_HK, 2 * C_HK, 2 * C_HK + ODD_MIX], axis=-1)
    f = lb + (1.0 - lb) * jax.nn.sigmoid(fz.astype(jnp.float32))
    shp = (B, T, C_HEADS, C_KEY_DIM)
    o, s_last = _hgrn2_chunked(q.reshape(shp), jnp.log(f).reshape(shp), (1.0 - f).reshape(shp),
                               v.reshape(B, T, C_HEADS, C_VAL_DIM), s0)
    o = _rms_norm(o, gnorm) * jax.nn.silu(g.reshape(B, T, C_HEADS, C_VAL_DIM).astype(jnp.float32))
    y = o.reshape(B, T, ODD_MIX).astype(h.dtype) @ w_out
    return y, s_last


def _trunk(x, c, pos0, k_win, v_win, conv_st, h_st, s_st, W):
    B = x.shape[0]
    new_k, new_v, new_conv, new_h, new_s = [], [], [], [], []
    for l in range(DEPTH):
        h, gate = _pre(x, c, W['norm_pre'][l, 0], W['ada_w'][l, 0], W['ada_b'][l, 0])
        x = _post(x, _swiglu(h, W['ffn1_w_in'][l], W['ffn1_w_out'][l]), gate, W['norm_post'][l, 0], 0.5)
        h, gate = _pre(x, c, W['norm_pre'][l, 1], W['ada_w'][l, 1], W['ada_b'][l, 1])
        if l % 2 == 0:
            e = l // 2
            if k_win is None:
                pk, pv = None, None
                cb = jnp.zeros((B, CONV_W - 1, B_WIDTH), x.dtype)
                h0 = jnp.zeros((B, B_WIDTH), jnp.float32)
            else:
                pk, pv, cb, h0 = k_win[e], v_win[e], conv_st[e], h_st[e]
            y, kw, vw, cbn, hl = _even_mixer(h, pos0, pk, pv, cb, h0, W['even_w_in'][e], W['even_w_out'][e],
                                            W['attn_sinks'][e], W['rg_conv_w'][e], W['rg_conv_b'][e],
                                            W['rg_wa'][e], W['rg_ba'][e], W['rg_wx'][e], W['rg_bx'][e],
                                            W['rg_lambda'][e])
            new_k.append(kw)
            new_v.append(vw)
            new_conv.append(cbn)
            new_h.append(hl)
        else:
            o = l // 2
            s0 = jnp.zeros((B, C_HEADS, C_KEY_DIM, C_VAL_DIM), jnp.float32) if s_st is None else s_st[o]
            lb = _lower_bound(W['hgrn_lb_logits'], l)
            y, sl = _odd_mixer(h, s0, lb, W['odd_w_in'][o], W['odd_w_out'][o], W['hgrn_gnorm'][o])
            new_s.append(sl)
        x = _post(x, y, gate, W['norm_post'][l, 1], 1.0)
        h, gate = _pre(x, c, W['norm_pre'][l, 2], W['ada_w'][l, 2], W['ada_b'][l, 2])
        x = _post(x, _swiglu(h, W['ffn2_w_in'][l], W['ffn2_w_out'][l]), gate, W['norm_post'][l, 2], 0.5)
    return (x, jnp.stack(new_k), jnp.stack(new_v), jnp.stack(new_conv), jnp.stack(new_h), jnp.stack(new_s))


def setup_inputs(seed: int = 0) -> dict:
    key = jax.random.key(seed)
    ks = iter(jax.random.split(key, 48))
    nrm = lambda shape, scale: scale * jax.random.normal(next(ks), shape, jnp.float32)
    win = min(WINDOW, PAST_LEN)
    D = D_MODEL
    a0 = jax.random.uniform(next(ks), (N_EVEN, B_WIDTH), jnp.float32, minval=0.9, maxval=0.999)
    s_a = a0 ** (1.0 / RG_C)
    rg_lambda = jnp.log(s_a) - jnp.log1p(-s_a)
    return {
        'x_prompt': nrm((BATCH, SEQ, D), 1.0),
        'x_sample': nrm((DEC_BATCH, DEC_SEQ, D), 1.0),
        'c_prompt': nrm((BATCH, D), 1.0),
        'c_sample': nrm((DEC_BATCH, D), 1.0),
        'cache_k_win': nrm((N_EVEN, DEC_BATCH, win, A_KV_HEADS, A_HEAD_DIM), 1.0),
        'cache_v_win': nrm((N_EVEN, DEC_BATCH, win, A_KV_HEADS, A_HEAD_DIM), 1.0),
        'state_conv_rglru': nrm((N_EVEN, DEC_BATCH, CONV_W - 1, B_WIDTH), 1.0),
        'state_h_rglru': nrm((N_EVEN, DEC_BATCH, B_WIDTH), 0.5),
        'state_s_hgrn': nrm((N_ODD, DEC_BATCH, C_HEADS, C_KEY_DIM, C_VAL_DIM), 0.5),
        'norm_pre': 1.0 + nrm((DEPTH, N_SUB, D), 0.05),
        'norm_post': 1.0 + nrm((DEPTH, N_SUB, D), 0.05),
        'ada_w': nrm((DEPTH, N_SUB, D, 3 * D), 0.2 * D ** -0.5),
        'ada_b': nrm((DEPTH, N_SUB, 3 * D), 0.01),
        'ffn1_w_in': nrm((DEPTH, D, 2 * D_FF), D ** -0.5),
        'ffn1_w_out': nrm((DEPTH, D_FF, D), D_FF ** -0.5),
        'ffn2_w_in': nrm((DEPTH, D, 2 * D_FF), D ** -0.5),
        'ffn2_w_out': nrm((DEPTH, D_FF, D), D_FF ** -0.5),
        'even_w_in': nrm((N_EVEN, D, EVEN_IN), D ** -0.5),
        'even_w_out': nrm((N_EVEN, EVEN_MIX, D), EVEN_MIX ** -0.5),
        'attn_sinks': nrm((N_EVEN, A_HEADS), 1.0),
        'rg_conv_w': nrm((N_EVEN, CONV_W, B_WIDTH), CONV_W ** -0.5),
        'rg_conv_b': nrm((N_EVEN, B_WIDTH), 0.01),
        'rg_wa': nrm((N_EVEN, B_BLOCKS, B_BLOCK_DIM, B_BLOCK_DIM), B_BLOCK_DIM ** -0.5),
        'rg_ba': nrm((N_EVEN, B_WIDTH), 0.01),
        'rg_wx': nrm((N_EVEN, B_BLOCKS, B_BLOCK_DIM, B_BLOCK_DIM), B_BLOCK_DIM ** -0.5),
        'rg_bx': nrm((N_EVEN, B_WIDTH), 0.01),
        'rg_lambda': rg_lambda,
        'odd_w_in': nrm((N_ODD, D, ODD_IN), D ** -0.5),
        'odd_w_out': nrm((N_ODD, ODD_MIX, D), ODD_MIX ** -0.5),
        'hgrn_lb_logits': nrm((DEPTH, C_HK), 0.5),
        'hgrn_gnorm': 1.0 + nrm((N_ODD, C_VAL_DIM), 0.05),
    }


def reference(x_prompt, x_sample, c_prompt, c_sample, cache_k_win, cache_v_win, state_conv_rglru,
              state_h_rglru, state_s_hgrn, norm_pre, norm_post, ada_w, ada_b, ffn1_w_in, ffn1_w_out,
              ffn2_w_in, ffn2_w_out, even_w_in, even_w_out, attn_sinks, rg_conv_w, rg_conv_b, rg_wa,
              rg_ba, rg_wx, rg_bx, rg_lambda, odd_w_in, odd_w_out, hgrn_lb_logits, hgrn_gnorm):
    W = dict(norm_pre=norm_pre, norm_post=norm_post, ada_w=ada_w, ada_b=ada_b,
             ffn1_w_in=ffn1_w_in, ffn1_w_out=ffn1_w_out, ffn2_w_in=ffn2_w_in, ffn2_w_out=ffn2_w_out,
             even_w_in=even_w_in, even_w_out=even_w_out, attn_sinks=attn_sinks,
             rg_conv_w=rg_conv_w, rg_conv_b=rg_conv_b, rg_wa=rg_wa, rg_ba=rg_ba, rg_wx=rg_wx,
             rg_bx=rg_bx, rg_lambda=rg_lambda, odd_w_in=odd_w_in, odd_w_out=odd_w_out,
             hgrn_lb_logits=hgrn_lb_logits, hgrn_gnorm=hgrn_gnorm)
    y_prompt, k_win_p, v_win_p, conv_p, h_p, s_p = _trunk(x_prompt, c_prompt, 0, None, None, None, None, None, W)
    y_sample, k_win_s, v_win_s, conv_s, h_s, s_s = _trunk(x_sample, c_sample, PAST_LEN, cache_k_win, cache_v_win,
                                                         state_conv_rglru, state_h_rglru, state_s_hgrn, W)
    return (y_prompt, y_sample, k_win_p, v_win_p, conv_p, h_p, s_p, k_win_s, v_win_s, conv_s, h_s, s_s)
```

```python
import functools

import jax
import jax.numpy as jnp
import numpy as np
from jax import lax
from jax.experimental import pallas as pl
from jax.experimental.pallas import tpu as pltpu

_F32 = jnp.float32
_BF16 = jnp.bfloat16

_EPS = 1e-6
_A_HEADS = 8
_A_KV_HEADS = 2
_A_HEAD_DIM = 64
_A_GROUP = _A_HEADS // _A_KV_HEADS
_WINDOW = 128
_ROPE_THETA = 500000.0
_ROT_DIM = _A_HEAD_DIM // 4
_RG_C = 8.0
_C_HEADS = 8
_PAST_LEN = 16384
_N_SUB = 3

_LANES = 128
_SUBLANES = 8
_VMEM_LIMIT = 48 * 1024 * 1024

_ROW_TILE = 512
_FF_CHUNK = 256
_RG_CHUNK = 256
_HG_CHUNK = 64
_SAMPLE_SEQS = 8


def _dot(a, b):
    return jnp.dot(a.astype(_BF16), b.astype(_BF16), preferred_element_type=_F32)


def _dot_nt(a, b):
    return lax.dot_general(a.astype(_BF16), b.astype(_BF16), (((1,), (1,)), ((), ())),
                           preferred_element_type=_F32)


def _dot_tn(a, b):
    return lax.dot_general(a.astype(_BF16), b.astype(_BF16), (((0,), (0,)), ((), ())),
                           preferred_element_type=_F32)


def _sigmoid(x):
    return 1.0 / (1.0 + jnp.exp(-x))


def _silu(x):
    return x * _sigmoid(x)


def _rms(x, gain):
    inv = lax.rsqrt(jnp.mean(x * x, axis=-1, keepdims=True) + _EPS)
    return x * inv * gain


def _pre(x, mod, gain):
    d = x.shape[-1]
    return _rms(x, gain) * (1.0 + mod[:, d:2 * d]) + mod[:, :d]


def _post(x, y, mod, gain, res_w):
    d = x.shape[-1]
    return x + (res_w * (1.0 + mod[:, 2 * d:])) * _rms(y, gain)


def _params(*sem):
    return pltpu.CompilerParams(dimension_semantics=sem, vmem_limit_bytes=_VMEM_LIMIT)


def _ada_kernel(c_ref, w_ref, b_ref, o_ref):
    o_ref[...] = _dot(_silu(c_ref[...]), w_ref[...]) + b_ref[...]


def _ada_call(c_all, ada_w, ada_b):
    m, d = c_all.shape
    n_sub = ada_w.shape[0]
    n = ada_w.shape[-1]
    tn = n // 2
    return pl.pallas_call(
        _ada_kernel,
        grid=(n_sub, n // tn),
        in_specs=[
            pl.BlockSpec((m, d), lambda s, j: (0, 0)),
            pl.BlockSpec((None, d, tn), lambda s, j: (s, 0, j)),
            pl.BlockSpec((None, 1, tn), lambda s, j: (s, 0, j)),
        ],
        out_specs=pl.BlockSpec((None, m, tn), lambda s, j: (s, 0, j)),
        out_shape=jax.ShapeDtypeStruct((n_sub, m, n), _F32),
        compiler_params=_params("parallel", "parallel"),
        name="ada_mod",
    )(c_all, ada_w, ada_b)


def _ffn_kernel(x_ref, mod_ref, gpre_ref, win_ref, wout_ref, gpost_ref, o_ref, *, res_w):
    x = x_ref[...]
    mod = mod_ref[0]
    dff = wout_ref.shape[0]
    h = _pre(x, mod, gpre_ref[...]).astype(_BF16)
    acc = jnp.zeros(x.shape, _F32)
    for j in range(dff // _FF_CHUNK):
        lo = j * _FF_CHUNK
        g = jnp.dot(h, win_ref[:, lo:lo + _FF_CHUNK], preferred_element_type=_F32)
        u = jnp.dot(h, win_ref[:, dff + lo:dff + lo + _FF_CHUNK], preferred_element_type=_F32)
        acc = acc + _dot(_silu(g) * u, wout_ref[lo:lo + _FF_CHUNK, :])
    o_ref[...] = _post(x, acc, mod, gpost_ref[...], res_w)


def _ffn_call(x, mod4, sub, layer, gpre, w_in, w_out, gpost, res_w, tiles_per_seq):
    n, d = x.shape
    dff = w_out.shape[1]
    r = mod4.shape[2]
    tm = min(_ROW_TILE, n)
    return pl.pallas_call(
        functools.partial(_ffn_kernel, res_w=res_w),
        grid=(n // tm,),
        in_specs=[
            pl.BlockSpec((tm, d), lambda i: (i, 0)),
            pl.BlockSpec((None, 1, r, 3 * d), lambda i: (sub, i // tiles_per_seq, 0, 0)),
            pl.BlockSpec((None, 1, d), lambda i: (sub, 0, 0)),
            pl.BlockSpec((None, d, 2 * dff), lambda i: (layer, 0, 0), pipeline_mode=pl.Buffered(1)),
            pl.BlockSpec((None, dff, d), lambda i: (layer, 0, 0), pipeline_mode=pl.Buffered(1)),
            pl.BlockSpec((None, 1, d), lambda i: (sub, 0, 0)),
        ],
        out_specs=pl.BlockSpec((tm, d), lambda i: (i, 0)),
        out_shape=jax.ShapeDtypeStruct((n, d), _F32),
        compiler_params=_params("parallel"),
        name="ffn_sublayer",
    )(x, mod4, gpre, w_in, w_out, gpost)


def _rope(x, cos, sin_lo, sin_hi):
    outs = []
    for j in range(x.shape[1] // _LANES):
        xc = x[:, j * _LANES:(j + 1) * _LANES]
        nxt = pltpu.roll(xc, _LANES - _ROT_DIM // 2, axis=1)
        prv = pltpu.roll(xc, _ROT_DIM // 2, axis=1)
        outs.append(xc * cos + nxt * sin_lo + prv * sin_hi)
    return outs[0] if len(outs) == 1 else jnp.concatenate(outs, axis=1)


def _even_in_kernel(x_ref, mod_ref, gpre_ref, w_ref, cos_ref, slo_ref, shi_ref,
                    q_ref, k_ref, v_ref, xg_ref, xr_ref):
    h = _pre(x_ref[...], mod_ref[0], gpre_ref[...])
    y = _dot(h, w_ref[...])
    nq, nk, nw = q_ref.shape[1], k_ref.shape[1], xg_ref.shape[1]
    cos, slo, shi = cos_ref[...], slo_ref[...], shi_ref[...]
    q_ref[...] = _rope(y[:, :nq], cos, slo, shi)
    k_ref[...] = _rope(y[:, nq:nq + nk], cos, slo, shi)
    v_ref[...] = y[:, nq + nk:nq + 2 * nk]
    xg_ref[...] = y[:, nq + 2 * nk:nq + 2 * nk + nw]
    xr_ref[...] = y[:, nq + 2 * nk + nw:]


def _even_in_call(x, mod4, sub, e, gpre, w_in, rope_tabs, tiles_per_seq, tab_tiles):
    n, d = x.shape
    r = mod4.shape[2]
    tm = min(_ROW_TILE, n)
    nq = _A_HEADS * _A_HEAD_DIM
    nk = _A_KV_HEADS * _A_HEAD_DIM
    nw = (w_in.shape[-1] - nq - 2 * nk) // 2
    row = lambda i: (i, 0)
    tab = pl.BlockSpec((tm, _LANES), lambda i: (i % tab_tiles, 0))
    return pl.pallas_call(
        _even_in_kernel,
        grid=(n // tm,),
        in_specs=[
            pl.BlockSpec((tm, d), row),
            pl.BlockSpec((None, 1, r, 3 * d), lambda i: (sub, i // tiles_per_seq, 0, 0)),
            pl.BlockSpec((None, 1, d), lambda i: (sub, 0, 0)),
            pl.BlockSpec((None, d, w_in.shape[-1]), lambda i: (e, 0, 0)),
            tab, tab, tab,
        ],
        out_specs=[pl.BlockSpec((tm, nq), row), pl.BlockSpec((tm, nk), row), pl.BlockSpec((tm, nk), row),
                   pl.BlockSpec((tm, nw), row), pl.BlockSpec((tm, nw), row)],
        out_shape=[jax.ShapeDtypeStruct((n, nq), _F32), jax.ShapeDtypeStruct((n, nk), _F32),
                   jax.ShapeDtypeStruct((n, nk), _F32), jax.ShapeDtypeStruct((n, nw), _F32),
                   jax.ShapeDtypeStruct((n, nw), _F32)],
        compiler_params=_params("parallel"),
        name="even_in_proj",
    )(x, mod4, gpre, w_in, *rope_tabs)


def _odd_in_kernel(x_ref, mod_ref, gpre_ref, w_ref, q_ref, f_ref, v_ref, g_ref):
    h = _pre(x_ref[...], mod_ref[0], gpre_ref[...])
    y = _dot(h, w_ref[...])
    n = q_ref.shape[1]
    q_ref[...] = y[:, :n]
    f_ref[...] = y[:, n:2 * n]
    v_ref[...] = y[:, 2 * n:3 * n]
    g_ref[...] = y[:, 3 * n:]


def _odd_in_call(x, mod4, sub, o, gpre, w_in, tiles_per_seq):
    n, d = x.shape
    r = mod4.shape[2]
    tm = min(_ROW_TILE, n)
    nw = w_in.shape[-1] // 4
    row = lambda i: (i, 0)
    return pl.pallas_call(
        _odd_in_kernel,
        grid=(n // tm,),
        in_specs=[
            pl.BlockSpec((tm, d), row),
            pl.BlockSpec((None, 1, r, 3 * d), lambda i: (sub, i // tiles_per_seq, 0, 0)),
            pl.BlockSpec((None, 1, d), lambda i: (sub, 0, 0)),
            pl.BlockSpec((None, d, 4 * nw), lambda i: (o, 0, 0)),
        ],
        out_specs=[pl.BlockSpec((tm, nw), row)] * 4,
        out_shape=[jax.ShapeDtypeStruct((n, nw), _F32)] * 4,
        compiler_params=_params("parallel"),
        name="odd_in_proj",
    )(x, mod4, gpre, w_in)


def _out_kernel(x_ref, mod_ref, gpost_ref, w_ref, *refs):
    a_refs, o_ref = refs[:-1], refs[-1]
    y = None
    off = 0
    for a_ref in a_refs:
        k = a_ref.shape[1]
        t = _dot(a_ref[...], w_ref[off:off + k, :])
        y = t if y is None else y + t
        off += k
    o_ref[...] = _post(x_ref[...], y, mod_ref[0], gpost_ref[...], 1.0)


def _out_call(x, mod4, sub, widx, gpost, w_out, acts, tiles_per_seq):
    n, d = x.shape
    r = mod4.shape[2]
    tm = min(_ROW_TILE, n)
    row = lambda i: (i, 0)
    return pl.pallas_call(
        _out_kernel,
        grid=(n // tm,),
        in_specs=[
            pl.BlockSpec((tm, d), row),
            pl.BlockSpec((None, 1, r, 3 * d), lambda i: (sub, i // tiles_per_seq, 0, 0)),
            pl.BlockSpec((None, 1, d), lambda i: (sub, 0, 0)),
            pl.BlockSpec((None, w_out.shape[1], d), lambda i: (widx, 0, 0)),
        ] + [pl.BlockSpec((tm, a.shape[1]), row) for a in acts],
        out_specs=pl.BlockSpec((tm, d), row),
        out_shape=jax.ShapeDtypeStruct((n, d), _F32),
        compiler_params=_params("parallel"),
        name="mixer_out_proj",
    )(x, mod4, gpost, w_out, *acts)


def _sink_softmax_pv(s, mask, sink, v):
    s = jnp.where(mask, s, -jnp.inf)
    m = jnp.maximum(jnp.max(s, axis=-1, keepdims=True), sink)
    p = jnp.exp(s - m)
    denom = jnp.sum(p, axis=-1, keepdims=True) + jnp.exp(sink - m)
    return _dot(p, v) / denom


def _swa_prompt_kernel(sink_ref, q_ref, kp_ref, kc_ref, vp_ref, vc_ref, o_ref):
    w = q_ref.shape[0]
    scale = np.float32(1.0 / np.sqrt(_A_HEAD_DIM))
    row = lax.broadcasted_iota(jnp.int32, (w, 2 * w), 0)
    col = lax.broadcasted_iota(jnp.int32, (w, 2 * w), 1)
    first = jnp.where(pl.program_id(1) > 0, 0, w)
    mask = jnp.logical_and(col > jnp.maximum(row, first - 1), col <= row + w)
    q = q_ref[...]
    outs = []
    for j in range(_A_KV_HEADS):
        ks = slice(j * _A_HEAD_DIM, (j + 1) * _A_HEAD_DIM)
        k = jnp.concatenate([kp_ref[:, ks], kc_ref[:, ks]], axis=0)
        v = jnp.concatenate([vp_ref[:, ks], vc_ref[:, ks]], axis=0)
        for g in range(_A_GROUP):
            h = j * _A_GROUP + g
            qh = q[:, h * _A_HEAD_DIM:(h + 1) * _A_HEAD_DIM]
            s = _dot_nt(qh, k) * scale
            outs.append(_sink_softmax_pv(s, mask, sink_ref[h], v))
    o_ref[...] = jnp.concatenate(outs, axis=1)


def _swa_prompt_call(q, k, v, sinks, batch):
    n, nq = q.shape
    nk = k.shape[1]
    w = _WINDOW
    nb = n // batch // w
    cur = lambda b, i: (b * nb + i, 0)
    prev = lambda b, i: (b * nb + jnp.maximum(i - 1, 0), 0)
    return pl.pallas_call(
        _swa_prompt_kernel,
        grid=(batch, nb),
        in_specs=[
            pl.BlockSpec(memory_space=pltpu.SMEM),
            pl.BlockSpec((w, nq), cur),
            pl.BlockSpec((w, nk), prev), pl.BlockSpec((w, nk), cur),
            pl.BlockSpec((w, nk), prev), pl.BlockSpec((w, nk), cur),
        ],
        out_specs=pl.BlockSpec((w, nq), cur),
        out_shape=jax.ShapeDtypeStruct((n, nq), _F32),
        compiler_params=_params("parallel", "parallel"),
        name="swa_prompt",
    )(sinks, q, k, k, v, v)


def _swa_sample_kernel(sink_ref, q_ref, kn_ref, vn_ref, ck_ref, cv_ref, o_ref, kw_ref, vw_ref, *, t_new):
    p = ck_ref.shape[1]
    scale = np.float32(1.0 / np.sqrt(_A_HEAD_DIM))
    rows = _A_GROUP * _SUBLANES
    t = lax.broadcasted_iota(jnp.int32, (rows, p + _SUBLANES), 0) & (_SUBLANES - 1)
    c = lax.broadcasted_iota(jnp.int32, (rows, p + _SUBLANES), 1)
    mask = jnp.logical_and(c <= t + p, c > t + p - _WINDOW)
    g_of_row = lax.broadcasted_iota(jnp.int32, (rows, 1), 0) >> (_SUBLANES.bit_length() - 1)
    seq_per_tile = _SUBLANES // t_new
    for s in range(ck_ref.shape[0]):
        base = (s // seq_per_tile) * _SUBLANES
        shift = (s % seq_per_tile) * t_new
        q8 = q_ref[base:base + _SUBLANES, :]
        kn8 = kn_ref[base:base + _SUBLANES, :]
        vn8 = vn_ref[base:base + _SUBLANES, :]
        if shift:
            q8 = pltpu.roll(q8, _SUBLANES - shift, axis=0)
            kn8 = pltpu.roll(kn8, _SUBLANES - shift, axis=0)
            vn8 = pltpu.roll(vn8, _SUBLANES - shift, axis=0)
        ck = ck_ref[s]
        cv = cv_ref[s]
        kw_ref[s, 0:p - t_new, :] = ck[t_new:, :]
        kw_ref[s, p - t_new:p, :] = kn8[0:t_new, :]
        vw_ref[s, 0:p - t_new, :] = cv[t_new:, :]
        vw_ref[s, p - t_new:p, :] = vn8[0:t_new, :]
        outs = []
        for j in range(_A_KV_HEADS):
            ks = slice(j * _A_HEAD_DIM, (j + 1) * _A_HEAD_DIM)
            k = jnp.concatenate([ck[:, ks], kn8[:, ks]], axis=0)
            v = jnp.concatenate([cv[:, ks], vn8[:, ks]], axis=0)
            qj = jnp.concatenate(
                [q8[:, (j * _A_GROUP + g) * _A_HEAD_DIM:(j * _A_GROUP + g + 1) * _A_HEAD_DIM]
                 for g in range(_A_GROUP)], axis=0)
            sink = jnp.zeros((rows, 1), _F32)
            for g in range(_A_GROUP):
                sink = jnp.where(g_of_row == g, sink_ref[j * _A_GROUP + g], sink)
            o = _sink_softmax_pv(_dot_nt(qj, k) * scale, mask, sink, v)
            outs.extend(o[g * _SUBLANES:(g + 1) * _SUBLANES, :] for g in range(_A_GROUP))
        o8 = jnp.concatenate(outs, axis=1)
        o_ref[base + shift:base + shift + t_new, :] = o8[0:t_new, :]


def _swa_sample_call(q, k_new, v_new, cache_k, cache_v, sinks, t_new):
    n, nq = q.shape
    nk = k_new.shape[1]
    nseq, p, _ = cache_k.shape
    bs = _SAMPLE_SEQS
    rows = lambda i: (i, 0)
    seqs = lambda i: (i, 0, 0)
    return pl.pallas_call(
        functools.partial(_swa_sample_kernel, t_new=t_new),
        grid=(nseq // bs,),
        in_specs=[
            pl.BlockSpec(memory_space=pltpu.SMEM),
            pl.BlockSpec((bs * t_new, nq), rows),
            pl.BlockSpec((bs * t_new, nk), rows), pl.BlockSpec((bs * t_new, nk), rows),
            pl.BlockSpec((bs, p, nk), seqs), pl.BlockSpec((bs, p, nk), seqs),
        ],
        out_specs=[pl.BlockSpec((bs * t_new, nq), rows),
                   pl.BlockSpec((bs, p, nk), seqs), pl.BlockSpec((bs, p, nk), seqs)],
        out_shape=[jax.ShapeDtypeStruct((n, nq), _F32),
                   jax.ShapeDtypeStruct(cache_k.shape, _F32), jax.ShapeDtypeStruct(cache_v.shape, _F32)],
        compiler_params=_params("parallel"),
        name="swa_sample",
    )(sinks, q, k_new, v_new, cache_k, cache_v)


def _softplus(z):
    return jnp.maximum(z, 0.0) + jnp.log1p(jnp.exp(-jnp.abs(z)))


def _gelu_tanh(x):
    return 0.5 * x * (1.0 + jnp.tanh(np.float32(np.sqrt(2.0 / np.pi)) * (x + 0.044715 * (x * x * x))))


def _rg_gates(xc, wa_ref, ba_ref, wx_ref, bx_ref, sp_neg_lam):
    r = _sigmoid(_dot(xc, wa_ref[...]) + ba_ref[...])
    i = _sigmoid(_dot(xc, wx_ref[...]) + bx_ref[...])
    log_a = (-_RG_C) * r * sp_neg_lam
    a = jnp.exp(log_a)
    mult = jnp.sqrt(jnp.maximum(1.0 - jnp.exp(2.0 * log_a), 0.0))
    return a, mult * (i * xc)


def _rg_prompt_kernel(xr_ref, xg_ref, cw_ref, cb_ref, wa_ref, ba_ref, wx_ref, bx_ref, lam_ref,
                      o_ref, conv_ref, h_ref, xpad, a_s, b_s):
    t_len, w = xr_ref.shape
    cw = cw_ref.shape[0]
    xpad[0:_SUBLANES, :] = jnp.zeros((_SUBLANES, w), _F32)
    xpad[_SUBLANES:, :] = xr_ref[...]
    sp = _softplus(-lam_ref[...])
    for c in range(t_len // _RG_CHUNK):
        r0 = c * _RG_CHUNK
        xc = cb_ref[...]
        for j in range(cw):
            lo = _SUBLANES + r0 - (cw - 1) + j
            xc = xc + xpad[lo:lo + _RG_CHUNK, :] * cw_ref[j:j + 1, :]
        a, b = _rg_gates(xc, wa_ref, ba_ref, wx_ref, bx_ref, sp)
        a_s[r0:r0 + _RG_CHUNK, :] = a
        b_s[r0:r0 + _RG_CHUNK, :] = b

    row = lax.broadcasted_iota(jnp.int32, (_SUBLANES, w), 0)

    def group(g, h):
        r0 = pl.multiple_of(g * _SUBLANES, _SUBLANES)
        a = a_s[pl.ds(r0, _SUBLANES), :]
        b = b_s[pl.ds(r0, _SUBLANES), :]
        sh = 1
        while sh < _SUBLANES:
            a_prev = jnp.where(row >= sh, pltpu.roll(a, sh, axis=0), 1.0)
            b_prev = jnp.where(row >= sh, pltpu.roll(b, sh, axis=0), 0.0)
            b = a * b_prev + b
            a = a * a_prev
            sh *= 2
        hs = a * h + b
        o_ref[pl.ds(r0, _SUBLANES), :] = _gelu_tanh(xg_ref[pl.ds(r0, _SUBLANES), :]) * hs
        return hs[_SUBLANES - 1:_SUBLANES, :]

    h_last = lax.fori_loop(0, t_len // _SUBLANES, group, jnp.zeros((1, w), _F32), unroll=2)
    h_ref[0] = h_last
    conv_ref[0] = xr_ref[t_len - (cw - 1):t_len, :]


def _rg_prompt_call(xr, xg, conv_w, conv_b, wa_bd, ba, wx_bd, bx, lam, batch):
    n, w = xr.shape
    t_len = n // batch
    cw = conv_w.shape[0]
    seq = lambda b: (b, 0)
    const = lambda b: (0, 0)
    vec = pl.BlockSpec((1, w), const)
    mat = pl.BlockSpec((w, w), const)
    return pl.pallas_call(
        _rg_prompt_kernel,
        grid=(batch,),
        in_specs=[pl.BlockSpec((t_len, w), seq), pl.BlockSpec((t_len, w), seq),
                  pl.BlockSpec((cw, w), const), vec, mat, vec, mat, vec, vec],
        out_specs=[pl.BlockSpec((t_len, w), seq),
                   pl.BlockSpec((1, cw - 1, w), lambda b: (b, 0, 0)),
                   pl.BlockSpec((1, 1, w), lambda b: (b, 0, 0))],
        out_shape=[jax.ShapeDtypeStruct((n, w), _F32),
                   jax.ShapeDtypeStruct((batch, cw - 1, w), _F32),
                   jax.ShapeDtypeStruct((batch, 1, w), _F32)],
        scratch_shapes=[pltpu.VMEM((t_len + _SUBLANES, w), _F32),
                        pltpu.VMEM((t_len, w), _F32), pltpu.VMEM((t_len, w), _F32)],
        compiler_params=_params("parallel"),
        name="rglru_prompt",
    )(xr, xg, conv_w, conv_b, wa_bd, ba, wx_bd, bx, lam)


def _rg_sample_kernel(xp_ref, xg_ref, h0_ref, cw_ref, cb_ref, wa_ref, ba_ref, wx_ref, bx_ref, lam_ref,
                      o_ref, h_ref):
    cw = cw_ref.shape[0]
    sp = _softplus(-lam_ref[...])
    h = h0_ref[...]
    for t in range(o_ref.shape[0]):
        xc = cb_ref[...]
        for j in range(cw):
            xc = xc + xp_ref[t + j] * cw_ref[j:j + 1, :]
        a, b = _rg_gates(xc, wa_ref, ba_ref, wx_ref, bx_ref, sp)
        h = a * h + b
        o_ref[t] = _gelu_tanh(xg_ref[t]) * h
    h_ref[...] = h


def _rg_sample_call(xp_t, xg_t, h0, conv_w, conv_b, wa_bd, ba, wx_bd, bx, lam):
    t_new, nseq, w = xg_t.shape
    return pl.pallas_call(
        _rg_sample_kernel,
        out_shape=[jax.ShapeDtypeStruct((t_new, nseq, w), _F32), jax.ShapeDtypeStruct((nseq, w), _F32)],
        compiler_params=pltpu.CompilerParams(vmem_limit_bytes=_VMEM_LIMIT),
        name="rglru_sample",
    )(xp_t, xg_t, h0, conv_w, conv_b, wa_bd, ba, wx_bd, bx, lam)


def _row_bcast(x, r, n):
    return jnp.broadcast_to(x[r:r + 1, :], (n, x.shape[1]))


def _chunk_cumsum(x):
    n_tiles = x.shape[0] // _SUBLANES
    row = lax.broadcasted_iota(jnp.int32, (_SUBLANES, x.shape[1]), 0)
    tiles = []
    carry = None
    for i in range(n_tiles):
        t = x[i * _SUBLANES:(i + 1) * _SUBLANES, :]
        sh = 1
        while sh < _SUBLANES:
            t = t + jnp.where(row >= sh, pltpu.roll(t, sh, axis=0), 0.0)
            sh *= 2
        if carry is not None:
            t = t + carry
        carry = _row_bcast(t, _SUBLANES - 1, _SUBLANES)
        tiles.append(t)
    return tiles[0] if n_tiles == 1 else jnp.concatenate(tiles, axis=0)


def _level_reference(b, m):
    n = b.shape[0]
    if 2 * m >= _SUBLANES:
        pieces = [_row_bcast(b, lo + m - 1, 2 * m) for lo in range(0, n, 2 * m)]
        return pieces[0] if len(pieces) == 1 else jnp.concatenate(pieces, axis=0)
    row = lax.broadcasted_iota(jnp.int32, (_SUBLANES, b.shape[1]), 0)
    tiles = []
    for i in range(n // _SUBLANES):
        t = b[i * _SUBLANES:(i + 1) * _SUBLANES, :]
        ref = None
        for lo in range(0, _SUBLANES, 2 * m):
            piece = _row_bcast(t, lo + m - 1, _SUBLANES)
            ref = piece if ref is None else jnp.where(row >= lo, piece, ref)
        tiles.append(ref)
    return tiles[0] if len(tiles) == 1 else jnp.concatenate(tiles, axis=0)


def _hgrn_chunk(q, fz, v, lb, state, n_valid):
    n, kd = q.shape
    f = lb + (1.0 - lb) * _sigmoid(fz)
    log_f = jnp.log(f)
    k = 1.0 - f
    if n_valid < n:
        valid = lax.broadcasted_iota(jnp.int32, (n, kd), 0) < n_valid
        log_f = jnp.where(valid, log_f, 0.0)
        k = jnp.where(valid, k, 0.0)
    b = _chunk_cumsum(log_f)
    b_last = _row_bcast(b, n - 1, n)

    o = _dot(q * jnp.exp(b), state)
    k_end = k * jnp.exp(b_last - b)
    decay = jnp.exp(jnp.broadcast_to(b_last[0:1, :], (v.shape[1], kd))).T
    new_state = decay * state + _dot_tn(k_end, v)

    row = lax.broadcasted_iota(jnp.int32, (n, n), 0)
    col = lax.broadcasted_iota(jnp.int32, (n, n), 1)
    upper = lax.broadcasted_iota(jnp.int32, (n, kd), 0)
    scores = jnp.where(row == col, jnp.sum(q * k, axis=-1, keepdims=True), 0.0)
    m = 1
    while m < n:
        e = jnp.exp(-jnp.abs(b - _level_reference(b, m)))
        z = jnp.where((upper & m) != 0, q, k) * e
        pair = jnp.logical_and((row & m) != 0, (row ^ m) >> (m.bit_length() - 1) == col >> (m.bit_length() - 1))
        scores = scores + jnp.where(pair, _dot_nt(z, z), 0.0)
        m *= 2
    return o + _dot(scores, v), new_state


def _lower_bound(logits, layer):
    m = jnp.max(logits, axis=0, keepdims=True)
    e = jnp.exp(logits - m)
    return jnp.sum(e[1:layer + 1, :], axis=0, keepdims=True) / jnp.sum(e, axis=0, keepdims=True)


def _hgrn_out(o, g, gnorm):
    return _rms(o, gnorm) * _silu(g)


def _hgrn_prompt_kernel(q_ref, f_ref, v_ref, g_ref, lbl_ref, gn_ref, o_ref, s_ref, *, layer):
    t_len, kd = q_ref.shape
    lb = _lower_bound(lbl_ref[...], layer)
    gn = gn_ref[...]

    def chunk(c, state):
        r0 = pl.multiple_of(c * _HG_CHUNK, _HG_CHUNK)
        rows = pl.ds(r0, _HG_CHUNK)
        o, state = _hgrn_chunk(q_ref[rows, :], f_ref[rows, :], v_ref[rows, :], lb, state, _HG_CHUNK)
        o_ref[rows, :] = _hgrn_out(o, g_ref[rows, :], gn)
        return state

    s_ref[0, 0] = lax.fori_loop(0, t_len // _HG_CHUNK, chunk, jnp.zeros((kd, v_ref.shape[1]), _F32))


def _hgrn_prompt_call(q, fz, v, g, lb_logits, gnorm, batch, layer):
    n, width = q.shape
    t_len = n // batch
    kd = width // _C_HEADS
    blk = pl.BlockSpec((t_len, kd), lambda b, h: (b, h))
    return pl.pallas_call(
        functools.partial(_hgrn_prompt_kernel, layer=layer),
        grid=(batch, _C_HEADS),
        in_specs=[blk, blk, blk, blk,
                  pl.BlockSpec((lb_logits.shape[0], kd), lambda b, h: (0, h)),
                  pl.BlockSpec((1, kd), lambda b, h: (0, 0))],
        out_specs=[blk, pl.BlockSpec((1, 1, kd, kd), lambda b, h: (b, h, 0, 0))],
        out_shape=[jax.ShapeDtypeStruct((n, width), _F32),
                   jax.ShapeDtypeStruct((batch, _C_HEADS, kd, kd), _F32)],
        compiler_params=_params("parallel", "parallel"),
        name="hgrn2_prompt",
    )(q, fz, v, g, lb_logits, gnorm)


def _hgrn_sample_kernel(q_ref, f_ref, v_ref, g_ref, lbl_ref, gn_ref, s0_ref, o_ref, s_ref, *, layer, t_new):
    lb = _lower_bound(lbl_ref[...], layer)
    gn = gn_ref[...]
    for s in range(s0_ref.shape[0]):
        rows = slice(s * _SUBLANES, (s + 1) * _SUBLANES)
        o, state = _hgrn_chunk(q_ref[rows, :], f_ref[rows, :], v_ref[rows, :], lb, s0_ref[s, 0], t_new)
        o_ref[rows, :] = _hgrn_out(o, g_ref[rows, :], gn)
        s_ref[s, 0] = state


def _hgrn_sample_call(q, fz, v, g, lb_logits, gnorm, s0, layer, t_new):
    n, width = q.shape
    nseq = s0.shape[0]
    kd = width // _C_HEADS
    bs = _SAMPLE_SEQS
    blk = pl.BlockSpec((bs * _SUBLANES, kd), lambda i, h: (i, h))
    st = pl.BlockSpec((bs, 1, kd, kd), lambda i, h: (i, h, 0, 0))
    return pl.pallas_call(
        functools.partial(_hgrn_sample_kernel, layer=layer, t_new=t_new),
        grid=(nseq // bs, _C_HEADS),
        in_specs=[blk, blk, blk, blk,
                  pl.BlockSpec((lb_logits.shape[0], kd), lambda i, h: (0, h)),
                  pl.BlockSpec((1, kd), lambda i, h: (0, 0)), st],
        out_specs=[blk, st],
        out_shape=[jax.ShapeDtypeStruct((n, width), _F32), jax.ShapeDtypeStruct(s0.shape, _F32)],
        compiler_params=_params("parallel", "parallel"),
        name="hgrn2_sample",
    )(q, fz, v, g, lb_logits, gnorm, s0)


def _rope_tables(pos):
    half = _ROT_DIM // 2
    inv_freq = _ROPE_THETA ** (-jnp.arange(0, _ROT_DIM, 2, dtype=_F32) / _ROT_DIM)
    ang = pos.astype(_F32)[:, None] * inv_freq[None, :]
    cos, sin = jnp.cos(ang), jnp.sin(ang)
    ones = jnp.ones((pos.shape[0], _A_HEAD_DIM - _ROT_DIM), _F32)
    zeros = jnp.zeros((pos.shape[0], _A_HEAD_DIM - half), _F32)
    zeros_h = jnp.zeros((pos.shape[0], half), _F32)
    reps = _LANES // _A_HEAD_DIM
    cos_t = jnp.tile(jnp.concatenate([cos, cos, ones], axis=1), (1, reps))
    sin_lo = jnp.tile(jnp.concatenate([-sin, zeros], axis=1), (1, reps))
    sin_hi = jnp.tile(jnp.concatenate([zeros_h, sin, ones * 0.0], axis=1), (1, reps))
    return cos_t, sin_lo, sin_hi


def _block_diag(w):
    nb, bd, _ = w.shape
    eye = jnp.eye(nb, dtype=w.dtype)
    return (w[:, :, None, :] * eye[:, None, :, None]).reshape(nb * bd, nb * bd)


def kernel(x_prompt, x_sample, c_prompt, c_sample, cache_k_win, cache_v_win, state_conv_rglru,
           state_h_rglru, state_s_hgrn, norm_pre, norm_post, ada_w, ada_b, ffn1_w_in, ffn1_w_out,
           ffn2_w_in, ffn2_w_out, even_w_in, even_w_out, attn_sinks, rg_conv_w, rg_conv_b, rg_wa,
           rg_ba, rg_wx, rg_bx, rg_lambda, odd_w_in, odd_w_out, hgrn_lb_logits, hgrn_gnorm):
    bp, tp, d = x_prompt.shape
    bs, ts, _ = x_sample.shape
    depth = norm_pre.shape[0]
    n_sub = depth * _N_SUB
    nk = _A_KV_HEADS * _A_HEAD_DIM
    win = cache_k_win.shape[2]
    cw = rg_conv_w.shape[1]
    bw = rg_conv_w.shape[2]
    kd = state_s_hgrn.shape[3]

    cast = lambda w: w.astype(_BF16)
    ffn1_in, ffn1_out, ffn2_in, ffn2_out = cast(ffn1_w_in), cast(ffn1_w_out), cast(ffn2_w_in), cast(ffn2_w_out)
    ev_in, ev_out, od_in, od_out = cast(even_w_in), cast(even_w_out), cast(odd_w_in), cast(odd_w_out)
    gpre = norm_pre.reshape(n_sub, 1, d)
    gpost = norm_post.reshape(n_sub, 1, d)

    mod = _ada_call(jnp.concatenate([c_prompt, c_sample], axis=0),
                    ada_w.reshape(n_sub, d, 3 * d), ada_b.reshape(n_sub, 1, 3 * d))
    mod_p = mod[:, :bp].reshape(n_sub, bp, 1, 3 * d)
    mod_s = jnp.repeat(mod[:, bp:], ts, axis=1).reshape(n_sub, 1, bs * ts, 3 * d)

    tabs_p = _rope_tables(jnp.arange(tp, dtype=jnp.int32))
    tabs_s = tuple(jnp.tile(t, (bs, 1)) for t in _rope_tables(_PAST_LEN + jnp.arange(ts, dtype=jnp.int32)))

    groups = {
        "p": dict(x=x_prompt.reshape(bp * tp, d), mod=mod_p, tps=tp // _ROW_TILE, tabs=tabs_p,
                  tab_tiles=tp // _ROW_TILE),
        "s": dict(x=x_sample.reshape(bs * ts, d), mod=mod_s, tps=1, tabs=tabs_s, tab_tiles=1),
    }
    outs = {g: dict(k=[], v=[], conv=[], h=[], s=[]) for g in groups}

    for l in range(depth):
        s0, s1, s2 = l * _N_SUB, l * _N_SUB + 1, l * _N_SUB + 2
        for name, grp in groups.items():
            x, mod4, tps = grp["x"], grp["mod"], grp["tps"]
            x = _ffn_call(x, mod4, s0, l, gpre, ffn1_in, ffn1_out, gpost, 0.5, tps)
            if l % 2 == 0:
                e = l // 2
                q, k, v, xg, xr = _even_in_call(x, mod4, s1, e, gpre, ev_in, grp["tabs"], tps, grp["tab_tiles"])
                wa_bd, wx_bd = cast(_block_diag(rg_wa[e])), cast(_block_diag(rg_wx[e]))
                vecs = [a[e].reshape(1, bw) for a in (rg_conv_b, rg_ba, rg_bx, rg_lambda)]
                if name == "p":
                    o_a = _swa_prompt_call(q, k, v, attn_sinks[e], bp)
                    o_b, conv, h_last = _rg_prompt_call(xr, xg, rg_conv_w[e], vecs[0], wa_bd, vecs[1], wx_bd,
                                                        vecs[2], vecs[3], bp)
                    outs[name]["k"].append(k.reshape(bp, tp, _A_KV_HEADS, _A_HEAD_DIM)[:, tp - win:])
                    outs[name]["v"].append(v.reshape(bp, tp, _A_KV_HEADS, _A_HEAD_DIM)[:, tp - win:])
                    outs[name]["h"].append(h_last.reshape(bp, bw))
                else:
                    o_a, kw, vw = _swa_sample_call(q, k, v, cache_k_win[e].reshape(bs, win, nk),
                                                   cache_v_win[e].reshape(bs, win, nk), attn_sinks[e], ts)
                    xp = jnp.concatenate([state_conv_rglru[e], xr.reshape(bs, ts, bw)], axis=1)
                    conv = xp[:, ts:]
                    o_b_t, h_last = _rg_sample_call(xp.transpose(1, 0, 2), xg.reshape(bs, ts, bw).transpose(1, 0, 2),
                                                    state_h_rglru[e], rg_conv_w[e], vecs[0], wa_bd, vecs[1],
                                                    wx_bd, vecs[2], vecs[3])
                    o_b = o_b_t.transpose(1, 0, 2).reshape(bs * ts, bw)
                    outs[name]["k"].append(kw.reshape(bs, win, _A_KV_HEADS, _A_HEAD_DIM))
                    outs[name]["v"].append(vw.reshape(bs, win, _A_KV_HEADS, _A_HEAD_DIM))
                    outs[name]["h"].append(h_last)
                outs[name]["conv"].append(conv)
                x = _out_call(x, mod4, s1, e, gpost, ev_out, [o_a, o_b], tps)
            else:
                o = l // 2
                q, fz, v, g = _odd_in_call(x, mod4, s1, o, gpre, od_in, tps)
                gn = hgrn_gnorm[o].reshape(1, kd)
                if name == "p":
                    y, s_last = _hgrn_prompt_call(q, fz, v, g, hgrn_lb_logits, gn, bp, l)
                else:
                    pad = lambda a: jnp.pad(a.reshape(bs, ts, -1), ((0, 0), (0, _SUBLANES - ts), (0, 0))
                                            ).reshape(bs * _SUBLANES, -1)
                    y8, s_last = _hgrn_sample_call(pad(q), pad(fz), pad(v), pad(g), hgrn_lb_logits, gn,
                                                   state_s_hgrn[o], l, ts)
                    y = y8.reshape(bs, _SUBLANES, -1)[:, :ts].reshape(bs * ts, -1)
                outs[name]["s"].append(s_last)
                x = _out_call(x, mod4, s1, o, gpost, od_out, [y], tps)
            x = _ffn_call(x, mod4, s2, l, gpre, ffn2_in, ffn2_out, gpost, 0.5, tps)
            grp["x"] = x

    res = []
    for name, (b, t) in (("p", (bp, tp)), ("s", (bs, ts))):
        o = outs[name]
        res.append((groups[name]["x"].reshape(b, t, d), jnp.stack(o["k"]), jnp.stack(o["v"]),
                    jnp.stack(o["conv"]), jnp.stack(o["h"]), jnp.stack(o["s"])))
    (yp, kp, vp, cp, hp, sp), (ys, ks, vs, cs, hs, ss) = res
    return (yp, ys, kp, vp, cp, hp, sp, ks, vs, cs, hs, ss)
```

```python
import functools

import jax
import jax.numpy as jnp
import numpy as np
from jax import lax
from jax.experimental import pallas as pl
from jax.experimental.pallas import tpu as pltpu

_F32 = jnp.float32
_BF16 = jnp.bfloat16

_EPS = 1e-6
_A_HEADS = 8
_A_KV_HEADS = 2
_A_HEAD_DIM = 64
_A_GROUP = _A_HEADS // _A_KV_HEADS
_WINDOW = 128
_ROPE_THETA = 500000.0
_ROT_DIM = _A_HEAD_DIM // 4
_RG_C = 8.0
_C_HEADS = 8
_PAST_LEN = 16384
_N_SUB = 3

_LANES = 128
_SUBLANES = 8
_VMEM_LIMIT = 48 * 1024 * 1024

_ROW_TILE = 512
_FF_CHUNK = 256
_RG_CHUNK = 256
_HG_CHUNK = 64
_HG_HEADS = 4
_HG_ROWS = 1024
_HG_SAFE_LOG_DECAY = 80.0
_SAMPLE_SEQS = 8


def _dot(a, b):
    return jnp.dot(a.astype(_BF16), b.astype(_BF16), preferred_element_type=_F32)


def _dot_nt(a, b):
    return lax.dot_general(a.astype(_BF16), b.astype(_BF16), (((1,), (1,)), ((), ())),
                           preferred_element_type=_F32)


def _dot_tn(a, b):
    return lax.dot_general(a.astype(_BF16), b.astype(_BF16), (((0,), (0,)), ((), ())),
                           preferred_element_type=_F32)


def _sigmoid(x):
    return 1.0 / (1.0 + jnp.exp(-x))


def _silu(x):
    return x * _sigmoid(x)


def _rms(x, gain):
    inv = lax.rsqrt(jnp.mean(x * x, axis=-1, keepdims=True) + _EPS)
    return x * inv * gain


def _pre(x, mod, gain):
    d = x.shape[-1]
    return _rms(x, gain) * (1.0 + mod[:, d:2 * d]) + mod[:, :d]


def _post(x, y, mod, gain, res_w):
    d = x.shape[-1]
    return x + (res_w * (1.0 + mod[:, 2 * d:])) * _rms(y, gain)


def _params(*sem):
    return pltpu.CompilerParams(dimension_semantics=sem, vmem_limit_bytes=_VMEM_LIMIT)


def _ada_kernel(c_ref, w_ref, b_ref, o_ref):
    o_ref[...] = _dot(_silu(c_ref[...]), w_ref[...]) + b_ref[...]


def _ada_call(c_all, ada_w, ada_b):
    m, d = c_all.shape
    n_sub = ada_w.shape[0]
    n = ada_w.shape[-1]
    tn = n // 2
    return pl.pallas_call(
        _ada_kernel,
        grid=(n_sub, n // tn),
        in_specs=[
            pl.BlockSpec((m, d), lambda s, j: (0, 0)),
            pl.BlockSpec((None, d, tn), lambda s, j: (s, 0, j)),
            pl.BlockSpec((None, 1, tn), lambda s, j: (s, 0, j)),
        ],
        out_specs=pl.BlockSpec((None, m, tn), lambda s, j: (s, 0, j)),
        out_shape=jax.ShapeDtypeStruct((n_sub, m, n), _F32),
        compiler_params=_params("parallel", "parallel"),
        name="ada_mod",
    )(c_all, ada_w, ada_b)


def _ffn_kernel(x_ref, mod_ref, gpre_ref, win_ref, wout_ref, gpost_ref, o_ref, *, res_w):
    x = x_ref[...]
    mod = mod_ref[0]
    dff = wout_ref.shape[0]
    h = _pre(x, mod, gpre_ref[...]).astype(_BF16)
    acc = jnp.zeros(x.shape, _F32)
    for j in range(dff // _FF_CHUNK):
        lo = j * _FF_CHUNK
        g = jnp.dot(h, win_ref[:, lo:lo + _FF_CHUNK], preferred_element_type=_F32)
        u = jnp.dot(h, win_ref[:, dff + lo:dff + lo + _FF_CHUNK], preferred_element_type=_F32)
        acc = acc + _dot(_silu(g) * u, wout_ref[lo:lo + _FF_CHUNK, :])
    o_ref[...] = _post(x, acc, mod, gpost_ref[...], res_w)


def _ffn_call(x, mod4, sub, layer, gpre, w_in, w_out, gpost, res_w, tiles_per_seq):
    n, d = x.shape
    dff = w_out.shape[1]
    r = mod4.shape[2]
    tm = min(_ROW_TILE, n)
    return pl.pallas_call(
        functools.partial(_ffn_kernel, res_w=res_w),
        grid=(n // tm,),
        in_specs=[
            pl.BlockSpec((tm, d), lambda i: (i, 0)),
            pl.BlockSpec((None, 1, r, 3 * d), lambda i: (sub, i // tiles_per_seq, 0, 0)),
            pl.BlockSpec((None, 1, d), lambda i: (sub, 0, 0)),
            pl.BlockSpec((None, d, 2 * dff), lambda i: (layer, 0, 0), pipeline_mode=pl.Buffered(1)),
            pl.BlockSpec((None, dff, d), lambda i: (layer, 0, 0), pipeline_mode=pl.Buffered(1)),
            pl.BlockSpec((None, 1, d), lambda i: (sub, 0, 0)),
        ],
        out_specs=pl.BlockSpec((tm, d), lambda i: (i, 0)),
        out_shape=jax.ShapeDtypeStruct((n, d), _F32),
        compiler_params=_params("parallel"),
        name="ffn_sublayer",
    )(x, mod4, gpre, w_in, w_out, gpost)


def _rope(x, cos, sin_lo, sin_hi):
    outs = []
    for j in range(x.shape[1] // _LANES):
        xc = x[:, j * _LANES:(j + 1) * _LANES]
        nxt = pltpu.roll(xc, _LANES - _ROT_DIM // 2, axis=1)
        prv = pltpu.roll(xc, _ROT_DIM // 2, axis=1)
        outs.append(xc * cos + nxt * sin_lo + prv * sin_hi)
    return outs[0] if len(outs) == 1 else jnp.concatenate(outs, axis=1)


def _even_in_kernel(x_ref, mod_ref, gpre_ref, w_ref, cos_ref, slo_ref, shi_ref,
                    q_ref, k_ref, v_ref, xg_ref, xr_ref):
    h = _pre(x_ref[...], mod_ref[0], gpre_ref[...])
    y = _dot(h, w_ref[...])
    nq, nk, nw = q_ref.shape[1], k_ref.shape[1], xg_ref.shape[1]
    cos, slo, shi = cos_ref[...], slo_ref[...], shi_ref[...]
    q_ref[...] = _rope(y[:, :nq], cos, slo, shi)
    k_ref[...] = _rope(y[:, nq:nq + nk], cos, slo, shi)
    v_ref[...] = y[:, nq + nk:nq + 2 * nk]
    xg_ref[...] = y[:, nq + 2 * nk:nq + 2 * nk + nw]
    xr_ref[...] = y[:, nq + 2 * nk + nw:]


def _even_in_call(x, mod4, sub, e, gpre, w_in, rope_tabs, tiles_per_seq, tab_tiles):
    n, d = x.shape
    r = mod4.shape[2]
    tm = min(_ROW_TILE, n)
    nq = _A_HEADS * _A_HEAD_DIM
    nk = _A_KV_HEADS * _A_HEAD_DIM
    nw = (w_in.shape[-1] - nq - 2 * nk) // 2
    row = lambda i: (i, 0)
    tab = pl.BlockSpec((tm, _LANES), lambda i: (i % tab_tiles, 0))
    return pl.pallas_call(
        _even_in_kernel,
        grid=(n // tm,),
        in_specs=[
            pl.BlockSpec((tm, d), row),
            pl.BlockSpec((None, 1, r, 3 * d), lambda i: (sub, i // tiles_per_seq, 0, 0)),
            pl.BlockSpec((None, 1, d), lambda i: (sub, 0, 0)),
            pl.BlockSpec((None, d, w_in.shape[-1]), lambda i: (e, 0, 0)),
            tab, tab, tab,
        ],
        out_specs=[pl.BlockSpec((tm, nq), row), pl.BlockSpec((tm, nk), row), pl.BlockSpec((tm, nk), row),
                   pl.BlockSpec((tm, nw), row), pl.BlockSpec((tm, nw), row)],
        out_shape=[jax.ShapeDtypeStruct((n, nq), _F32), jax.ShapeDtypeStruct((n, nk), _F32),
                   jax.ShapeDtypeStruct((n, nk), _F32), jax.ShapeDtypeStruct((n, nw), _F32),
                   jax.ShapeDtypeStruct((n, nw), _F32)],
        compiler_params=_params("parallel"),
        name="even_in_proj",
    )(x, mod4, gpre, w_in, *rope_tabs)


def _odd_in_kernel(x_ref, mod_ref, gpre_ref, w_ref, q_ref, f_ref, v_ref, g_ref):
    h = _pre(x_ref[...], mod_ref[0], gpre_ref[...])
    y = _dot(h, w_ref[...])
    n = q_ref.shape[1]
    q_ref[...] = y[:, :n]
    f_ref[...] = y[:, n:2 * n]
    v_ref[...] = y[:, 2 * n:3 * n]
    g_ref[...] = y[:, 3 * n:]


def _odd_in_call(x, mod4, sub, o, gpre, w_in, tiles_per_seq):
    n, d = x.shape
    r = mod4.shape[2]
    tm = min(_ROW_TILE, n)
    nw = w_in.shape[-1] // 4
    row = lambda i: (i, 0)
    return pl.pallas_call(
        _odd_in_kernel,
        grid=(n // tm,),
        in_specs=[
            pl.BlockSpec((tm, d), row),
            pl.BlockSpec((None, 1, r, 3 * d), lambda i: (sub, i // tiles_per_seq, 0, 0)),
            pl.BlockSpec((None, 1, d), lambda i: (sub, 0, 0)),
            pl.BlockSpec((None, d, 4 * nw), lambda i: (o, 0, 0)),
        ],
        out_specs=[pl.BlockSpec((tm, nw), row)] * 4,
        out_shape=[jax.ShapeDtypeStruct((n, nw), _F32)] * 4,
        compiler_params=_params("parallel"),
        name="odd_in_proj",
    )(x, mod4, gpre, w_in)


def _out_kernel(x_ref, mod_ref, gpost_ref, w_ref, *refs):
    a_refs, o_ref = refs[:-1], refs[-1]
    y = None
    off = 0
    for a_ref in a_refs:
        k = a_ref.shape[1]
        t = _dot(a_ref[...], w_ref[off:off + k, :])
        y = t if y is None else y + t
        off += k
    o_ref[...] = _post(x_ref[...], y, mod_ref[0], gpost_ref[...], 1.0)


def _out_call(x, mod4, sub, widx, gpost, w_out, acts, tiles_per_seq):
    n, d = x.shape
    r = mod4.shape[2]
    tm = min(_ROW_TILE, n)
    row = lambda i: (i, 0)
    return pl.pallas_call(
        _out_kernel,
        grid=(n // tm,),
        in_specs=[
            pl.BlockSpec((tm, d), row),
            pl.BlockSpec((None, 1, r, 3 * d), lambda i: (sub, i // tiles_per_seq, 0, 0)),
            pl.BlockSpec((None, 1, d), lambda i: (sub, 0, 0)),
            pl.BlockSpec((None, w_out.shape[1], d), lambda i: (widx, 0, 0)),
        ] + [pl.BlockSpec((tm, a.shape[1]), row) for a in acts],
        out_specs=pl.BlockSpec((tm, d), row),
        out_shape=jax.ShapeDtypeStruct((n, d), _F32),
        compiler_params=_params("parallel"),
        name="mixer_out_proj",
    )(x, mod4, gpost, w_out, *acts)


def _sink_softmax_pv(s, mask, sink, v):
    s = jnp.where(mask, s, -jnp.inf)
    m = jnp.maximum(jnp.max(s, axis=-1, keepdims=True), sink)
    p = jnp.exp(s - m)
    denom = jnp.sum(p, axis=-1, keepdims=True) + jnp.exp(sink - m)
    return _dot(p, v) / denom


def _swa_prompt_kernel(sink_ref, q_ref, kp_ref, kc_ref, vp_ref, vc_ref, o_ref):
    w = q_ref.shape[0]
    scale = np.float32(1.0 / np.sqrt(_A_HEAD_DIM))
    row = lax.broadcasted_iota(jnp.int32, (w, 2 * w), 0)
    col = lax.broadcasted_iota(jnp.int32, (w, 2 * w), 1)
    first = jnp.where(pl.program_id(1) > 0, 0, w)
    mask = jnp.logical_and(col > jnp.maximum(row, first - 1), col <= row + w)
    q = q_ref[...]
    outs = []
    for j in range(_A_KV_HEADS):
        ks = slice(j * _A_HEAD_DIM, (j + 1) * _A_HEAD_DIM)
        k = jnp.concatenate([kp_ref[:, ks], kc_ref[:, ks]], axis=0)
        v = jnp.concatenate([vp_ref[:, ks], vc_ref[:, ks]], axis=0)
        for g in range(_A_GROUP):
            h = j * _A_GROUP + g
            qh = q[:, h * _A_HEAD_DIM:(h + 1) * _A_HEAD_DIM]
            s = _dot_nt(qh, k) * scale
            outs.append(_sink_softmax_pv(s, mask, sink_ref[h], v))
    o_ref[...] = jnp.concatenate(outs, axis=1)


def _swa_prompt_call(q, k, v, sinks, batch):
    n, nq = q.shape
    nk = k.shape[1]
    w = _WINDOW
    nb = n // batch // w
    cur = lambda b, i: (b * nb + i, 0)
    prev = lambda b, i: (b * nb + jnp.maximum(i - 1, 0), 0)
    return pl.pallas_call(
        _swa_prompt_kernel,
        grid=(batch, nb),
        in_specs=[
            pl.BlockSpec(memory_space=pltpu.SMEM),
            pl.BlockSpec((w, nq), cur),
            pl.BlockSpec((w, nk), prev), pl.BlockSpec((w, nk), cur),
            pl.BlockSpec((w, nk), prev), pl.BlockSpec((w, nk), cur),
        ],
        out_specs=pl.BlockSpec((w, nq), cur),
        out_shape=jax.ShapeDtypeStruct((n, nq), _F32),
        compiler_params=_params("parallel", "parallel"),
        name="swa_prompt",
    )(sinks, q, k, k, v, v)


def _swa_sample_kernel(sink_ref, q_ref, kn_ref, vn_ref, ck_ref, cv_ref, o_ref, kw_ref, vw_ref, *, t_new):
    p = ck_ref.shape[1]
    scale = np.float32(1.0 / np.sqrt(_A_HEAD_DIM))
    rows = _A_GROUP * _SUBLANES
    t = lax.broadcasted_iota(jnp.int32, (rows, p + _SUBLANES), 0) & (_SUBLANES - 1)
    c = lax.broadcasted_iota(jnp.int32, (rows, p + _SUBLANES), 1)
    mask = jnp.logical_and(c <= t + p, c > t + p - _WINDOW)
    g_of_row = lax.broadcasted_iota(jnp.int32, (rows, 1), 0) >> (_SUBLANES.bit_length() - 1)
    seq_per_tile = _SUBLANES // t_new
    for s in range(ck_ref.shape[0]):
        base = (s // seq_per_tile) * _SUBLANES
        shift = (s % seq_per_tile) * t_new
        q8 = q_ref[base:base + _SUBLANES, :]
        kn8 = kn_ref[base:base + _SUBLANES, :]
        vn8 = vn_ref[base:base + _SUBLANES, :]
        if shift:
            q8 = pltpu.roll(q8, _SUBLANES - shift, axis=0)
            kn8 = pltpu.roll(kn8, _SUBLANES - shift, axis=0)
            vn8 = pltpu.roll(vn8, _SUBLANES - shift, axis=0)
        ck = ck_ref[s]
        cv = cv_ref[s]
        kw_ref[s, 0:p - t_new, :] = ck[t_new:, :]
        kw_ref[s, p - t_new:p, :] = kn8[0:t_new, :]
        vw_ref[s, 0:p - t_new, :] = cv[t_new:, :]
        vw_ref[s, p - t_new:p, :] = vn8[0:t_new, :]
        outs = []
        for j in range(_A_KV_HEADS):
            ks = slice(j * _A_HEAD_DIM, (j + 1) * _A_HEAD_DIM)
            k = jnp.concatenate([ck[:, ks], kn8[:, ks]], axis=0)
            v = jnp.concatenate([cv[:, ks], vn8[:, ks]], axis=0)
            qj = jnp.concatenate(
                [q8[:, (j * _A_GROUP + g) * _A_HEAD_DIM:(j * _A_GROUP + g + 1) * _A_HEAD_DIM]
                 for g in range(_A_GROUP)], axis=0)
            sink = jnp.zeros((rows, 1), _F32)
            for g in range(_A_GROUP):
                sink = jnp.where(g_of_row == g, sink_ref[j * _A_GROUP + g], sink)
            o = _sink_softmax_pv(_dot_nt(qj, k) * scale, mask, sink, v)
            outs.extend(o[g * _SUBLANES:(g + 1) * _SUBLANES, :] for g in range(_A_GROUP))
        o8 = jnp.concatenate(outs, axis=1)
        o_ref[base + shift:base + shift + t_new, :] = o8[0:t_new, :]


def _swa_sample_call(q, k_new, v_new, cache_k, cache_v, sinks, t_new):
    n, nq = q.shape
    nk = k_new.shape[1]
    nseq, p, _ = cache_k.shape
    bs = _SAMPLE_SEQS
    rows = lambda i: (i, 0)
    seqs = lambda i: (i, 0, 0)
    return pl.pallas_call(
        functools.partial(_swa_sample_kernel, t_new=t_new),
        grid=(nseq // bs,),
        in_specs=[
            pl.BlockSpec(memory_space=pltpu.SMEM),
            pl.BlockSpec((bs * t_new, nq), rows),
            pl.BlockSpec((bs * t_new, nk), rows), pl.BlockSpec((bs * t_new, nk), rows),
            pl.BlockSpec((bs, p, nk), seqs), pl.BlockSpec((bs, p, nk), seqs),
        ],
        out_specs=[pl.BlockSpec((bs * t_new, nq), rows),
                   pl.BlockSpec((bs, p, nk), seqs), pl.BlockSpec((bs, p, nk), seqs)],
        out_shape=[jax.ShapeDtypeStruct((n, nq), _F32),
                   jax.ShapeDtypeStruct(cache_k.shape, _F32), jax.ShapeDtypeStruct(cache_v.shape, _F32)],
        compiler_params=_params("parallel"),
        name="swa_sample",
    )(sinks, q, k_new, v_new, cache_k, cache_v)


def _softplus(z):
    return jnp.maximum(z, 0.0) + jnp.log1p(jnp.exp(-jnp.abs(z)))


def _gelu_tanh(x):
    return 0.5 * x * (1.0 + jnp.tanh(np.float32(np.sqrt(2.0 / np.pi)) * (x + 0.044715 * (x * x * x))))


def _rg_gates(xc, wa_ref, ba_ref, wx_ref, bx_ref, sp_neg_lam):
    r = _sigmoid(_dot(xc, wa_ref[...]) + ba_ref[...])
    i = _sigmoid(_dot(xc, wx_ref[...]) + bx_ref[...])
    log_a = (-_RG_C) * r * sp_neg_lam
    a = jnp.exp(log_a)
    mult = jnp.sqrt(jnp.maximum(1.0 - jnp.exp(2.0 * log_a), 0.0))
    return a, mult * (i * xc)


def _rg_prompt_kernel(xr_ref, xg_ref, cw_ref, cb_ref, wa_ref, ba_ref, wx_ref, bx_ref, lam_ref,
                      o_ref, conv_ref, h_ref, xpad, a_s, b_s):
    t_len, w = xr_ref.shape
    cw = cw_ref.shape[0]
    xpad[0:_SUBLANES, :] = jnp.zeros((_SUBLANES, w), _F32)
    xpad[_SUBLANES:, :] = xr_ref[...]
    sp = _softplus(-lam_ref[...])
    for c in range(t_len // _RG_CHUNK):
        r0 = c * _RG_CHUNK
        xc = cb_ref[...]
        for j in range(cw):
            lo = _SUBLANES + r0 - (cw - 1) + j
            xc = xc + xpad[lo:lo + _RG_CHUNK, :] * cw_ref[j:j + 1, :]
        a, b = _rg_gates(xc, wa_ref, ba_ref, wx_ref, bx_ref, sp)
        a_s[r0:r0 + _RG_CHUNK, :] = a
        b_s[r0:r0 + _RG_CHUNK, :] = b

    row = lax.broadcasted_iota(jnp.int32, (_SUBLANES, w), 0)

    def group(g, h):
        r0 = pl.multiple_of(g * _SUBLANES, _SUBLANES)
        a = a_s[pl.ds(r0, _SUBLANES), :]
        b = b_s[pl.ds(r0, _SUBLANES), :]
        sh = 1
        while sh < _SUBLANES:
            a_prev = jnp.where(row >= sh, pltpu.roll(a, sh, axis=0), 1.0)
            b_prev = jnp.where(row >= sh, pltpu.roll(b, sh, axis=0), 0.0)
            b = a * b_prev + b
            a = a * a_prev
            sh *= 2
        hs = a * h + b
        o_ref[pl.ds(r0, _SUBLANES), :] = _gelu_tanh(xg_ref[pl.ds(r0, _SUBLANES), :]) * hs
        return hs[_SUBLANES - 1:_SUBLANES, :]

    h_last = lax.fori_loop(0, t_len // _SUBLANES, group, jnp.zeros((1, w), _F32), unroll=2)
    h_ref[0] = h_last
    conv_ref[0] = xr_ref[t_len - (cw - 1):t_len, :]


def _rg_prompt_call(xr, xg, conv_w, conv_b, wa_bd, ba, wx_bd, bx, lam, batch):
    n, w = xr.shape
    t_len = n // batch
    cw = conv_w.shape[0]
    seq = lambda b: (b, 0)
    const = lambda b: (0, 0)
    vec = pl.BlockSpec((1, w), const)
    mat = pl.BlockSpec((w, w), const)
    return pl.pallas_call(
        _rg_prompt_kernel,
        grid=(batch,),
        in_specs=[pl.BlockSpec((t_len, w), seq), pl.BlockSpec((t_len, w), seq),
                  pl.BlockSpec((cw, w), const), vec, mat, vec, mat, vec, vec],
        out_specs=[pl.BlockSpec((t_len, w), seq),
                   pl.BlockSpec((1, cw - 1, w), lambda b: (b, 0, 0)),
                   pl.BlockSpec((1, 1, w), lambda b: (b, 0, 0))],
        out_shape=[jax.ShapeDtypeStruct((n, w), _F32),
                   jax.ShapeDtypeStruct((batch, cw - 1, w), _F32),
                   jax.ShapeDtypeStruct((batch, 1, w), _F32)],
        scratch_shapes=[pltpu.VMEM((t_len + _SUBLANES, w), _F32),
                        pltpu.VMEM((t_len, w), _F32), pltpu.VMEM((t_len, w), _F32)],
        compiler_params=_params("parallel"),
        name="rglru_prompt",
    )(xr, xg, conv_w, conv_b, wa_bd, ba, wx_bd, bx, lam)


def _rg_sample_kernel(xp_ref, xg_ref, h0_ref, cw_ref, cb_ref, wa_ref, ba_ref, wx_ref, bx_ref, lam_ref,
                      o_ref, h_ref):
    cw = cw_ref.shape[0]
    sp = _softplus(-lam_ref[...])
    h = h0_ref[...]
    for t in range(o_ref.shape[0]):
        xc = cb_ref[...]
        for j in range(cw):
            xc = xc + xp_ref[t + j] * cw_ref[j:j + 1, :]
        a, b = _rg_gates(xc, wa_ref, ba_ref, wx_ref, bx_ref, sp)
        h = a * h + b
        o_ref[t] = _gelu_tanh(xg_ref[t]) * h
    h_ref[...] = h


def _rg_sample_call(xp_t, xg_t, h0, conv_w, conv_b, wa_bd, ba, wx_bd, bx, lam):
    t_new, nseq, w = xg_t.shape
    return pl.pallas_call(
        _rg_sample_kernel,
        out_shape=[jax.ShapeDtypeStruct((t_new, nseq, w), _F32), jax.ShapeDtypeStruct((nseq, w), _F32)],
        compiler_params=pltpu.CompilerParams(vmem_limit_bytes=_VMEM_LIMIT),
        name="rglru_sample",
    )(xp_t, xg_t, h0, conv_w, conv_b, wa_bd, ba, wx_bd, bx, lam)


def _row_bcast(x, r, n):
    return jnp.broadcast_to(x[r:r + 1, :], (n, x.shape[1]))


def _chunk_cumsum(x):
    n_tiles = x.shape[0] // _SUBLANES
    row = lax.broadcasted_iota(jnp.int32, (_SUBLANES, x.shape[1]), 0)
    tiles = []
    carry = None
    for i in range(n_tiles):
        t = x[i * _SUBLANES:(i + 1) * _SUBLANES, :]
        sh = 1
        while sh < _SUBLANES:
            t = t + jnp.where(row >= sh, pltpu.roll(t, sh, axis=0), 0.0)
            sh *= 2
        if carry is not None:
            t = t + carry
        carry = _row_bcast(t, _SUBLANES - 1, _SUBLANES)
        tiles.append(t)
    return tiles[0] if n_tiles == 1 else jnp.concatenate(tiles, axis=0)


def _level_reference(b, m):
    n = b.shape[0]
    if 2 * m >= _SUBLANES:
        pieces = [_row_bcast(b, lo + m - 1, 2 * m) for lo in range(0, n, 2 * m)]
        return pieces[0] if len(pieces) == 1 else jnp.concatenate(pieces, axis=0)
    row = lax.broadcasted_iota(jnp.int32, (_SUBLANES, b.shape[1]), 0)
    tiles = []
    for i in range(n // _SUBLANES):
        t = b[i * _SUBLANES:(i + 1) * _SUBLANES, :]
        ref = None
        for lo in range(0, _SUBLANES, 2 * m):
            piece = _row_bcast(t, lo + m - 1, _SUBLANES)
            ref = piece if ref is None else jnp.where(row >= lo, piece, ref)
        tiles.append(ref)
    return tiles[0] if len(tiles) == 1 else jnp.concatenate(tiles, axis=0)


def _hgrn_gates(fz, lb):
    f = lb + (1.0 - lb) * _sigmoid(fz)
    return jnp.log(f), 1.0 - f


def _hgrn_chunk(q, fz, v, lb, state, n_valid, state_is_vk):
    n, kd = q.shape
    log_f, k = _hgrn_gates(fz, lb)
    if n_valid < n:
        valid = lax.broadcasted_iota(jnp.int32, (n, kd), 0) < n_valid
        log_f = jnp.where(valid, log_f, 0.0)
        k = jnp.where(valid, k, 0.0)
    b = _chunk_cumsum(log_f)
    b_last = _row_bcast(b, n - 1, n)

    q_in = q * jnp.exp(b)
    k_end = k * jnp.exp(b_last - b)
    if state_is_vk:
        o = _dot_nt(q_in, state)
        new_state = jnp.exp(b_last[0:1, :]) * state + _dot_tn(v, k_end)
    else:
        o = _dot(q_in, state)
        decay = jnp.exp(jnp.broadcast_to(b_last[0:1, :], (v.shape[1], kd))).T
        new_state = decay * state + _dot_tn(k_end, v)

    row = lax.broadcasted_iota(jnp.int32, (n, n), 0)
    col = lax.broadcasted_iota(jnp.int32, (n, n), 1)
    upper = lax.broadcasted_iota(jnp.int32, (n, kd), 0)
    scores = jnp.where(row == col, jnp.sum(q * k, axis=-1, keepdims=True), 0.0)
    m = 1
    while m < n_valid:
        e = jnp.exp(-jnp.abs(b - _level_reference(b, m)))
        z = jnp.where((upper & m) != 0, q, k) * e
        pair = jnp.logical_and((row & m) != 0, (row ^ m) >> (m.bit_length() - 1) == col >> (m.bit_length() - 1))
        scores = scores + jnp.where(pair, _dot_nt(z, z), 0.0)
        m *= 2
    return o + _dot(scores, v), new_state


def _lower_bound(logits, layer):
    m = jnp.max(logits, axis=0, keepdims=True)
    e = jnp.exp(logits - m)
    return jnp.sum(e[1:layer + 1, :], axis=0, keepdims=True) / jnp.sum(e, axis=0, keepdims=True)


def _hgrn_out(o, g, gnorm):
    return _rms(o, gnorm) * _silu(g)


def _hgrn_prompt_kernel(q_ref, f_ref, v_ref, g_ref, lbl_ref, gn_ref, o_ref, s_ref,
                        st_scr, qs_scr, ks_scr, ke_scr, vb_scr, dec_scr, inc_scr, *, layer):
    rows_blk, width = q_ref.shape
    kd = width // _HG_HEADS
    n_chunks = rows_blk // _HG_CHUNK
    lb = _lower_bound(lbl_ref[...], layer)
    gn = gn_ref[...]

    @pl.when(pl.program_id(2) == 0)
    def _():
        st_scr[...] = jnp.zeros(st_scr.shape, _F32)

    def chunk_rows(c):
        return pl.ds(pl.multiple_of(c * _HG_CHUNK, _HG_CHUNK), _HG_CHUNK)

    def prepare(c, min_b):
        rows = chunk_rows(c)
        log_f, k = _hgrn_gates(f_ref[rows, :], lb)
        b = _chunk_cumsum(log_f)
        b_end = b[_HG_CHUNK - 1:_HG_CHUNK, :]
        dec = jnp.exp(b_end)
        k_start = k * jnp.exp(-b)
        qs_scr[rows, :] = (q_ref[rows, :] * jnp.exp(b)).astype(_BF16)
        ks_scr[rows, :] = k_start.astype(_BF16)
        ke_scr[rows, :] = (k_start * dec).astype(_BF16)
        vb_scr[rows, :] = v_ref[rows, :].astype(_BF16)
        dec_scr[c] = jnp.broadcast_to(dec, (_SUBLANES, width))
        return jnp.minimum(min_b, b_end)

    min_b = lax.fori_loop(0, n_chunks, prepare, jnp.zeros((1, width), _F32))
    single_split_ok = jnp.min(min_b) > -_HG_SAFE_LOG_DECAY

    row = lax.broadcasted_iota(jnp.int32, (_HG_CHUNK, _HG_CHUNK), 0)
    col = lax.broadcasted_iota(jnp.int32, (_HG_CHUNK, _HG_CHUNK), 1)

    def finish(c, h, o):
        rows, lanes = chunk_rows(c), slice(h * kd, (h + 1) * kd)
        o_ref[rows, lanes] = _hgrn_out(o, g_ref[rows, lanes], gn)

    @pl.when(single_split_ok)
    def _():
        heads = [slice(h * kd, (h + 1) * kd) for h in range(_HG_HEADS)]

        def in_chunk(c, carry):
            rows = chunk_rows(c)
            scores = [_dot_nt(qs_scr[rows, l], ks_scr[rows, l]) for l in heads]
            incs = [_dot_tn(vb_scr[rows, l], ke_scr[rows, l]) for l in heads]
            for h in range(_HG_HEADS):
                inc_scr[c, h] = incs[h]
            outs = [_dot(jnp.where(row >= col, s, 0.0), vb_scr[rows, l]) for s, l in zip(scores, heads)]
            for o, l in zip(outs, heads):
                o_ref[rows, l] = o
            return carry
        lax.fori_loop(0, n_chunks, in_chunk, 0, unroll=2)

        def cross_chunk(c, carry):
            rows = chunk_rows(c)
            states = [st_scr[h] for h in range(_HG_HEADS)]
            carried = [_dot_nt(qs_scr[rows, l], st) for l, st in zip(heads, states)]
            for h, l in enumerate(heads):
                st_scr[h] = states[h] * dec_scr[c, 0:1, l] + inc_scr[c, h]
            for h, l in enumerate(heads):
                finish(c, h, o_ref[rows, l] + carried[h])
            return carry
        lax.fori_loop(0, n_chunks, cross_chunk, 0, unroll=2)

    @pl.when(jnp.logical_not(single_split_ok))
    def _():
        def chunk(c, carry):
            rows = chunk_rows(c)
            for h in range(_HG_HEADS):
                lanes = slice(h * kd, (h + 1) * kd)
                o, st_scr[h] = _hgrn_chunk(q_ref[rows, lanes], f_ref[rows, lanes], v_ref[rows, lanes],
                                           lb[:, lanes], st_scr[h], _HG_CHUNK, True)
                finish(c, h, o)
            return carry
        lax.fori_loop(0, n_chunks, chunk, 0)

    @pl.when(pl.program_id(2) == pl.num_programs(2) - 1)
    def _():
        for h in range(_HG_HEADS):
            s_ref[0, h] = st_scr[h].T


def _hgrn_prompt_call(q, fz, v, g, lb_logits, gnorm, batch, layer):
    n, width = q.shape
    t_len = n // batch
    kd = width // _C_HEADS
    gw = _HG_HEADS * kd
    rows_blk = min(_HG_ROWS, t_len)
    nt = t_len // rows_blk
    blk = pl.BlockSpec((rows_blk, gw), lambda b, h, t: (b * nt + t, h))
    return pl.pallas_call(
        functools.partial(_hgrn_prompt_kernel, layer=layer),
        grid=(batch, _C_HEADS // _HG_HEADS, nt),
        in_specs=[blk, blk, blk, blk,
                  pl.BlockSpec((lb_logits.shape[0], gw), lambda b, h, t: (0, h)),
                  pl.BlockSpec((1, kd), lambda b, h, t: (0, 0))],
        out_specs=[blk, pl.BlockSpec((1, _HG_HEADS, kd, kd), lambda b, h, t: (b, h, 0, 0))],
        out_shape=[jax.ShapeDtypeStruct((n, width), _F32),
                   jax.ShapeDtypeStruct((batch, _C_HEADS, kd, kd), _F32)],
        scratch_shapes=[pltpu.VMEM((_HG_HEADS, kd, kd), _F32)]
        + [pltpu.VMEM((rows_blk, gw), _BF16)] * 4
        + [pltpu.VMEM((rows_blk // _HG_CHUNK, _SUBLANES, gw), _F32),
           pltpu.VMEM((rows_blk // _HG_CHUNK, _HG_HEADS, kd, kd), _F32)],
        compiler_params=_params("parallel", "parallel", "arbitrary"),
        name="hgrn2_prompt",
    )(q, fz, v, g, lb_logits, gnorm)


def _hgrn_sample_kernel(q_ref, f_ref, v_ref, g_ref, lbl_ref, gn_ref, s0_ref, o_ref, s_ref, *, layer, t_new):
    lb = _lower_bound(lbl_ref[...], layer)
    gn = gn_ref[...]
    for s in range(s0_ref.shape[0]):
        rows = slice(s * _SUBLANES, (s + 1) * _SUBLANES)
        o, state = _hgrn_chunk(q_ref[rows, :], f_ref[rows, :], v_ref[rows, :], lb, s0_ref[s, 0], t_new, False)
        o_ref[rows, :] = _hgrn_out(o, g_ref[rows, :], gn)
        s_ref[s, 0] = state


def _hgrn_sample_call(q, fz, v, g, lb_logits, gnorm, s0, layer, t_new):
    n, width = q.shape
    nseq = s0.shape[0]
    kd = width // _C_HEADS
    bs = _SAMPLE_SEQS
    blk = pl.BlockSpec((bs * _SUBLANES, kd), lambda i, h: (i, h))
    st = pl.BlockSpec((bs, 1, kd, kd), lambda i, h: (i, h, 0, 0))
    return pl.pallas_call(
        functools.partial(_hgrn_sample_kernel, layer=layer, t_new=t_new),
        grid=(nseq // bs, _C_HEADS),
        in_specs=[blk, blk, blk, blk,
                  pl.BlockSpec((lb_logits.shape[0], kd), lambda i, h: (0, h)),
                  pl.BlockSpec((1, kd), lambda i, h: (0, 0)), st],
        out_specs=[blk, st],
        out_shape=[jax.ShapeDtypeStruct((n, width), _F32), jax.ShapeDtypeStruct(s0.shape, _F32)],
        compiler_params=_params("parallel", "parallel"),
        name="hgrn2_sample",
    )(q, fz, v, g, lb_logits, gnorm, s0)


def _rope_tables(pos):
    half = _ROT_DIM // 2
    inv_freq = _ROPE_THETA ** (-jnp.arange(0, _ROT_DIM, 2, dtype=_F32) / _ROT_DIM)
    ang = pos.astype(_F32)[:, None] * inv_freq[None, :]
    cos, sin = jnp.cos(ang), jnp.sin(ang)
    ones = jnp.ones((pos.shape[0], _A_HEAD_DIM - _ROT_DIM), _F32)
    zeros = jnp.zeros((pos.shape[0], _A_HEAD_DIM - half), _F32)
    zeros_h = jnp.zeros((pos.shape[0], half), _F32)
    reps = _LANES // _A_HEAD_DIM
    cos_t = jnp.tile(jnp.concatenate([cos, cos, ones], axis=1), (1, reps))
    sin_lo = jnp.tile(jnp.concatenate([-sin, zeros], axis=1), (1, reps))
    sin_hi = jnp.tile(jnp.concatenate([zeros_h, sin, ones * 0.0], axis=1), (1, reps))
    return cos_t, sin_lo, sin_hi


def _block_diag(w):
    nb, bd, _ = w.shape
    eye = jnp.eye(nb, dtype=w.dtype)
    return (w[:, :, None, :] * eye[:, None, :, None]).reshape(nb * bd, nb * bd)


def kernel(x_prompt, x_sample, c_prompt, c_sample, cache_k_win, cache_v_win, state_conv_rglru,
           state_h_rglru, state_s_hgrn, norm_pre, norm_post, ada_w, ada_b, ffn1_w_in, ffn1_w_out,
           ffn2_w_in, ffn2_w_out, even_w_in, even_w_out, attn_sinks, rg_conv_w, rg_conv_b, rg_wa,
           rg_ba, rg_wx, rg_bx, rg_lambda, odd_w_in, odd_w_out, hgrn_lb_logits, hgrn_gnorm):
    bp, tp, d = x_prompt.shape
    bs, ts, _ = x_sample.shape
    depth = norm_pre.shape[0]
    n_sub = depth * _N_SUB
    nk = _A_KV_HEADS * _A_HEAD_DIM
    win = cache_k_win.shape[2]
    cw = rg_conv_w.shape[1]
    bw = rg_conv_w.shape[2]
    kd = state_s_hgrn.shape[3]

    cast = lambda w: w.astype(_BF16)
    ffn1_in, ffn1_out, ffn2_in, ffn2_out = cast(ffn1_w_in), cast(ffn1_w_out), cast(ffn2_w_in), cast(ffn2_w_out)
    ev_in, ev_out, od_in, od_out = cast(even_w_in), cast(even_w_out), cast(odd_w_in), cast(odd_w_out)
    gpre = norm_pre.reshape(n_sub, 1, d)
    gpost = norm_post.reshape(n_sub, 1, d)

    mod = _ada_call(jnp.concatenate([c_prompt, c_sample], axis=0),
                    ada_w.reshape(n_sub, d, 3 * d), ada_b.reshape(n_sub, 1, 3 * d))
    mod_p = mod[:, :bp].reshape(n_sub, bp, 1, 3 * d)
    mod_s = jnp.repeat(mod[:, bp:], ts, axis=1).reshape(n_sub, 1, bs * ts, 3 * d)

    tabs_p = _rope_tables(jnp.arange(tp, dtype=jnp.int32))
    tabs_s = tuple(jnp.tile(t, (bs, 1)) for t in _rope_tables(_PAST_LEN + jnp.arange(ts, dtype=jnp.int32)))

    groups = {
        "p": dict(x=x_prompt.reshape(bp * tp, d), mod=mod_p, tps=tp // _ROW_TILE, tabs=tabs_p,
                  tab_tiles=tp // _ROW_TILE),
        "s": dict(x=x_sample.reshape(bs * ts, d), mod=mod_s, tps=1, tabs=tabs_s, tab_tiles=1),
    }
    outs = {g: dict(k=[], v=[], conv=[], h=[], s=[]) for g in groups}

    for l in range(depth):
        s0, s1, s2 = l * _N_SUB, l * _N_SUB + 1, l * _N_SUB + 2
        for name, grp in groups.items():
            x, mod4, tps = grp["x"], grp["mod"], grp["tps"]
            x = _ffn_call(x, mod4, s0, l, gpre, ffn1_in, ffn1_out, gpost, 0.5, tps)
            if l % 2 == 0:
                e = l // 2
                q, k, v, xg, xr = _even_in_call(x, mod4, s1, e, gpre, ev_in, grp["tabs"], tps, grp["tab_tiles"])
                wa_bd, wx_bd = cast(_block_diag(rg_wa[e])), cast(_block_diag(rg_wx[e]))
                vecs = [a[e].reshape(1, bw) for a in (rg_conv_b, rg_ba, rg_bx, rg_lambda)]
                if name == "p":
                    o_a = _swa_prompt_call(q, k, v, attn_sinks[e], bp)
                    o_b, conv, h_last = _rg_prompt_call(xr, xg, rg_conv_w[e], vecs[0], wa_bd, vecs[1], wx_bd,
                                                        vecs[2], vecs[3], bp)
                    outs[name]["k"].append(k.reshape(bp, tp, _A_KV_HEADS, _A_HEAD_DIM)[:, tp - win:])
                    outs[name]["v"].append(v.reshape(bp, tp, _A_KV_HEADS, _A_HEAD_DIM)[:, tp - win:])
                    outs[name]["h"].append(h_last.reshape(bp, bw))
                else:
                    o_a, kw, vw = _swa_sample_call(q, k, v, cache_k_win[e].reshape(bs, win, nk),
                                                   cache_v_win[e].reshape(bs, win, nk), attn_sinks[e], ts)
                    xp = jnp.concatenate([state_conv_rglru[e], xr.reshape(bs, ts, bw)], axis=1)
                    conv = xp[:, ts:]
                    o_b_t, h_last = _rg_sample_call(xp.transpose(1, 0, 2), xg.reshape(bs, ts, bw).transpose(1, 0, 2),
                                                    state_h_rglru[e], rg_conv_w[e], vecs[0], wa_bd, vecs[1],
                                                    wx_bd, vecs[2], vecs[3])
                    o_b = o_b_t.transpose(1, 0, 2).reshape(bs * ts, bw)
                    outs[name]["k"].append(kw.reshape(bs, win, _A_KV_HEADS, _A_HEAD_DIM))
                    outs[name]["v"].append(vw.reshape(bs, win, _A_KV_HEADS, _A_HEAD_DIM))
                    outs[name]["h"].append(h_last)
                outs[name]["conv"].append(conv)
                x = _out_call(x, mod4, s1, e, gpost, ev_out, [o_a, o_b], tps)
            else:
                o = l // 2
                q, fz, v, g = _odd_in_call(x, mod4, s1, o, gpre, od_in, tps)
                gn = hgrn_gnorm[o].reshape(1, kd)
                if name == "p":
                    y, s_last = _hgrn_prompt_call(q, fz, v, g, hgrn_lb_logits, gn, bp, l)
                else:
                    pad = lambda a: jnp.pad(a.reshape(bs, ts, -1), ((0, 0), (0, _SUBLANES - ts), (0, 0))
                                            ).reshape(bs * _SUBLANES, -1)
                    y8, s_last = _hgrn_sample_call(pad(q), pad(fz), pad(v), pad(g), hgrn_lb_logits, gn,
                                                   state_s_hgrn[o], l, ts)
                    y = y8.reshape(bs, _SUBLANES, -1)[:, :ts].reshape(bs * ts, -1)
                outs[name]["s"].append(s_last)
                x = _out_call(x, mod4, s1, o, gpost, od_out, [y], tps)
            x = _ffn_call(x, mod4, s2, l, gpre, ffn2_in, ffn2_out, gpost, 0.5, tps)
            grp["x"] = x

    res = []
    for name, (b, t) in (("p", (bp, tp)), ("s", (bs, ts))):
        o = outs[name]
        res.append((groups[name]["x"].reshape(b, t, d), jnp.stack(o["k"]), jnp.stack(o["v"]),
                    jnp.stack(o["conv"]), jnp.stack(o["h"]), jnp.stack(o["s"])))
    (yp, kp, vp, cp, hp, sp), (ys, ks, vs, cs, hs, ss) = res
    return (yp, ys, kp, vp, cp, hp, sp, ks, vs, cs, hs, ss)
```

```python
import functools

import jax
import jax.numpy as jnp
import numpy as np
from jax import lax
from jax.experimental import pallas as pl
from jax.experimental.pallas import tpu as pltpu

_F32 = jnp.float32
_BF16 = jnp.bfloat16

_EPS = 1e-6
_A_HEADS = 8
_A_KV_HEADS = 2
_A_HEAD_DIM = 64
_A_GROUP = _A_HEADS // _A_KV_HEADS
_WINDOW = 128
_ROPE_THETA = 500000.0
_ROT_DIM = _A_HEAD_DIM // 4
_RG_C = 8.0
_C_HEADS = 8
_PAST_LEN = 16384
_N_SUB = 3

_LANES = 128
_SUBLANES = 8
_VMEM_LIMIT = 48 * 1024 * 1024

_ROW_TILE = 512
_FF_CHUNK = 256
_RG_CHUNK = 256
_HG_CHUNK = 64
_HG_HEADS = 4
_HG_ROWS = 1024
_HG_SAFE_LOG_DECAY = 80.0
_SAMPLE_SEQS = 8
_SWA_QBLOCKS = 4


def _dot(a, b):
    return jnp.dot(a.astype(_BF16), b.astype(_BF16), preferred_element_type=_F32)


def _dot_nt(a, b):
    return lax.dot_general(a.astype(_BF16), b.astype(_BF16), (((1,), (1,)), ((), ())),
                           preferred_element_type=_F32)


def _dot_tn(a, b):
    return lax.dot_general(a.astype(_BF16), b.astype(_BF16), (((0,), (0,)), ((), ())),
                           preferred_element_type=_F32)


def _sigmoid(x):
    return 1.0 / (1.0 + jnp.exp(-x))


def _silu(x):
    return x * _sigmoid(x)


def _rms(x, gain):
    inv = lax.rsqrt(jnp.mean(x * x, axis=-1, keepdims=True) + _EPS)
    return x * inv * gain


def _per_seq(a, r):
    n = a.shape[0]
    return a if r in (1, n) else a.reshape(n // r, r, a.shape[1])


def _pre(x, mod, gain):
    n, d = x.shape
    h = _rms(_per_seq(x, mod.shape[0]), gain) * (1.0 + mod[:, d:2 * d]) + mod[:, :d]
    return h.reshape(n, d)


def _post(x, y, mod, gain, res_w):
    n, d = x.shape
    r = mod.shape[0]
    out = _per_seq(x, r) + (res_w * (1.0 + mod[:, 2 * d:])) * _rms(_per_seq(y, r), gain)
    return out.reshape(n, d)


def _params(*sem):
    return pltpu.CompilerParams(dimension_semantics=sem, vmem_limit_bytes=_VMEM_LIMIT)


def _ada_kernel(c_ref, w_ref, b_ref, o_ref):
    o_ref[...] = _dot(_silu(c_ref[...]), w_ref[...]) + b_ref[...]


def _ada_call(c_all, ada_w, ada_b):
    m, d = c_all.shape
    n_sub = ada_w.shape[0]
    n = ada_w.shape[-1]
    tn = n // 2
    return pl.pallas_call(
        _ada_kernel,
        grid=(n_sub, n // tn),
        in_specs=[
            pl.BlockSpec((m, d), lambda s, j: (0, 0)),
            pl.BlockSpec((None, d, tn), lambda s, j: (s, 0, j)),
            pl.BlockSpec((None, 1, tn), lambda s, j: (s, 0, j)),
        ],
        out_specs=pl.BlockSpec((None, m, tn), lambda s, j: (s, 0, j)),
        out_shape=jax.ShapeDtypeStruct((n_sub, m, n), _F32),
        compiler_params=_params("parallel", "parallel"),
        name="ada_mod",
    )(c_all, ada_w, ada_b)


def _ffn_kernel(x_ref, mod_ref, gpre_ref, win_ref, wout_ref, gpost_ref, o_ref, *, res_w):
    x = x_ref[...]
    mod = mod_ref[0]
    dff = wout_ref.shape[0]
    h = _pre(x, mod, gpre_ref[...]).astype(_BF16)
    acc = jnp.zeros(x.shape, _F32)
    for j in range(dff // _FF_CHUNK):
        lo = j * _FF_CHUNK
        g = jnp.dot(h, win_ref[:, lo:lo + _FF_CHUNK], preferred_element_type=_F32)
        u = jnp.dot(h, win_ref[:, dff + lo:dff + lo + _FF_CHUNK], preferred_element_type=_F32)
        acc = acc + _dot(_silu(g) * u, wout_ref[lo:lo + _FF_CHUNK, :])
    o_ref[...] = _post(x, acc, mod, gpost_ref[...], res_w)


def _ffn_call(x, mod4, sub, layer, gpre, w_in, w_out, gpost, res_w, tiles_per_seq):
    n, d = x.shape
    dff = w_out.shape[1]
    r = mod4.shape[2]
    tm = min(_ROW_TILE, n)
    return pl.pallas_call(
        functools.partial(_ffn_kernel, res_w=res_w),
        grid=(n // tm,),
        in_specs=[
            pl.BlockSpec((tm, d), lambda i: (i, 0)),
            pl.BlockSpec((None, 1, r, 3 * d), lambda i: (sub, i // tiles_per_seq, 0, 0)),
            pl.BlockSpec((None, 1, d), lambda i: (sub, 0, 0)),
            pl.BlockSpec((None, d, 2 * dff), lambda i: (layer, 0, 0), pipeline_mode=pl.Buffered(1)),
            pl.BlockSpec((None, dff, d), lambda i: (layer, 0, 0), pipeline_mode=pl.Buffered(1)),
            pl.BlockSpec((None, 1, d), lambda i: (sub, 0, 0)),
        ],
        out_specs=pl.BlockSpec((tm, d), lambda i: (i, 0)),
        out_shape=jax.ShapeDtypeStruct((n, d), _F32),
        compiler_params=_params("parallel"),
        name="ffn_sublayer",
    )(x, mod4, gpre, w_in, w_out, gpost)


def _rope(x, cos, sin_lo, sin_hi):
    outs = []
    for j in range(x.shape[1] // _LANES):
        xc = x[:, j * _LANES:(j + 1) * _LANES]
        nxt = pltpu.roll(xc, _LANES - _ROT_DIM // 2, axis=1)
        prv = pltpu.roll(xc, _ROT_DIM // 2, axis=1)
        outs.append(xc * cos + nxt * sin_lo + prv * sin_hi)
    return outs[0] if len(outs) == 1 else jnp.concatenate(outs, axis=1)


def _even_in_kernel(x_ref, mod_ref, gpre_ref, w_ref, cos_ref, slo_ref, shi_ref,
                    q_ref, k_ref, v_ref, xg_ref, xr_ref):
    h = _pre(x_ref[...], mod_ref[0], gpre_ref[...])
    y = _dot(h, w_ref[...])
    nq, nk, nw = q_ref.shape[1], k_ref.shape[1], xg_ref.shape[1]
    cos, slo, shi = cos_ref[...], slo_ref[...], shi_ref[...]
    q_ref[...] = _rope(y[:, :nq], cos, slo, shi)
    k_ref[...] = _rope(y[:, nq:nq + nk], cos, slo, shi)
    v_ref[...] = y[:, nq + nk:nq + 2 * nk]
    xg_ref[...] = y[:, nq + 2 * nk:nq + 2 * nk + nw]
    xr_ref[...] = y[:, nq + 2 * nk + nw:]


def _even_in_call(x, mod4, sub, e, gpre, w_in, rope_tabs, tiles_per_seq, tab_tiles):
    n, d = x.shape
    r = mod4.shape[2]
    tm = min(_ROW_TILE, n)
    nq = _A_HEADS * _A_HEAD_DIM
    nk = _A_KV_HEADS * _A_HEAD_DIM
    nw = (w_in.shape[-1] - nq - 2 * nk) // 2
    row = lambda i: (i, 0)
    tab = pl.BlockSpec((tm, _LANES), lambda i: (i % tab_tiles, 0))
    return pl.pallas_call(
        _even_in_kernel,
        grid=(n // tm,),
        in_specs=[
            pl.BlockSpec((tm, d), row),
            pl.BlockSpec((None, 1, r, 3 * d), lambda i: (sub, i // tiles_per_seq, 0, 0)),
            pl.BlockSpec((None, 1, d), lambda i: (sub, 0, 0)),
            pl.BlockSpec((None, d, w_in.shape[-1]), lambda i: (e, 0, 0)),
            tab, tab, tab,
        ],
        out_specs=[pl.BlockSpec((tm, nq), row), pl.BlockSpec((tm, nk), row), pl.BlockSpec((tm, nk), row),
                   pl.BlockSpec((tm, nw), row), pl.BlockSpec((tm, nw), row)],
        out_shape=[jax.ShapeDtypeStruct((n, nq), _F32), jax.ShapeDtypeStruct((n, nk), _F32),
                   jax.ShapeDtypeStruct((n, nk), _F32), jax.ShapeDtypeStruct((n, nw), _F32),
                   jax.ShapeDtypeStruct((n, nw), _F32)],
        compiler_params=_params("parallel"),
        name="even_in_proj",
    )(x, mod4, gpre, w_in, *rope_tabs)


def _odd_in_kernel(x_ref, mod_ref, gpre_ref, w_ref, q_ref, f_ref, v_ref, g_ref):
    h = _pre(x_ref[...], mod_ref[0], gpre_ref[...])
    y = _dot(h, w_ref[...])
    n = q_ref.shape[1]
    q_ref[...] = y[:, :n]
    f_ref[...] = y[:, n:2 * n]
    v_ref[...] = y[:, 2 * n:3 * n]
    g_ref[...] = y[:, 3 * n:]


def _odd_in_call(x, mod4, sub, o, gpre, w_in, tiles_per_seq):
    n, d = x.shape
    r = mod4.shape[2]
    tm = min(_ROW_TILE, n)
    nw = w_in.shape[-1] // 4
    row = lambda i: (i, 0)
    return pl.pallas_call(
        _odd_in_kernel,
        grid=(n // tm,),
        in_specs=[
            pl.BlockSpec((tm, d), row),
            pl.BlockSpec((None, 1, r, 3 * d), lambda i: (sub, i // tiles_per_seq, 0, 0)),
            pl.BlockSpec((None, 1, d), lambda i: (sub, 0, 0)),
            pl.BlockSpec((None, d, 4 * nw), lambda i: (o, 0, 0)),
        ],
        out_specs=[pl.BlockSpec((tm, nw), row)] * 4,
        out_shape=[jax.ShapeDtypeStruct((n, nw), _F32)] * 4,
        compiler_params=_params("parallel"),
        name="odd_in_proj",
    )(x, mod4, gpre, w_in)


def _out_kernel(x_ref, mod_ref, gpost_ref, w_ref, *refs):
    a_refs, o_ref = refs[:-1], refs[-1]
    y = None
    off = 0
    for a_ref in a_refs:
        k = a_ref.shape[1]
        t = _dot(a_ref[...], w_ref[off:off + k, :])
        y = t if y is None else y + t
        off += k
    o_ref[...] = _post(x_ref[...], y, mod_ref[0], gpost_ref[...], 1.0)


def _out_call(x, mod4, sub, widx, gpost, w_out, acts, tiles_per_seq):
    n, d = x.shape
    r = mod4.shape[2]
    tm = min(_ROW_TILE, n)
    row = lambda i: (i, 0)
    return pl.pallas_call(
        _out_kernel,
        grid=(n // tm,),
        in_specs=[
            pl.BlockSpec((tm, d), row),
            pl.BlockSpec((None, 1, r, 3 * d), lambda i: (sub, i // tiles_per_seq, 0, 0)),
            pl.BlockSpec((None, 1, d), lambda i: (sub, 0, 0)),
            pl.BlockSpec((None, w_out.shape[1], d), lambda i: (widx, 0, 0)),
        ] + [pl.BlockSpec((tm, a.shape[1]), row) for a in acts],
        out_specs=pl.BlockSpec((tm, d), row),
        out_shape=jax.ShapeDtypeStruct((n, d), _F32),
        compiler_params=_params("parallel"),
        name="mixer_out_proj",
    )(x, mod4, gpost, w_out, *acts)


def _seq_tile(time_tiles, s):
    row = lax.broadcasted_iota(jnp.int32, time_tiles[0].shape, 0)
    out = jnp.zeros(time_tiles[0].shape, time_tiles[0].dtype)
    for t, x in enumerate(time_tiles):
        shift = (t - s) % _SUBLANES
        out = jnp.where(row == t, pltpu.roll(x, shift, axis=0) if shift else x, out)
    return out


def _time_tiles(seq_tiles, n_t):
    row = lax.broadcasted_iota(jnp.int32, seq_tiles[0].shape, 0)
    outs = []
    for t in range(n_t):
        acc = jnp.zeros(seq_tiles[0].shape, seq_tiles[0].dtype)
        for s, x in enumerate(seq_tiles):
            shift = (s - t) % _SUBLANES
            acc = jnp.where(row == s, pltpu.roll(x, shift, axis=0) if shift else x, acc)
        outs.append(acc)
    return outs


def _sink_softmax_pv(s, mask, sink, v):
    s = jnp.where(mask, s, -jnp.inf)
    m = jnp.maximum(jnp.max(s, axis=-1, keepdims=True), sink)
    p = jnp.exp(s - m)
    denom = jnp.sum(p, axis=-1, keepdims=True) + jnp.exp(sink - m)
    return _dot(p, v) / denom


def _swa_prompt_kernel(sink_ref, q_ref, kp_ref, kc_ref, vp_ref, vc_ref, o_ref):
    w = _WINDOW
    hd = _A_HEAD_DIM
    assert _LANES == 2 * hd and _A_GROUP % 2 == 0
    log2e = np.float32(np.log2(np.e))
    scale = np.float32(1.0 / np.sqrt(hd)) * log2e
    low = lax.broadcasted_iota(jnp.int32, (2 * w, _LANES), 1) < hd
    ones_lo = jnp.where(low, 1.0, 0.0).astype(_BF16)
    ones_hi = jnp.where(low, 0.0, 1.0).astype(_BF16)
    low_q = lax.broadcasted_iota(jnp.int32, (w, _LANES), 1) < hd
    row = lax.broadcasted_iota(jnp.int32, (w, 4 * w), 0)
    col = lax.broadcasted_iota(jnp.int32, (w, 4 * w), 1) & (2 * w - 1)
    for qb in range(q_ref.shape[0] // w):
        rows = slice(qb * w, (qb + 1) * w)
        if qb == 0:
            k2 = jnp.concatenate([kp_ref[...], kc_ref[0:w, :]], axis=0)
            v2 = jnp.concatenate([vp_ref[...], vc_ref[0:w, :]], axis=0)
            first = jnp.where(pl.program_id(1) > 0, 0, w)
        else:
            k2 = kc_ref[(qb - 1) * w:(qb + 1) * w, :]
            v2 = vc_ref[(qb - 1) * w:(qb + 1) * w, :]
            first = 0
        mask = jnp.logical_and(col > jnp.maximum(row, first - 1), col <= row + w)
        keys, vals = [], []
        for j in range(_A_KV_HEADS):
            own_k = jnp.where(low, k2, 0.0) if j == 0 else jnp.where(low, 0.0, k2)
            own_v = jnp.where(low, v2, 0.0) if j == 0 else jnp.where(low, 0.0, v2)
            oth_k = pltpu.roll(own_k, hd, axis=1)
            oth_v = pltpu.roll(own_v, hd, axis=1)
            lo_k, hi_k = (own_k, oth_k) if j == 0 else (oth_k, own_k)
            lo_v, hi_v = (own_v, oth_v) if j == 0 else (oth_v, own_v)
            keys.append(jnp.concatenate([lo_k, hi_k], axis=0).astype(_BF16))
            vals.append(jnp.concatenate([jnp.concatenate([lo_v.astype(_BF16), ones_lo], axis=1),
                                         jnp.concatenate([hi_v.astype(_BF16), ones_hi], axis=1)], axis=0))
        pairs = range(_A_HEADS // 2)
        kv_of = [(2 * p) // _A_GROUP for p in pairs]
        scores = [_dot_nt(q_ref[rows, p * _LANES:(p + 1) * _LANES] * scale, keys[kv_of[p]]) for p in pairs]
        probs, sink_terms = [], []
        for p in pairs:
            s = jnp.where(mask, scores[p], -jnp.inf)
            halves = []
            for i in range(2):
                sh = s[:, i * 2 * w:(i + 1) * 2 * w]
                sink = sink_ref[2 * p + i] * log2e
                m = jnp.maximum(jnp.max(sh, axis=-1, keepdims=True), sink)
                halves.append((jnp.exp2(sh - m), jnp.exp2(sink - m)))
            probs.append(jnp.concatenate([halves[0][0], halves[1][0]], axis=1))
            sink_terms.append(jnp.where(low_q, halves[0][1], halves[1][1]))
        for p in pairs:
            r = _dot(probs[p], vals[kv_of[p]])
            o_ref[rows, p * _LANES:(p + 1) * _LANES] = r[:, :_LANES] / (r[:, _LANES:] + sink_terms[p])


def _swa_prompt_call(q, k, v, sinks, batch):
    n, nq = q.shape
    nk = k.shape[1]
    w = _WINDOW
    qb = _SWA_QBLOCKS
    nb = n // batch // (w * qb)
    cur = lambda b, i: (b * nb + i, 0)
    prev = lambda b, i: ((b * nb + i) * qb - jnp.minimum(i, 1), 0)
    return pl.pallas_call(
        _swa_prompt_kernel,
        grid=(batch, nb),
        in_specs=[
            pl.BlockSpec(memory_space=pltpu.SMEM),
            pl.BlockSpec((qb * w, nq), cur),
            pl.BlockSpec((w, nk), prev), pl.BlockSpec((qb * w, nk), cur),
            pl.BlockSpec((w, nk), prev), pl.BlockSpec((qb * w, nk), cur),
        ],
        out_specs=pl.BlockSpec((qb * w, nq), cur),
        out_shape=jax.ShapeDtypeStruct((n, nq), _F32),
        compiler_params=_params("parallel", "parallel"),
        name="swa_prompt",
    )(sinks, q, k, k, v, v)


def _swa_sample_kernel(sink_ref, q_ref, kn_ref, vn_ref, ck_ref, cv_ref, o_ref, kw_ref, vw_ref, *, t_new):
    p = ck_ref.shape[1]
    scale = np.float32(1.0 / np.sqrt(_A_HEAD_DIM))
    rows = _A_GROUP * _SUBLANES
    t = lax.broadcasted_iota(jnp.int32, (rows, p + _SUBLANES), 0) & (_SUBLANES - 1)
    c = lax.broadcasted_iota(jnp.int32, (rows, p + _SUBLANES), 1)
    mask = jnp.logical_and(c <= t + p, c > t + p - _WINDOW)
    g_of_row = lax.broadcasted_iota(jnp.int32, (rows, 1), 0) >> (_SUBLANES.bit_length() - 1)
    nseq = ck_ref.shape[0]
    q_t = [q_ref[t] for t in range(t_new)]
    kn_t = [kn_ref[t] for t in range(t_new)]
    vn_t = [vn_ref[t] for t in range(t_new)]
    sinks = []
    for j in range(_A_KV_HEADS):
        sink = jnp.zeros((rows, 1), _F32)
        for g in range(_A_GROUP):
            sink = jnp.where(g_of_row == g, sink_ref[j * _A_GROUP + g], sink)
        sinks.append(sink)
    new_k, new_v, qs, keys, vals = [], [], [], [], []
    for s in range(nseq):
        q8, kn8, vn8 = _seq_tile(q_t, s), _seq_tile(kn_t, s), _seq_tile(vn_t, s)
        ck, cv = ck_ref[s], cv_ref[s]
        kw_ref[s, 0:p - t_new, :] = ck[t_new:, :]
        kw_ref[s, p - t_new:p, :] = kn8[0:t_new, :]
        vw_ref[s, 0:p - t_new, :] = cv[t_new:, :]
        vw_ref[s, p - t_new:p, :] = vn8[0:t_new, :]
        for j in range(_A_KV_HEADS):
            ks = slice(j * _A_HEAD_DIM, (j + 1) * _A_HEAD_DIM)
            keys.append(jnp.concatenate([ck[:, ks], kn8[:, ks]], axis=0))
            vals.append(jnp.concatenate([cv[:, ks], vn8[:, ks]], axis=0))
            qs.append(jnp.concatenate(
                [q8[:, (j * _A_GROUP + g) * _A_HEAD_DIM:(j * _A_GROUP + g + 1) * _A_HEAD_DIM]
                 for g in range(_A_GROUP)], axis=0))
    scores = [_dot_nt(qj, k) * scale for qj, k in zip(qs, keys)]
    outs = [_sink_softmax_pv(sc, mask, sinks[i % _A_KV_HEADS], v) for i, (sc, v) in enumerate(zip(scores, vals))]
    per_seq = []
    for s in range(nseq):
        heads = []
        for j in range(_A_KV_HEADS):
            o = outs[s * _A_KV_HEADS + j]
            heads.extend(o[g * _SUBLANES:(g + 1) * _SUBLANES, :] for g in range(_A_GROUP))
        per_seq.append(jnp.concatenate(heads, axis=1))
    for t, tile in enumerate(_time_tiles(per_seq, t_new)):
        o_ref[t] = tile


def _swa_sample_call(q, k_new, v_new, cache_k, cache_v, sinks):
    t_new, nseq, nq = q.shape
    nk = k_new.shape[2]
    p = cache_k.shape[1]
    bs = _SAMPLE_SEQS
    toks = lambda i: (0, i, 0)
    seqs = lambda i: (i, 0, 0)
    return pl.pallas_call(
        functools.partial(_swa_sample_kernel, t_new=t_new),
        grid=(nseq // bs,),
        in_specs=[
            pl.BlockSpec(memory_space=pltpu.SMEM),
            pl.BlockSpec((t_new, bs, nq), toks),
            pl.BlockSpec((t_new, bs, nk), toks), pl.BlockSpec((t_new, bs, nk), toks),
            pl.BlockSpec((bs, p, nk), seqs), pl.BlockSpec((bs, p, nk), seqs),
        ],
        out_specs=[pl.BlockSpec((t_new, bs, nq), toks),
                   pl.BlockSpec((bs, p, nk), seqs), pl.BlockSpec((bs, p, nk), seqs)],
        out_shape=[jax.ShapeDtypeStruct((t_new, nseq, nq), _F32),
                   jax.ShapeDtypeStruct(cache_k.shape, _F32), jax.ShapeDtypeStruct(cache_v.shape, _F32)],
        compiler_params=_params("parallel"),
        name="swa_sample",
    )(sinks, q, k_new, v_new, cache_k, cache_v)


def _softplus(z):
    return jnp.maximum(z, 0.0) + jnp.log1p(jnp.exp(-jnp.abs(z)))


def _gelu_tanh(x):
    return 0.5 * x * (1.0 + jnp.tanh(np.float32(np.sqrt(2.0 / np.pi)) * (x + 0.044715 * (x * x * x))))


def _rg_gates(xc, wa_ref, ba_ref, wx_ref, bx_ref, sp_neg_lam):
    r = _sigmoid(_dot(xc, wa_ref[...]) + ba_ref[...])
    i = _sigmoid(_dot(xc, wx_ref[...]) + bx_ref[...])
    log_a = (-_RG_C) * r * sp_neg_lam
    a = jnp.exp(log_a)
    mult = jnp.sqrt(jnp.maximum(1.0 - jnp.exp(2.0 * log_a), 0.0))
    return a, mult * (i * xc)


def _rg_prompt_kernel(xr_ref, xg_ref, cw_ref, cb_ref, wa_ref, ba_ref, wx_ref, bx_ref, lam_ref,
                      o_ref, conv_ref, h_ref, xpad, a_s, b_s):
    t_len, w = xr_ref.shape
    cw = cw_ref.shape[0]
    xpad[0:_SUBLANES, :] = jnp.zeros((_SUBLANES, w), _F32)
    xpad[_SUBLANES:, :] = xr_ref[...]
    sp = _softplus(-lam_ref[...])
    for c in range(t_len // _RG_CHUNK):
        r0 = c * _RG_CHUNK
        xc = cb_ref[...]
        for j in range(cw):
            lo = _SUBLANES + r0 - (cw - 1) + j
            xc = xc + xpad[lo:lo + _RG_CHUNK, :] * cw_ref[j:j + 1, :]
        a, b = _rg_gates(xc, wa_ref, ba_ref, wx_ref, bx_ref, sp)
        a_s[r0:r0 + _RG_CHUNK, :] = a
        b_s[r0:r0 + _RG_CHUNK, :] = b

    row = lax.broadcasted_iota(jnp.int32, (_SUBLANES, w), 0)

    def group(g, h):
        r0 = pl.multiple_of(g * _SUBLANES, _SUBLANES)
        a = a_s[pl.ds(r0, _SUBLANES), :]
        b = b_s[pl.ds(r0, _SUBLANES), :]
        sh = 1
        while sh < _SUBLANES:
            a_prev = jnp.where(row >= sh, pltpu.roll(a, sh, axis=0), 1.0)
            b_prev = jnp.where(row >= sh, pltpu.roll(b, sh, axis=0), 0.0)
            b = a * b_prev + b
            a = a * a_prev
            sh *= 2
        hs = a * h + b
        o_ref[pl.ds(r0, _SUBLANES), :] = _gelu_tanh(xg_ref[pl.ds(r0, _SUBLANES), :]) * hs
        return hs[_SUBLANES - 1:_SUBLANES, :]

    h_last = lax.fori_loop(0, t_len // _SUBLANES, group, jnp.zeros((1, w), _F32), unroll=2)
    h_ref[0] = h_last
    conv_ref[0] = xr_ref[t_len - (cw - 1):t_len, :]


def _rg_prompt_call(xr, xg, conv_w, conv_b, wa_bd, ba, wx_bd, bx, lam, batch):
    n, w = xr.shape
    t_len = n // batch
    cw = conv_w.shape[0]
    seq = lambda b: (b, 0)
    const = lambda b: (0, 0)
    vec = pl.BlockSpec((1, w), const)
    mat = pl.BlockSpec((w, w), const)
    return pl.pallas_call(
        _rg_prompt_kernel,
        grid=(batch,),
        in_specs=[pl.BlockSpec((t_len, w), seq), pl.BlockSpec((t_len, w), seq),
                  pl.BlockSpec((cw, w), const), vec, mat, vec, mat, vec, vec],
        out_specs=[pl.BlockSpec((t_len, w), seq),
                   pl.BlockSpec((1, cw - 1, w), lambda b: (b, 0, 0)),
                   pl.BlockSpec((1, 1, w), lambda b: (b, 0, 0))],
        out_shape=[jax.ShapeDtypeStruct((n, w), _F32),
                   jax.ShapeDtypeStruct((batch, cw - 1, w), _F32),
                   jax.ShapeDtypeStruct((batch, 1, w), _F32)],
        scratch_shapes=[pltpu.VMEM((t_len + _SUBLANES, w), _F32),
                        pltpu.VMEM((t_len, w), _F32), pltpu.VMEM((t_len, w), _F32)],
        compiler_params=_params("parallel"),
        name="rglru_prompt",
    )(xr, xg, conv_w, conv_b, wa_bd, ba, wx_bd, bx, lam)


def _rg_sample_kernel(xr_ref, xg_ref, conv0_ref, h0_ref, cw_ref, cb_ref, wa_ref, ba_ref, wx_ref, bx_ref, lam_ref,
                      o_ref, conv_ref, h_ref):
    cw = cw_ref.shape[0]
    t_new = xr_ref.shape[0]
    sp = _softplus(-lam_ref[...])
    hist = [conv0_ref[j] for j in range(cw - 1)] + [xr_ref[t] for t in range(t_new)]
    h = h0_ref[...]
    for t in range(t_new):
        xc = cb_ref[...]
        for j in range(cw):
            xc = xc + hist[t + j] * cw_ref[j:j + 1, :]
        a, b = _rg_gates(xc, wa_ref, ba_ref, wx_ref, bx_ref, sp)
        h = a * h + b
        o_ref[t] = _gelu_tanh(xg_ref[t]) * h
    h_ref[...] = h
    for j in range(cw - 1):
        conv_ref[j] = hist[t_new + j]


def _rg_sample_call(xr_t, xg_t, conv0_t, h0, conv_w, conv_b, wa_bd, ba, wx_bd, bx, lam):
    t_new, nseq, w = xg_t.shape
    return pl.pallas_call(
        _rg_sample_kernel,
        out_shape=[jax.ShapeDtypeStruct((t_new, nseq, w), _F32), jax.ShapeDtypeStruct(conv0_t.shape, _F32),
                   jax.ShapeDtypeStruct((nseq, w), _F32)],
        compiler_params=pltpu.CompilerParams(vmem_limit_bytes=_VMEM_LIMIT),
        name="rglru_sample",
    )(xr_t, xg_t, conv0_t, h0, conv_w, conv_b, wa_bd, ba, wx_bd, bx, lam)


def _row_bcast(x, r, n):
    return jnp.broadcast_to(x[r:r + 1, :], (n, x.shape[1]))


def _chunk_cumsum(x):
    n_tiles = x.shape[0] // _SUBLANES
    row = lax.broadcasted_iota(jnp.int32, (_SUBLANES, x.shape[1]), 0)
    tiles = []
    carry = None
    for i in range(n_tiles):
        t = x[i * _SUBLANES:(i + 1) * _SUBLANES, :]
        sh = 1
        while sh < _SUBLANES:
            t = t + jnp.where(row >= sh, pltpu.roll(t, sh, axis=0), 0.0)
            sh *= 2
        if carry is not None:
            t = t + carry
        carry = _row_bcast(t, _SUBLANES - 1, _SUBLANES)
        tiles.append(t)
    return tiles[0] if n_tiles == 1 else jnp.concatenate(tiles, axis=0)


def _level_reference(b, m):
    n = b.shape[0]
    if 2 * m >= _SUBLANES:
        pieces = [_row_bcast(b, lo + m - 1, 2 * m) for lo in range(0, n, 2 * m)]
        return pieces[0] if len(pieces) == 1 else jnp.concatenate(pieces, axis=0)
    row = lax.broadcasted_iota(jnp.int32, (_SUBLANES, b.shape[1]), 0)
    tiles = []
    for i in range(n // _SUBLANES):
        t = b[i * _SUBLANES:(i + 1) * _SUBLANES, :]
        ref = None
        for lo in range(0, _SUBLANES, 2 * m):
            piece = _row_bcast(t, lo + m - 1, _SUBLANES)
            ref = piece if ref is None else jnp.where(row >= lo, piece, ref)
        tiles.append(ref)
    return tiles[0] if len(tiles) == 1 else jnp.concatenate(tiles, axis=0)


def _hgrn_gates(fz, lb):
    f = lb + (1.0 - lb) * _sigmoid(fz)
    return jnp.log(f), 1.0 - f


def _hgrn_chunk(q, fz, v, lb, state, n_valid, state_is_vk):
    n, kd = q.shape
    log_f, k = _hgrn_gates(fz, lb)
    if n_valid < n:
        valid = lax.broadcasted_iota(jnp.int32, (n, kd), 0) < n_valid
        log_f = jnp.where(valid, log_f, 0.0)
        k = jnp.where(valid, k, 0.0)
    b = _chunk_cumsum(log_f)
    b_last = _row_bcast(b, n - 1, n)

    q_in = q * jnp.exp(b)
    k_end = k * jnp.exp(b_last - b)
    if state_is_vk:
        o = _dot_nt(q_in, state)
        new_state = jnp.exp(b_last[0:1, :]) * state + _dot_tn(v, k_end)
    else:
        o = _dot(q_in, state)
        decay = jnp.exp(jnp.broadcast_to(b_last[0:1, :], (v.shape[1], kd))).T
        new_state = decay * state + _dot_tn(k_end, v)

    row = lax.broadcasted_iota(jnp.int32, (n, n), 0)
    col = lax.broadcasted_iota(jnp.int32, (n, n), 1)
    upper = lax.broadcasted_iota(jnp.int32, (n, kd), 0)
    scores = jnp.where(row == col, jnp.sum(q * k, axis=-1, keepdims=True), 0.0)
    m = 1
    while m < n_valid:
        e = jnp.exp(-jnp.abs(b - _level_reference(b, m)))
        z = jnp.where((upper & m) != 0, q, k) * e
        pair = jnp.logical_and((row & m) != 0, (row ^ m) >> (m.bit_length() - 1) == col >> (m.bit_length() - 1))
        scores = scores + jnp.where(pair, _dot_nt(z, z), 0.0)
        m *= 2
    return o + _dot(scores, v), new_state


def _lower_bound(logits, layer):
    m = jnp.max(logits, axis=0, keepdims=True)
    e = jnp.exp(logits - m)
    return jnp.sum(e[1:layer + 1, :], axis=0, keepdims=True) / jnp.sum(e, axis=0, keepdims=True)


def _hgrn_out(o, g, gnorm):
    return _rms(o, gnorm) * _silu(g)


def _hgrn_prompt_kernel(q_ref, f_ref, v_ref, g_ref, lbl_ref, gn_ref, o_ref, s_ref,
                        st_scr, qs_scr, ks_scr, ke_scr, vb_scr, dec_scr, inc_scr, *, layer):
    rows_blk, width = q_ref.shape
    kd = width // _HG_HEADS
    n_chunks = rows_blk // _HG_CHUNK
    lb = _lower_bound(lbl_ref[...], layer)
    gn = gn_ref[...]

    @pl.when(pl.program_id(2) == 0)
    def _():
        st_scr[...] = jnp.zeros(st_scr.shape, _F32)

    def chunk_rows(c):
        return pl.ds(pl.multiple_of(c * _HG_CHUNK, _HG_CHUNK), _HG_CHUNK)

    def prepare(c, min_b):
        rows = chunk_rows(c)
        log_f, k = _hgrn_gates(f_ref[rows, :], lb)
        b = _chunk_cumsum(log_f)
        b_end = b[_HG_CHUNK - 1:_HG_CHUNK, :]
        dec = jnp.exp(b_end)
        k_start = k * jnp.exp(-b)
        qs_scr[rows, :] = (q_ref[rows, :] * jnp.exp(b)).astype(_BF16)
        ks_scr[rows, :] = k_start.astype(_BF16)
        ke_scr[rows, :] = (k_start * dec).astype(_BF16)
        vb_scr[rows, :] = v_ref[rows, :].astype(_BF16)
        dec_scr[c] = jnp.broadcast_to(dec, (_SUBLANES, width))
        return jnp.minimum(min_b, b_end)

    min_b = lax.fori_loop(0, n_chunks, prepare, jnp.zeros((1, width), _F32))
    single_split_ok = jnp.min(min_b) > -_HG_SAFE_LOG_DECAY

    row = lax.broadcasted_iota(jnp.int32, (_HG_CHUNK, _HG_CHUNK), 0)
    col = lax.broadcasted_iota(jnp.int32, (_HG_CHUNK, _HG_CHUNK), 1)

    def finish(c, h, o):
        rows, lanes = chunk_rows(c), slice(h * kd, (h + 1) * kd)
        o_ref[rows, lanes] = _hgrn_out(o, g_ref[rows, lanes], gn)

    @pl.when(single_split_ok)
    def _():
        heads = [slice(h * kd, (h + 1) * kd) for h in range(_HG_HEADS)]

        def in_chunk(c, carry):
            rows = chunk_rows(c)
            scores = [_dot_nt(qs_scr[rows, l], ks_scr[rows, l]) for l in heads]
            incs = [_dot_tn(vb_scr[rows, l], ke_scr[rows, l]) for l in heads]
            for h in range(_HG_HEADS):
                inc_scr[c, h] = incs[h]
            outs = [_dot(jnp.where(row >= col, s, 0.0), vb_scr[rows, l]) for s, l in zip(scores, heads)]
            for o, l in zip(outs, heads):
                o_ref[rows, l] = o
            return carry
        lax.fori_loop(0, n_chunks, in_chunk, 0, unroll=2)

        def cross_chunk(c, carry):
            rows = chunk_rows(c)
            states = [st_scr[h] for h in range(_HG_HEADS)]
            carried = [_dot_nt(qs_scr[rows, l], st) for l, st in zip(heads, states)]
            for h, l in enumerate(heads):
                st_scr[h] = states[h] * dec_scr[c, 0:1, l] + inc_scr[c, h]
            for h, l in enumerate(heads):
                finish(c, h, o_ref[rows, l] + carried[h])
            return carry
        lax.fori_loop(0, n_chunks, cross_chunk, 0, unroll=2)

    @pl.when(jnp.logical_not(single_split_ok))
    def _():
        def chunk(c, carry):
            rows = chunk_rows(c)
            for h in range(_HG_HEADS):
                lanes = slice(h * kd, (h + 1) * kd)
                o, st_scr[h] = _hgrn_chunk(q_ref[rows, lanes], f_ref[rows, lanes], v_ref[rows, lanes],
                                           lb[:, lanes], st_scr[h], _HG_CHUNK, True)
                finish(c, h, o)
            return carry
        lax.fori_loop(0, n_chunks, chunk, 0)

    @pl.when(pl.program_id(2) == pl.num_programs(2) - 1)
    def _():
        for h in range(_HG_HEADS):
            s_ref[0, h] = st_scr[h].T


def _hgrn_prompt_call(q, fz, v, g, lb_logits, gnorm, batch, layer):
    n, width = q.shape
    t_len = n // batch
    kd = width // _C_HEADS
    gw = _HG_HEADS * kd
    rows_blk = min(_HG_ROWS, t_len)
    nt = t_len // rows_blk
    blk = pl.BlockSpec((rows_blk, gw), lambda b, h, t: (b * nt + t, h))
    return pl.pallas_call(
        functools.partial(_hgrn_prompt_kernel, layer=layer),
        grid=(batch, _C_HEADS // _HG_HEADS, nt),
        in_specs=[blk, blk, blk, blk,
                  pl.BlockSpec((lb_logits.shape[0], gw), lambda b, h, t: (0, h)),
                  pl.BlockSpec((1, kd), lambda b, h, t: (0, 0))],
        out_specs=[blk, pl.BlockSpec((1, _HG_HEADS, kd, kd), lambda b, h, t: (b, h, 0, 0))],
        out_shape=[jax.ShapeDtypeStruct((n, width), _F32),
                   jax.ShapeDtypeStruct((batch, _C_HEADS, kd, kd), _F32)],
        scratch_shapes=[pltpu.VMEM((_HG_HEADS, kd, kd), _F32)]
        + [pltpu.VMEM((rows_blk, gw), _BF16)] * 4
        + [pltpu.VMEM((rows_blk // _HG_CHUNK, _SUBLANES, gw), _F32),
           pltpu.VMEM((rows_blk // _HG_CHUNK, _HG_HEADS, kd, kd), _F32)],
        compiler_params=_params("parallel", "parallel", "arbitrary"),
        name="hgrn2_prompt",
    )(q, fz, v, g, lb_logits, gnorm)


def _hgrn_sample_kernel(q_ref, f_ref, v_ref, g_ref, lbl_ref, gn_ref, s0_ref, o_ref, s_ref, *, layer):
    t_new, nseq, width = q_ref.shape
    n_heads = s0_ref.shape[1]
    kd = width // n_heads
    lanes = [slice(h * kd, (h + 1) * kd) for h in range(n_heads)]
    lb = _lower_bound(lbl_ref[...], layer)
    gn = gn_ref[...]
    q = [q_ref[t] for t in range(t_new)]
    v = [v_ref[t] for t in range(t_new)]
    keys, b = [], []
    for t in range(t_new):
        log_f, k = _hgrn_gates(f_ref[t], lb)
        keys.append(k)
        b.append(log_f if t == 0 else b[-1] + log_f)

    def per_head_sum(w):
        return jnp.concatenate([jnp.broadcast_to(jnp.sum(w[:, l], axis=-1, keepdims=True), (nseq, kd))
                                for l in lanes], axis=1)

    within = []
    for t in range(t_new):
        acc = per_head_sum(q[t] * keys[t]) * v[t]
        for s in range(t):
            acc = acc + per_head_sum(q[t] * keys[s] * jnp.exp(b[t] - b[s])) * v[s]
        within.append(acc)

    q_in = [q[t] * jnp.exp(b[t]) for t in range(t_new)]
    k_end = [keys[t] * jnp.exp(b[-1] - b[t]) for t in range(t_new)]
    decay = jnp.exp(b[-1])
    pairs = [(s, h) for h in range(n_heads) for s in range(nseq)]
    lhs = [_seq_tile([x[:, lanes[h]] for x in q_in], s) for s, h in pairs]
    k_seq = [_seq_tile([x[:, lanes[h]] for x in k_end], s) for s, h in pairs]
    v_seq = [_seq_tile([x[:, lanes[h]] for x in v], s) for s, h in pairs]
    carried = [_dot(x, s0_ref[s, h]) for x, (s, h) in zip(lhs, pairs)]
    incs = [_dot_tn(ks, vs) for ks, vs in zip(k_seq, v_seq)]
    carried_t = []
    for h in range(n_heads):
        dec_cols = jnp.concatenate([decay[:, lanes[h]], jnp.zeros((kd - nseq, kd), _F32)], axis=0).T
        for s in range(nseq):
            i = h * nseq + s
            s_ref[s, h] = jnp.broadcast_to(dec_cols[:, s:s + 1], (kd, kd)) * s0_ref[s, h] + incs[i]
        carried_t.append(_time_tiles(carried[h * nseq:(h + 1) * nseq], t_new))
    for t in range(t_new):
        o = within[t] + jnp.concatenate([carried_t[h][t] for h in range(n_heads)], axis=1)
        g = g_ref[t]
        o_ref[t] = jnp.concatenate([_hgrn_out(o[:, l], g[:, l], gn) for l in lanes], axis=1)


def _hgrn_sample_call(q, fz, v, g, lb_logits, gnorm, s0, layer):
    t_new, nseq, width = q.shape
    kd = width // _C_HEADS
    gw = _HG_HEADS * kd
    bs = _SAMPLE_SEQS
    blk = pl.BlockSpec((t_new, bs, gw), lambda i, h: (0, i, h))
    st = pl.BlockSpec((bs, _HG_HEADS, kd, kd), lambda i, h: (i, h, 0, 0))
    return pl.pallas_call(
        functools.partial(_hgrn_sample_kernel, layer=layer),
        grid=(nseq // bs, _C_HEADS // _HG_HEADS),
        in_specs=[blk, blk, blk, blk,
                  pl.BlockSpec((lb_logits.shape[0], gw), lambda i, h: (0, h)),
                  pl.BlockSpec((1, kd), lambda i, h: (0, 0)), st],
        out_specs=[blk, st],
        out_shape=[jax.ShapeDtypeStruct((t_new, nseq, width), _F32), jax.ShapeDtypeStruct(s0.shape, _F32)],
        compiler_params=_params("parallel", "parallel"),
        name="hgrn2_sample",
    )(q, fz, v, g, lb_logits, gnorm, s0)


def _rope_tables(pos):
    half = _ROT_DIM // 2
    inv_freq = _ROPE_THETA ** (-jnp.arange(0, _ROT_DIM, 2, dtype=_F32) / _ROT_DIM)
    ang = pos.astype(_F32)[:, None] * inv_freq[None, :]
    cos, sin = jnp.cos(ang), jnp.sin(ang)
    ones = jnp.ones((pos.shape[0], _A_HEAD_DIM - _ROT_DIM), _F32)
    zeros = jnp.zeros((pos.shape[0], _A_HEAD_DIM - half), _F32)
    zeros_h = jnp.zeros((pos.shape[0], half), _F32)
    reps = _LANES // _A_HEAD_DIM
    cos_t = jnp.tile(jnp.concatenate([cos, cos, ones], axis=1), (1, reps))
    sin_lo = jnp.tile(jnp.concatenate([-sin, zeros], axis=1), (1, reps))
    sin_hi = jnp.tile(jnp.concatenate([zeros_h, sin, ones * 0.0], axis=1), (1, reps))
    return cos_t, sin_lo, sin_hi


def _block_diag(w):
    nb, bd, _ = w.shape
    eye = jnp.eye(nb, dtype=w.dtype)
    return (w[:, :, None, :] * eye[:, None, :, None]).reshape(nb * bd, nb * bd)


def kernel(x_prompt, x_sample, c_prompt, c_sample, cache_k_win, cache_v_win, state_conv_rglru,
           state_h_rglru, state_s_hgrn, norm_pre, norm_post, ada_w, ada_b, ffn1_w_in, ffn1_w_out,
           ffn2_w_in, ffn2_w_out, even_w_in, even_w_out, attn_sinks, rg_conv_w, rg_conv_b, rg_wa,
           rg_ba, rg_wx, rg_bx, rg_lambda, odd_w_in, odd_w_out, hgrn_lb_logits, hgrn_gnorm):
    bp, tp, d = x_prompt.shape
    bs, ts, _ = x_sample.shape
    depth = norm_pre.shape[0]
    n_sub = depth * _N_SUB
    nk = _A_KV_HEADS * _A_HEAD_DIM
    win = cache_k_win.shape[2]
    cw = rg_conv_w.shape[1]
    bw = rg_conv_w.shape[2]
    kd = state_s_hgrn.shape[3]

    cast = lambda w: w.astype(_BF16)
    ffn1_in, ffn1_out, ffn2_in, ffn2_out = cast(ffn1_w_in), cast(ffn1_w_out), cast(ffn2_w_in), cast(ffn2_w_out)
    ev_in, ev_out, od_in, od_out = cast(even_w_in), cast(even_w_out), cast(odd_w_in), cast(odd_w_out)
    gpre = norm_pre.reshape(n_sub, 1, d)
    gpost = norm_post.reshape(n_sub, 1, d)

    mod = _ada_call(jnp.concatenate([c_prompt, c_sample], axis=0),
                    ada_w.reshape(n_sub, d, 3 * d), ada_b.reshape(n_sub, 1, 3 * d))
    mod_p = mod[:, :bp].reshape(n_sub, bp, 1, 3 * d)
    mod_s = mod[:, bp:].reshape(n_sub, 1, bs, 3 * d)

    tabs_p = _rope_tables(jnp.arange(tp, dtype=jnp.int32))
    tabs_s = tuple(jnp.repeat(t, bs, axis=0) for t in _rope_tables(_PAST_LEN + jnp.arange(ts, dtype=jnp.int32)))

    time_major = lambda a: a.reshape(ts, bs, a.shape[-1])
    groups = {
        "p": dict(x=x_prompt.reshape(bp * tp, d), mod=mod_p, tps=tp // _ROW_TILE, tabs=tabs_p,
                  tab_tiles=tp // _ROW_TILE),
        "s": dict(x=x_sample.transpose(1, 0, 2).reshape(ts * bs, d), mod=mod_s, tps=1, tabs=tabs_s, tab_tiles=1),
    }
    outs = {g: dict(k=[], v=[], conv=[], h=[], s=[]) for g in groups}

    for l in range(depth):
        s0, s1, s2 = l * _N_SUB, l * _N_SUB + 1, l * _N_SUB + 2
        for name, grp in groups.items():
            x, mod4, tps = grp["x"], grp["mod"], grp["tps"]
            x = _ffn_call(x, mod4, s0, l, gpre, ffn1_in, ffn1_out, gpost, 0.5, tps)
            if l % 2 == 0:
                e = l // 2
                q, k, v, xg, xr = _even_in_call(x, mod4, s1, e, gpre, ev_in, grp["tabs"], tps, grp["tab_tiles"])
                wa_bd, wx_bd = cast(_block_diag(rg_wa[e])), cast(_block_diag(rg_wx[e]))
                vecs = [a[e].reshape(1, bw) for a in (rg_conv_b, rg_ba, rg_bx, rg_lambda)]
                if name == "p":
                    o_a = _swa_prompt_call(q, k, v, attn_sinks[e], bp)
                    o_b, conv, h_last = _rg_prompt_call(xr, xg, rg_conv_w[e], vecs[0], wa_bd, vecs[1], wx_bd,
                                                        vecs[2], vecs[3], bp)
                    outs[name]["k"].append(k.reshape(bp, tp, _A_KV_HEADS, _A_HEAD_DIM)[:, tp - win:])
                    outs[name]["v"].append(v.reshape(bp, tp, _A_KV_HEADS, _A_HEAD_DIM)[:, tp - win:])
                    outs[name]["h"].append(h_last.reshape(bp, bw))
                else:
                    o_a, kw, vw = _swa_sample_call(time_major(q), time_major(k), time_major(v),
                                                   cache_k_win[e].reshape(bs, win, nk),
                                                   cache_v_win[e].reshape(bs, win, nk), attn_sinks[e])
                    o_b, conv_t, h_last = _rg_sample_call(time_major(xr), time_major(xg),
                                                          state_conv_rglru[e].transpose(1, 0, 2), state_h_rglru[e],
                                                          rg_conv_w[e], vecs[0], wa_bd, vecs[1], wx_bd, vecs[2], vecs[3])
                    o_a, o_b = o_a.reshape(ts * bs, -1), o_b.reshape(ts * bs, bw)
                    conv = conv_t.transpose(1, 0, 2)
                    outs[name]["k"].append(kw.reshape(bs, win, _A_KV_HEADS, _A_HEAD_DIM))
                    outs[name]["v"].append(vw.reshape(bs, win, _A_KV_HEADS, _A_HEAD_DIM))
                    outs[name]["h"].append(h_last)
                outs[name]["conv"].append(conv)
                x = _out_call(x, mod4, s1, e, gpost, ev_out, [o_a, o_b], tps)
            else:
                o = l // 2
                q, fz, v, g = _odd_in_call(x, mod4, s1, o, gpre, od_in, tps)
                gn = hgrn_gnorm[o].reshape(1, kd)
                if name == "p":
                    y, s_last = _hgrn_prompt_call(q, fz, v, g, hgrn_lb_logits, gn, bp, l)
                else:
                    y, s_last = _hgrn_sample_call(time_major(q), time_major(fz), time_major(v), time_major(g),
                                                  hgrn_lb_logits, gn, state_s_hgrn[o], l)
                    y = y.reshape(ts * bs, -1)
                outs[name]["s"].append(s_last)
                x = _out_call(x, mod4, s1, o, gpost, od_out, [y], tps)
            x = _ffn_call(x, mod4, s2, l, gpre, ffn2_in, ffn2_out, gpost, 0.5, tps)
            grp["x"] = x

    ys = {"p": groups["p"]["x"].reshape(bp, tp, d), "s": groups["s"]["x"].reshape(ts, bs, d).transpose(1, 0, 2)}
    res = []
    for name in ("p", "s"):
        o = outs[name]
        res.append((jnp.stack(o["k"]), jnp.stack(o["v"]), jnp.stack(o["conv"]), jnp.stack(o["h"]), jnp.stack(o["s"])))
    return (ys["p"], ys["s"]) + res[0] + res[1]
```

```python
import functools

import jax
import jax.numpy as jnp
import numpy as np
from jax import lax
from jax.experimental import pallas as pl
from jax.experimental.pallas import tpu as pltpu

_F32 = jnp.float32
_BF16 = jnp.bfloat16

_EPS = 1e-6
_A_HEADS = 8
_A_KV_HEADS = 2
_A_HEAD_DIM = 64
_A_GROUP = _A_HEADS // _A_KV_HEADS
_WINDOW = 128
_ROPE_THETA = 500000.0
_ROT_DIM = _A_HEAD_DIM // 4
_RG_C = 8.0
_C_HEADS = 8
_PAST_LEN = 16384
_N_SUB = 3

_LANES = 128
_SUBLANES = 8
_VMEM_LIMIT = 48 * 1024 * 1024

_ROW_TILE = 512
_FF_CHUNK = 256
_RG_CHUNK = 256
_HG_CHUNK = 64
_HG_HEADS = 4
_HG_ROWS = 1024
_HG_SAFE_LOG_DECAY = 80.0
_SAMPLE_SEQS = 8
_SWA_QBLOCKS = 4


def _dot(a, b):
    return jnp.dot(a.astype(_BF16), b.astype(_BF16), preferred_element_type=_F32)


def _dot_nt(a, b):
    return lax.dot_general(a.astype(_BF16), b.astype(_BF16), (((1,), (1,)), ((), ())),
                           preferred_element_type=_F32)


def _dot_tn(a, b):
    return lax.dot_general(a.astype(_BF16), b.astype(_BF16), (((0,), (0,)), ((), ())),
                           preferred_element_type=_F32)


def _sigmoid(x):
    return 1.0 / (1.0 + jnp.exp(-x))


def _silu(x):
    return x * _sigmoid(x)


def _rms(x, gain):
    inv = lax.rsqrt(jnp.mean(x * x, axis=-1, keepdims=True) + _EPS)
    return x * inv * gain


def _per_seq(a, r):
    n = a.shape[0]
    return a if r in (1, n) else a.reshape(n // r, r, a.shape[1])


def _pre(x, mod, gain):
    n, d = x.shape
    h = _rms(_per_seq(x, mod.shape[0]), gain) * (1.0 + mod[:, d:2 * d]) + mod[:, :d]
    return h.reshape(n, d)


def _post(x, y, mod, gain, res_w):
    n, d = x.shape
    r = mod.shape[0]
    out = _per_seq(x, r) + (res_w * (1.0 + mod[:, 2 * d:])) * _rms(_per_seq(y, r), gain)
    return out.reshape(n, d)


def _params(*sem):
    return pltpu.CompilerParams(dimension_semantics=sem, vmem_limit_bytes=_VMEM_LIMIT)


def _ada_kernel(c_ref, w_ref, b_ref, o_ref):
    o_ref[...] = _dot(_silu(c_ref[...]), w_ref[...]) + b_ref[...]


def _ada_call(c_all, ada_w, ada_b):
    m, d = c_all.shape
    n_sub = ada_w.shape[0]
    n = ada_w.shape[-1]
    tn = n // 2
    return pl.pallas_call(
        _ada_kernel,
        grid=(n_sub, n // tn),
        in_specs=[
            pl.BlockSpec((m, d), lambda s, j: (0, 0)),
            pl.BlockSpec((None, d, tn), lambda s, j: (s, 0, j)),
            pl.BlockSpec((None, 1, tn), lambda s, j: (s, 0, j)),
        ],
        out_specs=pl.BlockSpec((None, m, tn), lambda s, j: (s, 0, j)),
        out_shape=jax.ShapeDtypeStruct((n_sub, m, n), _F32),
        compiler_params=_params("parallel", "parallel"),
        name="ada_mod",
    )(c_all, ada_w, ada_b)


def _ffn_kernel(x_ref, mod_ref, gpre_ref, win_ref, wout_ref, gpost_ref, o_ref, *, res_w):
    x = x_ref[...]
    mod = mod_ref[0]
    dff = wout_ref.shape[0]
    h = _pre(x, mod, gpre_ref[...]).astype(_BF16)
    acc = jnp.zeros(x.shape, _F32)
    for j in range(dff // _FF_CHUNK):
        lo = j * _FF_CHUNK
        g = jnp.dot(h, win_ref[:, lo:lo + _FF_CHUNK], preferred_element_type=_F32)
        u = jnp.dot(h, win_ref[:, dff + lo:dff + lo + _FF_CHUNK], preferred_element_type=_F32)
        acc = acc + _dot(_silu(g) * u, wout_ref[lo:lo + _FF_CHUNK, :])
    o_ref[...] = _post(x, acc, mod, gpost_ref[...], res_w)


def _ffn_call(x, mod4, sub, layer, gpre, w_in, w_out, gpost, res_w, tiles_per_seq):
    n, d = x.shape
    dff = w_out.shape[1]
    r = mod4.shape[2]
    tm = min(_ROW_TILE, n)
    return pl.pallas_call(
        functools.partial(_ffn_kernel, res_w=res_w),
        grid=(n // tm,),
        in_specs=[
            pl.BlockSpec((tm, d), lambda i: (i, 0)),
            pl.BlockSpec((None, 1, r, 3 * d), lambda i: (sub, i // tiles_per_seq, 0, 0)),
            pl.BlockSpec((None, 1, d), lambda i: (sub, 0, 0)),
            pl.BlockSpec((None, d, 2 * dff), lambda i: (layer, 0, 0), pipeline_mode=pl.Buffered(1)),
            pl.BlockSpec((None, dff, d), lambda i: (layer, 0, 0), pipeline_mode=pl.Buffered(1)),
            pl.BlockSpec((None, 1, d), lambda i: (sub, 0, 0)),
        ],
        out_specs=pl.BlockSpec((tm, d), lambda i: (i, 0)),
        out_shape=jax.ShapeDtypeStruct((n, d), _F32),
        compiler_params=_params("parallel"),
        name="ffn_sublayer",
    )(x, mod4, gpre, w_in, w_out, gpost)


def _rope(x, cos, sin_lo, sin_hi):
    outs = []
    for j in range(x.shape[1] // _LANES):
        xc = x[:, j * _LANES:(j + 1) * _LANES]
        nxt = pltpu.roll(xc, _LANES - _ROT_DIM // 2, axis=1)
        prv = pltpu.roll(xc, _ROT_DIM // 2, axis=1)
        outs.append(xc * cos + nxt * sin_lo + prv * sin_hi)
    return outs[0] if len(outs) == 1 else jnp.concatenate(outs, axis=1)


def _even_in_kernel(x_ref, mod_ref, gpre_ref, w_ref, cos_ref, slo_ref, shi_ref,
                    q_ref, k_ref, v_ref, xg_ref, xr_ref):
    h = _pre(x_ref[...], mod_ref[0], gpre_ref[...])
    y = _dot(h, w_ref[...])
    nq, nk, nw = q_ref.shape[1], k_ref.shape[1], xg_ref.shape[1]
    cos, slo, shi = cos_ref[...], slo_ref[...], shi_ref[...]
    q_ref[...] = _rope(y[:, :nq], cos, slo, shi)
    k_ref[...] = _rope(y[:, nq:nq + nk], cos, slo, shi)
    v_ref[...] = y[:, nq + nk:nq + 2 * nk]
    xg_ref[...] = y[:, nq + 2 * nk:nq + 2 * nk + nw]
    xr_ref[...] = y[:, nq + 2 * nk + nw:]


def _even_in_call(x, mod4, sub, e, gpre, w_in, rope_tabs, tiles_per_seq, tab_tiles):
    n, d = x.shape
    r = mod4.shape[2]
    tm = min(_ROW_TILE, n)
    nq = _A_HEADS * _A_HEAD_DIM
    nk = _A_KV_HEADS * _A_HEAD_DIM
    nw = (w_in.shape[-1] - nq - 2 * nk) // 2
    row = lambda i: (i, 0)
    tab = pl.BlockSpec((tm, _LANES), lambda i: (i % tab_tiles, 0))
    return pl.pallas_call(
        _even_in_kernel,
        grid=(n // tm,),
        in_specs=[
            pl.BlockSpec((tm, d), row),
            pl.BlockSpec((None, 1, r, 3 * d), lambda i: (sub, i // tiles_per_seq, 0, 0)),
            pl.BlockSpec((None, 1, d), lambda i: (sub, 0, 0)),
            pl.BlockSpec((None, d, w_in.shape[-1]), lambda i: (e, 0, 0)),
            tab, tab, tab,
        ],
        out_specs=[pl.BlockSpec((tm, nq), row), pl.BlockSpec((tm, nk), row), pl.BlockSpec((tm, nk), row),
                   pl.BlockSpec((tm, nw), row), pl.BlockSpec((tm, nw), row)],
        out_shape=[jax.ShapeDtypeStruct((n, nq), _F32), jax.ShapeDtypeStruct((n, nk), _F32),
                   jax.ShapeDtypeStruct((n, nk), _F32), jax.ShapeDtypeStruct((n, nw), _F32),
                   jax.ShapeDtypeStruct((n, nw), _F32)],
        compiler_params=_params("parallel"),
        name="even_in_proj",
    )(x, mod4, gpre, w_in, *rope_tabs)


def _odd_in_kernel(x_ref, mod_ref, gpre_ref, w_ref, q_ref, f_ref, v_ref, g_ref):
    h = _pre(x_ref[...], mod_ref[0], gpre_ref[...])
    y = _dot(h, w_ref[...])
    n = q_ref.shape[1]
    q_ref[...] = y[:, :n]
    f_ref[...] = y[:, n:2 * n]
    v_ref[...] = y[:, 2 * n:3 * n]
    g_ref[...] = y[:, 3 * n:]


def _odd_in_call(x, mod4, sub, o, gpre, w_in, tiles_per_seq):
    n, d = x.shape
    r = mod4.shape[2]
    tm = min(_ROW_TILE, n)
    nw = w_in.shape[-1] // 4
    row = lambda i: (i, 0)
    return pl.pallas_call(
        _odd_in_kernel,
        grid=(n // tm,),
        in_specs=[
            pl.BlockSpec((tm, d), row),
            pl.BlockSpec((None, 1, r, 3 * d), lambda i: (sub, i // tiles_per_seq, 0, 0)),
            pl.BlockSpec((None, 1, d), lambda i: (sub, 0, 0)),
            pl.BlockSpec((None, d, 4 * nw), lambda i: (o, 0, 0)),
        ],
        out_specs=[pl.BlockSpec((tm, nw), row)] * 4,
        out_shape=[jax.ShapeDtypeStruct((n, nw), _F32)] * 4,
        compiler_params=_params("parallel"),
        name="odd_in_proj",
    )(x, mod4, gpre, w_in)


def _out_kernel(x_ref, mod_ref, gpost_ref, w_ref, *refs):
    a_refs, o_ref = refs[:-1], refs[-1]
    y = None
    off = 0
    for a_ref in a_refs:
        k = a_ref.shape[1]
        t = _dot(a_ref[...], w_ref[off:off + k, :])
        y = t if y is None else y + t
        off += k
    o_ref[...] = _post(x_ref[...], y, mod_ref[0], gpost_ref[...], 1.0)


def _out_call(x, mod4, sub, widx, gpost, w_out, acts, tiles_per_seq):
    n, d = x.shape
    r = mod4.shape[2]
    tm = min(_ROW_TILE, n)
    row = lambda i: (i, 0)
    return pl.pallas_call(
        _out_kernel,
        grid=(n // tm,),
        in_specs=[
            pl.BlockSpec((tm, d), row),
            pl.BlockSpec((None, 1, r, 3 * d), lambda i: (sub, i // tiles_per_seq, 0, 0)),
            pl.BlockSpec((None, 1, d), lambda i: (sub, 0, 0)),
            pl.BlockSpec((None, w_out.shape[1], d), lambda i: (widx, 0, 0)),
        ] + [pl.BlockSpec((tm, a.shape[1]), row) for a in acts],
        out_specs=pl.BlockSpec((tm, d), row),
        out_shape=jax.ShapeDtypeStruct((n, d), _F32),
        compiler_params=_params("parallel"),
        name="mixer_out_proj",
    )(x, mod4, gpost, w_out, *acts)


def _seq_tile(time_tiles, s):
    row = lax.broadcasted_iota(jnp.int32, time_tiles[0].shape, 0)
    out = jnp.zeros(time_tiles[0].shape, time_tiles[0].dtype)
    for t, x in enumerate(time_tiles):
        shift = (t - s) % _SUBLANES
        out = jnp.where(row == t, pltpu.roll(x, shift, axis=0) if shift else x, out)
    return out


def _time_tiles(seq_tiles, n_t):
    row = lax.broadcasted_iota(jnp.int32, seq_tiles[0].shape, 0)
    outs = []
    for t in range(n_t):
        acc = jnp.zeros(seq_tiles[0].shape, seq_tiles[0].dtype)
        for s, x in enumerate(seq_tiles):
            shift = (s - t) % _SUBLANES
            acc = jnp.where(row == s, pltpu.roll(x, shift, axis=0) if shift else x, acc)
        outs.append(acc)
    return outs


def _sink_softmax_pv(s, mask, sink, v):
    s = jnp.where(mask, s, -jnp.inf)
    m = jnp.maximum(jnp.max(s, axis=-1, keepdims=True), sink)
    p = jnp.exp(s - m)
    denom = jnp.sum(p, axis=-1, keepdims=True) + jnp.exp(sink - m)
    return _dot(p, v) / denom


def _swa_prompt_kernel(sink_ref, q_ref, kp_ref, kc_ref, vp_ref, vc_ref, o_ref):
    w = _WINDOW
    hd = _A_HEAD_DIM
    assert _LANES == 2 * hd and _A_GROUP % 2 == 0
    log2e = np.float32(np.log2(np.e))
    scale = np.float32(1.0 / np.sqrt(hd)) * log2e
    low = lax.broadcasted_iota(jnp.int32, (2 * w, _LANES), 1) < hd
    ones_lo = jnp.where(low, 1.0, 0.0).astype(_BF16)
    ones_hi = jnp.where(low, 0.0, 1.0).astype(_BF16)
    low_q = lax.broadcasted_iota(jnp.int32, (w, _LANES), 1) < hd
    row = lax.broadcasted_iota(jnp.int32, (w, 4 * w), 0)
    col = lax.broadcasted_iota(jnp.int32, (w, 4 * w), 1) & (2 * w - 1)
    for qb in range(q_ref.shape[0] // w):
        rows = slice(qb * w, (qb + 1) * w)
        if qb == 0:
            k2 = jnp.concatenate([kp_ref[...], kc_ref[0:w, :]], axis=0)
            v2 = jnp.concatenate([vp_ref[...], vc_ref[0:w, :]], axis=0)
            first = jnp.where(pl.program_id(1) > 0, 0, w)
        else:
            k2 = kc_ref[(qb - 1) * w:(qb + 1) * w, :]
            v2 = vc_ref[(qb - 1) * w:(qb + 1) * w, :]
            first = 0
        mask = jnp.logical_and(col > jnp.maximum(row, first - 1), col <= row + w)
        keys, vals = [], []
        for j in range(_A_KV_HEADS):
            own_k = jnp.where(low, k2, 0.0) if j == 0 else jnp.where(low, 0.0, k2)
            own_v = jnp.where(low, v2, 0.0) if j == 0 else jnp.where(low, 0.0, v2)
            oth_k = pltpu.roll(own_k, hd, axis=1)
            oth_v = pltpu.roll(own_v, hd, axis=1)
            lo_k, hi_k = (own_k, oth_k) if j == 0 else (oth_k, own_k)
            lo_v, hi_v = (own_v, oth_v) if j == 0 else (oth_v, own_v)
            keys.append(jnp.concatenate([lo_k, hi_k], axis=0).astype(_BF16))
            vals.append(jnp.concatenate([jnp.concatenate([lo_v.astype(_BF16), ones_lo], axis=1),
                                         jnp.concatenate([hi_v.astype(_BF16), ones_hi], axis=1)], axis=0))
        pairs = range(_A_HEADS // 2)
        kv_of = [(2 * p) // _A_GROUP for p in pairs]
        scores = [_dot_nt(q_ref[rows, p * _LANES:(p + 1) * _LANES] * scale, keys[kv_of[p]]) for p in pairs]
        probs, sink_terms = [], []
        for p in pairs:
            s = jnp.where(mask, scores[p], -jnp.inf)
            halves = []
            for i in range(2):
                sh = s[:, i * 2 * w:(i + 1) * 2 * w]
                sink = sink_ref[2 * p + i] * log2e
                m = jnp.maximum(jnp.max(sh, axis=-1, keepdims=True), sink)
                halves.append((jnp.exp2(sh - m), jnp.exp2(sink - m)))
            probs.append(jnp.concatenate([halves[0][0], halves[1][0]], axis=1))
            sink_terms.append(jnp.where(low_q, halves[0][1], halves[1][1]))
        for p in pairs:
            r = _dot(probs[p], vals[kv_of[p]])
            o_ref[rows, p * _LANES:(p + 1) * _LANES] = r[:, :_LANES] / (r[:, _LANES:] + sink_terms[p])


def _swa_prompt_call(q, k, v, sinks, batch):
    n, nq = q.shape
    nk = k.shape[1]
    w = _WINDOW
    qb = _SWA_QBLOCKS
    nb = n // batch // (w * qb)
    cur = lambda b, i: (b * nb + i, 0)
    prev = lambda b, i: ((b * nb + i) * qb - jnp.minimum(i, 1), 0)
    return pl.pallas_call(
        _swa_prompt_kernel,
        grid=(batch, nb),
        in_specs=[
            pl.BlockSpec(memory_space=pltpu.SMEM),
            pl.BlockSpec((qb * w, nq), cur),
            pl.BlockSpec((w, nk), prev), pl.BlockSpec((qb * w, nk), cur),
            pl.BlockSpec((w, nk), prev), pl.BlockSpec((qb * w, nk), cur),
        ],
        out_specs=pl.BlockSpec((qb * w, nq), cur),
        out_shape=jax.ShapeDtypeStruct((n, nq), _F32),
        compiler_params=_params("parallel", "parallel"),
        name="swa_prompt",
    )(sinks, q, k, k, v, v)


def _swa_sample_kernel(sink_ref, q_ref, kn_ref, vn_ref, ck_ref, cv_ref, o_ref, kw_ref, vw_ref, *, t_new):
    p = ck_ref.shape[1]
    scale = np.float32(1.0 / np.sqrt(_A_HEAD_DIM))
    rows = _A_GROUP * _SUBLANES
    t = lax.broadcasted_iota(jnp.int32, (rows, p + _SUBLANES), 0) & (_SUBLANES - 1)
    c = lax.broadcasted_iota(jnp.int32, (rows, p + _SUBLANES), 1)
    mask = jnp.logical_and(c <= t + p, c > t + p - _WINDOW)
    g_of_row = lax.broadcasted_iota(jnp.int32, (rows, 1), 0) >> (_SUBLANES.bit_length() - 1)
    nseq = ck_ref.shape[0]
    q_t = [q_ref[t] for t in range(t_new)]
    kn_t = [kn_ref[t] for t in range(t_new)]
    vn_t = [vn_ref[t] for t in range(t_new)]
    sinks = []
    for j in range(_A_KV_HEADS):
        sink = jnp.zeros((rows, 1), _F32)
        for g in range(_A_GROUP):
            sink = jnp.where(g_of_row == g, sink_ref[j * _A_GROUP + g], sink)
        sinks.append(sink)
    new_k, new_v, qs, keys, vals = [], [], [], [], []
    for s in range(nseq):
        q8, kn8, vn8 = _seq_tile(q_t, s), _seq_tile(kn_t, s), _seq_tile(vn_t, s)
        ck, cv = ck_ref[s], cv_ref[s]
        kw_ref[s, 0:p - t_new, :] = ck[t_new:, :]
        kw_ref[s, p - t_new:p, :] = kn8[0:t_new, :]
        vw_ref[s, 0:p - t_new, :] = cv[t_new:, :]
        vw_ref[s, p - t_new:p, :] = vn8[0:t_new, :]
        for j in range(_A_KV_HEADS):
            ks = slice(j * _A_HEAD_DIM, (j + 1) * _A_HEAD_DIM)
            keys.append(jnp.concatenate([ck[:, ks], kn8[:, ks]], axis=0))
            vals.append(jnp.concatenate([cv[:, ks], vn8[:, ks]], axis=0))
            qs.append(jnp.concatenate(
                [q8[:, (j * _A_GROUP + g) * _A_HEAD_DIM:(j * _A_GROUP + g + 1) * _A_HEAD_DIM]
                 for g in range(_A_GROUP)], axis=0))
    scores = [_dot_nt(qj, k) * scale for qj, k in zip(qs, keys)]
    outs = [_sink_softmax_pv(sc, mask, sinks[i % _A_KV_HEADS], v) for i, (sc, v) in enumerate(zip(scores, vals))]
    per_seq = []
    for s in range(nseq):
        heads = []
        for j in range(_A_KV_HEADS):
            o = outs[s * _A_KV_HEADS + j]
            heads.extend(o[g * _SUBLANES:(g + 1) * _SUBLANES, :] for g in range(_A_GROUP))
        per_seq.append(jnp.concatenate(heads, axis=1))
    for t, tile in enumerate(_time_tiles(per_seq, t_new)):
        o_ref[t] = tile


def _swa_sample_call(q, k_new, v_new, cache_k, cache_v, sinks):
    t_new, nseq, nq = q.shape
    nk = k_new.shape[2]
    p = cache_k.shape[1]
    bs = _SAMPLE_SEQS
    toks = lambda i: (0, i, 0)
    seqs = lambda i: (i, 0, 0)
    return pl.pallas_call(
        functools.partial(_swa_sample_kernel, t_new=t_new),
        grid=(nseq // bs,),
        in_specs=[
            pl.BlockSpec(memory_space=pltpu.SMEM),
            pl.BlockSpec((t_new, bs, nq), toks),
            pl.BlockSpec((t_new, bs, nk), toks), pl.BlockSpec((t_new, bs, nk), toks),
            pl.BlockSpec((bs, p, nk), seqs), pl.BlockSpec((bs, p, nk), seqs),
        ],
        out_specs=[pl.BlockSpec((t_new, bs, nq), toks),
                   pl.BlockSpec((bs, p, nk), seqs), pl.BlockSpec((bs, p, nk), seqs)],
        out_shape=[jax.ShapeDtypeStruct((t_new, nseq, nq), _F32),
                   jax.ShapeDtypeStruct(cache_k.shape, _F32), jax.ShapeDtypeStruct(cache_v.shape, _F32)],
        compiler_params=_params("parallel"),
        name="swa_sample",
    )(sinks, q, k_new, v_new, cache_k, cache_v)


def _softplus(z):
    return jnp.maximum(z, 0.0) + jnp.log1p(jnp.exp(-jnp.abs(z)))


def _gelu_tanh(x):
    return 0.5 * x * (1.0 + jnp.tanh(np.float32(np.sqrt(2.0 / np.pi)) * (x + 0.044715 * (x * x * x))))


def _rg_gates(xc, wa_ref, ba_ref, wx_ref, bx_ref, sp_neg_lam):
    r = _sigmoid(_dot(xc, wa_ref[...]) + ba_ref[...])
    i = _sigmoid(_dot(xc, wx_ref[...]) + bx_ref[...])
    log_a = (-_RG_C) * r * sp_neg_lam
    a = jnp.exp(log_a)
    mult = jnp.sqrt(jnp.maximum(1.0 - jnp.exp(2.0 * log_a), 0.0))
    return a, mult * (i * xc)


def _rg_prompt_kernel(xr_ref, xg_ref, cw_ref, cb_ref, wa_ref, ba_ref, wx_ref, bx_ref, lam_ref,
                      o_ref, conv_ref, h_ref, xpad, a_s, b_s):
    t_len, w = xr_ref.shape
    cw = cw_ref.shape[0]
    xpad[0:_SUBLANES, :] = jnp.zeros((_SUBLANES, w), _F32)
    xpad[_SUBLANES:, :] = xr_ref[...]
    sp = _softplus(-lam_ref[...])
    for c in range(t_len // _RG_CHUNK):
        r0 = c * _RG_CHUNK
        xc = cb_ref[...]
        for j in range(cw):
            lo = _SUBLANES + r0 - (cw - 1) + j
            xc = xc + xpad[lo:lo + _RG_CHUNK, :] * cw_ref[j:j + 1, :]
        a, b = _rg_gates(xc, wa_ref, ba_ref, wx_ref, bx_ref, sp)
        a_s[r0:r0 + _RG_CHUNK, :] = a
        b_s[r0:r0 + _RG_CHUNK, :] = b

    row = lax.broadcasted_iota(jnp.int32, (_SUBLANES, w), 0)

    def group(g, h):
        r0 = pl.multiple_of(g * _SUBLANES, _SUBLANES)
        a = a_s[pl.ds(r0, _SUBLANES), :]
        b = b_s[pl.ds(r0, _SUBLANES), :]
        sh = 1
        while sh < _SUBLANES:
            a_prev = jnp.where(row >= sh, pltpu.roll(a, sh, axis=0), 1.0)
            b_prev = jnp.where(row >= sh, pltpu.roll(b, sh, axis=0), 0.0)
            b = a * b_prev + b
            a = a * a_prev
            sh *= 2
        hs = a * h + b
        o_ref[pl.ds(r0, _SUBLANES), :] = _gelu_tanh(xg_ref[pl.ds(r0, _SUBLANES), :]) * hs
        return hs[_SUBLANES - 1:_SUBLANES, :]

    h_last = lax.fori_loop(0, t_len // _SUBLANES, group, jnp.zeros((1, w), _F32), unroll=2)
    h_ref[0] = h_last
    conv_ref[0] = xr_ref[t_len - (cw - 1):t_len, :]


def _rg_prompt_call(xr, xg, conv_w, conv_b, wa_bd, ba, wx_bd, bx, lam, batch):
    n, w = xr.shape
    t_len = n // batch
    cw = conv_w.shape[0]
    seq = lambda b: (b, 0)
    const = lambda b: (0, 0)
    vec = pl.BlockSpec((1, w), const)
    mat = pl.BlockSpec((w, w), const)
    return pl.pallas_call(
        _rg_prompt_kernel,
        grid=(batch,),
        in_specs=[pl.BlockSpec((t_len, w), seq), pl.BlockSpec((t_len, w), seq),
                  pl.BlockSpec((cw, w), const), vec, mat, vec, mat, vec, vec],
        out_specs=[pl.BlockSpec((t_len, w), seq),
                   pl.BlockSpec((1, cw - 1, w), lambda b: (b, 0, 0)),
                   pl.BlockSpec((1, 1, w), lambda b: (b, 0, 0))],
        out_shape=[jax.ShapeDtypeStruct((n, w), _F32),
                   jax.ShapeDtypeStruct((batch, cw - 1, w), _F32),
                   jax.ShapeDtypeStruct((batch, 1, w), _F32)],
        scratch_shapes=[pltpu.VMEM((t_len + _SUBLANES, w), _F32),
                        pltpu.VMEM((t_len, w), _F32), pltpu.VMEM((t_len, w), _F32)],
        compiler_params=_params("parallel"),
        name="rglru_prompt",
    )(xr, xg, conv_w, conv_b, wa_bd, ba, wx_bd, bx, lam)


def _rg_sample_kernel(xr_ref, xg_ref, conv0_ref, h0_ref, cw_ref, cb_ref, wa_ref, ba_ref, wx_ref, bx_ref, lam_ref,
                      o_ref, conv_ref, h_ref):
    cw = cw_ref.shape[0]
    t_new = xr_ref.shape[0]
    sp = _softplus(-lam_ref[...])
    hist = [conv0_ref[j] for j in range(cw - 1)] + [xr_ref[t] for t in range(t_new)]
    h = h0_ref[...]
    for t in range(t_new):
        xc = cb_ref[...]
        for j in range(cw):
            xc = xc + hist[t + j] * cw_ref[j:j + 1, :]
        a, b = _rg_gates(xc, wa_ref, ba_ref, wx_ref, bx_ref, sp)
        h = a * h + b
        o_ref[t] = _gelu_tanh(xg_ref[t]) * h
    h_ref[...] = h
    for j in range(cw - 1):
        conv_ref[j] = hist[t_new + j]


def _rg_sample_call(xr_t, xg_t, conv0_t, h0, conv_w, conv_b, wa_bd, ba, wx_bd, bx, lam):
    t_new, nseq, w = xg_t.shape
    return pl.pallas_call(
        _rg_sample_kernel,
        out_shape=[jax.ShapeDtypeStruct((t_new, nseq, w), _F32), jax.ShapeDtypeStruct(conv0_t.shape, _F32),
                   jax.ShapeDtypeStruct((nseq, w), _F32)],
        compiler_params=pltpu.CompilerParams(vmem_limit_bytes=_VMEM_LIMIT),
        name="rglru_sample",
    )(xr_t, xg_t, conv0_t, h0, conv_w, conv_b, wa_bd, ba, wx_bd, bx, lam)


def _row_bcast(x, r, n):
    return jnp.broadcast_to(x[r:r + 1, :], (n, x.shape[1]))


def _chunk_cumsum(x):
    n_tiles = x.shape[0] // _SUBLANES
    row = lax.broadcasted_iota(jnp.int32, (_SUBLANES, x.shape[1]), 0)
    tiles = []
    carry = None
    for i in range(n_tiles):
        t = x[i * _SUBLANES:(i + 1) * _SUBLANES, :]
        sh = 1
        while sh < _SUBLANES:
            t = t + jnp.where(row >= sh, pltpu.roll(t, sh, axis=0), 0.0)
            sh *= 2
        if carry is not None:
            t = t + carry
        carry = _row_bcast(t, _SUBLANES - 1, _SUBLANES)
        tiles.append(t)
    return tiles[0] if n_tiles == 1 else jnp.concatenate(tiles, axis=0)


def _level_reference(b, m):
    n = b.shape[0]
    if 2 * m >= _SUBLANES:
        pieces = [_row_bcast(b, lo + m - 1, 2 * m) for lo in range(0, n, 2 * m)]
        return pieces[0] if len(pieces) == 1 else jnp.concatenate(pieces, axis=0)
    row = lax.broadcasted_iota(jnp.int32, (_SUBLANES, b.shape[1]), 0)
    tiles = []
    for i in range(n // _SUBLANES):
        t = b[i * _SUBLANES:(i + 1) * _SUBLANES, :]
        ref = None
        for lo in range(0, _SUBLANES, 2 * m):
            piece = _row_bcast(t, lo + m - 1, _SUBLANES)
            ref = piece if ref is None else jnp.where(row >= lo, piece, ref)
        tiles.append(ref)
    return tiles[0] if len(tiles) == 1 else jnp.concatenate(tiles, axis=0)


def _hgrn_gates(fz, lb):
    f = lb + (1.0 - lb) * _sigmoid(fz)
    return jnp.log(f), 1.0 - f


def _hgrn_chunk(q, fz, v, lb, state, n_valid, state_is_vk):
    n, kd = q.shape
    log_f, k = _hgrn_gates(fz, lb)
    if n_valid < n:
        valid = lax.broadcasted_iota(jnp.int32, (n, kd), 0) < n_valid
        log_f = jnp.where(valid, log_f, 0.0)
        k = jnp.where(valid, k, 0.0)
    b = _chunk_cumsum(log_f)
    b_last = _row_bcast(b, n - 1, n)

    q_in = q * jnp.exp(b)
    k_end = k * jnp.exp(b_last - b)
    if state_is_vk:
        o = _dot_nt(q_in, state)
        new_state = jnp.exp(b_last[0:1, :]) * state + _dot_tn(v, k_end)
    else:
        o = _dot(q_in, state)
        decay = jnp.exp(jnp.broadcast_to(b_last[0:1, :], (v.shape[1], kd))).T
        new_state = decay * state + _dot_tn(k_end, v)

    row = lax.broadcasted_iota(jnp.int32, (n, n), 0)
    col = lax.broadcasted_iota(jnp.int32, (n, n), 1)
    upper = lax.broadcasted_iota(jnp.int32, (n, kd), 0)
    scores = jnp.where(row == col, jnp.sum(q * k, axis=-1, keepdims=True), 0.0)
    m = 1
    while m < n_valid:
        e = jnp.exp(-jnp.abs(b - _level_reference(b, m)))
        z = jnp.where((upper & m) != 0, q, k) * e
        pair = jnp.logical_and((row & m) != 0, (row ^ m) >> (m.bit_length() - 1) == col >> (m.bit_length() - 1))
        scores = scores + jnp.where(pair, _dot_nt(z, z), 0.0)
        m *= 2
    return o + _dot(scores, v), new_state


def _lower_bound(logits, layer):
    m = jnp.max(logits, axis=0, keepdims=True)
    e = jnp.exp(logits - m)
    return jnp.sum(e[1:layer + 1, :], axis=0, keepdims=True) / jnp.sum(e, axis=0, keepdims=True)


def _hgrn_out(o, g, gnorm):
    return _rms(o, gnorm) * _silu(g)


def _hgrn_prompt_kernel(q_ref, f_ref, v_ref, g_ref, lbl_ref, gn_ref, o_ref, s_ref,
                        st_scr, st0_scr, qs_scr, ks_scr, ke_scr, vb_scr, dec_scr, *, layer):
    rows_blk, width = q_ref.shape
    kd = width // _HG_HEADS
    n_chunks = rows_blk // _HG_CHUNK
    heads = [slice(h * kd, (h + 1) * kd) for h in range(_HG_HEADS)]
    lb = _lower_bound(lbl_ref[...], layer)
    gn = gn_ref[...]

    @pl.when(pl.program_id(2) == 0)
    def _():
        st_scr[...] = jnp.zeros(st_scr.shape, _F32)

    st0_scr[...] = st_scr[...]

    def chunk_rows(c):
        return pl.ds(pl.multiple_of(c * _HG_CHUNK, _HG_CHUNK), _HG_CHUNK)

    def prepare(c):
        rows = chunk_rows(c)
        log_f, k = _hgrn_gates(f_ref[rows, :], lb)
        b = _chunk_cumsum(log_f)
        b_end = b[_HG_CHUNK - 1:_HG_CHUNK, :]
        dec = jnp.exp(b_end)
        e_b = jnp.exp(b)
        k_start = k / e_b
        qs_scr[rows, :] = (q_ref[rows, :] * e_b).astype(_BF16)
        ks_scr[rows, :] = k_start.astype(_BF16)
        ke_scr[rows, :] = (k_start * dec).astype(_BF16)
        vb_scr[rows, :] = v_ref[rows, :].astype(_BF16)
        dec_scr[c] = jnp.broadcast_to(dec, (_SUBLANES, width))
        return b_end

    row = lax.broadcasted_iota(jnp.int32, (_HG_CHUNK, _HG_CHUNK), 0)
    col = lax.broadcasted_iota(jnp.int32, (_HG_CHUNK, _HG_CHUNK), 1)

    def finish(c, h, o):
        rows = chunk_rows(c)
        o_ref[rows, heads[h]] = _hgrn_out(o, g_ref[rows, heads[h]], gn)

    def contract(c):
        rows = chunk_rows(c)
        states = [st_scr[h] for h in range(_HG_HEADS)]
        scores = [_dot_nt(qs_scr[rows, l], ks_scr[rows, l]) for l in heads]
        carried = [_dot_nt(qs_scr[rows, l], st) for l, st in zip(heads, states)]
        incs = [_dot_tn(vb_scr[rows, l], ke_scr[rows, l]) for l in heads]
        outs = [_dot(jnp.where(row >= col, s, 0.0), vb_scr[rows, l]) for s, l in zip(scores, heads)]
        for h, l in enumerate(heads):
            st_scr[h] = states[h] * dec_scr[c, 0:1, l] + incs[h]
        for h in range(_HG_HEADS):
            finish(c, h, outs[h] + carried[h])

    def step(c, min_b):
        b_end = prepare(c + 1)
        contract(c)
        return jnp.minimum(min_b, b_end)

    min_b = lax.fori_loop(0, n_chunks - 1, step, prepare(0), unroll=2)
    contract(n_chunks - 1)

    @pl.when(jnp.min(min_b) <= -_HG_SAFE_LOG_DECAY)
    def _():
        st_scr[...] = st0_scr[...]

        def chunk(c, carry):
            rows = chunk_rows(c)
            for h, l in enumerate(heads):
                o, st_scr[h] = _hgrn_chunk(q_ref[rows, l], f_ref[rows, l], v_ref[rows, l],
                                           lb[:, l], st_scr[h], _HG_CHUNK, True)
                finish(c, h, o)
            return carry
        lax.fori_loop(0, n_chunks, chunk, 0)

    @pl.when(pl.program_id(2) == pl.num_programs(2) - 1)
    def _():
        for h in range(_HG_HEADS):
            s_ref[0, h] = st_scr[h].T


def _hgrn_prompt_call(q, fz, v, g, lb_logits, gnorm, batch, layer):
    n, width = q.shape
    t_len = n // batch
    kd = width // _C_HEADS
    gw = _HG_HEADS * kd
    rows_blk = min(_HG_ROWS, t_len)
    nt = t_len // rows_blk
    blk = pl.BlockSpec((rows_blk, gw), lambda b, h, t: (b * nt + t, h))
    return pl.pallas_call(
        functools.partial(_hgrn_prompt_kernel, layer=layer),
        grid=(batch, _C_HEADS // _HG_HEADS, nt),
        in_specs=[blk, blk, blk, blk,
                  pl.BlockSpec((lb_logits.shape[0], gw), lambda b, h, t: (0, h)),
                  pl.BlockSpec((1, kd), lambda b, h, t: (0, 0))],
        out_specs=[blk, pl.BlockSpec((1, _HG_HEADS, kd, kd), lambda b, h, t: (b, h, 0, 0))],
        out_shape=[jax.ShapeDtypeStruct((n, width), _F32),
                   jax.ShapeDtypeStruct((batch, _C_HEADS, kd, kd), _F32)],
        scratch_shapes=[pltpu.VMEM((_HG_HEADS, kd, kd), _F32)] * 2
        + [pltpu.VMEM((rows_blk, gw), _BF16)] * 4
        + [pltpu.VMEM((rows_blk // _HG_CHUNK, _SUBLANES, gw), _F32)],
        compiler_params=_params("parallel", "parallel", "arbitrary"),
        name="hgrn2_prompt",
    )(q, fz, v, g, lb_logits, gnorm)


def _hgrn_sample_kernel(q_ref, f_ref, v_ref, g_ref, lbl_ref, gn_ref, s0_ref, o_ref, s_ref, *, layer):
    t_new, nseq, width = q_ref.shape
    n_heads = s0_ref.shape[1]
    kd = width // n_heads
    lanes = [slice(h * kd, (h + 1) * kd) for h in range(n_heads)]
    lb = _lower_bound(lbl_ref[...], layer)
    gn = gn_ref[...]
    q = [q_ref[t] for t in range(t_new)]
    v = [v_ref[t] for t in range(t_new)]
    keys, b = [], []
    for t in range(t_new):
        log_f, k = _hgrn_gates(f_ref[t], lb)
        keys.append(k)
        b.append(log_f if t == 0 else b[-1] + log_f)

    def per_head_sum(w):
        return jnp.concatenate([jnp.broadcast_to(jnp.sum(w[:, l], axis=-1, keepdims=True), (nseq, kd))
                                for l in lanes], axis=1)

    within = []
    for t in range(t_new):
        acc = per_head_sum(q[t] * keys[t]) * v[t]
        for s in range(t):
            acc = acc + per_head_sum(q[t] * keys[s] * jnp.exp(b[t] - b[s])) * v[s]
        within.append(acc)

    q_in = [q[t] * jnp.exp(b[t]) for t in range(t_new)]
    k_end = [keys[t] * jnp.exp(b[-1] - b[t]) for t in range(t_new)]
    decay = jnp.exp(b[-1])
    pairs = [(s, h) for h in range(n_heads) for s in range(nseq)]
    lhs = [_seq_tile([x[:, lanes[h]] for x in q_in], s) for s, h in pairs]
    k_seq = [_seq_tile([x[:, lanes[h]] for x in k_end], s) for s, h in pairs]
    v_seq = [_seq_tile([x[:, lanes[h]] for x in v], s) for s, h in pairs]
    carried = [_dot(x, s0_ref[s, h]) for x, (s, h) in zip(lhs, pairs)]
    incs = [_dot_tn(ks, vs) for ks, vs in zip(k_seq, v_seq)]
    carried_t = []
    for h in range(n_heads):
        dec_cols = jnp.concatenate([decay[:, lanes[h]], jnp.zeros((kd - nseq, kd), _F32)], axis=0).T
        for s in range(nseq):
            i = h * nseq + s
            s_ref[s, h] = jnp.broadcast_to(dec_cols[:, s:s + 1], (kd, kd)) * s0_ref[s, h] + incs[i]
        carried_t.append(_time_tiles(carried[h * nseq:(h + 1) * nseq], t_new))
    for t in range(t_new):
        o = within[t] + jnp.concatenate([carried_t[h][t] for h in range(n_heads)], axis=1)
        g = g_ref[t]
        o_ref[t] = jnp.concatenate([_hgrn_out(o[:, l], g[:, l], gn) for l in lanes], axis=1)


def _hgrn_sample_call(q, fz, v, g, lb_logits, gnorm, s0, layer):
    t_new, nseq, width = q.shape
    kd = width // _C_HEADS
    gw = _HG_HEADS * kd
    bs = _SAMPLE_SEQS
    blk = pl.BlockSpec((t_new, bs, gw), lambda i, h: (0, i, h))
    st = pl.BlockSpec((bs, _HG_HEADS, kd, kd), lambda i, h: (i, h, 0, 0))
    return pl.pallas_call(
        functools.partial(_hgrn_sample_kernel, layer=layer),
        grid=(nseq // bs, _C_HEADS // _HG_HEADS),
        in_specs=[blk, blk, blk, blk,
                  pl.BlockSpec((lb_logits.shape[0], gw), lambda i, h: (0, h)),
                  pl.BlockSpec((1, kd), lambda i, h: (0, 0)), st],
        out_specs=[blk, st],
        out_shape=[jax.ShapeDtypeStruct((t_new, nseq, width), _F32), jax.ShapeDtypeStruct(s0.shape, _F32)],
        compiler_params=_params("parallel", "parallel"),
        name="hgrn2_sample",
    )(q, fz, v, g, lb_logits, gnorm, s0)


def _rope_tables(pos):
    half = _ROT_DIM // 2
    inv_freq = _ROPE_THETA ** (-jnp.arange(0, _ROT_DIM, 2, dtype=_F32) / _ROT_DIM)
    ang = pos.astype(_F32)[:, None] * inv_freq[None, :]
    cos, sin = jnp.cos(ang), jnp.sin(ang)
    ones = jnp.ones((pos.shape[0], _A_HEAD_DIM - _ROT_DIM), _F32)
    zeros = jnp.zeros((pos.shape[0], _A_HEAD_DIM - half), _F32)
    zeros_h = jnp.zeros((pos.shape[0], half), _F32)
    reps = _LANES // _A_HEAD_DIM
    cos_t = jnp.tile(jnp.concatenate([cos, cos, ones], axis=1), (1, reps))
    sin_lo = jnp.tile(jnp.concatenate([-sin, zeros], axis=1), (1, reps))
    sin_hi = jnp.tile(jnp.concatenate([zeros_h, sin, ones * 0.0], axis=1), (1, reps))
    return cos_t, sin_lo, sin_hi


def _block_diag(w):
    nb, bd, _ = w.shape
    eye = jnp.eye(nb, dtype=w.dtype)
    return (w[:, :, None, :] * eye[:, None, :, None]).reshape(nb * bd, nb * bd)


def kernel(x_prompt, x_sample, c_prompt, c_sample, cache_k_win, cache_v_win, state_conv_rglru,
           state_h_rglru, state_s_hgrn, norm_pre, norm_post, ada_w, ada_b, ffn1_w_in, ffn1_w_out,
           ffn2_w_in, ffn2_w_out, even_w_in, even_w_out, attn_sinks, rg_conv_w, rg_conv_b, rg_wa,
           rg_ba, rg_wx, rg_bx, rg_lambda, odd_w_in, odd_w_out, hgrn_lb_logits, hgrn_gnorm):
    bp, tp, d = x_prompt.shape
    bs, ts, _ = x_sample.shape
    depth = norm_pre.shape[0]
    n_sub = depth * _N_SUB
    nk = _A_KV_HEADS * _A_HEAD_DIM
    win = cache_k_win.shape[2]
    cw = rg_conv_w.shape[1]
    bw = rg_conv_w.shape[2]
    kd = state_s_hgrn.shape[3]

    cast = lambda w: w.astype(_BF16)
    ffn1_in, ffn1_out, ffn2_in, ffn2_out = cast(ffn1_w_in), cast(ffn1_w_out), cast(ffn2_w_in), cast(ffn2_w_out)
    ev_in, ev_out, od_in, od_out = cast(even_w_in), cast(even_w_out), cast(odd_w_in), cast(odd_w_out)
    gpre = norm_pre.reshape(n_sub, 1, d)
    gpost = norm_post.reshape(n_sub, 1, d)

    mod = _ada_call(jnp.concatenate([c_prompt, c_sample], axis=0),
                    ada_w.reshape(n_sub, d, 3 * d), ada_b.reshape(n_sub, 1, 3 * d))
    mod_p = mod[:, :bp].reshape(n_sub, bp, 1, 3 * d)
    mod_s = mod[:, bp:].reshape(n_sub, 1, bs, 3 * d)

    tabs_p = _rope_tables(jnp.arange(tp, dtype=jnp.int32))
    tabs_s = tuple(jnp.repeat(t, bs, axis=0) for t in _rope_tables(_PAST_LEN + jnp.arange(ts, dtype=jnp.int32)))

    time_major = lambda a: a.reshape(ts, bs, a.shape[-1])
    groups = {
        "p": dict(x=x_prompt.reshape(bp * tp, d), mod=mod_p, tps=tp // _ROW_TILE, tabs=tabs_p,
                  tab_tiles=tp // _ROW_TILE),
        "s": dict(x=x_sample.transpose(1, 0, 2).reshape(ts * bs, d), mod=mod_s, tps=1, tabs=tabs_s, tab_tiles=1),
    }
    outs = {g: dict(k=[], v=[], conv=[], h=[], s=[]) for g in groups}

    for l in range(depth):
        s0, s1, s2 = l * _N_SUB, l * _N_SUB + 1, l * _N_SUB + 2
        for name, grp in groups.items():
            x, mod4, tps = grp["x"], grp["mod"], grp["tps"]
            x = _ffn_call(x, mod4, s0, l, gpre, ffn1_in, ffn1_out, gpost, 0.5, tps)
            if l % 2 == 0:
                e = l // 2
                q, k, v, xg, xr = _even_in_call(x, mod4, s1, e, gpre, ev_in, grp["tabs"], tps, grp["tab_tiles"])
                wa_bd, wx_bd = cast(_block_diag(rg_wa[e])), cast(_block_diag(rg_wx[e]))
                vecs = [a[e].reshape(1, bw) for a in (rg_conv_b, rg_ba, rg_bx, rg_lambda)]
                if name == "p":
                    o_a = _swa_prompt_call(q, k, v, attn_sinks[e], bp)
                    o_b, conv, h_last = _rg_prompt_call(xr, xg, rg_conv_w[e], vecs[0], wa_bd, vecs[1], wx_bd,
                                                        vecs[2], vecs[3], bp)
                    outs[name]["k"].append(k.reshape(bp, tp, _A_KV_HEADS, _A_HEAD_DIM)[:, tp - win:])
                    outs[name]["v"].append(v.reshape(bp, tp, _A_KV_HEADS, _A_HEAD_DIM)[:, tp - win:])
                    outs[name]["h"].append(h_last.reshape(bp, bw))
                else:
                    o_a, kw, vw = _swa_sample_call(time_major(q), time_major(k), time_major(v),
                                                   cache_k_win[e].reshape(bs, win, nk),
                                                   cache_v_win[e].reshape(bs, win, nk), attn_sinks[e])
                    o_b, conv_t, h_last = _rg_sample_call(time_major(xr), time_major(xg),
                                                          state_conv_rglru[e].transpose(1, 0, 2), state_h_rglru[e],
                                                          rg_conv_w[e], vecs[0], wa_bd, vecs[1], wx_bd, vecs[2], vecs[3])
                    o_a, o_b = o_a.reshape(ts * bs, -1), o_b.reshape(ts * bs, bw)
                    conv = conv_t.transpose(1, 0, 2)
                    outs[name]["k"].append(kw.reshape(bs, win, _A_KV_HEADS, _A_HEAD_DIM))
                    outs[name]["v"].append(vw.reshape(bs, win, _A_KV_HEADS, _A_HEAD_DIM))
                    outs[name]["h"].append(h_last)
                outs[name]["conv"].append(conv)
                x = _out_call(x, mod4, s1, e, gpost, ev_out, [o_a, o_b], tps)
            else:
                o = l // 2
                q, fz, v, g = _odd_in_call(x, mod4, s1, o, gpre, od_in, tps)
                gn = hgrn_gnorm[o].reshape(1, kd)
                if name == "p":
                    y, s_last = _hgrn_prompt_call(q, fz, v, g, hgrn_lb_logits, gn, bp, l)
                else:
                    y, s_last = _hgrn_sample_call(time_major(q), time_major(fz), time_major(v), time_major(g),
                                                  hgrn_lb_logits, gn, state_s_hgrn[o], l)
                    y = y.reshape(ts * bs, -1)
                outs[name]["s"].append(s_last)
                x = _out_call(x, mod4, s1, o, gpost, od_out, [y], tps)
            x = _ffn_call(x, mod4, s2, l, gpre, ffn2_in, ffn2_out, gpost, 0.5, tps)
            grp["x"] = x

    ys = {"p": groups["p"]["x"].reshape(bp, tp, d), "s": groups["s"]["x"].reshape(ts, bs, d).transpose(1, 0, 2)}
    res = []
    for name in ("p", "s"):
        o = outs[name]
        res.append((jnp.stack(o["k"]), jnp.stack(o["v"]), jnp.stack(o["conv"]), jnp.stack(o["h"]), jnp.stack(o["s"])))
    return (ys["p"], ys["s"]) + res[0] + res[1]
```

```python
import functools

import jax
import jax.numpy as jnp
import numpy as np
from jax import lax
from jax.experimental import pallas as pl
from jax.experimental.pallas import tpu as pltpu

_F32 = jnp.float32
_BF16 = jnp.bfloat16

_EPS = 1e-6
_A_HEADS = 8
_A_KV_HEADS = 2
_A_HEAD_DIM = 64
_A_GROUP = _A_HEADS // _A_KV_HEADS
_WINDOW = 128
_ROPE_THETA = 500000.0
_ROT_DIM = _A_HEAD_DIM // 4
_RG_C = 8.0
_C_HEADS = 8
_PAST_LEN = 16384
_N_SUB = 3

_LANES = 128
_SUBLANES = 8
_VMEM_LIMIT = 48 * 1024 * 1024

_ROW_TILE = 512
_FF_CHUNK = 256
_RG_CHUNK = 256
_HG_CHUNK = 64
_HG_HEADS = 4
_HG_ROWS = 1024
_HG_SAFE_LOG_DECAY = 80.0
_SAMPLE_SEQS = 8
_SWA_QBLOCKS = 4


def _dot(a, b):
    return jnp.dot(a.astype(_BF16), b.astype(_BF16), preferred_element_type=_F32)


def _dot_nt(a, b):
    return lax.dot_general(a.astype(_BF16), b.astype(_BF16), (((1,), (1,)), ((), ())),
                           preferred_element_type=_F32)


def _dot_tn(a, b):
    return lax.dot_general(a.astype(_BF16), b.astype(_BF16), (((0,), (0,)), ((), ())),
                           preferred_element_type=_F32)


def _sigmoid(x):
    return 1.0 / (1.0 + jnp.exp(-x))


def _silu(x):
    return x * _sigmoid(x)


def _rms(x, gain):
    inv = lax.rsqrt(jnp.mean(x * x, axis=-1, keepdims=True) + _EPS)
    return x * inv * gain


def _per_seq(a, r):
    n = a.shape[0]
    return a if r in (1, n) else a.reshape(n // r, r, a.shape[1])


def _pre(x, mod, gain):
    n, d = x.shape
    h = _rms(_per_seq(x, mod.shape[0]), gain) * (1.0 + mod[:, d:2 * d]) + mod[:, :d]
    return h.reshape(n, d)


def _post(x, y, mod, gain, res_w):
    n, d = x.shape
    r = mod.shape[0]
    out = _per_seq(x, r) + (res_w * (1.0 + mod[:, 2 * d:])) * _rms(_per_seq(y, r), gain)
    return out.reshape(n, d)


def _params(*sem):
    return pltpu.CompilerParams(dimension_semantics=sem, vmem_limit_bytes=_VMEM_LIMIT)


def _ada_kernel(c_ref, w_ref, b_ref, o_ref):
    o_ref[...] = _dot(_silu(c_ref[...]), w_ref[...]) + b_ref[...]


def _ada_call(c_all, ada_w, ada_b):
    m, d = c_all.shape
    n_sub = ada_w.shape[0]
    n = ada_w.shape[-1]
    tn = n // 2
    return pl.pallas_call(
        _ada_kernel,
        grid=(n_sub, n // tn),
        in_specs=[
            pl.BlockSpec((m, d), lambda s, j: (0, 0)),
            pl.BlockSpec((None, d, tn), lambda s, j: (s, 0, j)),
            pl.BlockSpec((None, 1, tn), lambda s, j: (s, 0, j)),
        ],
        out_specs=pl.BlockSpec((None, m, tn), lambda s, j: (s, 0, j)),
        out_shape=jax.ShapeDtypeStruct((n_sub, m, n), _F32),
        compiler_params=_params("parallel", "parallel"),
        name="ada_mod",
    )(c_all, ada_w, ada_b)


def _mixer_residual(x, mod, gpost, w_ref, act_refs):
    y = None
    off = 0
    for a_ref in act_refs:
        k = a_ref.shape[1]
        t = _dot(a_ref[...], w_ref[off:off + k, :])
        y = t if y is None else y + t
        off += k
    return _post(x, y, mod, gpost, 1.0)


def _ffn_kernel(x_ref, mod_ref, gpre_ref, win_ref, wout_ref, gpost_ref, *refs, res_w):
    o_ref = refs[-1]
    x = x_ref[...]
    if len(refs) > 1:
        mmod_ref, mgpost_ref, mw_ref = refs[:3]
        x = _mixer_residual(x, mmod_ref[0], mgpost_ref[...], mw_ref, refs[3:-1])
    mod = mod_ref[0]
    dff = wout_ref.shape[0]
    h = _pre(x, mod, gpre_ref[...]).astype(_BF16)
    acc = jnp.zeros(x.shape, _F32)
    for j in range(dff // _FF_CHUNK):
        lo = j * _FF_CHUNK
        g = jnp.dot(h, win_ref[:, lo:lo + _FF_CHUNK], preferred_element_type=_F32)
        u = jnp.dot(h, win_ref[:, dff + lo:dff + lo + _FF_CHUNK], preferred_element_type=_F32)
        acc = acc + _dot(_silu(g) * u, wout_ref[lo:lo + _FF_CHUNK, :])
    o_ref[...] = _post(x, acc, mod, gpost_ref[...], res_w)


def _ffn_call(x, mod4, sub, layer, gpre, w_in, w_out, gpost, res_w, tiles_per_seq, mixer=None):
    n, d = x.shape
    dff = w_out.shape[1]
    r = mod4.shape[2]
    tm = min(_ROW_TILE, n)
    row = lambda i: (i, 0)
    mod_spec = lambda s: pl.BlockSpec((None, 1, r, 3 * d), lambda i: (s, i // tiles_per_seq, 0, 0))
    gain_spec = lambda s: pl.BlockSpec((None, 1, d), lambda i: (s, 0, 0))
    in_specs = [
        pl.BlockSpec((tm, d), row), mod_spec(sub), gain_spec(sub),
        pl.BlockSpec((None, d, 2 * dff), lambda i: (layer, 0, 0), pipeline_mode=pl.Buffered(1)),
        pl.BlockSpec((None, dff, d), lambda i: (layer, 0, 0), pipeline_mode=pl.Buffered(1)),
        gain_spec(sub),
    ]
    args = [x, mod4, gpre, w_in, w_out, gpost]
    if mixer is not None:
        msub, widx, mw, acts = mixer
        in_specs += [mod_spec(msub), gain_spec(msub),
                     pl.BlockSpec((None, mw.shape[1], d), lambda i: (widx, 0, 0), pipeline_mode=pl.Buffered(1))]
        in_specs += [pl.BlockSpec((tm, a.shape[1]), row) for a in acts]
        args += [mod4, gpost, mw] + list(acts)
    return pl.pallas_call(
        functools.partial(_ffn_kernel, res_w=res_w),
        grid=(n // tm,),
        in_specs=in_specs,
        out_specs=pl.BlockSpec((tm, d), row),
        out_shape=jax.ShapeDtypeStruct((n, d), _F32),
        compiler_params=_params("parallel"),
        name="ffn_sublayer" if mixer is None else "mixer_out_ffn",
    )(*args)


def _rope(x, cos, sin_lo, sin_hi):
    outs = []
    for j in range(x.shape[1] // _LANES):
        xc = x[:, j * _LANES:(j + 1) * _LANES]
        nxt = pltpu.roll(xc, _LANES - _ROT_DIM // 2, axis=1)
        prv = pltpu.roll(xc, _ROT_DIM // 2, axis=1)
        outs.append(xc * cos + nxt * sin_lo + prv * sin_hi)
    return outs[0] if len(outs) == 1 else jnp.concatenate(outs, axis=1)


def _even_in_kernel(x_ref, mod_ref, gpre_ref, w_ref, cos_ref, slo_ref, shi_ref,
                    q_ref, k_ref, v_ref, xg_ref, xr_ref):
    h = _pre(x_ref[...], mod_ref[0], gpre_ref[...])
    y = _dot(h, w_ref[...])
    nq, nk, nw = q_ref.shape[1], k_ref.shape[1], xg_ref.shape[1]
    cos, slo, shi = cos_ref[...], slo_ref[...], shi_ref[...]
    q_ref[...] = _rope(y[:, :nq], cos, slo, shi)
    k_ref[...] = _rope(y[:, nq:nq + nk], cos, slo, shi)
    v_ref[...] = y[:, nq + nk:nq + 2 * nk]
    xg_ref[...] = y[:, nq + 2 * nk:nq + 2 * nk + nw]
    xr_ref[...] = y[:, nq + 2 * nk + nw:]


def _even_in_call(x, mod4, sub, e, gpre, w_in, rope_tabs, tiles_per_seq, tab_tiles):
    n, d = x.shape
    r = mod4.shape[2]
    tm = min(_ROW_TILE, n)
    nq = _A_HEADS * _A_HEAD_DIM
    nk = _A_KV_HEADS * _A_HEAD_DIM
    nw = (w_in.shape[-1] - nq - 2 * nk) // 2
    row = lambda i: (i, 0)
    tab = pl.BlockSpec((tm, _LANES), lambda i: (i % tab_tiles, 0))
    return pl.pallas_call(
        _even_in_kernel,
        grid=(n // tm,),
        in_specs=[
            pl.BlockSpec((tm, d), row),
            pl.BlockSpec((None, 1, r, 3 * d), lambda i: (sub, i // tiles_per_seq, 0, 0)),
            pl.BlockSpec((None, 1, d), lambda i: (sub, 0, 0)),
            pl.BlockSpec((None, d, w_in.shape[-1]), lambda i: (e, 0, 0)),
            tab, tab, tab,
        ],
        out_specs=[pl.BlockSpec((tm, nq), row), pl.BlockSpec((tm, nk), row), pl.BlockSpec((tm, nk), row),
                   pl.BlockSpec((tm, nw), row), pl.BlockSpec((tm, nw), row)],
        out_shape=[jax.ShapeDtypeStruct((n, nq), _F32), jax.ShapeDtypeStruct((n, nk), _F32),
                   jax.ShapeDtypeStruct((n, nk), _F32), jax.ShapeDtypeStruct((n, nw), _F32),
                   jax.ShapeDtypeStruct((n, nw), _F32)],
        compiler_params=_params("parallel"),
        name="even_in_proj",
    )(x, mod4, gpre, w_in, *rope_tabs)


def _odd_in_kernel(x_ref, mod_ref, gpre_ref, w_ref, q_ref, f_ref, v_ref, g_ref):
    h = _pre(x_ref[...], mod_ref[0], gpre_ref[...])
    y = _dot(h, w_ref[...])
    n = q_ref.shape[1]
    q_ref[...] = y[:, :n]
    f_ref[...] = y[:, n:2 * n]
    v_ref[...] = y[:, 2 * n:3 * n]
    g_ref[...] = y[:, 3 * n:]


def _odd_in_call(x, mod4, sub, o, gpre, w_in, tiles_per_seq):
    n, d = x.shape
    r = mod4.shape[2]
    tm = min(_ROW_TILE, n)
    nw = w_in.shape[-1] // 4
    row = lambda i: (i, 0)
    return pl.pallas_call(
        _odd_in_kernel,
        grid=(n // tm,),
        in_specs=[
            pl.BlockSpec((tm, d), row),
            pl.BlockSpec((None, 1, r, 3 * d), lambda i: (sub, i // tiles_per_seq, 0, 0)),
            pl.BlockSpec((None, 1, d), lambda i: (sub, 0, 0)),
            pl.BlockSpec((None, d, 4 * nw), lambda i: (o, 0, 0)),
        ],
        out_specs=[pl.BlockSpec((tm, nw), row)] * 4,
        out_shape=[jax.ShapeDtypeStruct((n, nw), _F32)] * 4,
        compiler_params=_params("parallel"),
        name="odd_in_proj",
    )(x, mod4, gpre, w_in)


def _seq_tile(time_tiles, s):
    row = lax.broadcasted_iota(jnp.int32, time_tiles[0].shape, 0)
    out = jnp.zeros(time_tiles[0].shape, time_tiles[0].dtype)
    for t, x in enumerate(time_tiles):
        shift = (t - s) % _SUBLANES
        out = jnp.where(row == t, pltpu.roll(x, shift, axis=0) if shift else x, out)
    return out


def _time_tiles(seq_tiles, n_t):
    row = lax.broadcasted_iota(jnp.int32, seq_tiles[0].shape, 0)
    outs = []
    for t in range(n_t):
        acc = jnp.zeros(seq_tiles[0].shape, seq_tiles[0].dtype)
        for s, x in enumerate(seq_tiles):
            shift = (s - t) % _SUBLANES
            acc = jnp.where(row == s, pltpu.roll(x, shift, axis=0) if shift else x, acc)
        outs.append(acc)
    return outs


def _sink_softmax_pv(s, mask, sink, v):
    s = jnp.where(mask, s, -jnp.inf)
    m = jnp.maximum(jnp.max(s, axis=-1, keepdims=True), sink)
    p = jnp.exp(s - m)
    denom = jnp.sum(p, axis=-1, keepdims=True) + jnp.exp(sink - m)
    return _dot(p, v) / denom


def _swa_prompt_kernel(sink_ref, q_ref, kp_ref, kc_ref, vp_ref, vc_ref, o_ref):
    w = _WINDOW
    hd = _A_HEAD_DIM
    assert _LANES == 2 * hd and _A_GROUP % 2 == 0
    log2e = np.float32(np.log2(np.e))
    scale = np.float32(1.0 / np.sqrt(hd)) * log2e
    low = lax.broadcasted_iota(jnp.int32, (2 * w, _LANES), 1) < hd
    ones_lo = jnp.where(low, 1.0, 0.0).astype(_BF16)
    ones_hi = jnp.where(low, 0.0, 1.0).astype(_BF16)
    low_q = lax.broadcasted_iota(jnp.int32, (w, _LANES), 1) < hd
    row = lax.broadcasted_iota(jnp.int32, (w, 4 * w), 0)
    col = lax.broadcasted_iota(jnp.int32, (w, 4 * w), 1) & (2 * w - 1)
    for qb in range(q_ref.shape[0] // w):
        rows = slice(qb * w, (qb + 1) * w)
        if qb == 0:
            k2 = jnp.concatenate([kp_ref[...], kc_ref[0:w, :]], axis=0)
            v2 = jnp.concatenate([vp_ref[...], vc_ref[0:w, :]], axis=0)
            first = jnp.where(pl.program_id(1) > 0, 0, w)
        else:
            k2 = kc_ref[(qb - 1) * w:(qb + 1) * w, :]
            v2 = vc_ref[(qb - 1) * w:(qb + 1) * w, :]
            first = 0
        mask = jnp.logical_and(col > jnp.maximum(row, first - 1), col <= row + w)
        keys, vals = [], []
        for j in range(_A_KV_HEADS):
            own_k = jnp.where(low, k2, 0.0) if j == 0 else jnp.where(low, 0.0, k2)
            own_v = jnp.where(low, v2, 0.0) if j == 0 else jnp.where(low, 0.0, v2)
            oth_k = pltpu.roll(own_k, hd, axis=1)
            oth_v = pltpu.roll(own_v, hd, axis=1)
            lo_k, hi_k = (own_k, oth_k) if j == 0 else (oth_k, own_k)
            lo_v, hi_v = (own_v, oth_v) if j == 0 else (oth_v, own_v)
            keys.append(jnp.concatenate([lo_k, hi_k], axis=0).astype(_BF16))
            vals.append(jnp.concatenate([jnp.concatenate([lo_v.astype(_BF16), ones_lo], axis=1),
                                         jnp.concatenate([hi_v.astype(_BF16), ones_hi], axis=1)], axis=0))
        pairs = range(_A_HEADS // 2)
        kv_of = [(2 * p) // _A_GROUP for p in pairs]
        scores = [_dot_nt(q_ref[rows, p * _LANES:(p + 1) * _LANES] * scale, keys[kv_of[p]]) for p in pairs]
        probs, sink_terms = [], []
        for p in pairs:
            s = jnp.where(mask, scores[p], -jnp.inf)
            halves = []
            for i in range(2):
                sh = s[:, i * 2 * w:(i + 1) * 2 * w]
                sink = sink_ref[2 * p + i] * log2e
                m = jnp.maximum(jnp.max(sh, axis=-1, keepdims=True), sink)
                halves.append((jnp.exp2(sh - m), jnp.exp2(sink - m)))
            probs.append(jnp.concatenate([halves[0][0], halves[1][0]], axis=1))
            sink_terms.append(jnp.where(low_q, halves[0][1], halves[1][1]))
        for p in pairs:
            r = _dot(probs[p], vals[kv_of[p]])
            o_ref[rows, p * _LANES:(p + 1) * _LANES] = r[:, :_LANES] / (r[:, _LANES:] + sink_terms[p])


def _swa_prompt_call(q, k, v, sinks, batch):
    n, nq = q.shape
    nk = k.shape[1]
    w = _WINDOW
    qb = _SWA_QBLOCKS
    nb = n // batch // (w * qb)
    cur = lambda b, i: (b * nb + i, 0)
    prev = lambda b, i: ((b * nb + i) * qb - jnp.minimum(i, 1), 0)
    return pl.pallas_call(
        _swa_prompt_kernel,
        grid=(batch, nb),
        in_specs=[
            pl.BlockSpec(memory_space=pltpu.SMEM),
            pl.BlockSpec((qb * w, nq), cur),
            pl.BlockSpec((w, nk), prev), pl.BlockSpec((qb * w, nk), cur),
            pl.BlockSpec((w, nk), prev), pl.BlockSpec((qb * w, nk), cur),
        ],
        out_specs=pl.BlockSpec((qb * w, nq), cur),
        out_shape=jax.ShapeDtypeStruct((n, nq), _F32),
        compiler_params=_params("parallel", "parallel"),
        name="swa_prompt",
    )(sinks, q, k, k, v, v)


def _swa_sample_kernel(sink_ref, q_ref, kn_ref, vn_ref, ck_ref, cv_ref, o_ref, kw_ref, vw_ref, *, t_new):
    p = ck_ref.shape[1]
    scale = np.float32(1.0 / np.sqrt(_A_HEAD_DIM))
    rows = _A_GROUP * _SUBLANES
    t = lax.broadcasted_iota(jnp.int32, (rows, p + _SUBLANES), 0) & (_SUBLANES - 1)
    c = lax.broadcasted_iota(jnp.int32, (rows, p + _SUBLANES), 1)
    mask = jnp.logical_and(c <= t + p, c > t + p - _WINDOW)
    g_of_row = lax.broadcasted_iota(jnp.int32, (rows, 1), 0) >> (_SUBLANES.bit_length() - 1)
    nseq = ck_ref.shape[0]
    q_t = [q_ref[t] for t in range(t_new)]
    kn_t = [kn_ref[t] for t in range(t_new)]
    vn_t = [vn_ref[t] for t in range(t_new)]
    sinks = []
    for j in range(_A_KV_HEADS):
        sink = jnp.zeros((rows, 1), _F32)
        for g in range(_A_GROUP):
            sink = jnp.where(g_of_row == g, sink_ref[j * _A_GROUP + g], sink)
        sinks.append(sink)
    new_k, new_v, qs, keys, vals = [], [], [], [], []
    for s in range(nseq):
        q8, kn8, vn8 = _seq_tile(q_t, s), _seq_tile(kn_t, s), _seq_tile(vn_t, s)
        ck, cv = ck_ref[s], cv_ref[s]
        kw_ref[s, 0:p - t_new, :] = ck[t_new:, :]
        kw_ref[s, p - t_new:p, :] = kn8[0:t_new, :]
        vw_ref[s, 0:p - t_new, :] = cv[t_new:, :]
        vw_ref[s, p - t_new:p, :] = vn8[0:t_new, :]
        for j in range(_A_KV_HEADS):
            ks = slice(j * _A_HEAD_DIM, (j + 1) * _A_HEAD_DIM)
            keys.append(jnp.concatenate([ck[:, ks], kn8[:, ks]], axis=0))
            vals.append(jnp.concatenate([cv[:, ks], vn8[:, ks]], axis=0))
            qs.append(jnp.concatenate(
                [q8[:, (j * _A_GROUP + g) * _A_HEAD_DIM:(j * _A_GROUP + g + 1) * _A_HEAD_DIM]
                 for g in range(_A_GROUP)], axis=0))
    scores = [_dot_nt(qj, k) * scale for qj, k in zip(qs, keys)]
    outs = [_sink_softmax_pv(sc, mask, sinks[i % _A_KV_HEADS], v) for i, (sc, v) in enumerate(zip(scores, vals))]
    per_seq = []
    for s in range(nseq):
        heads = []
        for j in range(_A_KV_HEADS):
            o = outs[s * _A_KV_HEADS + j]
            heads.extend(o[g * _SUBLANES:(g + 1) * _SUBLANES, :] for g in range(_A_GROUP))
        per_seq.append(jnp.concatenate(heads, axis=1))
    for t, tile in enumerate(_time_tiles(per_seq, t_new)):
        o_ref[t] = tile


def _swa_sample_call(q, k_new, v_new, cache_k, cache_v, sinks):
    t_new, nseq, nq = q.shape
    nk = k_new.shape[2]
    p = cache_k.shape[1]
    bs = _SAMPLE_SEQS
    toks = lambda i: (0, i, 0)
    seqs = lambda i: (i, 0, 0)
    return pl.pallas_call(
        functools.partial(_swa_sample_kernel, t_new=t_new),
        grid=(nseq // bs,),
        in_specs=[
            pl.BlockSpec(memory_space=pltpu.SMEM),
            pl.BlockSpec((t_new, bs, nq), toks),
            pl.BlockSpec((t_new, bs, nk), toks), pl.BlockSpec((t_new, bs, nk), toks),
            pl.BlockSpec((bs, p, nk), seqs), pl.BlockSpec((bs, p, nk), seqs),
        ],
        out_specs=[pl.BlockSpec((t_new, bs, nq), toks),
                   pl.BlockSpec((bs, p, nk), seqs), pl.BlockSpec((bs, p, nk), seqs)],
        out_shape=[jax.ShapeDtypeStruct((t_new, nseq, nq), _F32),
                   jax.ShapeDtypeStruct(cache_k.shape, _F32), jax.ShapeDtypeStruct(cache_v.shape, _F32)],
        compiler_params=_params("parallel"),
        name="swa_sample",
    )(sinks, q, k_new, v_new, cache_k, cache_v)


def _softplus(z):
    return jnp.maximum(z, 0.0) + jnp.log1p(jnp.exp(-jnp.abs(z)))


def _gelu_tanh(x):
    return 0.5 * x * (1.0 + jnp.tanh(np.float32(np.sqrt(2.0 / np.pi)) * (x + 0.044715 * (x * x * x))))


def _rg_gates(xc, wa_ref, ba_ref, wx_ref, bx_ref, sp_neg_lam):
    r = _sigmoid(_dot(xc, wa_ref[...]) + ba_ref[...])
    i = _sigmoid(_dot(xc, wx_ref[...]) + bx_ref[...])
    log_a = (-_RG_C) * r * sp_neg_lam
    a = jnp.exp(log_a)
    mult = jnp.sqrt(jnp.maximum(1.0 - jnp.exp(2.0 * log_a), 0.0))
    return a, mult * (i * xc)


def _rg_prompt_kernel(xr_ref, xg_ref, cw_ref, cb_ref, wa_ref, ba_ref, wx_ref, bx_ref, lam_ref,
                      o_ref, conv_ref, h_ref, xpad, a_s, b_s):
    t_len, w = xr_ref.shape
    cw = cw_ref.shape[0]
    xpad[0:_SUBLANES, :] = jnp.zeros((_SUBLANES, w), _F32)
    xpad[_SUBLANES:, :] = xr_ref[...]
    sp = _softplus(-lam_ref[...])
    for c in range(t_len // _RG_CHUNK):
        r0 = c * _RG_CHUNK
        xc = cb_ref[...]
        for j in range(cw):
            lo = _SUBLANES + r0 - (cw - 1) + j
            xc = xc + xpad[lo:lo + _RG_CHUNK, :] * cw_ref[j:j + 1, :]
        a, b = _rg_gates(xc, wa_ref, ba_ref, wx_ref, bx_ref, sp)
        a_s[r0:r0 + _RG_CHUNK, :] = a
        b_s[r0:r0 + _RG_CHUNK, :] = b

    row = lax.broadcasted_iota(jnp.int32, (_SUBLANES, w), 0)

    def group(g, h):
        r0 = pl.multiple_of(g * _SUBLANES, _SUBLANES)
        a = a_s[pl.ds(r0, _SUBLANES), :]
        b = b_s[pl.ds(r0, _SUBLANES), :]
        sh = 1
        while sh < _SUBLANES:
            a_prev = jnp.where(row >= sh, pltpu.roll(a, sh, axis=0), 1.0)
            b_prev = jnp.where(row >= sh, pltpu.roll(b, sh, axis=0), 0.0)
            b = a * b_prev + b
            a = a * a_prev
            sh *= 2
        hs = a * h + b
        o_ref[pl.ds(r0, _SUBLANES), :] = _gelu_tanh(xg_ref[pl.ds(r0, _SUBLANES), :]) * hs
        return hs[_SUBLANES - 1:_SUBLANES, :]

    h_last = lax.fori_loop(0, t_len // _SUBLANES, group, jnp.zeros((1, w), _F32), unroll=2)
    h_ref[0] = h_last
    conv_ref[0] = xr_ref[t_len - (cw - 1):t_len, :]


def _rg_prompt_call(xr, xg, conv_w, conv_b, wa_bd, ba, wx_bd, bx, lam, batch):
    n, w = xr.shape
    t_len = n // batch
    cw = conv_w.shape[0]
    seq = lambda b: (b, 0)
    const = lambda b: (0, 0)
    vec = pl.BlockSpec((1, w), const)
    mat = pl.BlockSpec((w, w), const)
    return pl.pallas_call(
        _rg_prompt_kernel,
        grid=(batch,),
        in_specs=[pl.BlockSpec((t_len, w), seq), pl.BlockSpec((t_len, w), seq),
                  pl.BlockSpec((cw, w), const), vec, mat, vec, mat, vec, vec],
        out_specs=[pl.BlockSpec((t_len, w), seq),
                   pl.BlockSpec((1, cw - 1, w), lambda b: (b, 0, 0)),
                   pl.BlockSpec((1, 1, w), lambda b: (b, 0, 0))],
        out_shape=[jax.ShapeDtypeStruct((n, w), _F32),
                   jax.ShapeDtypeStruct((batch, cw - 1, w), _F32),
                   jax.ShapeDtypeStruct((batch, 1, w), _F32)],
        scratch_shapes=[pltpu.VMEM((t_len + _SUBLANES, w), _F32),
                        pltpu.VMEM((t_len, w), _F32), pltpu.VMEM((t_len, w), _F32)],
        compiler_params=_params("parallel"),
        name="rglru_prompt",
    )(xr, xg, conv_w, conv_b, wa_bd, ba, wx_bd, bx, lam)


def _rg_sample_kernel(xr_ref, xg_ref, conv0_ref, h0_ref, cw_ref, cb_ref, wa_ref, ba_ref, wx_ref, bx_ref, lam_ref,
                      o_ref, conv_ref, h_ref):
    cw = cw_ref.shape[0]
    t_new = xr_ref.shape[0]
    sp = _softplus(-lam_ref[...])
    hist = [conv0_ref[j] for j in range(cw - 1)] + [xr_ref[t] for t in range(t_new)]
    h = h0_ref[...]
    for t in range(t_new):
        xc = cb_ref[...]
        for j in range(cw):
            xc = xc + hist[t + j] * cw_ref[j:j + 1, :]
        a, b = _rg_gates(xc, wa_ref, ba_ref, wx_ref, bx_ref, sp)
        h = a * h + b
        o_ref[t] = _gelu_tanh(xg_ref[t]) * h
    h_ref[...] = h
    for j in range(cw - 1):
        conv_ref[j] = hist[t_new + j]


def _rg_sample_call(xr_t, xg_t, conv0_t, h0, conv_w, conv_b, wa_bd, ba, wx_bd, bx, lam):
    t_new, nseq, w = xg_t.shape
    return pl.pallas_call(
        _rg_sample_kernel,
        out_shape=[jax.ShapeDtypeStruct((t_new, nseq, w), _F32), jax.ShapeDtypeStruct(conv0_t.shape, _F32),
                   jax.ShapeDtypeStruct((nseq, w), _F32)],
        compiler_params=pltpu.CompilerParams(vmem_limit_bytes=_VMEM_LIMIT),
        name="rglru_sample",
    )(xr_t, xg_t, conv0_t, h0, conv_w, conv_b, wa_bd, ba, wx_bd, bx, lam)


def _row_bcast(x, r, n):
    return jnp.broadcast_to(x[r:r + 1, :], (n, x.shape[1]))


def _chunk_cumsum(x):
    n_tiles = x.shape[0] // _SUBLANES
    row = lax.broadcasted_iota(jnp.int32, (_SUBLANES, x.shape[1]), 0)
    tiles = []
    carry = None
    for i in range(n_tiles):
        t = x[i * _SUBLANES:(i + 1) * _SUBLANES, :]
        sh = 1
        while sh < _SUBLANES:
            t = t + jnp.where(row >= sh, pltpu.roll(t, sh, axis=0), 0.0)
            sh *= 2
        if carry is not None:
            t = t + carry
        carry = _row_bcast(t, _SUBLANES - 1, _SUBLANES)
        tiles.append(t)
    return tiles[0] if n_tiles == 1 else jnp.concatenate(tiles, axis=0)


def _level_reference(b, m):
    n = b.shape[0]
    if 2 * m >= _SUBLANES:
        pieces = [_row_bcast(b, lo + m - 1, 2 * m) for lo in range(0, n, 2 * m)]
        return pieces[0] if len(pieces) == 1 else jnp.concatenate(pieces, axis=0)
    row = lax.broadcasted_iota(jnp.int32, (_SUBLANES, b.shape[1]), 0)
    tiles = []
    for i in range(n // _SUBLANES):
        t = b[i * _SUBLANES:(i + 1) * _SUBLANES, :]
        ref = None
        for lo in range(0, _SUBLANES, 2 * m):
            piece = _row_bcast(t, lo + m - 1, _SUBLANES)
            ref = piece if ref is None else jnp.where(row >= lo, piece, ref)
        tiles.append(ref)
    return tiles[0] if len(tiles) == 1 else jnp.concatenate(tiles, axis=0)


def _hgrn_gates(fz, lb):
    f = lb + (1.0 - lb) * _sigmoid(fz)
    return jnp.log(f), 1.0 - f


def _hgrn_chunk(q, fz, v, lb, state, n_valid, state_is_vk):
    n, kd = q.shape
    log_f, k = _hgrn_gates(fz, lb)
    if n_valid < n:
        valid = lax.broadcasted_iota(jnp.int32, (n, kd), 0) < n_valid
        log_f = jnp.where(valid, log_f, 0.0)
        k = jnp.where(valid, k, 0.0)
    b = _chunk_cumsum(log_f)
    b_last = _row_bcast(b, n - 1, n)

    q_in = q * jnp.exp(b)
    k_end = k * jnp.exp(b_last - b)
    if state_is_vk:
        o = _dot_nt(q_in, state)
        new_state = jnp.exp(b_last[0:1, :]) * state + _dot_tn(v, k_end)
    else:
        o = _dot(q_in, state)
        decay = jnp.exp(jnp.broadcast_to(b_last[0:1, :], (v.shape[1], kd))).T
        new_state = decay * state + _dot_tn(k_end, v)

    row = lax.broadcasted_iota(jnp.int32, (n, n), 0)
    col = lax.broadcasted_iota(jnp.int32, (n, n), 1)
    upper = lax.broadcasted_iota(jnp.int32, (n, kd), 0)
    scores = jnp.where(row == col, jnp.sum(q * k, axis=-1, keepdims=True), 0.0)
    m = 1
    while m < n_valid:
        e = jnp.exp(-jnp.abs(b - _level_reference(b, m)))
        z = jnp.where((upper & m) != 0, q, k) * e
        pair = jnp.logical_and((row & m) != 0, (row ^ m) >> (m.bit_length() - 1) == col >> (m.bit_length() - 1))
        scores = scores + jnp.where(pair, _dot_nt(z, z), 0.0)
        m *= 2
    return o + _dot(scores, v), new_state


def _lower_bound(logits, layer):
    m = jnp.max(logits, axis=0, keepdims=True)
    e = jnp.exp(logits - m)
    return jnp.sum(e[1:layer + 1, :], axis=0, keepdims=True) / jnp.sum(e, axis=0, keepdims=True)


def _hgrn_out(o, g, gnorm):
    return _rms(o, gnorm) * _silu(g)


def _hgrn_prompt_kernel(q_ref, f_ref, v_ref, g_ref, lbl_ref, gn_ref, o_ref, s_ref,
                        st_scr, st0_scr, *, layer):
    rows_blk, width = q_ref.shape
    kd = width // _HG_HEADS
    n_chunks = rows_blk // _HG_CHUNK
    heads = [slice(h * kd, (h + 1) * kd) for h in range(_HG_HEADS)]
    lb = _lower_bound(lbl_ref[...], layer)
    gn = gn_ref[...]

    @pl.when(pl.program_id(2) == 0)
    def _():
        st_scr[...] = jnp.zeros(st_scr.shape, _F32)

    st0_scr[...] = st_scr[...]

    def chunk_rows(c):
        return pl.ds(pl.multiple_of(c * _HG_CHUNK, _HG_CHUNK), _HG_CHUNK)

    def prepare(c):
        rows = chunk_rows(c)
        log_f, k = _hgrn_gates(f_ref[rows, :], lb)
        b = _chunk_cumsum(log_f)
        b_end = b[_HG_CHUNK - 1:_HG_CHUNK, :]
        dec = jnp.exp(b_end)
        e_b = jnp.exp(b)
        k_start = k / e_b
        ops = ((q_ref[rows, :] * e_b).astype(_BF16), k_start.astype(_BF16), (k_start * dec).astype(_BF16),
               v_ref[rows, :].astype(_BF16), dec)
        return ops, b_end

    row = lax.broadcasted_iota(jnp.int32, (_HG_CHUNK, _HG_CHUNK), 0)
    col = lax.broadcasted_iota(jnp.int32, (_HG_CHUNK, _HG_CHUNK), 1)

    def finish(c, h, o):
        rows = chunk_rows(c)
        o_ref[rows, heads[h]] = _hgrn_out(o, g_ref[rows, heads[h]], gn)

    def contract(c, ops):
        qs, ks, ke, vb, dec = ops
        states = [st_scr[h] for h in range(_HG_HEADS)]
        scores = [_dot_nt(qs[:, l], ks[:, l]) for l in heads]
        carried = [_dot_nt(qs[:, l], st) for l, st in zip(heads, states)]
        incs = [_dot_tn(vb[:, l], ke[:, l]) for l in heads]
        outs = [_dot(jnp.where(row >= col, s, 0.0), vb[:, l]) for s, l in zip(scores, heads)]
        for h, l in enumerate(heads):
            st_scr[h] = states[h] * dec[:, l] + incs[h]
        for h in range(_HG_HEADS):
            finish(c, h, outs[h] + carried[h])

    def step(c, carry):
        ops, min_b = carry
        contract(c, ops)
        ops, b_end = prepare(c + 1)
        return ops, jnp.minimum(min_b, b_end)

    last_ops, min_b = lax.fori_loop(0, n_chunks - 1, step, prepare(0), unroll=5)
    contract(n_chunks - 1, last_ops)

    @pl.when(jnp.min(min_b) <= -_HG_SAFE_LOG_DECAY)
    def _():
        st_scr[...] = st0_scr[...]

        def chunk(c, carry):
            rows = chunk_rows(c)
            for h, l in enumerate(heads):
                o, st_scr[h] = _hgrn_chunk(q_ref[rows, l], f_ref[rows, l], v_ref[rows, l],
                                           lb[:, l], st_scr[h], _HG_CHUNK, True)
                finish(c, h, o)
            return carry
        lax.fori_loop(0, n_chunks, chunk, 0)

    @pl.when(pl.program_id(2) == pl.num_programs(2) - 1)
    def _():
        for h in range(_HG_HEADS):
            s_ref[0, h] = st_scr[h].T


def _hgrn_prompt_call(q, fz, v, g, lb_logits, gnorm, batch, layer):
    n, width = q.shape
    t_len = n // batch
    kd = width // _C_HEADS
    gw = _HG_HEADS * kd
    rows_blk = min(_HG_ROWS, t_len)
    nt = t_len // rows_blk
    blk = pl.BlockSpec((rows_blk, gw), lambda b, h, t: (b * nt + t, h))
    return pl.pallas_call(
        functools.partial(_hgrn_prompt_kernel, layer=layer),
        grid=(batch, _C_HEADS // _HG_HEADS, nt),
        in_specs=[blk, blk, blk, blk,
                  pl.BlockSpec((lb_logits.shape[0], gw), lambda b, h, t: (0, h)),
                  pl.BlockSpec((1, kd), lambda b, h, t: (0, 0))],
        out_specs=[blk, pl.BlockSpec((1, _HG_HEADS, kd, kd), lambda b, h, t: (b, h, 0, 0))],
        out_shape=[jax.ShapeDtypeStruct((n, width), _F32),
                   jax.ShapeDtypeStruct((batch, _C_HEADS, kd, kd), _F32)],
        scratch_shapes=[pltpu.VMEM((_HG_HEADS, kd, kd), _F32)] * 2,
        compiler_params=_params("parallel", "parallel", "arbitrary"),
        name="hgrn2_prompt",
    )(q, fz, v, g, lb_logits, gnorm)


def _hgrn_sample_kernel(q_ref, f_ref, v_ref, g_ref, lbl_ref, gn_ref, s0_ref, o_ref, s_ref, *, layer):
    t_new, nseq, width = q_ref.shape
    n_heads = s0_ref.shape[1]
    kd = width // n_heads
    lanes = [slice(h * kd, (h + 1) * kd) for h in range(n_heads)]
    lb = _lower_bound(lbl_ref[...], layer)
    gn = gn_ref[...]
    q = [q_ref[t] for t in range(t_new)]
    v = [v_ref[t] for t in range(t_new)]
    keys, b = [], []
    for t in range(t_new):
        log_f, k = _hgrn_gates(f_ref[t], lb)
        keys.append(k)
        b.append(log_f if t == 0 else b[-1] + log_f)

    def per_head_sum(w):
        return jnp.concatenate([jnp.broadcast_to(jnp.sum(w[:, l], axis=-1, keepdims=True), (nseq, kd))
                                for l in lanes], axis=1)

    within = []
    for t in range(t_new):
        acc = per_head_sum(q[t] * keys[t]) * v[t]
        for s in range(t):
            acc = acc + per_head_sum(q[t] * keys[s] * jnp.exp(b[t] - b[s])) * v[s]
        within.append(acc)

    q_in = [q[t] * jnp.exp(b[t]) for t in range(t_new)]
    k_end = [keys[t] * jnp.exp(b[-1] - b[t]) for t in range(t_new)]
    decay = jnp.exp(b[-1])
    pairs = [(s, h) for h in range(n_heads) for s in range(nseq)]
    lhs = [_seq_tile([x[:, lanes[h]] for x in q_in], s) for s, h in pairs]
    k_seq = [_seq_tile([x[:, lanes[h]] for x in k_end], s) for s, h in pairs]
    v_seq = [_seq_tile([x[:, lanes[h]] for x in v], s) for s, h in pairs]
    carried = [_dot(x, s0_ref[s, h]) for x, (s, h) in zip(lhs, pairs)]
    incs = [_dot_tn(ks, vs) for ks, vs in zip(k_seq, v_seq)]
    carried_t = []
    for h in range(n_heads):
        dec_cols = jnp.concatenate([decay[:, lanes[h]], jnp.zeros((kd - nseq, kd), _F32)], axis=0).T
        for s in range(nseq):
            i = h * nseq + s
            s_ref[s, h] = jnp.broadcast_to(dec_cols[:, s:s + 1], (kd, kd)) * s0_ref[s, h] + incs[i]
        carried_t.append(_time_tiles(carried[h * nseq:(h + 1) * nseq], t_new))
    for t in range(t_new):
        o = within[t] + jnp.concatenate([carried_t[h][t] for h in range(n_heads)], axis=1)
        g = g_ref[t]
        o_ref[t] = jnp.concatenate([_hgrn_out(o[:, l], g[:, l], gn) for l in lanes], axis=1)


def _hgrn_sample_call(q, fz, v, g, lb_logits, gnorm, s0, layer):
    t_new, nseq, width = q.shape
    kd = width // _C_HEADS
    gw = _HG_HEADS * kd
    bs = _SAMPLE_SEQS
    blk = pl.BlockSpec((t_new, bs, gw), lambda i, h: (0, i, h))
    st = pl.BlockSpec((bs, _HG_HEADS, kd, kd), lambda i, h: (i, h, 0, 0))
    return pl.pallas_call(
        functools.partial(_hgrn_sample_kernel, layer=layer),
        grid=(nseq // bs, _C_HEADS // _HG_HEADS),
        in_specs=[blk, blk, blk, blk,
                  pl.BlockSpec((lb_logits.shape[0], gw), lambda i, h: (0, h)),
                  pl.BlockSpec((1, kd), lambda i, h: (0, 0)), st],
        out_specs=[blk, st],
        out_shape=[jax.ShapeDtypeStruct((t_new, nseq, width), _F32), jax.ShapeDtypeStruct(s0.shape, _F32)],
        compiler_params=_params("parallel", "parallel"),
        name="hgrn2_sample",
    )(q, fz, v, g, lb_logits, gnorm, s0)


def _rope_tables(pos):
    half = _ROT_DIM // 2
    inv_freq = _ROPE_THETA ** (-jnp.arange(0, _ROT_DIM, 2, dtype=_F32) / _ROT_DIM)
    ang = pos.astype(_F32)[:, None] * inv_freq[None, :]
    cos, sin = jnp.cos(ang), jnp.sin(ang)
    ones = jnp.ones((pos.shape[0], _A_HEAD_DIM - _ROT_DIM), _F32)
    zeros = jnp.zeros((pos.shape[0], _A_HEAD_DIM - half), _F32)
    zeros_h = jnp.zeros((pos.shape[0], half), _F32)
    reps = _LANES // _A_HEAD_DIM
    cos_t = jnp.tile(jnp.concatenate([cos, cos, ones], axis=1), (1, reps))
    sin_lo = jnp.tile(jnp.concatenate([-sin, zeros], axis=1), (1, reps))
    sin_hi = jnp.tile(jnp.concatenate([zeros_h, sin, ones * 0.0], axis=1), (1, reps))
    return cos_t, sin_lo, sin_hi


def _block_diag(w):
    nb, bd, _ = w.shape
    eye = jnp.eye(nb, dtype=w.dtype)
    return (w[:, :, None, :] * eye[:, None, :, None]).reshape(nb * bd, nb * bd)


def kernel(x_prompt, x_sample, c_prompt, c_sample, cache_k_win, cache_v_win, state_conv_rglru,
           state_h_rglru, state_s_hgrn, norm_pre, norm_post, ada_w, ada_b, ffn1_w_in, ffn1_w_out,
           ffn2_w_in, ffn2_w_out, even_w_in, even_w_out, attn_sinks, rg_conv_w, rg_conv_b, rg_wa,
           rg_ba, rg_wx, rg_bx, rg_lambda, odd_w_in, odd_w_out, hgrn_lb_logits, hgrn_gnorm):
    bp, tp, d = x_prompt.shape
    bs, ts, _ = x_sample.shape
    depth = norm_pre.shape[0]
    n_sub = depth * _N_SUB
    nk = _A_KV_HEADS * _A_HEAD_DIM
    win = cache_k_win.shape[2]
    cw = rg_conv_w.shape[1]
    bw = rg_conv_w.shape[2]
    kd = state_s_hgrn.shape[3]

    cast = lambda w: w.astype(_BF16)
    ffn1_in, ffn1_out, ffn2_in, ffn2_out = cast(ffn1_w_in), cast(ffn1_w_out), cast(ffn2_w_in), cast(ffn2_w_out)
    ev_in, ev_out, od_in, od_out = cast(even_w_in), cast(even_w_out), cast(odd_w_in), cast(odd_w_out)
    gpre = norm_pre.reshape(n_sub, 1, d)
    gpost = norm_post.reshape(n_sub, 1, d)

    mod = _ada_call(jnp.concatenate([c_prompt, c_sample], axis=0),
                    ada_w.reshape(n_sub, d, 3 * d), ada_b.reshape(n_sub, 1, 3 * d))
    mod_p = mod[:, :bp].reshape(n_sub, bp, 1, 3 * d)
    mod_s = mod[:, bp:].reshape(n_sub, 1, bs, 3 * d)

    tabs_p = _rope_tables(jnp.arange(tp, dtype=jnp.int32))
    tabs_s = tuple(jnp.repeat(t, bs, axis=0) for t in _rope_tables(_PAST_LEN + jnp.arange(ts, dtype=jnp.int32)))

    time_major = lambda a: a.reshape(ts, bs, a.shape[-1])
    groups = {
        "p": dict(x=x_prompt.reshape(bp * tp, d), mod=mod_p, tps=tp // _ROW_TILE, tabs=tabs_p,
                  tab_tiles=tp // _ROW_TILE),
        "s": dict(x=x_sample.transpose(1, 0, 2).reshape(ts * bs, d), mod=mod_s, tps=1, tabs=tabs_s, tab_tiles=1),
    }
    outs = {g: dict(k=[], v=[], conv=[], h=[], s=[]) for g in groups}

    for l in range(depth):
        s0, s1, s2 = l * _N_SUB, l * _N_SUB + 1, l * _N_SUB + 2
        for name, grp in groups.items():
            x, mod4, tps = grp["x"], grp["mod"], grp["tps"]
            x = _ffn_call(x, mod4, s0, l, gpre, ffn1_in, ffn1_out, gpost, 0.5, tps)
            if l % 2 == 0:
                e = l // 2
                q, k, v, xg, xr = _even_in_call(x, mod4, s1, e, gpre, ev_in, grp["tabs"], tps, grp["tab_tiles"])
                wa_bd, wx_bd = cast(_block_diag(rg_wa[e])), cast(_block_diag(rg_wx[e]))
                vecs = [a[e].reshape(1, bw) for a in (rg_conv_b, rg_ba, rg_bx, rg_lambda)]
                if name == "p":
                    o_a = _swa_prompt_call(q, k, v, attn_sinks[e], bp)
                    o_b, conv, h_last = _rg_prompt_call(xr, xg, rg_conv_w[e], vecs[0], wa_bd, vecs[1], wx_bd,
                                                        vecs[2], vecs[3], bp)
                    outs[name]["k"].append(k.reshape(bp, tp, _A_KV_HEADS, _A_HEAD_DIM)[:, tp - win:])
                    outs[name]["v"].append(v.reshape(bp, tp, _A_KV_HEADS, _A_HEAD_DIM)[:, tp - win:])
                    outs[name]["h"].append(h_last.reshape(bp, bw))
                else:
                    o_a, kw, vw = _swa_sample_call(time_major(q), time_major(k), time_major(v),
                                                   cache_k_win[e].reshape(bs, win, nk),
                                                   cache_v_win[e].reshape(bs, win, nk), attn_sinks[e])
                    o_b, conv_t, h_last = _rg_sample_call(time_major(xr), time_major(xg),
                                                          state_conv_rglru[e].transpose(1, 0, 2), state_h_rglru[e],
                                                          rg_conv_w[e], vecs[0], wa_bd, vecs[1], wx_bd, vecs[2], vecs[3])
                    o_a, o_b = o_a.reshape(ts * bs, -1), o_b.reshape(ts * bs, bw)
                    conv = conv_t.transpose(1, 0, 2)
                    outs[name]["k"].append(kw.reshape(bs, win, _A_KV_HEADS, _A_HEAD_DIM))
                    outs[name]["v"].append(vw.reshape(bs, win, _A_KV_HEADS, _A_HEAD_DIM))
                    outs[name]["h"].append(h_last)
                outs[name]["conv"].append(conv)
                mixer = (s1, e, ev_out, [o_a, o_b])
            else:
                o = l // 2
                q, fz, v, g = _odd_in_call(x, mod4, s1, o, gpre, od_in, tps)
                gn = hgrn_gnorm[o].reshape(1, kd)
                if name == "p":
                    y, s_last = _hgrn_prompt_call(q, fz, v, g, hgrn_lb_logits, gn, bp, l)
                else:
                    y, s_last = _hgrn_sample_call(time_major(q), time_major(fz), time_major(v), time_major(g),
                                                  hgrn_lb_logits, gn, state_s_hgrn[o], l)
                    y = y.reshape(ts * bs, -1)
                outs[name]["s"].append(s_last)
                mixer = (s1, o, od_out, [y])
            x = _ffn_call(x, mod4, s2, l, gpre, ffn2_in, ffn2_out, gpost, 0.5, tps, mixer)
            grp["x"] = x

    ys = {"p": groups["p"]["x"].reshape(bp, tp, d), "s": groups["s"]["x"].reshape(ts, bs, d).transpose(1, 0, 2)}
    res = []
    for name in ("p", "s"):
        o = outs[name]
        res.append((jnp.stack(o["k"]), jnp.stack(o["v"]), jnp.stack(o["conv"]), jnp.stack(o["h"]), jnp.stack(o["s"])))
    return (ys["p"], ys["s"]) + res[0] + res[1]
```

```python
import functools

import jax
import jax.numpy as jnp
import numpy as np
from jax import lax
from jax.experimental import pallas as pl
from jax.experimental.pallas import tpu as pltpu

_F32 = jnp.float32
_BF16 = jnp.bfloat16

_EPS = 1e-6
_A_HEADS = 8
_A_KV_HEADS = 2
_A_HEAD_DIM = 64
_A_GROUP = _A_HEADS // _A_KV_HEADS
_WINDOW = 128
_ROPE_THETA = 500000.0
_ROT_DIM = _A_HEAD_DIM // 4
_RG_C = 8.0
_C_HEADS = 8
_PAST_LEN = 16384
_N_SUB = 3

_LANES = 128
_SUBLANES = 8
_VMEM_BYTES = 64 * 1024 * 1024
_VMEM_LIMIT = _VMEM_BYTES * 3 // 4
_VMEM_LIMIT_FFN = _VMEM_BYTES * 7 // 8

_ROW_TILE = 512
_FFN_TILE = 1024
_FF_CHUNK = 256
_RG_CHUNK = 256
_HG_CHUNK = 64
_HG_HEADS = 4
_HG_ROWS = 1024
_HG_SAFE_LOG_DECAY = 80.0
_SAMPLE_SEQS = 8
_SWA_QBLOCKS = 4


def _dot(a, b):
    return jnp.dot(a.astype(_BF16), b.astype(_BF16), preferred_element_type=_F32)


def _dot_nt(a, b):
    return lax.dot_general(a.astype(_BF16), b.astype(_BF16), (((1,), (1,)), ((), ())),
                           preferred_element_type=_F32)


def _dot_tn(a, b):
    return lax.dot_general(a.astype(_BF16), b.astype(_BF16), (((0,), (0,)), ((), ())),
                           preferred_element_type=_F32)


def _sigmoid(x):
    return 1.0 / (1.0 + jnp.exp(-x))


def _silu(x):
    return x * _sigmoid(x)


def _rms(x, gain):
    inv = lax.rsqrt(jnp.mean(x * x, axis=-1, keepdims=True) + _EPS)
    return x * inv * gain


def _per_seq(a, r):
    n = a.shape[0]
    return a if r in (1, n) else a.reshape(n // r, r, a.shape[1])


def _pre(x, mod, gain):
    n, d = x.shape
    h = _rms(_per_seq(x, mod.shape[0]), gain) * (1.0 + mod[:, d:2 * d]) + mod[:, :d]
    return h.reshape(n, d)


def _post(x, y, mod, gain, res_w):
    n, d = x.shape
    r = mod.shape[0]
    out = _per_seq(x, r) + (res_w * (1.0 + mod[:, 2 * d:])) * _rms(_per_seq(y, r), gain)
    return out.reshape(n, d)


def _params(*sem):
    return pltpu.CompilerParams(dimension_semantics=sem, vmem_limit_bytes=_VMEM_LIMIT)


def _ada_kernel(c_ref, w_ref, b_ref, o_ref):
    o_ref[...] = _dot(_silu(c_ref[...]), w_ref[...]) + b_ref[...]


def _ada_call(c_all, ada_w, ada_b):
    m, d = c_all.shape
    n_sub = ada_w.shape[0]
    n = ada_w.shape[-1]
    tn = n // 2
    return pl.pallas_call(
        _ada_kernel,
        grid=(n_sub, n // tn),
        in_specs=[
            pl.BlockSpec((m, d), lambda s, j: (0, 0)),
            pl.BlockSpec((None, d, tn), lambda s, j: (s, 0, j)),
            pl.BlockSpec((None, 1, tn), lambda s, j: (s, 0, j)),
        ],
        out_specs=pl.BlockSpec((None, m, tn), lambda s, j: (s, 0, j)),
        out_shape=jax.ShapeDtypeStruct((n_sub, m, n), _F32),
        compiler_params=_params("parallel", "parallel"),
        name="ada_mod",
    )(c_all, ada_w, ada_b)


def _mixer_residual(x, mod, gpost, w_ref, act_refs):
    y = None
    off = 0
    for a_ref in act_refs:
        k = a_ref.shape[1]
        t = _dot(a_ref[...], w_ref[off:off + k, :])
        y = t if y is None else y + t
        off += k
    return _post(x, y, mod, gpost, 1.0)


def _ffn_kernel(x_ref, mod_ref, gpre_ref, win_ref, wout_ref, gpost_ref, *refs, res_w):
    o_ref = refs[-1]
    mod = mod_ref[0]
    dff = wout_ref.shape[0]
    sub = min(_ROW_TILE, x_ref.shape[0])
    for s in range(x_ref.shape[0] // sub):
        rows = slice(s * sub, (s + 1) * sub)
        x = x_ref[rows, :]
        if len(refs) > 1:
            mmod_ref, mgpost_ref, mw_ref = refs[:3]
            x = _mixer_residual(x, mmod_ref[0], mgpost_ref[...], mw_ref, [a.at[rows, :] for a in refs[3:-1]])
        h = _pre(x, mod, gpre_ref[...]).astype(_BF16)
        acc = jnp.zeros(x.shape, _F32)
        for j in range(dff // _FF_CHUNK):
            lo = j * _FF_CHUNK
            g = jnp.dot(h, win_ref[:, lo:lo + _FF_CHUNK], preferred_element_type=_F32)
            u = jnp.dot(h, win_ref[:, dff + lo:dff + lo + _FF_CHUNK], preferred_element_type=_F32)
            acc = acc + _dot(_silu(g) * u, wout_ref[lo:lo + _FF_CHUNK, :])
        o_ref[rows, :] = _post(x, acc, mod, gpost_ref[...], res_w)


def _ffn_call(x, mod4, sub, layer, gpre, w_in, w_out, gpost, res_w, seq_rows, mixer=None):
    n, d = x.shape
    dff = w_out.shape[1]
    r = mod4.shape[2]
    tm = min(_FFN_TILE, n)
    tiles_per_seq = seq_rows // tm
    row = lambda i: (i, 0)
    mod_spec = lambda s: pl.BlockSpec((None, 1, r, 3 * d), lambda i: (s, i // tiles_per_seq, 0, 0))
    gain_spec = lambda s: pl.BlockSpec((None, 1, d), lambda i: (s, 0, 0))
    in_specs = [
        pl.BlockSpec((tm, d), row), mod_spec(sub), gain_spec(sub),
        pl.BlockSpec((None, d, 2 * dff), lambda i: (layer, 0, 0), pipeline_mode=pl.Buffered(1)),
        pl.BlockSpec((None, dff, d), lambda i: (layer, 0, 0), pipeline_mode=pl.Buffered(1)),
        gain_spec(sub),
    ]
    args = [x, mod4, gpre, w_in, w_out, gpost]
    if mixer is not None:
        msub, widx, mw, acts = mixer
        in_specs += [mod_spec(msub), gain_spec(msub),
                     pl.BlockSpec((None, mw.shape[1], d), lambda i: (widx, 0, 0), pipeline_mode=pl.Buffered(1))]
        in_specs += [pl.BlockSpec((tm, a.shape[1]), row) for a in acts]
        args += [mod4, gpost, mw] + list(acts)
    return pl.pallas_call(
        functools.partial(_ffn_kernel, res_w=res_w),
        grid=(n // tm,),
        in_specs=in_specs,
        out_specs=pl.BlockSpec((tm, d), row),
        out_shape=jax.ShapeDtypeStruct((n, d), _F32),
        compiler_params=pltpu.CompilerParams(dimension_semantics=("parallel",), vmem_limit_bytes=_VMEM_LIMIT_FFN),
        name="ffn_sublayer" if mixer is None else "mixer_out_ffn",
    )(*args)


def _rope(x, cos, sin_lo, sin_hi):
    outs = []
    for j in range(x.shape[1] // _LANES):
        xc = x[:, j * _LANES:(j + 1) * _LANES]
        nxt = pltpu.roll(xc, _LANES - _ROT_DIM // 2, axis=1)
        prv = pltpu.roll(xc, _ROT_DIM // 2, axis=1)
        outs.append(xc * cos + nxt * sin_lo + prv * sin_hi)
    return outs[0] if len(outs) == 1 else jnp.concatenate(outs, axis=1)


def _even_in_kernel(x_ref, mod_ref, gpre_ref, w_ref, cos_ref, slo_ref, shi_ref,
                    q_ref, k_ref, v_ref, xg_ref, xr_ref):
    h = _pre(x_ref[...], mod_ref[0], gpre_ref[...])
    y = _dot(h, w_ref[...])
    nq, nk, nw = q_ref.shape[1], k_ref.shape[1], xg_ref.shape[1]
    cos, slo, shi = cos_ref[...], slo_ref[...], shi_ref[...]
    q_ref[...] = _rope(y[:, :nq], cos, slo, shi)
    k_ref[...] = _rope(y[:, nq:nq + nk], cos, slo, shi)
    v_ref[...] = y[:, nq + nk:nq + 2 * nk]
    xg_ref[...] = y[:, nq + 2 * nk:nq + 2 * nk + nw]
    xr_ref[...] = y[:, nq + 2 * nk + nw:]


def _even_in_call(x, mod4, sub, e, gpre, w_in, rope_tabs, seq_rows):
    n, d = x.shape
    r = mod4.shape[2]
    tm = min(_ROW_TILE, n)
    tiles_per_seq = tab_tiles = seq_rows // tm
    nq = _A_HEADS * _A_HEAD_DIM
    nk = _A_KV_HEADS * _A_HEAD_DIM
    nw = (w_in.shape[-1] - nq - 2 * nk) // 2
    row = lambda i: (i, 0)
    tab = pl.BlockSpec((tm, _LANES), lambda i: (i % tab_tiles, 0))
    return pl.pallas_call(
        _even_in_kernel,
        grid=(n // tm,),
        in_specs=[
            pl.BlockSpec((tm, d), row),
            pl.BlockSpec((None, 1, r, 3 * d), lambda i: (sub, i // tiles_per_seq, 0, 0)),
            pl.BlockSpec((None, 1, d), lambda i: (sub, 0, 0)),
            pl.BlockSpec((None, d, w_in.shape[-1]), lambda i: (e, 0, 0)),
            tab, tab, tab,
        ],
        out_specs=[pl.BlockSpec((tm, nq), row), pl.BlockSpec((tm, nk), row), pl.BlockSpec((tm, nk), row),
                   pl.BlockSpec((tm, nw), row), pl.BlockSpec((tm, nw), row)],
        out_shape=[jax.ShapeDtypeStruct((n, nq), _F32), jax.ShapeDtypeStruct((n, nk), _F32),
                   jax.ShapeDtypeStruct((n, nk), _F32), jax.ShapeDtypeStruct((n, nw), _F32),
                   jax.ShapeDtypeStruct((n, nw), _F32)],
        compiler_params=_params("parallel"),
        name="even_in_proj",
    )(x, mod4, gpre, w_in, *rope_tabs)


def _odd_in_kernel(x_ref, mod_ref, gpre_ref, w_ref, q_ref, f_ref, v_ref, g_ref):
    h = _pre(x_ref[...], mod_ref[0], gpre_ref[...])
    y = _dot(h, w_ref[...])
    n = q_ref.shape[1]
    q_ref[...] = y[:, :n]
    f_ref[...] = y[:, n:2 * n]
    v_ref[...] = y[:, 2 * n:3 * n]
    g_ref[...] = y[:, 3 * n:]


def _odd_in_call(x, mod4, sub, o, gpre, w_in, seq_rows):
    n, d = x.shape
    r = mod4.shape[2]
    tm = min(_ROW_TILE, n)
    tiles_per_seq = seq_rows // tm
    nw = w_in.shape[-1] // 4
    row = lambda i: (i, 0)
    return pl.pallas_call(
        _odd_in_kernel,
        grid=(n // tm,),
        in_specs=[
            pl.BlockSpec((tm, d), row),
            pl.BlockSpec((None, 1, r, 3 * d), lambda i: (sub, i // tiles_per_seq, 0, 0)),
            pl.BlockSpec((None, 1, d), lambda i: (sub, 0, 0)),
            pl.BlockSpec((None, d, 4 * nw), lambda i: (o, 0, 0)),
        ],
        out_specs=[pl.BlockSpec((tm, nw), row)] * 4,
        out_shape=[jax.ShapeDtypeStruct((n, nw), _F32)] * 4,
        compiler_params=_params("parallel"),
        name="odd_in_proj",
    )(x, mod4, gpre, w_in)


def _seq_tile(time_tiles, s):
    row = lax.broadcasted_iota(jnp.int32, time_tiles[0].shape, 0)
    out = jnp.zeros(time_tiles[0].shape, time_tiles[0].dtype)
    for t, x in enumerate(time_tiles):
        shift = (t - s) % _SUBLANES
        out = jnp.where(row == t, pltpu.roll(x, shift, axis=0) if shift else x, out)
    return out


def _time_tiles(seq_tiles, n_t):
    row = lax.broadcasted_iota(jnp.int32, seq_tiles[0].shape, 0)
    outs = []
    for t in range(n_t):
        acc = jnp.zeros(seq_tiles[0].shape, seq_tiles[0].dtype)
        for s, x in enumerate(seq_tiles):
            shift = (s - t) % _SUBLANES
            acc = jnp.where(row == s, pltpu.roll(x, shift, axis=0) if shift else x, acc)
        outs.append(acc)
    return outs


def _sink_softmax_pv(s, mask, sink, v):
    s = jnp.where(mask, s, -jnp.inf)
    m = jnp.maximum(jnp.max(s, axis=-1, keepdims=True), sink)
    p = jnp.exp(s - m)
    denom = jnp.sum(p, axis=-1, keepdims=True) + jnp.exp(sink - m)
    return _dot(p, v) / denom


def _swa_prompt_kernel(sink_ref, q_ref, kp_ref, kc_ref, vp_ref, vc_ref, o_ref):
    w = _WINDOW
    hd = _A_HEAD_DIM
    assert _LANES == 2 * hd and _A_GROUP % 2 == 0
    log2e = np.float32(np.log2(np.e))
    scale = np.float32(1.0 / np.sqrt(hd)) * log2e
    low = lax.broadcasted_iota(jnp.int32, (2 * w, _LANES), 1) < hd
    ones_lo = jnp.where(low, 1.0, 0.0).astype(_BF16)
    ones_hi = jnp.where(low, 0.0, 1.0).astype(_BF16)
    low_q = lax.broadcasted_iota(jnp.int32, (w, _LANES), 1) < hd
    row = lax.broadcasted_iota(jnp.int32, (w, 4 * w), 0)
    col = lax.broadcasted_iota(jnp.int32, (w, 4 * w), 1) & (2 * w - 1)
    for qb in range(q_ref.shape[0] // w):
        rows = slice(qb * w, (qb + 1) * w)
        if qb == 0:
            k2 = jnp.concatenate([kp_ref[...], kc_ref[0:w, :]], axis=0)
            v2 = jnp.concatenate([vp_ref[...], vc_ref[0:w, :]], axis=0)
            first = jnp.where(pl.program_id(1) > 0, 0, w)
        else:
            k2 = kc_ref[(qb - 1) * w:(qb + 1) * w, :]
            v2 = vc_ref[(qb - 1) * w:(qb + 1) * w, :]
            first = 0
        mask = jnp.logical_and(col > jnp.maximum(row, first - 1), col <= row + w)
        keys, vals = [], []
        for j in range(_A_KV_HEADS):
            own_k = jnp.where(low, k2, 0.0) if j == 0 else jnp.where(low, 0.0, k2)
            own_v = jnp.where(low, v2, 0.0) if j == 0 else jnp.where(low, 0.0, v2)
            oth_k = pltpu.roll(own_k, hd, axis=1)
            oth_v = pltpu.roll(own_v, hd, axis=1)
            lo_k, hi_k = (own_k, oth_k) if j == 0 else (oth_k, own_k)
            lo_v, hi_v = (own_v, oth_v) if j == 0 else (oth_v, own_v)
            keys.append(jnp.concatenate([lo_k, hi_k], axis=0).astype(_BF16))
            vals.append(jnp.concatenate([jnp.concatenate([lo_v.astype(_BF16), ones_lo], axis=1),
                                         jnp.concatenate([hi_v.astype(_BF16), ones_hi], axis=1)], axis=0))
        pairs = range(_A_HEADS // 2)
        kv_of = [(2 * p) // _A_GROUP for p in pairs]
        scores = [_dot_nt(q_ref[rows, p * _LANES:(p + 1) * _LANES] * scale, keys[kv_of[p]]) for p in pairs]
        probs, sink_terms = [], []
        for p in pairs:
            s = jnp.where(mask, scores[p], -jnp.inf)
            halves = []
            for i in range(2):
                sh = s[:, i * 2 * w:(i + 1) * 2 * w]
                sink = sink_ref[2 * p + i] * log2e
                m = jnp.maximum(jnp.max(sh, axis=-1, keepdims=True), sink)
                halves.append((jnp.exp2(sh - m), jnp.exp2(sink - m)))
            probs.append(jnp.concatenate([halves[0][0], halves[1][0]], axis=1))
            sink_terms.append(jnp.where(low_q, halves[0][1], halves[1][1]))
        for p in pairs:
            r = _dot(probs[p], vals[kv_of[p]])
            o_ref[rows, p * _LANES:(p + 1) * _LANES] = r[:, :_LANES] / (r[:, _LANES:] + sink_terms[p])


def _swa_prompt_call(q, k, v, sinks, batch):
    n, nq = q.shape
    nk = k.shape[1]
    w = _WINDOW
    qb = _SWA_QBLOCKS
    nb = n // batch // (w * qb)
    cur = lambda b, i: (b * nb + i, 0)
    prev = lambda b, i: ((b * nb + i) * qb - jnp.minimum(i, 1), 0)
    return pl.pallas_call(
        _swa_prompt_kernel,
        grid=(batch, nb),
        in_specs=[
            pl.BlockSpec(memory_space=pltpu.SMEM),
            pl.BlockSpec((qb * w, nq), cur),
            pl.BlockSpec((w, nk), prev), pl.BlockSpec((qb * w, nk), cur),
            pl.BlockSpec((w, nk), prev), pl.BlockSpec((qb * w, nk), cur),
        ],
        out_specs=pl.BlockSpec((qb * w, nq), cur),
        out_shape=jax.ShapeDtypeStruct((n, nq), _F32),
        compiler_params=_params("parallel", "parallel"),
        name="swa_prompt",
    )(sinks, q, k, k, v, v)


def _swa_sample_kernel(sink_ref, q_ref, kn_ref, vn_ref, ck_ref, cv_ref, o_ref, kw_ref, vw_ref, *, t_new):
    p = ck_ref.shape[1]
    scale = np.float32(1.0 / np.sqrt(_A_HEAD_DIM))
    rows = _A_GROUP * _SUBLANES
    t = lax.broadcasted_iota(jnp.int32, (rows, p + _SUBLANES), 0) & (_SUBLANES - 1)
    c = lax.broadcasted_iota(jnp.int32, (rows, p + _SUBLANES), 1)
    mask = jnp.logical_and(c <= t + p, c > t + p - _WINDOW)
    g_of_row = lax.broadcasted_iota(jnp.int32, (rows, 1), 0) >> (_SUBLANES.bit_length() - 1)
    nseq = ck_ref.shape[0]
    q_t = [q_ref[t] for t in range(t_new)]
    kn_t = [kn_ref[t] for t in range(t_new)]
    vn_t = [vn_ref[t] for t in range(t_new)]
    sinks = []
    for j in range(_A_KV_HEADS):
        sink = jnp.zeros((rows, 1), _F32)
        for g in range(_A_GROUP):
            sink = jnp.where(g_of_row == g, sink_ref[j * _A_GROUP + g], sink)
        sinks.append(sink)
    new_k, new_v, qs, keys, vals = [], [], [], [], []
    for s in range(nseq):
        q8, kn8, vn8 = _seq_tile(q_t, s), _seq_tile(kn_t, s), _seq_tile(vn_t, s)
        ck, cv = ck_ref[s], cv_ref[s]
        kw_ref[s, 0:p - t_new, :] = ck[t_new:, :]
        kw_ref[s, p - t_new:p, :] = kn8[0:t_new, :]
        vw_ref[s, 0:p - t_new, :] = cv[t_new:, :]
        vw_ref[s, p - t_new:p, :] = vn8[0:t_new, :]
        for j in range(_A_KV_HEADS):
            ks = slice(j * _A_HEAD_DIM, (j + 1) * _A_HEAD_DIM)
            keys.append(jnp.concatenate([ck[:, ks], kn8[:, ks]], axis=0))
            vals.append(jnp.concatenate([cv[:, ks], vn8[:, ks]], axis=0))
            qs.append(jnp.concatenate(
                [q8[:, (j * _A_GROUP + g) * _A_HEAD_DIM:(j * _A_GROUP + g + 1) * _A_HEAD_DIM]
                 for g in range(_A_GROUP)], axis=0))
    scores = [_dot_nt(qj, k) * scale for qj, k in zip(qs, keys)]
    outs = [_sink_softmax_pv(sc, mask, sinks[i % _A_KV_HEADS], v) for i, (sc, v) in enumerate(zip(scores, vals))]
    per_seq = []
    for s in range(nseq):
        heads = []
        for j in range(_A_KV_HEADS):
            o = outs[s * _A_KV_HEADS + j]
            heads.extend(o[g * _SUBLANES:(g + 1) * _SUBLANES, :] for g in range(_A_GROUP))
        per_seq.append(jnp.concatenate(heads, axis=1))
    for t, tile in enumerate(_time_tiles(per_seq, t_new)):
        o_ref[t] = tile


def _swa_sample_call(q, k_new, v_new, cache_k, cache_v, sinks):
    t_new, nseq, nq = q.shape
    nk = k_new.shape[2]
    p = cache_k.shape[1]
    bs = _SAMPLE_SEQS
    toks = lambda i: (0, i, 0)
    seqs = lambda i: (i, 0, 0)
    return pl.pallas_call(
        functools.partial(_swa_sample_kernel, t_new=t_new),
        grid=(nseq // bs,),
        in_specs=[
            pl.BlockSpec(memory_space=pltpu.SMEM),
            pl.BlockSpec((t_new, bs, nq), toks),
            pl.BlockSpec((t_new, bs, nk), toks), pl.BlockSpec((t_new, bs, nk), toks),
            pl.BlockSpec((bs, p, nk), seqs), pl.BlockSpec((bs, p, nk), seqs),
        ],
        out_specs=[pl.BlockSpec((t_new, bs, nq), toks),
                   pl.BlockSpec((bs, p, nk), seqs), pl.BlockSpec((bs, p, nk), seqs)],
        out_shape=[jax.ShapeDtypeStruct((t_new, nseq, nq), _F32),
                   jax.ShapeDtypeStruct(cache_k.shape, _F32), jax.ShapeDtypeStruct(cache_v.shape, _F32)],
        compiler_params=_params("parallel"),
        name="swa_sample",
    )(sinks, q, k_new, v_new, cache_k, cache_v)


def _softplus(z):
    return jnp.maximum(z, 0.0) + jnp.log1p(jnp.exp(-jnp.abs(z)))


def _gelu_tanh(x):
    return 0.5 * x * (1.0 + jnp.tanh(np.float32(np.sqrt(2.0 / np.pi)) * (x + 0.044715 * (x * x * x))))


def _rg_gates(xc, wa_ref, ba_ref, wx_ref, bx_ref, sp_neg_lam):
    r = _sigmoid(_dot(xc, wa_ref[...]) + ba_ref[...])
    i = _sigmoid(_dot(xc, wx_ref[...]) + bx_ref[...])
    a = jnp.exp((-_RG_C) * r * sp_neg_lam)
    gap = jnp.maximum(1.0 - a * a, 0.0)
    mult = jnp.where(gap > 0.0, gap * lax.rsqrt(gap), 0.0)
    return a, mult * (i * xc)


def _rg_prompt_kernel(xr_ref, xg_ref, cw_ref, cb_ref, wa_ref, ba_ref, wx_ref, bx_ref, lam_ref,
                      o_ref, conv_ref, h_ref, xpad, a_s, b_s):
    t_len, w = xr_ref.shape
    cw = cw_ref.shape[0]
    xpad[0:_SUBLANES, :] = jnp.zeros((_SUBLANES, w), _F32)
    xpad[_SUBLANES:, :] = xr_ref[...]
    sp = _softplus(-lam_ref[...])
    for c in range(t_len // _RG_CHUNK):
        r0 = c * _RG_CHUNK
        xc = cb_ref[...]
        for j in range(cw):
            lo = _SUBLANES + r0 - (cw - 1) + j
            xc = xc + xpad[lo:lo + _RG_CHUNK, :] * cw_ref[j:j + 1, :]
        a, b = _rg_gates(xc, wa_ref, ba_ref, wx_ref, bx_ref, sp)
        a_s[r0:r0 + _RG_CHUNK, :] = a
        b_s[r0:r0 + _RG_CHUNK, :] = b

    row = lax.broadcasted_iota(jnp.int32, (_SUBLANES, w), 0)

    def group(g, h):
        r0 = pl.multiple_of(g * _SUBLANES, _SUBLANES)
        a = a_s[pl.ds(r0, _SUBLANES), :]
        b = b_s[pl.ds(r0, _SUBLANES), :]
        sh = 1
        while sh < _SUBLANES:
            a_prev = jnp.where(row >= sh, pltpu.roll(a, sh, axis=0), 1.0)
            b_prev = jnp.where(row >= sh, pltpu.roll(b, sh, axis=0), 0.0)
            b = a * b_prev + b
            a = a * a_prev
            sh *= 2
        hs = a * h + b
        o_ref[pl.ds(r0, _SUBLANES), :] = _gelu_tanh(xg_ref[pl.ds(r0, _SUBLANES), :]) * hs
        return hs[_SUBLANES - 1:_SUBLANES, :]

    h_last = lax.fori_loop(0, t_len // _SUBLANES, group, jnp.zeros((1, w), _F32), unroll=2)
    h_ref[0] = h_last
    conv_ref[0] = xr_ref[t_len - (cw - 1):t_len, :]


def _rg_prompt_call(xr, xg, conv_w, conv_b, wa_bd, ba, wx_bd, bx, lam, batch):
    n, w = xr.shape
    t_len = n // batch
    cw = conv_w.shape[0]
    seq = lambda b: (b, 0)
    const = lambda b: (0, 0)
    vec = pl.BlockSpec((1, w), const)
    mat = pl.BlockSpec((w, w), const)
    return pl.pallas_call(
        _rg_prompt_kernel,
        grid=(batch,),
        in_specs=[pl.BlockSpec((t_len, w), seq), pl.BlockSpec((t_len, w), seq),
                  pl.BlockSpec((cw, w), const), vec, mat, vec, mat, vec, vec],
        out_specs=[pl.BlockSpec((t_len, w), seq),
                   pl.BlockSpec((1, cw - 1, w), lambda b: (b, 0, 0)),
                   pl.BlockSpec((1, 1, w), lambda b: (b, 0, 0))],
        out_shape=[jax.ShapeDtypeStruct((n, w), _F32),
                   jax.ShapeDtypeStruct((batch, cw - 1, w), _F32),
                   jax.ShapeDtypeStruct((batch, 1, w), _F32)],
        scratch_shapes=[pltpu.VMEM((t_len + _SUBLANES, w), _F32),
                        pltpu.VMEM((t_len, w), _F32), pltpu.VMEM((t_len, w), _F32)],
        compiler_params=_params("parallel"),
        name="rglru_prompt",
    )(xr, xg, conv_w, conv_b, wa_bd, ba, wx_bd, bx, lam)


def _rg_sample_kernel(xr_ref, xg_ref, conv0_ref, h0_ref, cw_ref, cb_ref, wa_ref, ba_ref, wx_ref, bx_ref, lam_ref,
                      o_ref, conv_ref, h_ref):
    cw = cw_ref.shape[0]
    t_new = xr_ref.shape[0]
    sp = _softplus(-lam_ref[...])
    hist = [conv0_ref[j] for j in range(cw - 1)] + [xr_ref[t] for t in range(t_new)]
    h = h0_ref[...]
    for t in range(t_new):
        xc = cb_ref[...]
        for j in range(cw):
            xc = xc + hist[t + j] * cw_ref[j:j + 1, :]
        a, b = _rg_gates(xc, wa_ref, ba_ref, wx_ref, bx_ref, sp)
        h = a * h + b
        o_ref[t] = _gelu_tanh(xg_ref[t]) * h
    h_ref[...] = h
    for j in range(cw - 1):
        conv_ref[j] = hist[t_new + j]


def _rg_sample_call(xr_t, xg_t, conv0_t, h0, conv_w, conv_b, wa_bd, ba, wx_bd, bx, lam):
    t_new, nseq, w = xg_t.shape
    return pl.pallas_call(
        _rg_sample_kernel,
        out_shape=[jax.ShapeDtypeStruct((t_new, nseq, w), _F32), jax.ShapeDtypeStruct(conv0_t.shape, _F32),
                   jax.ShapeDtypeStruct((nseq, w), _F32)],
        compiler_params=pltpu.CompilerParams(vmem_limit_bytes=_VMEM_LIMIT),
        name="rglru_sample",
    )(xr_t, xg_t, conv0_t, h0, conv_w, conv_b, wa_bd, ba, wx_bd, bx, lam)


def _row_bcast(x, r, n):
    return jnp.broadcast_to(x[r:r + 1, :], (n, x.shape[1]))


def _chunk_cumsum(x):
    n_tiles = x.shape[0] // _SUBLANES
    row = lax.broadcasted_iota(jnp.int32, (_SUBLANES, x.shape[1]), 0)
    tiles = []
    carry = None
    for i in range(n_tiles):
        t = x[i * _SUBLANES:(i + 1) * _SUBLANES, :]
        sh = 1
        while sh < _SUBLANES:
            t = t + jnp.where(row >= sh, pltpu.roll(t, sh, axis=0), 0.0)
            sh *= 2
        if carry is not None:
            t = t + carry
        carry = _row_bcast(t, _SUBLANES - 1, _SUBLANES)
        tiles.append(t)
    return tiles[0] if n_tiles == 1 else jnp.concatenate(tiles, axis=0)


def _level_reference(b, m):
    n = b.shape[0]
    if 2 * m >= _SUBLANES:
        pieces = [_row_bcast(b, lo + m - 1, 2 * m) for lo in range(0, n, 2 * m)]
        return pieces[0] if len(pieces) == 1 else jnp.concatenate(pieces, axis=0)
    row = lax.broadcasted_iota(jnp.int32, (_SUBLANES, b.shape[1]), 0)
    tiles = []
    for i in range(n // _SUBLANES):
        t = b[i * _SUBLANES:(i + 1) * _SUBLANES, :]
        ref = None
        for lo in range(0, _SUBLANES, 2 * m):
            piece = _row_bcast(t, lo + m - 1, _SUBLANES)
            ref = piece if ref is None else jnp.where(row >= lo, piece, ref)
        tiles.append(ref)
    return tiles[0] if len(tiles) == 1 else jnp.concatenate(tiles, axis=0)


def _hgrn_gates(fz, lb):
    f = lb + (1.0 - lb) * _sigmoid(fz)
    return jnp.log(f), 1.0 - f


def _hgrn_chunk(q, fz, v, lb, state, n_valid, state_is_vk):
    n, kd = q.shape
    log_f, k = _hgrn_gates(fz, lb)
    if n_valid < n:
        valid = lax.broadcasted_iota(jnp.int32, (n, kd), 0) < n_valid
        log_f = jnp.where(valid, log_f, 0.0)
        k = jnp.where(valid, k, 0.0)
    b = _chunk_cumsum(log_f)
    b_last = _row_bcast(b, n - 1, n)

    q_in = q * jnp.exp(b)
    k_end = k * jnp.exp(b_last - b)
    if state_is_vk:
        o = _dot_nt(q_in, state)
        new_state = jnp.exp(b_last[0:1, :]) * state + _dot_tn(v, k_end)
    else:
        o = _dot(q_in, state)
        decay = jnp.exp(jnp.broadcast_to(b_last[0:1, :], (v.shape[1], kd))).T
        new_state = decay * state + _dot_tn(k_end, v)

    row = lax.broadcasted_iota(jnp.int32, (n, n), 0)
    col = lax.broadcasted_iota(jnp.int32, (n, n), 1)
    upper = lax.broadcasted_iota(jnp.int32, (n, kd), 0)
    scores = jnp.where(row == col, jnp.sum(q * k, axis=-1, keepdims=True), 0.0)
    m = 1
    while m < n_valid:
        e = jnp.exp(-jnp.abs(b - _level_reference(b, m)))
        z = jnp.where((upper & m) != 0, q, k) * e
        pair = jnp.logical_and((row & m) != 0, (row ^ m) >> (m.bit_length() - 1) == col >> (m.bit_length() - 1))
        scores = scores + jnp.where(pair, _dot_nt(z, z), 0.0)
        m *= 2
    return o + _dot(scores, v), new_state


def _lower_bound(logits, layer):
    m = jnp.max(logits, axis=0, keepdims=True)
    e = jnp.exp(logits - m)
    return jnp.sum(e[1:layer + 1, :], axis=0, keepdims=True) / jnp.sum(e, axis=0, keepdims=True)


def _hgrn_out(o, g, gnorm):
    return _rms(o, gnorm) * _silu(g)


def _hgrn_prompt_kernel(q_ref, f_ref, v_ref, g_ref, lbl_ref, gn_ref, o_ref, s_ref,
                        st_scr, st0_scr, *, layer):
    rows_blk, width = q_ref.shape
    kd = width // _HG_HEADS
    n_chunks = rows_blk // _HG_CHUNK
    heads = [slice(h * kd, (h + 1) * kd) for h in range(_HG_HEADS)]
    lb = _lower_bound(lbl_ref[...], layer)
    gn = gn_ref[...]

    @pl.when(pl.program_id(2) == 0)
    def _():
        st_scr[...] = jnp.zeros(st_scr.shape, _F32)

    st0_scr[...] = st_scr[...]

    def chunk_rows(c):
        return pl.ds(pl.multiple_of(c * _HG_CHUNK, _HG_CHUNK), _HG_CHUNK)

    def prepare(c):
        rows = chunk_rows(c)
        log_f, k = _hgrn_gates(f_ref[rows, :], lb)
        b = _chunk_cumsum(log_f)
        b_end = b[_HG_CHUNK - 1:_HG_CHUNK, :]
        dec = jnp.exp(b_end)
        e_b = jnp.exp(b)
        k_start = k / e_b
        ops = ((q_ref[rows, :] * e_b).astype(_BF16), k_start.astype(_BF16), (k_start * dec).astype(_BF16),
               v_ref[rows, :].astype(_BF16), dec)
        return ops, b_end

    row = lax.broadcasted_iota(jnp.int32, (_HG_CHUNK, _HG_CHUNK), 0)
    col = lax.broadcasted_iota(jnp.int32, (_HG_CHUNK, _HG_CHUNK), 1)

    def finish(c, h, o):
        rows = chunk_rows(c)
        o_ref[rows, heads[h]] = _hgrn_out(o, g_ref[rows, heads[h]], gn)

    def contract(c, ops):
        qs, ks, ke, vb, dec = ops
        states = [st_scr[h] for h in range(_HG_HEADS)]
        scores = [_dot_nt(qs[:, l], ks[:, l]) for l in heads]
        carried = [_dot_nt(qs[:, l], st) for l, st in zip(heads, states)]
        incs = [_dot_tn(vb[:, l], ke[:, l]) for l in heads]
        outs = [_dot(jnp.where(row >= col, s, 0.0), vb[:, l]) for s, l in zip(scores, heads)]
        for h, l in enumerate(heads):
            st_scr[h] = states[h] * dec[:, l] + incs[h]
        for h in range(_HG_HEADS):
            finish(c, h, outs[h] + carried[h])

    def step(c, carry):
        ops, min_b = carry
        contract(c, ops)
        ops, b_end = prepare(c + 1)
        return ops, jnp.minimum(min_b, b_end)

    last_ops, min_b = lax.fori_loop(0, n_chunks - 1, step, prepare(0), unroll=5)
    contract(n_chunks - 1, last_ops)

    @pl.when(jnp.min(min_b) <= -_HG_SAFE_LOG_DECAY)
    def _():
        st_scr[...] = st0_scr[...]

        def chunk(c, carry):
            rows = chunk_rows(c)
            for h, l in enumerate(heads):
                o, st_scr[h] = _hgrn_chunk(q_ref[rows, l], f_ref[rows, l], v_ref[rows, l],
                                           lb[:, l], st_scr[h], _HG_CHUNK, True)
                finish(c, h, o)
            return carry
        lax.fori_loop(0, n_chunks, chunk, 0)

    @pl.when(pl.program_id(2) == pl.num_programs(2) - 1)
    def _():
        for h in range(_HG_HEADS):
            s_ref[0, h] = st_scr[h].T


def _hgrn_prompt_call(q, fz, v, g, lb_logits, gnorm, batch, layer):
    n, width = q.shape
    t_len = n // batch
    kd = width // _C_HEADS
    gw = _HG_HEADS * kd
    rows_blk = min(_HG_ROWS, t_len)
    nt = t_len // rows_blk
    blk = pl.BlockSpec((rows_blk, gw), lambda b, h, t: (b * nt + t, h))
    return pl.pallas_call(
        functools.partial(_hgrn_prompt_kernel, layer=layer),
        grid=(batch, _C_HEADS // _HG_HEADS, nt),
        in_specs=[blk, blk, blk, blk,
                  pl.BlockSpec((lb_logits.shape[0], gw), lambda b, h, t: (0, h)),
                  pl.BlockSpec((1, kd), lambda b, h, t: (0, 0))],
        out_specs=[blk, pl.BlockSpec((1, _HG_HEADS, kd, kd), lambda b, h, t: (b, h, 0, 0))],
        out_shape=[jax.ShapeDtypeStruct((n, width), _F32),
                   jax.ShapeDtypeStruct((batch, _C_HEADS, kd, kd), _F32)],
        scratch_shapes=[pltpu.VMEM((_HG_HEADS, kd, kd), _F32)] * 2,
        compiler_params=_params("parallel", "parallel", "arbitrary"),
        name="hgrn2_prompt",
    )(q, fz, v, g, lb_logits, gnorm)


def _hgrn_sample_kernel(q_ref, f_ref, v_ref, g_ref, lbl_ref, gn_ref, s0_ref, o_ref, s_ref, *, layer):
    t_new, nseq, width = q_ref.shape
    n_heads = s0_ref.shape[1]
    kd = width // n_heads
    lanes = [slice(h * kd, (h + 1) * kd) for h in range(n_heads)]
    lb = _lower_bound(lbl_ref[...], layer)
    gn = gn_ref[...]
    q = [q_ref[t] for t in range(t_new)]
    v = [v_ref[t] for t in range(t_new)]
    keys, b = [], []
    for t in range(t_new):
        log_f, k = _hgrn_gates(f_ref[t], lb)
        keys.append(k)
        b.append(log_f if t == 0 else b[-1] + log_f)

    def per_head_sum(w):
        return jnp.concatenate([jnp.broadcast_to(jnp.sum(w[:, l], axis=-1, keepdims=True), (nseq, kd))
                                for l in lanes], axis=1)

    within = []
    for t in range(t_new):
        acc = per_head_sum(q[t] * keys[t]) * v[t]
        for s in range(t):
            acc = acc + per_head_sum(q[t] * keys[s] * jnp.exp(b[t] - b[s])) * v[s]
        within.append(acc)

    q_in = [q[t] * jnp.exp(b[t]) for t in range(t_new)]
    k_end = [keys[t] * jnp.exp(b[-1] - b[t]) for t in range(t_new)]
    decay = jnp.exp(b[-1])
    pairs = [(s, h) for h in range(n_heads) for s in range(nseq)]
    lhs = [_seq_tile([x[:, lanes[h]] for x in q_in], s) for s, h in pairs]
    k_seq = [_seq_tile([x[:, lanes[h]] for x in k_end], s) for s, h in pairs]
    v_seq = [_seq_tile([x[:, lanes[h]] for x in v], s) for s, h in pairs]
    carried = [_dot(x, s0_ref[s, h]) for x, (s, h) in zip(lhs, pairs)]
    incs = [_dot_tn(ks, vs) for ks, vs in zip(k_seq, v_seq)]
    carried_t = []
    for h in range(n_heads):
        dec_cols = jnp.concatenate([decay[:, lanes[h]], jnp.zeros((kd - nseq, kd), _F32)], axis=0).T
        for s in range(nseq):
            i = h * nseq + s
            s_ref[s, h] = jnp.broadcast_to(dec_cols[:, s:s + 1], (kd, kd)) * s0_ref[s, h] + incs[i]
        carried_t.append(_time_tiles(carried[h * nseq:(h + 1) * nseq], t_new))
    for t in range(t_new):
        o = within[t] + jnp.concatenate([carried_t[h][t] for h in range(n_heads)], axis=1)
        g = g_ref[t]
        o_ref[t] = jnp.concatenate([_hgrn_out(o[:, l], g[:, l], gn) for l in lanes], axis=1)


def _hgrn_sample_call(q, fz, v, g, lb_logits, gnorm, s0, layer):
    t_new, nseq, width = q.shape
    kd = width // _C_HEADS
    gw = _HG_HEADS * kd
    bs = _SAMPLE_SEQS
    blk = pl.BlockSpec((t_new, bs, gw), lambda i, h: (0, i, h))
    st = pl.BlockSpec((bs, _HG_HEADS, kd, kd), lambda i, h: (i, h, 0, 0))
    return pl.pallas_call(
        functools.partial(_hgrn_sample_kernel, layer=layer),
        grid=(nseq // bs, _C_HEADS // _HG_HEADS),
        in_specs=[blk, blk, blk, blk,
                  pl.BlockSpec((lb_logits.shape[0], gw), lambda i, h: (0, h)),
                  pl.BlockSpec((1, kd), lambda i, h: (0, 0)), st],
        out_specs=[blk, st],
        out_shape=[jax.ShapeDtypeStruct((t_new, nseq, width), _F32), jax.ShapeDtypeStruct(s0.shape, _F32)],
        compiler_params=_params("parallel", "parallel"),
        name="hgrn2_sample",
    )(q, fz, v, g, lb_logits, gnorm, s0)


def _rope_tables(pos):
    half = _ROT_DIM // 2
    inv_freq = _ROPE_THETA ** (-jnp.arange(0, _ROT_DIM, 2, dtype=_F32) / _ROT_DIM)
    ang = pos.astype(_F32)[:, None] * inv_freq[None, :]
    cos, sin = jnp.cos(ang), jnp.sin(ang)
    ones = jnp.ones((pos.shape[0], _A_HEAD_DIM - _ROT_DIM), _F32)
    zeros = jnp.zeros((pos.shape[0], _A_HEAD_DIM - half), _F32)
    zeros_h = jnp.zeros((pos.shape[0], half), _F32)
    reps = _LANES // _A_HEAD_DIM
    cos_t = jnp.tile(jnp.concatenate([cos, cos, ones], axis=1), (1, reps))
    sin_lo = jnp.tile(jnp.concatenate([-sin, zeros], axis=1), (1, reps))
    sin_hi = jnp.tile(jnp.concatenate([zeros_h, sin, ones * 0.0], axis=1), (1, reps))
    return cos_t, sin_lo, sin_hi


def _block_diag(w):
    nb, bd, _ = w.shape
    eye = jnp.eye(nb, dtype=w.dtype)
    return (w[:, :, None, :] * eye[:, None, :, None]).reshape(nb * bd, nb * bd)


def kernel(x_prompt, x_sample, c_prompt, c_sample, cache_k_win, cache_v_win, state_conv_rglru,
           state_h_rglru, state_s_hgrn, norm_pre, norm_post, ada_w, ada_b, ffn1_w_in, ffn1_w_out,
           ffn2_w_in, ffn2_w_out, even_w_in, even_w_out, attn_sinks, rg_conv_w, rg_conv_b, rg_wa,
           rg_ba, rg_wx, rg_bx, rg_lambda, odd_w_in, odd_w_out, hgrn_lb_logits, hgrn_gnorm):
    bp, tp, d = x_prompt.shape
    bs, ts, _ = x_sample.shape
    depth = norm_pre.shape[0]
    n_sub = depth * _N_SUB
    nk = _A_KV_HEADS * _A_HEAD_DIM
    win = cache_k_win.shape[2]
    cw = rg_conv_w.shape[1]
    bw = rg_conv_w.shape[2]
    kd = state_s_hgrn.shape[3]

    cast = lambda w: w.astype(_BF16)
    ffn1_in, ffn1_out, ffn2_in, ffn2_out = cast(ffn1_w_in), cast(ffn1_w_out), cast(ffn2_w_in), cast(ffn2_w_out)
    ev_in, ev_out, od_in, od_out = cast(even_w_in), cast(even_w_out), cast(odd_w_in), cast(odd_w_out)
    gpre = norm_pre.reshape(n_sub, 1, d)
    gpost = norm_post.reshape(n_sub, 1, d)

    mod = _ada_call(jnp.concatenate([c_prompt, c_sample], axis=0),
                    ada_w.reshape(n_sub, d, 3 * d), ada_b.reshape(n_sub, 1, 3 * d))
    mod_p = mod[:, :bp].reshape(n_sub, bp, 1, 3 * d)
    mod_s = mod[:, bp:].reshape(n_sub, 1, bs, 3 * d)

    tabs_p = _rope_tables(jnp.arange(tp, dtype=jnp.int32))
    tabs_s = tuple(jnp.repeat(t, bs, axis=0) for t in _rope_tables(_PAST_LEN + jnp.arange(ts, dtype=jnp.int32)))

    time_major = lambda a: a.reshape(ts, bs, a.shape[-1])
    groups = {
        "p": dict(x=x_prompt.reshape(bp * tp, d), mod=mod_p, seq_rows=tp, tabs=tabs_p),
        "s": dict(x=x_sample.transpose(1, 0, 2).reshape(ts * bs, d), mod=mod_s, seq_rows=ts * bs, tabs=tabs_s),
    }
    outs = {g: dict(k=[], v=[], conv=[], h=[], s=[]) for g in groups}

    for l in range(depth):
        s0, s1, s2 = l * _N_SUB, l * _N_SUB + 1, l * _N_SUB + 2
        for name, grp in groups.items():
            x, mod4, tps = grp["x"], grp["mod"], grp["seq_rows"]
            x = _ffn_call(x, mod4, s0, l, gpre, ffn1_in, ffn1_out, gpost, 0.5, tps)
            if l % 2 == 0:
                e = l // 2
                q, k, v, xg, xr = _even_in_call(x, mod4, s1, e, gpre, ev_in, grp["tabs"], tps)
                wa_bd, wx_bd = cast(_block_diag(rg_wa[e])), cast(_block_diag(rg_wx[e]))
                vecs = [a[e].reshape(1, bw) for a in (rg_conv_b, rg_ba, rg_bx, rg_lambda)]
                if name == "p":
                    o_a = _swa_prompt_call(q, k, v, attn_sinks[e], bp)
                    o_b, conv, h_last = _rg_prompt_call(xr, xg, rg_conv_w[e], vecs[0], wa_bd, vecs[1], wx_bd,
                                                        vecs[2], vecs[3], bp)
                    outs[name]["k"].append(k.reshape(bp, tp, _A_KV_HEADS, _A_HEAD_DIM)[:, tp - win:])
                    outs[name]["v"].append(v.reshape(bp, tp, _A_KV_HEADS, _A_HEAD_DIM)[:, tp - win:])
                    outs[name]["h"].append(h_last.reshape(bp, bw))
                else:
                    o_a, kw, vw = _swa_sample_call(time_major(q), time_major(k), time_major(v),
                                                   cache_k_win[e].reshape(bs, win, nk),
                                                   cache_v_win[e].reshape(bs, win, nk), attn_sinks[e])
                    o_b, conv_t, h_last = _rg_sample_call(time_major(xr), time_major(xg),
                                                          state_conv_rglru[e].transpose(1, 0, 2), state_h_rglru[e],
                                                          rg_conv_w[e], vecs[0], wa_bd, vecs[1], wx_bd, vecs[2], vecs[3])
                    o_a, o_b = o_a.reshape(ts * bs, -1), o_b.reshape(ts * bs, bw)
                    conv = conv_t.transpose(1, 0, 2)
                    outs[name]["k"].append(kw.reshape(bs, win, _A_KV_HEADS, _A_HEAD_DIM))
                    outs[name]["v"].append(vw.reshape(bs, win, _A_KV_HEADS, _A_HEAD_DIM))
                    outs[name]["h"].append(h_last)
                outs[name]["conv"].append(conv)
                mixer = (s1, e, ev_out, [o_a, o_b])
            else:
                o = l // 2
                q, fz, v, g = _odd_in_call(x, mod4, s1, o, gpre, od_in, tps)
                gn = hgrn_gnorm[o].reshape(1, kd)
                if name == "p":
                    y, s_last = _hgrn_prompt_call(q, fz, v, g, hgrn_lb_logits, gn, bp, l)
                else:
                    y, s_last = _hgrn_sample_call(time_major(q), time_major(fz), time_major(v), time_major(g),
                                                  hgrn_lb_logits, gn, state_s_hgrn[o], l)
                    y = y.reshape(ts * bs, -1)
                outs[name]["s"].append(s_last)
                mixer = (s1, o, od_out, [y])
            x = _ffn_call(x, mod4, s2, l, gpre, ffn2_in, ffn2_out, gpost, 0.5, tps, mixer)
            grp["x"] = x

    ys = {"p": groups["p"]["x"].reshape(bp, tp, d), "s": groups["s"]["x"].reshape(ts, bs, d).transpose(1, 0, 2)}
    res = []
    for name in ("p", "s"):
        o = outs[name]
        res.append((jnp.stack(o["k"]), jnp.stack(o["v"]), jnp.stack(o["conv"]), jnp.stack(o["h"]), jnp.stack(o["s"])))
    return (ys["p"], ys["s"]) + res[0] + res[1]
```

```python
import functools

import jax
import jax.numpy as jnp
import numpy as np
from jax import lax
from jax.experimental import pallas as pl
from jax.experimental.pallas import tpu as pltpu

_F32 = jnp.float32
_BF16 = jnp.bfloat16

_EPS = 1e-6
_A_HEADS = 8
_A_KV_HEADS = 2
_A_HEAD_DIM = 64
_A_GROUP = _A_HEADS // _A_KV_HEADS
_WINDOW = 128
_ROPE_THETA = 500000.0
_ROT_DIM = _A_HEAD_DIM // 4
_RG_C = 8.0
_C_HEADS = 8
_PAST_LEN = 16384
_N_SUB = 3

_LANES = 128
_SUBLANES = 8
_VMEM_BYTES = 64 * 1024 * 1024
_VMEM_LIMIT = _VMEM_BYTES * 3 // 4
_VMEM_LIMIT_FFN = _VMEM_BYTES * 7 // 8

_ROW_TILE = 512
_FF_CHUNK = 256
_RG_CHUNK = 256
_HG_CHUNK = 64
_HG_HEADS = 4
_HG_ROWS = 1024
_HG_SAFE_LOG_DECAY = 80.0
_SAMPLE_SEQS = 8
_SWA_QBLOCKS = 4


def _dot(a, b):
    return jnp.dot(a.astype(_BF16), b.astype(_BF16), preferred_element_type=_F32)


def _dot_nt(a, b):
    return lax.dot_general(a.astype(_BF16), b.astype(_BF16), (((1,), (1,)), ((), ())),
                           preferred_element_type=_F32)


def _dot_tn(a, b):
    return lax.dot_general(a.astype(_BF16), b.astype(_BF16), (((0,), (0,)), ((), ())),
                           preferred_element_type=_F32)


def _sigmoid(x):
    return 1.0 / (1.0 + jnp.exp(-x))


def _silu(x):
    return x * _sigmoid(x)


def _rms(x, gain):
    inv = lax.rsqrt(jnp.mean(x * x, axis=-1, keepdims=True) + _EPS)
    return x * inv * gain


def _per_seq(a, r):
    n = a.shape[0]
    return a if r in (1, n) else a.reshape(n // r, r, a.shape[1])


def _pre(x, mod, gain):
    n, d = x.shape
    h = _rms(_per_seq(x, mod.shape[0]), gain) * (1.0 + mod[:, d:2 * d]) + mod[:, :d]
    return h.reshape(n, d)


def _post(x, y, mod, gain, res_w):
    n, d = x.shape
    r = mod.shape[0]
    out = _per_seq(x, r) + (res_w * (1.0 + mod[:, 2 * d:])) * _rms(_per_seq(y, r), gain)
    return out.reshape(n, d)


def _params(*sem):
    return pltpu.CompilerParams(dimension_semantics=sem, vmem_limit_bytes=_VMEM_LIMIT)


def _ada_kernel(c_ref, w_ref, b_ref, o_ref):
    o_ref[...] = _dot(_silu(c_ref[...]), w_ref[...]) + b_ref[...]


def _ada_call(c_all, ada_w, ada_b):
    m, d = c_all.shape
    n_sub = ada_w.shape[0]
    n = ada_w.shape[-1]
    tn = n // 2
    return pl.pallas_call(
        _ada_kernel,
        grid=(n_sub, n // tn),
        in_specs=[
            pl.BlockSpec((m, d), lambda s, j: (0, 0)),
            pl.BlockSpec((None, d, tn), lambda s, j: (s, 0, j)),
            pl.BlockSpec((None, 1, tn), lambda s, j: (s, 0, j)),
        ],
        out_specs=pl.BlockSpec((None, m, tn), lambda s, j: (s, 0, j)),
        out_shape=jax.ShapeDtypeStruct((n_sub, m, n), _F32),
        compiler_params=_params("parallel", "parallel"),
        name="ada_mod",
    )(c_all, ada_w, ada_b)


def _ffn_kernel(xp_ref, xs_ref, modp_ref, mods_ref, gpre_ref, win_ref, wout_ref, gpost_ref, *refs,
                res_w, n_load, n_prompt, n_acts):
    op_ref, os_ref, win_bf, wout_bf = refs[-4:]
    i = pl.program_id(0)
    ld = win_ref.shape[1]
    dff = wout_bf.shape[0] * wout_bf.shape[1]

    @pl.when(i < n_load)
    def _():
        win_bf[i] = win_ref[...].astype(_BF16)
        wout_bf[i] = wout_ref[...].astype(_BF16)

    def w_in_cols(lo):
        return win_bf[lo // ld, :, lo % ld:lo % ld + _FF_CHUNK]

    def row_tile(x_ref, mod_ref, mmod_ref, act_refs, o_ref):
        x = x_ref[...]
        if n_acts:
            mgpost_ref, mw_ref = refs[2:4]
            y = None
            off = 0
            for a_ref in act_refs:
                k = a_ref.shape[1]
                t = _dot(a_ref[...], mw_ref[off:off + k, :])
                y = t if y is None else y + t
                off += k
            x = _post(x, y, mmod_ref[0], mgpost_ref[...], 1.0)
        mod = mod_ref[0]
        h = _pre(x, mod, gpre_ref[...]).astype(_BF16)
        acc = jnp.zeros(x.shape, _F32)
        for j in range(dff // _FF_CHUNK):
            lo = j * _FF_CHUNK
            g = _dot(h, w_in_cols(lo))
            u = _dot(h, w_in_cols(dff + lo))
            acc = acc + _dot(_silu(g) * u, wout_bf[j])
        o_ref[...] = _post(x, acc, mod, gpost_ref[...], res_w)

    acts = refs[4:4 + 2 * n_acts] if n_acts else ()

    @pl.when(jnp.logical_and(i >= n_load, i < n_load + n_prompt))
    def _():
        row_tile(xp_ref, modp_ref, refs[0] if n_acts else None, acts[:n_acts], op_ref)

    @pl.when(i == n_load + n_prompt)
    def _():
        row_tile(xs_ref, mods_ref, refs[1] if n_acts else None, acts[n_acts:], os_ref)


def _ffn_call(xp, xs, modp, mods, sub, layer, gpre, w_in, w_out, gpost, res_w, seq_rows, mixer=None):
    n, d = xp.shape
    tm = xs.shape[0]
    assert n % tm == 0 and seq_rows % tm == 0 and mods.shape[2] * (tm // mods.shape[2]) == tm
    dff = w_out.shape[1]
    n_load = dff // _FF_CHUNK
    n_prompt = n // tm
    tiles_per_seq = seq_rows // tm
    tile = lambda i: jnp.clip(i - n_load, 0, n_prompt - 1)
    rows_p = pl.BlockSpec((tm, d), lambda i: (tile(i), 0))
    once = dict(pipeline_mode=pl.Buffered(1))
    rows_s = pl.BlockSpec((tm, d), lambda i: (0, 0), **once)
    modp_spec = lambda s: pl.BlockSpec((None, 1, 1, 3 * d), lambda i: (s, tile(i) // tiles_per_seq, 0, 0))
    mods_spec = lambda s: pl.BlockSpec((None, 1, mods.shape[2], 3 * d), lambda i: (s, 0, 0, 0), **once)
    gain_spec = lambda s: pl.BlockSpec((None, 1, d), lambda i: (s, 0, 0))
    chunk = lambda i: jnp.minimum(i, n_load - 1)
    in_specs = [
        rows_p, rows_s, modp_spec(sub), mods_spec(sub), gain_spec(sub),
        pl.BlockSpec((None, d, 2 * dff // n_load), lambda i: (layer, 0, chunk(i))),
        pl.BlockSpec((None, _FF_CHUNK, d), lambda i: (layer, chunk(i), 0)),
        gain_spec(sub),
    ]
    args = [xp, xs, modp, mods, gpre, w_in, w_out, gpost]
    n_acts = 0
    if mixer is not None:
        msub, widx, mw, acts_p, acts_s = mixer
        n_acts = len(acts_p)
        in_specs += [modp_spec(msub), mods_spec(msub), gain_spec(msub),
                     pl.BlockSpec((None, mw.shape[1], d), lambda i: (widx, 0, 0), pipeline_mode=pl.Buffered(1))]
        in_specs += [pl.BlockSpec((tm, a.shape[1]), lambda i: (tile(i), 0)) for a in acts_p]
        in_specs += [pl.BlockSpec((tm, a.shape[1]), lambda i: (0, 0), **once) for a in acts_s]
        args += [modp, mods, gpost, mw] + list(acts_p) + list(acts_s)
    return pl.pallas_call(
        functools.partial(_ffn_kernel, res_w=res_w, n_load=n_load, n_prompt=n_prompt, n_acts=n_acts),
        grid=(n_load + n_prompt + 1,),
        in_specs=in_specs,
        out_specs=[rows_p, pl.BlockSpec((tm, d), lambda i: (0, 0))],
        out_shape=[jax.ShapeDtypeStruct((n, d), _F32), jax.ShapeDtypeStruct((tm, d), _F32)],
        scratch_shapes=[pltpu.VMEM((n_load, d, 2 * dff // n_load), _BF16),
                        pltpu.VMEM((n_load, _FF_CHUNK, d), _BF16)],
        compiler_params=pltpu.CompilerParams(dimension_semantics=("arbitrary",), vmem_limit_bytes=_VMEM_LIMIT_FFN),
        name="ffn_sublayer" if mixer is None else "mixer_out_ffn",
    )(*args)


def _rope(x, cos, sin_lo, sin_hi):
    outs = []
    for j in range(x.shape[1] // _LANES):
        xc = x[:, j * _LANES:(j + 1) * _LANES]
        nxt = pltpu.roll(xc, _LANES - _ROT_DIM // 2, axis=1)
        prv = pltpu.roll(xc, _ROT_DIM // 2, axis=1)
        outs.append(xc * cos + nxt * sin_lo + prv * sin_hi)
    return outs[0] if len(outs) == 1 else jnp.concatenate(outs, axis=1)


def _even_in_kernel(x_ref, mod_ref, gpre_ref, w_ref, cos_ref, slo_ref, shi_ref,
                    q_ref, k_ref, v_ref, xg_ref, xr_ref):
    h = _pre(x_ref[...], mod_ref[0], gpre_ref[...])
    y = _dot(h, w_ref[...])
    nq, nk, nw = q_ref.shape[1], k_ref.shape[1], xg_ref.shape[1]
    cos, slo, shi = cos_ref[...], slo_ref[...], shi_ref[...]
    q_ref[...] = _rope(y[:, :nq], cos, slo, shi)
    k_ref[...] = _rope(y[:, nq:nq + nk], cos, slo, shi)
    v_ref[...] = y[:, nq + nk:nq + 2 * nk]
    xg_ref[...] = y[:, nq + 2 * nk:nq + 2 * nk + nw]
    xr_ref[...] = y[:, nq + 2 * nk + nw:]


def _even_in_call(x, mod4, sub, e, gpre, w_in, rope_tabs, seq_rows):
    n, d = x.shape
    r = mod4.shape[2]
    tm = min(_ROW_TILE, n)
    tiles_per_seq = tab_tiles = seq_rows // tm
    nq = _A_HEADS * _A_HEAD_DIM
    nk = _A_KV_HEADS * _A_HEAD_DIM
    nw = (w_in.shape[-1] - nq - 2 * nk) // 2
    row = lambda i: (i, 0)
    tab = pl.BlockSpec((tm, _LANES), lambda i: (i % tab_tiles, 0))
    return pl.pallas_call(
        _even_in_kernel,
        grid=(n // tm,),
        in_specs=[
            pl.BlockSpec((tm, d), row),
            pl.BlockSpec((None, 1, r, 3 * d), lambda i: (sub, i // tiles_per_seq, 0, 0)),
            pl.BlockSpec((None, 1, d), lambda i: (sub, 0, 0)),
            pl.BlockSpec((None, d, w_in.shape[-1]), lambda i: (e, 0, 0)),
            tab, tab, tab,
        ],
        out_specs=[pl.BlockSpec((tm, nq), row), pl.BlockSpec((tm, nk), row), pl.BlockSpec((tm, nk), row),
                   pl.BlockSpec((tm, nw), row), pl.BlockSpec((tm, nw), row)],
        out_shape=[jax.ShapeDtypeStruct((n, nq), _F32), jax.ShapeDtypeStruct((n, nk), _F32),
                   jax.ShapeDtypeStruct((n, nk), _F32), jax.ShapeDtypeStruct((n, nw), _F32),
                   jax.ShapeDtypeStruct((n, nw), _F32)],
        compiler_params=_params("parallel"),
        name="even_in_proj",
    )(x, mod4, gpre, w_in, *rope_tabs)


def _odd_in_kernel(x_ref, mod_ref, gpre_ref, w_ref, q_ref, f_ref, v_ref, g_ref):
    h = _pre(x_ref[...], mod_ref[0], gpre_ref[...])
    y = _dot(h, w_ref[...])
    n = q_ref.shape[1]
    q_ref[...] = y[:, :n]
    f_ref[...] = y[:, n:2 * n]
    v_ref[...] = y[:, 2 * n:3 * n]
    g_ref[...] = y[:, 3 * n:]


def _odd_in_call(x, mod4, sub, o, gpre, w_in, seq_rows):
    n, d = x.shape
    r = mod4.shape[2]
    tm = min(_ROW_TILE, n)
    tiles_per_seq = seq_rows // tm
    nw = w_in.shape[-1] // 4
    row = lambda i: (i, 0)
    return pl.pallas_call(
        _odd_in_kernel,
        grid=(n // tm,),
        in_specs=[
            pl.BlockSpec((tm, d), row),
            pl.BlockSpec((None, 1, r, 3 * d), lambda i: (sub, i // tiles_per_seq, 0, 0)),
            pl.BlockSpec((None, 1, d), lambda i: (sub, 0, 0)),
            pl.BlockSpec((None, d, 4 * nw), lambda i: (o, 0, 0)),
        ],
        out_specs=[pl.BlockSpec((tm, nw), row)] * 4,
        out_shape=[jax.ShapeDtypeStruct((n, nw), _F32)] * 4,
        compiler_params=_params("parallel"),
        name="odd_in_proj",
    )(x, mod4, gpre, w_in)


def _seq_tile(time_tiles, s):
    row = lax.broadcasted_iota(jnp.int32, time_tiles[0].shape, 0)
    out = jnp.zeros(time_tiles[0].shape, time_tiles[0].dtype)
    for t, x in enumerate(time_tiles):
        shift = (t - s) % _SUBLANES
        out = jnp.where(row == t, pltpu.roll(x, shift, axis=0) if shift else x, out)
    return out


def _time_tiles(seq_tiles, n_t):
    row = lax.broadcasted_iota(jnp.int32, seq_tiles[0].shape, 0)
    outs = []
    for t in range(n_t):
        acc = jnp.zeros(seq_tiles[0].shape, seq_tiles[0].dtype)
        for s, x in enumerate(seq_tiles):
            shift = (s - t) % _SUBLANES
            acc = jnp.where(row == s, pltpu.roll(x, shift, axis=0) if shift else x, acc)
        outs.append(acc)
    return outs


def _sink_softmax_pv(s, mask, sink, v):
    s = jnp.where(mask, s, -jnp.inf)
    m = jnp.maximum(jnp.max(s, axis=-1, keepdims=True), sink)
    p = jnp.exp(s - m)
    denom = jnp.sum(p, axis=-1, keepdims=True) + jnp.exp(sink - m)
    return _dot(p, v) / denom


def _swa_prompt_kernel(sink_ref, q_ref, kp_ref, kc_ref, vp_ref, vc_ref, o_ref):
    w = _WINDOW
    hd = _A_HEAD_DIM
    assert _LANES == 2 * hd and _A_GROUP % 2 == 0
    log2e = np.float32(np.log2(np.e))
    scale = np.float32(1.0 / np.sqrt(hd)) * log2e
    low = lax.broadcasted_iota(jnp.int32, (2 * w, _LANES), 1) < hd
    ones_lo = jnp.where(low, 1.0, 0.0).astype(_BF16)
    ones_hi = jnp.where(low, 0.0, 1.0).astype(_BF16)
    low_q = lax.broadcasted_iota(jnp.int32, (w, _LANES), 1) < hd
    row = lax.broadcasted_iota(jnp.int32, (w, 4 * w), 0)
    col = lax.broadcasted_iota(jnp.int32, (w, 4 * w), 1) & (2 * w - 1)
    for qb in range(q_ref.shape[0] // w):
        rows = slice(qb * w, (qb + 1) * w)
        if qb == 0:
            k2 = jnp.concatenate([kp_ref[...], kc_ref[0:w, :]], axis=0)
            v2 = jnp.concatenate([vp_ref[...], vc_ref[0:w, :]], axis=0)
            first = jnp.where(pl.program_id(1) > 0, 0, w)
        else:
            k2 = kc_ref[(qb - 1) * w:(qb + 1) * w, :]
            v2 = vc_ref[(qb - 1) * w:(qb + 1) * w, :]
            first = 0
        mask = jnp.logical_and(col > jnp.maximum(row, first - 1), col <= row + w)
        keys, vals = [], []
        for j in range(_A_KV_HEADS):
            own_k = jnp.where(low, k2, 0.0) if j == 0 else jnp.where(low, 0.0, k2)
            own_v = jnp.where(low, v2, 0.0) if j == 0 else jnp.where(low, 0.0, v2)
            oth_k = pltpu.roll(own_k, hd, axis=1)
            oth_v = pltpu.roll(own_v, hd, axis=1)
            lo_k, hi_k = (own_k, oth_k) if j == 0 else (oth_k, own_k)
            lo_v, hi_v = (own_v, oth_v) if j == 0 else (oth_v, own_v)
            keys.append(jnp.concatenate([lo_k, hi_k], axis=0).astype(_BF16))
            vals.append(jnp.concatenate([jnp.concatenate([lo_v.astype(_BF16), ones_lo], axis=1),
                                         jnp.concatenate([hi_v.astype(_BF16), ones_hi], axis=1)], axis=0))
        pairs = range(_A_HEADS // 2)
        kv_of = [(2 * p) // _A_GROUP for p in pairs]
        scores = [_dot_nt(q_ref[rows, p * _LANES:(p + 1) * _LANES] * scale, keys[kv_of[p]]) for p in pairs]
        probs, sink_terms = [], []
        for p in pairs:
            s = jnp.where(mask, scores[p], -jnp.inf)
            halves = []
            for i in range(2):
                sh = s[:, i * 2 * w:(i + 1) * 2 * w]
                sink = sink_ref[2 * p + i] * log2e
                m = jnp.maximum(jnp.max(sh, axis=-1, keepdims=True), sink)
                halves.append((jnp.exp2(sh - m), jnp.exp2(sink - m)))
            probs.append(jnp.concatenate([halves[0][0], halves[1][0]], axis=1))
            sink_terms.append(jnp.where(low_q, halves[0][1], halves[1][1]))
        for p in pairs:
            r = _dot(probs[p], vals[kv_of[p]])
            o_ref[rows, p * _LANES:(p + 1) * _LANES] = r[:, :_LANES] / (r[:, _LANES:] + sink_terms[p])


def _swa_prompt_call(q, k, v, sinks, batch):
    n, nq = q.shape
    nk = k.shape[1]
    w = _WINDOW
    qb = _SWA_QBLOCKS
    nb = n // batch // (w * qb)
    cur = lambda b, i: (b * nb + i, 0)
    prev = lambda b, i: ((b * nb + i) * qb - jnp.minimum(i, 1), 0)
    return pl.pallas_call(
        _swa_prompt_kernel,
        grid=(batch, nb),
        in_specs=[
            pl.BlockSpec(memory_space=pltpu.SMEM),
            pl.BlockSpec((qb * w, nq), cur),
            pl.BlockSpec((w, nk), prev), pl.BlockSpec((qb * w, nk), cur),
            pl.BlockSpec((w, nk), prev), pl.BlockSpec((qb * w, nk), cur),
        ],
        out_specs=pl.BlockSpec((qb * w, nq), cur),
        out_shape=jax.ShapeDtypeStruct((n, nq), _F32),
        compiler_params=_params("parallel", "parallel"),
        name="swa_prompt",
    )(sinks, q, k, k, v, v)


def _swa_sample_kernel(sink_ref, q_ref, kn_ref, vn_ref, ck_ref, cv_ref, o_ref, kw_ref, vw_ref, *, t_new):
    p = ck_ref.shape[1]
    scale = np.float32(1.0 / np.sqrt(_A_HEAD_DIM))
    rows = _A_GROUP * _SUBLANES
    t = lax.broadcasted_iota(jnp.int32, (rows, p + _SUBLANES), 0) & (_SUBLANES - 1)
    c = lax.broadcasted_iota(jnp.int32, (rows, p + _SUBLANES), 1)
    mask = jnp.logical_and(c <= t + p, c > t + p - _WINDOW)
    g_of_row = lax.broadcasted_iota(jnp.int32, (rows, 1), 0) >> (_SUBLANES.bit_length() - 1)
    nseq = ck_ref.shape[0]
    q_t = [q_ref[t] for t in range(t_new)]
    kn_t = [kn_ref[t] for t in range(t_new)]
    vn_t = [vn_ref[t] for t in range(t_new)]
    sinks = []
    for j in range(_A_KV_HEADS):
        sink = jnp.zeros((rows, 1), _F32)
        for g in range(_A_GROUP):
            sink = jnp.where(g_of_row == g, sink_ref[j * _A_GROUP + g], sink)
        sinks.append(sink)
    new_k, new_v, qs, keys, vals = [], [], [], [], []
    for s in range(nseq):
        q8, kn8, vn8 = _seq_tile(q_t, s), _seq_tile(kn_t, s), _seq_tile(vn_t, s)
        ck, cv = ck_ref[s], cv_ref[s]
        kw_ref[s, 0:p - t_new, :] = ck[t_new:, :]
        kw_ref[s, p - t_new:p, :] = kn8[0:t_new, :]
        vw_ref[s, 0:p - t_new, :] = cv[t_new:, :]
        vw_ref[s, p - t_new:p, :] = vn8[0:t_new, :]
        for j in range(_A_KV_HEADS):
            ks = slice(j * _A_HEAD_DIM, (j + 1) * _A_HEAD_DIM)
            keys.append(jnp.concatenate([ck[:, ks], kn8[:, ks]], axis=0))
            vals.append(jnp.concatenate([cv[:, ks], vn8[:, ks]], axis=0))
            qs.append(jnp.concatenate(
                [q8[:, (j * _A_GROUP + g) * _A_HEAD_DIM:(j * _A_GROUP + g + 1) * _A_HEAD_DIM]
                 for g in range(_A_GROUP)], axis=0))
    scores = [_dot_nt(qj, k) * scale for qj, k in zip(qs, keys)]
    outs = [_sink_softmax_pv(sc, mask, sinks[i % _A_KV_HEADS], v) for i, (sc, v) in enumerate(zip(scores, vals))]
    per_seq = []
    for s in range(nseq):
        heads = []
        for j in range(_A_KV_HEADS):
            o = outs[s * _A_KV_HEADS + j]
            heads.extend(o[g * _SUBLANES:(g + 1) * _SUBLANES, :] for g in range(_A_GROUP))
        per_seq.append(jnp.concatenate(heads, axis=1))
    for t, tile in enumerate(_time_tiles(per_seq, t_new)):
        o_ref[t] = tile


def _swa_sample_call(q, k_new, v_new, cache_k, cache_v, sinks):
    t_new, nseq, nq = q.shape
    nk = k_new.shape[2]
    p = cache_k.shape[1]
    bs = _SAMPLE_SEQS
    toks = lambda i: (0, i, 0)
    seqs = lambda i: (i, 0, 0)
    return pl.pallas_call(
        functools.partial(_swa_sample_kernel, t_new=t_new),
        grid=(nseq // bs,),
        in_specs=[
            pl.BlockSpec(memory_space=pltpu.SMEM),
            pl.BlockSpec((t_new, bs, nq), toks),
            pl.BlockSpec((t_new, bs, nk), toks), pl.BlockSpec((t_new, bs, nk), toks),
            pl.BlockSpec((bs, p, nk), seqs), pl.BlockSpec((bs, p, nk), seqs),
        ],
        out_specs=[pl.BlockSpec((t_new, bs, nq), toks),
                   pl.BlockSpec((bs, p, nk), seqs), pl.BlockSpec((bs, p, nk), seqs)],
        out_shape=[jax.ShapeDtypeStruct((t_new, nseq, nq), _F32),
                   jax.ShapeDtypeStruct(cache_k.shape, _F32), jax.ShapeDtypeStruct(cache_v.shape, _F32)],
        compiler_params=_params("parallel"),
        name="swa_sample",
    )(sinks, q, k_new, v_new, cache_k, cache_v)


def _softplus(z):
    return jnp.maximum(z, 0.0) + jnp.log1p(jnp.exp(-jnp.abs(z)))


def _gelu_tanh(x):
    return 0.5 * x * (1.0 + jnp.tanh(np.float32(np.sqrt(2.0 / np.pi)) * (x + 0.044715 * (x * x * x))))


def _rg_gates(xc, wa_ref, ba_ref, wx_ref, bx_ref, sp_neg_lam):
    r = _sigmoid(_dot(xc, wa_ref[...]) + ba_ref[...])
    i = _sigmoid(_dot(xc, wx_ref[...]) + bx_ref[...])
    a = jnp.exp((-_RG_C) * r * sp_neg_lam)
    gap = jnp.maximum(1.0 - a * a, 0.0)
    mult = jnp.where(gap > 0.0, gap * lax.rsqrt(gap), 0.0)
    return a, mult * (i * xc)


def _rg_prompt_kernel(xr_ref, xg_ref, cw_ref, cb_ref, wa_ref, ba_ref, wx_ref, bx_ref, lam_ref,
                      o_ref, conv_ref, h_ref, xpad, a_s, b_s):
    t_len, w = xr_ref.shape
    cw = cw_ref.shape[0]
    xpad[0:_SUBLANES, :] = jnp.zeros((_SUBLANES, w), _F32)
    xpad[_SUBLANES:, :] = xr_ref[...]
    sp = _softplus(-lam_ref[...])
    for c in range(t_len // _RG_CHUNK):
        r0 = c * _RG_CHUNK
        xc = cb_ref[...]
        for j in range(cw):
            lo = _SUBLANES + r0 - (cw - 1) + j
            xc = xc + xpad[lo:lo + _RG_CHUNK, :] * cw_ref[j:j + 1, :]
        a, b = _rg_gates(xc, wa_ref, ba_ref, wx_ref, bx_ref, sp)
        a_s[r0:r0 + _RG_CHUNK, :] = a
        b_s[r0:r0 + _RG_CHUNK, :] = b

    row = lax.broadcasted_iota(jnp.int32, (_SUBLANES, w), 0)

    def group(g, h):
        r0 = pl.multiple_of(g * _SUBLANES, _SUBLANES)
        a = a_s[pl.ds(r0, _SUBLANES), :]
        b = b_s[pl.ds(r0, _SUBLANES), :]
        sh = 1
        while sh < _SUBLANES:
            a_prev = jnp.where(row >= sh, pltpu.roll(a, sh, axis=0), 1.0)
            b_prev = jnp.where(row >= sh, pltpu.roll(b, sh, axis=0), 0.0)
            b = a * b_prev + b
            a = a * a_prev
            sh *= 2
        hs = a * h + b
        o_ref[pl.ds(r0, _SUBLANES), :] = _gelu_tanh(xg_ref[pl.ds(r0, _SUBLANES), :]) * hs
        return hs[_SUBLANES - 1:_SUBLANES, :]

    h_last = lax.fori_loop(0, t_len // _SUBLANES, group, jnp.zeros((1, w), _F32), unroll=2)
    h_ref[0] = h_last
    conv_ref[0] = xr_ref[t_len - (cw - 1):t_len, :]


def _rg_prompt_call(xr, xg, conv_w, conv_b, wa_bd, ba, wx_bd, bx, lam, batch):
    n, w = xr.shape
    t_len = n // batch
    cw = conv_w.shape[0]
    seq = lambda b: (b, 0)
    const = lambda b: (0, 0)
    vec = pl.BlockSpec((1, w), const)
    mat = pl.BlockSpec((w, w), const)
    return pl.pallas_call(
        _rg_prompt_kernel,
        grid=(batch,),
        in_specs=[pl.BlockSpec((t_len, w), seq), pl.BlockSpec((t_len, w), seq),
                  pl.BlockSpec((cw, w), const), vec, mat, vec, mat, vec, vec],
        out_specs=[pl.BlockSpec((t_len, w), seq),
                   pl.BlockSpec((1, cw - 1, w), lambda b: (b, 0, 0)),
                   pl.BlockSpec((1, 1, w), lambda b: (b, 0, 0))],
        out_shape=[jax.ShapeDtypeStruct((n, w), _F32),
                   jax.ShapeDtypeStruct((batch, cw - 1, w), _F32),
                   jax.ShapeDtypeStruct((batch, 1, w), _F32)],
        scratch_shapes=[pltpu.VMEM((t_len + _SUBLANES, w), _F32),
                        pltpu.VMEM((t_len, w), _F32), pltpu.VMEM((t_len, w), _F32)],
        compiler_params=_params("parallel"),
        name="rglru_prompt",
    )(xr, xg, conv_w, conv_b, wa_bd, ba, wx_bd, bx, lam)


def _rg_sample_kernel(xr_ref, xg_ref, conv0_ref, h0_ref, cw_ref, cb_ref, wa_ref, ba_ref, wx_ref, bx_ref, lam_ref,
                      o_ref, conv_ref, h_ref):
    cw = cw_ref.shape[0]
    t_new = xr_ref.shape[0]
    sp = _softplus(-lam_ref[...])
    hist = [conv0_ref[j] for j in range(cw - 1)] + [xr_ref[t] for t in range(t_new)]
    h = h0_ref[...]
    for t in range(t_new):
        xc = cb_ref[...]
        for j in range(cw):
            xc = xc + hist[t + j] * cw_ref[j:j + 1, :]
        a, b = _rg_gates(xc, wa_ref, ba_ref, wx_ref, bx_ref, sp)
        h = a * h + b
        o_ref[t] = _gelu_tanh(xg_ref[t]) * h
    h_ref[...] = h
    for j in range(cw - 1):
        conv_ref[j] = hist[t_new + j]


def _rg_sample_call(xr_t, xg_t, conv0_t, h0, conv_w, conv_b, wa_bd, ba, wx_bd, bx, lam):
    t_new, nseq, w = xg_t.shape
    return pl.pallas_call(
        _rg_sample_kernel,
        out_shape=[jax.ShapeDtypeStruct((t_new, nseq, w), _F32), jax.ShapeDtypeStruct(conv0_t.shape, _F32),
                   jax.ShapeDtypeStruct((nseq, w), _F32)],
        compiler_params=pltpu.CompilerParams(vmem_limit_bytes=_VMEM_LIMIT),
        name="rglru_sample",
    )(xr_t, xg_t, conv0_t, h0, conv_w, conv_b, wa_bd, ba, wx_bd, bx, lam)


def _row_bcast(x, r, n):
    return jnp.broadcast_to(x[r:r + 1, :], (n, x.shape[1]))


def _chunk_cumsum(x):
    n_tiles = x.shape[0] // _SUBLANES
    row = lax.broadcasted_iota(jnp.int32, (_SUBLANES, x.shape[1]), 0)
    tiles = []
    carry = None
    for i in range(n_tiles):
        t = x[i * _SUBLANES:(i + 1) * _SUBLANES, :]
        sh = 1
        while sh < _SUBLANES:
            t = t + jnp.where(row >= sh, pltpu.roll(t, sh, axis=0), 0.0)
            sh *= 2
        if carry is not None:
            t = t + carry
        carry = _row_bcast(t, _SUBLANES - 1, _SUBLANES)
        tiles.append(t)
    return tiles[0] if n_tiles == 1 else jnp.concatenate(tiles, axis=0)


def _level_reference(b, m):
    n = b.shape[0]
    if 2 * m >= _SUBLANES:
        pieces = [_row_bcast(b, lo + m - 1, 2 * m) for lo in range(0, n, 2 * m)]
        return pieces[0] if len(pieces) == 1 else jnp.concatenate(pieces, axis=0)
    row = lax.broadcasted_iota(jnp.int32, (_SUBLANES, b.shape[1]), 0)
    tiles = []
    for i in range(n // _SUBLANES):
        t = b[i * _SUBLANES:(i + 1) * _SUBLANES, :]
        ref = None
        for lo in range(0, _SUBLANES, 2 * m):
            piece = _row_bcast(t, lo + m - 1, _SUBLANES)
            ref = piece if ref is None else jnp.where(row >= lo, piece, ref)
        tiles.append(ref)
    return tiles[0] if len(tiles) == 1 else jnp.concatenate(tiles, axis=0)


def _hgrn_gates(fz, lb):
    f = lb + (1.0 - lb) * _sigmoid(fz)
    return jnp.log(f), 1.0 - f


def _hgrn_chunk(q, fz, v, lb, state, n_valid, state_is_vk):
    n, kd = q.shape
    log_f, k = _hgrn_gates(fz, lb)
    if n_valid < n:
        valid = lax.broadcasted_iota(jnp.int32, (n, kd), 0) < n_valid
        log_f = jnp.where(valid, log_f, 0.0)
        k = jnp.where(valid, k, 0.0)
    b = _chunk_cumsum(log_f)
    b_last = _row_bcast(b, n - 1, n)

    q_in = q * jnp.exp(b)
    k_end = k * jnp.exp(b_last - b)
    if state_is_vk:
        o = _dot_nt(q_in, state)
        new_state = jnp.exp(b_last[0:1, :]) * state + _dot_tn(v, k_end)
    else:
        o = _dot(q_in, state)
        decay = jnp.exp(jnp.broadcast_to(b_last[0:1, :], (v.shape[1], kd))).T
        new_state = decay * state + _dot_tn(k_end, v)

    row = lax.broadcasted_iota(jnp.int32, (n, n), 0)
    col = lax.broadcasted_iota(jnp.int32, (n, n), 1)
    upper = lax.broadcasted_iota(jnp.int32, (n, kd), 0)
    scores = jnp.where(row == col, jnp.sum(q * k, axis=-1, keepdims=True), 0.0)
    m = 1
    while m < n_valid:
        e = jnp.exp(-jnp.abs(b - _level_reference(b, m)))
        z = jnp.where((upper & m) != 0, q, k) * e
        pair = jnp.logical_and((row & m) != 0, (row ^ m) >> (m.bit_length() - 1) == col >> (m.bit_length() - 1))
        scores = scores + jnp.where(pair, _dot_nt(z, z), 0.0)
        m *= 2
    return o + _dot(scores, v), new_state


def _lower_bound(logits, layer):
    m = jnp.max(logits, axis=0, keepdims=True)
    e = jnp.exp(logits - m)
    return jnp.sum(e[1:layer + 1, :], axis=0, keepdims=True) / jnp.sum(e, axis=0, keepdims=True)


def _hgrn_out(o, g, gnorm):
    return _rms(o, gnorm) * _silu(g)


def _hgrn_prompt_kernel(q_ref, f_ref, v_ref, g_ref, lbl_ref, gn_ref, o_ref, s_ref,
                        st_scr, st0_scr, *, layer):
    rows_blk, width = q_ref.shape
    kd = width // _HG_HEADS
    n_chunks = rows_blk // _HG_CHUNK
    heads = [slice(h * kd, (h + 1) * kd) for h in range(_HG_HEADS)]
    lb = _lower_bound(lbl_ref[...], layer)
    gn = gn_ref[...]

    @pl.when(pl.program_id(2) == 0)
    def _():
        st_scr[...] = jnp.zeros(st_scr.shape, _F32)

    st0_scr[...] = st_scr[...]

    def chunk_rows(c):
        return pl.ds(pl.multiple_of(c * _HG_CHUNK, _HG_CHUNK), _HG_CHUNK)

    def prepare(c):
        rows = chunk_rows(c)
        log_f, k = _hgrn_gates(f_ref[rows, :], lb)
        b = _chunk_cumsum(log_f)
        b_end = b[_HG_CHUNK - 1:_HG_CHUNK, :]
        dec = jnp.exp(b_end)
        e_b = jnp.exp(b)
        k_start = k / e_b
        ops = ((q_ref[rows, :] * e_b).astype(_BF16), k_start.astype(_BF16), (k_start * dec).astype(_BF16),
               v_ref[rows, :].astype(_BF16), dec)
        return ops, b_end

    row = lax.broadcasted_iota(jnp.int32, (_HG_CHUNK, _HG_CHUNK), 0)
    col = lax.broadcasted_iota(jnp.int32, (_HG_CHUNK, _HG_CHUNK), 1)

    def finish(c, h, o):
        rows = chunk_rows(c)
        o_ref[rows, heads[h]] = _hgrn_out(o, g_ref[rows, heads[h]], gn)

    def contract(c, ops):
        qs, ks, ke, vb, dec = ops
        states = [st_scr[h] for h in range(_HG_HEADS)]
        scores = [_dot_nt(qs[:, l], ks[:, l]) for l in heads]
        carried = [_dot_nt(qs[:, l], st) for l, st in zip(heads, states)]
        incs = [_dot_tn(vb[:, l], ke[:, l]) for l in heads]
        outs = [_dot(jnp.where(row >= col, s, 0.0), vb[:, l]) for s, l in zip(scores, heads)]
        for h, l in enumerate(heads):
            st_scr[h] = states[h] * dec[:, l] + incs[h]
        for h in range(_HG_HEADS):
            finish(c, h, outs[h] + carried[h])

    def step(c, carry):
        ops, min_b = carry
        contract(c, ops)
        ops, b_end = prepare(c + 1)
        return ops, jnp.minimum(min_b, b_end)

    last_ops, min_b = lax.fori_loop(0, n_chunks - 1, step, prepare(0), unroll=5)
    contract(n_chunks - 1, last_ops)

    @pl.when(jnp.min(min_b) <= -_HG_SAFE_LOG_DECAY)
    def _():
        st_scr[...] = st0_scr[...]

        def chunk(c, carry):
            rows = chunk_rows(c)
            for h, l in enumerate(heads):
                o, st_scr[h] = _hgrn_chunk(q_ref[rows, l], f_ref[rows, l], v_ref[rows, l],
                                           lb[:, l], st_scr[h], _HG_CHUNK, True)
                finish(c, h, o)
            return carry
        lax.fori_loop(0, n_chunks, chunk, 0)

    @pl.when(pl.program_id(2) == pl.num_programs(2) - 1)
    def _():
        for h in range(_HG_HEADS):
            s_ref[0, h] = st_scr[h].T


def _hgrn_prompt_call(q, fz, v, g, lb_logits, gnorm, batch, layer):
    n, width = q.shape
    t_len = n // batch
    kd = width // _C_HEADS
    gw = _HG_HEADS * kd
    rows_blk = min(_HG_ROWS, t_len)
    nt = t_len // rows_blk
    blk = pl.BlockSpec((rows_blk, gw), lambda b, h, t: (b * nt + t, h))
    return pl.pallas_call(
        functools.partial(_hgrn_prompt_kernel, layer=layer),
        grid=(batch, _C_HEADS // _HG_HEADS, nt),
        in_specs=[blk, blk, blk, blk,
                  pl.BlockSpec((lb_logits.shape[0], gw), lambda b, h, t: (0, h)),
                  pl.BlockSpec((1, kd), lambda b, h, t: (0, 0))],
        out_specs=[blk, pl.BlockSpec((1, _HG_HEADS, kd, kd), lambda b, h, t: (b, h, 0, 0))],
        out_shape=[jax.ShapeDtypeStruct((n, width), _F32),
                   jax.ShapeDtypeStruct((batch, _C_HEADS, kd, kd), _F32)],
        scratch_shapes=[pltpu.VMEM((_HG_HEADS, kd, kd), _F32)] * 2,
        compiler_params=_params("parallel", "parallel", "arbitrary"),
        name="hgrn2_prompt",
    )(q, fz, v, g, lb_logits, gnorm)


def _hgrn_sample_kernel(q_ref, f_ref, v_ref, g_ref, lbl_ref, gn_ref, s0_ref, o_ref, s_ref, *, layer):
    t_new, nseq, width = q_ref.shape
    n_heads = s0_ref.shape[1]
    kd = width // n_heads
    lanes = [slice(h * kd, (h + 1) * kd) for h in range(n_heads)]
    lb = _lower_bound(lbl_ref[...], layer)
    gn = gn_ref[...]
    q = [q_ref[t] for t in range(t_new)]
    v = [v_ref[t] for t in range(t_new)]
    keys, b = [], []
    for t in range(t_new):
        log_f, k = _hgrn_gates(f_ref[t], lb)
        keys.append(k)
        b.append(log_f if t == 0 else b[-1] + log_f)

    def per_head_sum(w):
        return jnp.concatenate([jnp.broadcast_to(jnp.sum(w[:, l], axis=-1, keepdims=True), (nseq, kd))
                                for l in lanes], axis=1)

    within = []
    for t in range(t_new):
        acc = per_head_sum(q[t] * keys[t]) * v[t]
        for s in range(t):
            acc = acc + per_head_sum(q[t] * keys[s] * jnp.exp(b[t] - b[s])) * v[s]
        within.append(acc)

    q_in = [q[t] * jnp.exp(b[t]) for t in range(t_new)]
    k_end = [keys[t] * jnp.exp(b[-1] - b[t]) for t in range(t_new)]
    decay = jnp.exp(b[-1])
    pairs = [(s, h) for h in range(n_heads) for s in range(nseq)]
    lhs = [_seq_tile([x[:, lanes[h]] for x in q_in], s) for s, h in pairs]
    k_seq = [_seq_tile([x[:, lanes[h]] for x in k_end], s) for s, h in pairs]
    v_seq = [_seq_tile([x[:, lanes[h]] for x in v], s) for s, h in pairs]
    carried = [_dot(x, s0_ref[s, h]) for x, (s, h) in zip(lhs, pairs)]
    incs = [_dot_tn(ks, vs) for ks, vs in zip(k_seq, v_seq)]
    carried_t = []
    for h in range(n_heads):
        dec_cols = jnp.concatenate([decay[:, lanes[h]], jnp.zeros((kd - nseq, kd), _F32)], axis=0).T
        for s in range(nseq):
            i = h * nseq + s
            s_ref[s, h] = jnp.broadcast_to(dec_cols[:, s:s + 1], (kd, kd)) * s0_ref[s, h] + incs[i]
        carried_t.append(_time_tiles(carried[h * nseq:(h + 1) * nseq], t_new))
    for t in range(t_new):
        o = within[t] + jnp.concatenate([carried_t[h][t] for h in range(n_heads)], axis=1)
        g = g_ref[t]
        o_ref[t] = jnp.concatenate([_hgrn_out(o[:, l], g[:, l], gn) for l in lanes], axis=1)


def _hgrn_sample_call(q, fz, v, g, lb_logits, gnorm, s0, layer):
    t_new, nseq, width = q.shape
    kd = width // _C_HEADS
    gw = _HG_HEADS * kd
    bs = _SAMPLE_SEQS
    blk = pl.BlockSpec((t_new, bs, gw), lambda i, h: (0, i, h))
    st = pl.BlockSpec((bs, _HG_HEADS, kd, kd), lambda i, h: (i, h, 0, 0))
    return pl.pallas_call(
        functools.partial(_hgrn_sample_kernel, layer=layer),
        grid=(nseq // bs, _C_HEADS // _HG_HEADS),
        in_specs=[blk, blk, blk, blk,
                  pl.BlockSpec((lb_logits.shape[0], gw), lambda i, h: (0, h)),
                  pl.BlockSpec((1, kd), lambda i, h: (0, 0)), st],
        out_specs=[blk, st],
        out_shape=[jax.ShapeDtypeStruct((t_new, nseq, width), _F32), jax.ShapeDtypeStruct(s0.shape, _F32)],
        compiler_params=_params("parallel", "parallel"),
        name="hgrn2_sample",
    )(q, fz, v, g, lb_logits, gnorm, s0)


def _rope_tables(pos):
    half = _ROT_DIM // 2
    inv_freq = _ROPE_THETA ** (-jnp.arange(0, _ROT_DIM, 2, dtype=_F32) / _ROT_DIM)
    ang = pos.astype(_F32)[:, None] * inv_freq[None, :]
    cos, sin = jnp.cos(ang), jnp.sin(ang)
    ones = jnp.ones((pos.shape[0], _A_HEAD_DIM - _ROT_DIM), _F32)
    zeros = jnp.zeros((pos.shape[0], _A_HEAD_DIM - half), _F32)
    zeros_h = jnp.zeros((pos.shape[0], half), _F32)
    reps = _LANES // _A_HEAD_DIM
    cos_t = jnp.tile(jnp.concatenate([cos, cos, ones], axis=1), (1, reps))
    sin_lo = jnp.tile(jnp.concatenate([-sin, zeros], axis=1), (1, reps))
    sin_hi = jnp.tile(jnp.concatenate([zeros_h, sin, ones * 0.0], axis=1), (1, reps))
    return cos_t, sin_lo, sin_hi


def _block_diag(w):
    nb, bd, _ = w.shape
    eye = jnp.eye(nb, dtype=w.dtype)
    return (w[:, :, None, :] * eye[:, None, :, None]).reshape(nb * bd, nb * bd)


def kernel(x_prompt, x_sample, c_prompt, c_sample, cache_k_win, cache_v_win, state_conv_rglru,
           state_h_rglru, state_s_hgrn, norm_pre, norm_post, ada_w, ada_b, ffn1_w_in, ffn1_w_out,
           ffn2_w_in, ffn2_w_out, even_w_in, even_w_out, attn_sinks, rg_conv_w, rg_conv_b, rg_wa,
           rg_ba, rg_wx, rg_bx, rg_lambda, odd_w_in, odd_w_out, hgrn_lb_logits, hgrn_gnorm):
    bp, tp, d = x_prompt.shape
    bs, ts, _ = x_sample.shape
    depth = norm_pre.shape[0]
    n_sub = depth * _N_SUB
    nk = _A_KV_HEADS * _A_HEAD_DIM
    win = cache_k_win.shape[2]
    cw = rg_conv_w.shape[1]
    bw = rg_conv_w.shape[2]
    kd = state_s_hgrn.shape[3]

    cast = lambda w: w.astype(_BF16)
    ffn1_in, ffn1_out, ffn2_in, ffn2_out = ffn1_w_in, ffn1_w_out, ffn2_w_in, ffn2_w_out
    ev_in, ev_out, od_in, od_out = even_w_in, even_w_out, odd_w_in, odd_w_out
    gpre = norm_pre.reshape(n_sub, 1, d)
    gpost = norm_post.reshape(n_sub, 1, d)

    mod = _ada_call(jnp.concatenate([c_prompt, c_sample], axis=0),
                    ada_w.reshape(n_sub, d, 3 * d), ada_b.reshape(n_sub, 1, 3 * d))
    mod_p = mod[:, :bp].reshape(n_sub, bp, 1, 3 * d)
    mod_s = mod[:, bp:].reshape(n_sub, 1, bs, 3 * d)

    tabs_p = _rope_tables(jnp.arange(tp, dtype=jnp.int32))
    tabs_s = tuple(jnp.repeat(t, bs, axis=0) for t in _rope_tables(_PAST_LEN + jnp.arange(ts, dtype=jnp.int32)))

    time_major = lambda a: a.reshape(ts, bs, a.shape[-1])
    xp = x_prompt.reshape(bp * tp, d)
    xs = x_sample.transpose(1, 0, 2).reshape(ts * bs, d)
    groups = {"p": dict(mod=mod_p, seq_rows=tp, tabs=tabs_p), "s": dict(mod=mod_s, seq_rows=ts * bs, tabs=tabs_s)}
    outs = {g: dict(k=[], v=[], conv=[], h=[], s=[]) for g in groups}

    for l in range(depth):
        s0, s1, s2 = l * _N_SUB, l * _N_SUB + 1, l * _N_SUB + 2
        xp, xs = _ffn_call(xp, xs, mod_p, mod_s, s0, l, gpre, ffn1_in, ffn1_out, gpost, 0.5, tp)
        acts = {}
        for name, x in (("p", xp), ("s", xs)):
            grp = groups[name]
            mod4, tps = grp["mod"], grp["seq_rows"]
            if l % 2 == 0:
                e = l // 2
                q, k, v, xg, xr = _even_in_call(x, mod4, s1, e, gpre, ev_in, grp["tabs"], tps)
                wa_bd, wx_bd = cast(_block_diag(rg_wa[e])), cast(_block_diag(rg_wx[e]))
                vecs = [a[e].reshape(1, bw) for a in (rg_conv_b, rg_ba, rg_bx, rg_lambda)]
                if name == "p":
                    o_a = _swa_prompt_call(q, k, v, attn_sinks[e], bp)
                    o_b, conv, h_last = _rg_prompt_call(xr, xg, rg_conv_w[e], vecs[0], wa_bd, vecs[1], wx_bd,
                                                        vecs[2], vecs[3], bp)
                    outs[name]["k"].append(k.reshape(bp, tp, _A_KV_HEADS, _A_HEAD_DIM)[:, tp - win:])
                    outs[name]["v"].append(v.reshape(bp, tp, _A_KV_HEADS, _A_HEAD_DIM)[:, tp - win:])
                    outs[name]["h"].append(h_last.reshape(bp, bw))
                else:
                    o_a, kw, vw = _swa_sample_call(time_major(q), time_major(k), time_major(v),
                                                   cache_k_win[e].reshape(bs, win, nk),
                                                   cache_v_win[e].reshape(bs, win, nk), attn_sinks[e])
                    o_b, conv_t, h_last = _rg_sample_call(time_major(xr), time_major(xg),
                                                          state_conv_rglru[e].transpose(1, 0, 2), state_h_rglru[e],
                                                          rg_conv_w[e], vecs[0], wa_bd, vecs[1], wx_bd, vecs[2], vecs[3])
                    o_a, o_b = o_a.reshape(ts * bs, -1), o_b.reshape(ts * bs, bw)
                    conv = conv_t.transpose(1, 0, 2)
                    outs[name]["k"].append(kw.reshape(bs, win, _A_KV_HEADS, _A_HEAD_DIM))
                    outs[name]["v"].append(vw.reshape(bs, win, _A_KV_HEADS, _A_HEAD_DIM))
                    outs[name]["h"].append(h_last)
                outs[name]["conv"].append(conv)
                acts[name] = [o_a, o_b]
                widx, w_mix = e, ev_out
            else:
                o = l // 2
                q, fz, v, g = _odd_in_call(x, mod4, s1, o, gpre, od_in, tps)
                gn = hgrn_gnorm[o].reshape(1, kd)
                if name == "p":
                    y, s_last = _hgrn_prompt_call(q, fz, v, g, hgrn_lb_logits, gn, bp, l)
                else:
                    y, s_last = _hgrn_sample_call(time_major(q), time_major(fz), time_major(v), time_major(g),
                                                  hgrn_lb_logits, gn, state_s_hgrn[o], l)
                    y = y.reshape(ts * bs, -1)
                outs[name]["s"].append(s_last)
                acts[name] = [y]
                widx, w_mix = o, od_out
        xp, xs = _ffn_call(xp, xs, mod_p, mod_s, s2, l, gpre, ffn2_in, ffn2_out, gpost, 0.5, tp,
                           (s1, widx, w_mix, acts["p"], acts["s"]))

    ys = {"p": xp.reshape(bp, tp, d), "s": xs.reshape(ts, bs, d).transpose(1, 0, 2)}
    res = []
    for name in ("p", "s"):
        o = outs[name]
        res.append((jnp.stack(o["k"]), jnp.stack(o["v"]), jnp.stack(o["conv"]), jnp.stack(o["h"]), jnp.stack(o["s"])))
    return (ys["p"], ys["s"]) + res[0] + res[1]
```

```python
import functools

import jax
import jax.numpy as jnp
import numpy as np
from jax import lax
from jax.experimental import pallas as pl
from jax.experimental.pallas import tpu as pltpu

_F32 = jnp.float32
_BF16 = jnp.bfloat16

_EPS = 1e-6
_A_HEADS = 8
_A_KV_HEADS = 2
_A_HEAD_DIM = 64
_A_GROUP = _A_HEADS // _A_KV_HEADS
_WINDOW = 128
_ROPE_THETA = 500000.0
_ROT_DIM = _A_HEAD_DIM // 4
_RG_C = 8.0
_C_HEADS = 8
_PAST_LEN = 16384
_N_SUB = 3

_LANES = 128
_SUBLANES = 8
_VMEM_BYTES = 64 * 1024 * 1024
_VMEM_LIMIT = _VMEM_BYTES * 3 // 4
_VMEM_LIMIT_FFN = _VMEM_BYTES * 7 // 8

_ROW_TILE = 512
_FF_CHUNK = 256
_RG_CHUNK = 256
_HG_CHUNK = 64
_HG_HEADS = 4
_HG_ROWS = 1024
_HG_SAFE_LOG_DECAY = 80.0
_SAMPLE_SEQS = 8
_SWA_QBLOCKS = 4


def _dot(a, b):
    return jnp.dot(a.astype(_BF16), b.astype(_BF16), preferred_element_type=_F32)


def _dot_nt(a, b):
    return lax.dot_general(a.astype(_BF16), b.astype(_BF16), (((1,), (1,)), ((), ())),
                           preferred_element_type=_F32)


def _dot_tn(a, b):
    return lax.dot_general(a.astype(_BF16), b.astype(_BF16), (((0,), (0,)), ((), ())),
                           preferred_element_type=_F32)


def _sigmoid(x):
    return 1.0 / (1.0 + jnp.exp(-x))


def _silu(x):
    return x * _sigmoid(x)


def _rms(x, gain):
    inv = lax.rsqrt(jnp.mean(x * x, axis=-1, keepdims=True) + _EPS)
    return x * inv * gain


def _per_seq(a, r):
    n = a.shape[0]
    return a if r in (1, n) else a.reshape(n // r, r, a.shape[1])


def _pre(x, mod, gain):
    n, d = x.shape
    h = _rms(_per_seq(x, mod.shape[0]), gain) * (1.0 + mod[:, d:2 * d]) + mod[:, :d]
    return h.reshape(n, d)


def _post(x, y, mod, gain, res_w):
    n, d = x.shape
    r = mod.shape[0]
    out = _per_seq(x, r) + (res_w * (1.0 + mod[:, 2 * d:])) * _rms(_per_seq(y, r), gain)
    return out.reshape(n, d)


def _params(*sem):
    return pltpu.CompilerParams(dimension_semantics=sem, vmem_limit_bytes=_VMEM_LIMIT)


def _ada_kernel(c_ref, w_ref, b_ref, o_ref):
    o_ref[...] = _dot(_silu(c_ref[...]), w_ref[...]) + b_ref[...]


def _ada_call(c_all, ada_w, ada_b):
    m, d = c_all.shape
    n_sub = ada_w.shape[0]
    n = ada_w.shape[-1]
    tn = n // 2
    return pl.pallas_call(
        _ada_kernel,
        grid=(n_sub, n // tn),
        in_specs=[
            pl.BlockSpec((m, d), lambda s, j: (0, 0)),
            pl.BlockSpec((None, d, tn), lambda s, j: (s, 0, j)),
            pl.BlockSpec((None, 1, tn), lambda s, j: (s, 0, j)),
        ],
        out_specs=pl.BlockSpec((None, m, tn), lambda s, j: (s, 0, j)),
        out_shape=jax.ShapeDtypeStruct((n_sub, m, n), _F32),
        compiler_params=_params("parallel", "parallel"),
        name="ada_mod",
    )(c_all, ada_w, ada_b)


def _ffn_kernel(xp_ref, xs_ref, modp_ref, mods_ref, gpre_ref, win_ref, wout_ref, gpost_ref, *refs,
                res_w, n_load, n_prompt, n_acts):
    op_ref, os_ref, win_bf, wout_bf = refs[-4:]
    i = pl.program_id(0)
    ld = win_ref.shape[1]
    dff = wout_bf.shape[0] * wout_bf.shape[1]

    @pl.when(i < n_load)
    def _():
        win_bf[i] = win_ref[...].astype(_BF16)
        wout_bf[i] = wout_ref[...].astype(_BF16)

    def w_in_cols(lo):
        return win_bf[lo // ld, :, lo % ld:lo % ld + _FF_CHUNK]

    def row_tile(x_ref, mod_ref, mmod_ref, act_refs, o_ref):
        x = x_ref[...]
        if n_acts:
            mgpost_ref, mw_ref = refs[2:4]
            y = None
            off = 0
            for a_ref in act_refs:
                k = a_ref.shape[1]
                t = _dot(a_ref[...], mw_ref[off:off + k, :])
                y = t if y is None else y + t
                off += k
            x = _post(x, y, mmod_ref[0], mgpost_ref[...], 1.0)
        mod = mod_ref[0]
        h = _pre(x, mod, gpre_ref[...]).astype(_BF16)
        acc = jnp.zeros(x.shape, _F32)
        for j in range(dff // _FF_CHUNK):
            lo = j * _FF_CHUNK
            g = _dot(h, w_in_cols(lo))
            u = _dot(h, w_in_cols(dff + lo))
            acc = acc + _dot(_silu(g) * u, wout_bf[j])
        o_ref[...] = _post(x, acc, mod, gpost_ref[...], res_w)

    acts = refs[4:4 + 2 * n_acts] if n_acts else ()

    @pl.when(jnp.logical_and(i >= n_load, i < n_load + n_prompt))
    def _():
        row_tile(xp_ref, modp_ref, refs[0] if n_acts else None, acts[:n_acts], op_ref)

    @pl.when(i == n_load + n_prompt)
    def _():
        row_tile(xs_ref, mods_ref, refs[1] if n_acts else None, acts[n_acts:], os_ref)


def _ffn_call(xp, xs, modp, mods, sub, layer, gpre, w_in, w_out, gpost, res_w, seq_rows, mixer=None):
    n, d = xp.shape
    tm = xs.shape[0]
    assert n % tm == 0 and seq_rows % tm == 0 and mods.shape[2] * (tm // mods.shape[2]) == tm
    dff = w_out.shape[1]
    n_load = dff // _FF_CHUNK
    n_prompt = n // tm
    tiles_per_seq = seq_rows // tm
    tile = lambda i: jnp.clip(i - n_load, 0, n_prompt - 1)
    rows_p = pl.BlockSpec((tm, d), lambda i: (tile(i), 0))
    once = dict(pipeline_mode=pl.Buffered(1))
    rows_s = pl.BlockSpec((tm, d), lambda i: (0, 0), **once)
    modp_spec = lambda s: pl.BlockSpec((None, 1, 1, 3 * d), lambda i: (s, tile(i) // tiles_per_seq, 0, 0))
    mods_spec = lambda s: pl.BlockSpec((None, 1, mods.shape[2], 3 * d), lambda i: (s, 0, 0, 0), **once)
    gain_spec = lambda s: pl.BlockSpec((None, 1, d), lambda i: (s, 0, 0))
    chunk = lambda i: jnp.minimum(i, n_load - 1)
    in_specs = [
        rows_p, rows_s, modp_spec(sub), mods_spec(sub), gain_spec(sub),
        pl.BlockSpec((None, d, 2 * dff // n_load), lambda i: (layer, 0, chunk(i))),
        pl.BlockSpec((None, _FF_CHUNK, d), lambda i: (layer, chunk(i), 0)),
        gain_spec(sub),
    ]
    args = [xp, xs, modp, mods, gpre, w_in, w_out, gpost]
    n_acts = 0
    if mixer is not None:
        msub, widx, mw, acts_p, acts_s = mixer
        n_acts = len(acts_p)
        in_specs += [modp_spec(msub), mods_spec(msub), gain_spec(msub),
                     pl.BlockSpec((None, mw.shape[1], d), lambda i: (widx, 0, 0), pipeline_mode=pl.Buffered(1))]
        in_specs += [pl.BlockSpec((tm, a.shape[1]), lambda i: (tile(i), 0)) for a in acts_p]
        in_specs += [pl.BlockSpec((tm, a.shape[1]), lambda i: (0, 0), **once) for a in acts_s]
        args += [modp, mods, gpost, mw] + list(acts_p) + list(acts_s)
    return pl.pallas_call(
        functools.partial(_ffn_kernel, res_w=res_w, n_load=n_load, n_prompt=n_prompt, n_acts=n_acts),
        grid=(n_load + n_prompt + 1,),
        in_specs=in_specs,
        out_specs=[rows_p, pl.BlockSpec((tm, d), lambda i: (0, 0))],
        out_shape=[jax.ShapeDtypeStruct((n, d), _F32), jax.ShapeDtypeStruct((tm, d), _F32)],
        scratch_shapes=[pltpu.VMEM((n_load, d, 2 * dff // n_load), _BF16),
                        pltpu.VMEM((n_load, _FF_CHUNK, d), _BF16)],
        compiler_params=pltpu.CompilerParams(dimension_semantics=("arbitrary",), vmem_limit_bytes=_VMEM_LIMIT_FFN),
        name="ffn_sublayer" if mixer is None else "mixer_out_ffn",
    )(*args)


def _rope(x, cos, sin_lo, sin_hi):
    outs = []
    for j in range(x.shape[1] // _LANES):
        xc = x[:, j * _LANES:(j + 1) * _LANES]
        nxt = pltpu.roll(xc, _LANES - _ROT_DIM // 2, axis=1)
        prv = pltpu.roll(xc, _ROT_DIM // 2, axis=1)
        outs.append(xc * cos + nxt * sin_lo + prv * sin_hi)
    return outs[0] if len(outs) == 1 else jnp.concatenate(outs, axis=1)


def _even_in_kernel(x_ref, mod_ref, gpre_ref, w_ref, cos_ref, slo_ref, shi_ref, *refs, tiles_per_seq):
    with_gates = len(refs) > 5
    outs = refs[7:] if with_gates else refs
    q_ref, k_ref, v_ref, xg_ref = outs[:4]
    nq, nk, nw = q_ref.shape[1], k_ref.shape[1], xg_ref.shape[1]
    tm = x_ref.shape[0]
    n_sub = 2 if with_gates else 1
    sub = tm // n_sub
    mod = mod_ref[0]
    ys = [_dot(_pre(x_ref[s * sub:(s + 1) * sub, :], mod, gpre_ref[...]), w_ref[...]) for s in range(n_sub)]
    if with_gates:
        cw_ref, cb_ref, wa_ref, ba_ref, wx_ref, bx_ref, lam_ref = refs[:7]
        a_ref, b_ref, tail_ref, carry = outs[4:]
        cw = cw_ref.shape[0]
        sp = _softplus(-lam_ref[...])

        @pl.when(pl.program_id(0) % tiles_per_seq == 0)
        def _():
            carry[...] = jnp.zeros(carry.shape, _F32)

        past = carry[...]
    for s, y in enumerate(ys):
        rows = slice(s * sub, (s + 1) * sub)
        cos, slo, shi = cos_ref[rows, :], slo_ref[rows, :], shi_ref[rows, :]
        q_ref[rows, :] = _rope(y[:, :nq], cos, slo, shi)
        k_ref[rows, :] = _rope(y[:, nq:nq + nk], cos, slo, shi)
        v_ref[rows, :] = y[:, nq + nk:nq + 2 * nk]
        xg_ref[rows, :] = y[:, nq + 2 * nk:nq + 2 * nk + nw]
        xr = y[:, nq + 2 * nk + nw:]
        if not with_gates:
            outs[4][rows, :] = xr
            continue
        hist = jnp.concatenate([past, xr], axis=0)
        xc = cb_ref[...]
        for j in range(cw):
            lo = _SUBLANES - (cw - 1) + j
            xc = xc + hist[lo:lo + sub, :] * cw_ref[j:j + 1, :]
        a, b = _rg_gates(xc, wa_ref, ba_ref, wx_ref, bx_ref, sp)
        a_ref[rows, :] = a
        b_ref[rows, :] = b
        past = xr[sub - _SUBLANES:, :]
    if with_gates:
        carry[...] = past
        tail_ref[0] = past


def _even_in_call(x, mod4, sub, e, gpre, w_in, rope_tabs, seq_rows, gates=None):
    n, d = x.shape
    r = mod4.shape[2]
    tm = min(_ROW_TILE, n)
    tiles_per_seq = tab_tiles = seq_rows // tm
    nq = _A_HEADS * _A_HEAD_DIM
    nk = _A_KV_HEADS * _A_HEAD_DIM
    nw = (w_in.shape[-1] - nq - 2 * nk) // 2
    row = lambda i: (i, 0)
    const = lambda i: (0, 0)
    tab = pl.BlockSpec((tm, _LANES), lambda i: (i % tab_tiles, 0))
    in_specs = [
        pl.BlockSpec((tm, d), row),
        pl.BlockSpec((None, 1, r, 3 * d), lambda i: (sub, i // tiles_per_seq, 0, 0)),
        pl.BlockSpec((None, 1, d), lambda i: (sub, 0, 0)),
        pl.BlockSpec((None, d, w_in.shape[-1]), lambda i: (e, 0, 0)),
        tab, tab, tab,
    ]
    widths = [nq, nk, nk, nw, nw] + ([nw] if gates else [])
    out_specs = [pl.BlockSpec((tm, c), row) for c in widths]
    out_shape = [jax.ShapeDtypeStruct((n, c), _F32) for c in widths]
    scratch = []
    if gates:
        in_specs += [pl.BlockSpec(g.shape, const) for g in gates]
        out_specs.append(pl.BlockSpec((1, _SUBLANES, nw), lambda i: (i // tiles_per_seq, 0, 0)))
        out_shape.append(jax.ShapeDtypeStruct((n // seq_rows, _SUBLANES, nw), _F32))
        scratch = [pltpu.VMEM((_SUBLANES, nw), _F32)]
    return pl.pallas_call(
        functools.partial(_even_in_kernel, tiles_per_seq=tiles_per_seq),
        grid=(n // tm,),
        in_specs=in_specs,
        out_specs=out_specs,
        out_shape=out_shape,
        scratch_shapes=scratch,
        compiler_params=_params("arbitrary" if gates else "parallel"),
        name="even_in_proj",
    )(x, mod4, gpre, w_in, *rope_tabs, *(gates or ()))


def _odd_in_kernel(x_ref, mod_ref, gpre_ref, w_ref, q_ref, f_ref, v_ref, g_ref):
    h = _pre(x_ref[...], mod_ref[0], gpre_ref[...])
    y = _dot(h, w_ref[...])
    n = q_ref.shape[1]
    q_ref[...] = y[:, :n]
    f_ref[...] = y[:, n:2 * n]
    v_ref[...] = y[:, 2 * n:3 * n]
    g_ref[...] = y[:, 3 * n:]


def _odd_in_call(x, mod4, sub, o, gpre, w_in, seq_rows):
    n, d = x.shape
    r = mod4.shape[2]
    tm = min(_ROW_TILE, n)
    tiles_per_seq = seq_rows // tm
    nw = w_in.shape[-1] // 4
    row = lambda i: (i, 0)
    return pl.pallas_call(
        _odd_in_kernel,
        grid=(n // tm,),
        in_specs=[
            pl.BlockSpec((tm, d), row),
            pl.BlockSpec((None, 1, r, 3 * d), lambda i: (sub, i // tiles_per_seq, 0, 0)),
            pl.BlockSpec((None, 1, d), lambda i: (sub, 0, 0)),
            pl.BlockSpec((None, d, 4 * nw), lambda i: (o, 0, 0)),
        ],
        out_specs=[pl.BlockSpec((tm, nw), row)] * 4,
        out_shape=[jax.ShapeDtypeStruct((n, nw), _F32)] * 4,
        compiler_params=_params("parallel"),
        name="odd_in_proj",
    )(x, mod4, gpre, w_in)


def _seq_tile(time_tiles, s):
    row = lax.broadcasted_iota(jnp.int32, time_tiles[0].shape, 0)
    out = jnp.zeros(time_tiles[0].shape, time_tiles[0].dtype)
    for t, x in enumerate(time_tiles):
        shift = (t - s) % _SUBLANES
        out = jnp.where(row == t, pltpu.roll(x, shift, axis=0) if shift else x, out)
    return out


def _time_tiles(seq_tiles, n_t):
    row = lax.broadcasted_iota(jnp.int32, seq_tiles[0].shape, 0)
    outs = []
    for t in range(n_t):
        acc = jnp.zeros(seq_tiles[0].shape, seq_tiles[0].dtype)
        for s, x in enumerate(seq_tiles):
            shift = (s - t) % _SUBLANES
            acc = jnp.where(row == s, pltpu.roll(x, shift, axis=0) if shift else x, acc)
        outs.append(acc)
    return outs


def _sink_softmax_pv(s, mask, sink, v):
    s = jnp.where(mask, s, -jnp.inf)
    m = jnp.maximum(jnp.max(s, axis=-1, keepdims=True), sink)
    p = jnp.exp(s - m)
    denom = jnp.sum(p, axis=-1, keepdims=True) + jnp.exp(sink - m)
    return _dot(p, v) / denom


def _swa_prompt_kernel(sink_ref, q_ref, kp_ref, kc_ref, vp_ref, vc_ref, o_ref):
    w = _WINDOW
    hd = _A_HEAD_DIM
    assert _LANES == 2 * hd and _A_GROUP % 2 == 0
    log2e = np.float32(np.log2(np.e))
    scale = np.float32(1.0 / np.sqrt(hd)) * log2e
    low = lax.broadcasted_iota(jnp.int32, (2 * w, _LANES), 1) < hd
    ones_lo = jnp.where(low, 1.0, 0.0).astype(_BF16)
    ones_hi = jnp.where(low, 0.0, 1.0).astype(_BF16)
    low_q = lax.broadcasted_iota(jnp.int32, (w, _LANES), 1) < hd
    row = lax.broadcasted_iota(jnp.int32, (w, 4 * w), 0)
    col = lax.broadcasted_iota(jnp.int32, (w, 4 * w), 1) & (2 * w - 1)
    pairs = range(_A_HEADS // 2)
    kv_of = [(2 * p) // _A_GROUP for p in pairs]

    def score_stage(qb):
        rows = slice(qb * w, (qb + 1) * w)
        if qb == 0:
            k2 = jnp.concatenate([kp_ref[...], kc_ref[0:w, :]], axis=0)
            v2 = jnp.concatenate([vp_ref[...], vc_ref[0:w, :]], axis=0)
            first = jnp.where(pl.program_id(1) > 0, 0, w)
        else:
            k2 = kc_ref[(qb - 1) * w:(qb + 1) * w, :]
            v2 = vc_ref[(qb - 1) * w:(qb + 1) * w, :]
            first = 0
        mask = jnp.logical_and(col > jnp.maximum(row, first - 1), col <= row + w)
        keys, vals = [], []
        for j in range(_A_KV_HEADS):
            own_k = jnp.where(low, k2, 0.0) if j == 0 else jnp.where(low, 0.0, k2)
            own_v = jnp.where(low, v2, 0.0) if j == 0 else jnp.where(low, 0.0, v2)
            oth_k = pltpu.roll(own_k, hd, axis=1)
            oth_v = pltpu.roll(own_v, hd, axis=1)
            lo_k, hi_k = (own_k, oth_k) if j == 0 else (oth_k, own_k)
            lo_v, hi_v = (own_v, oth_v) if j == 0 else (oth_v, own_v)
            keys.append(jnp.concatenate([lo_k, hi_k], axis=0).astype(_BF16))
            vals.append(jnp.concatenate([jnp.concatenate([lo_v.astype(_BF16), ones_lo], axis=1),
                                         jnp.concatenate([hi_v.astype(_BF16), ones_hi], axis=1)], axis=0))
        scores = [_dot_nt(q_ref[rows, p * _LANES:(p + 1) * _LANES] * scale, keys[kv_of[p]]) for p in pairs]
        return scores, mask, vals

    def softmax_stage(scores, mask):
        probs, sink_terms = [], []
        for p in pairs:
            s = jnp.where(mask, scores[p], -jnp.inf)
            halves = []
            for i in range(2):
                sh = s[:, i * 2 * w:(i + 1) * 2 * w]
                sink = sink_ref[2 * p + i] * log2e
                m = jnp.maximum(jnp.max(sh, axis=-1, keepdims=True), sink)
                halves.append((jnp.exp2(sh - m), jnp.exp2(sink - m)))
            probs.append(jnp.concatenate([halves[0][0], halves[1][0]], axis=1).astype(_BF16))
            sink_terms.append(jnp.where(low_q, halves[0][1], halves[1][1]))
        return probs, sink_terms

    def value_stage(qb, probs, sink_terms, vals):
        rows = slice(qb * w, (qb + 1) * w)
        for p in pairs:
            r = _dot(probs[p], vals[kv_of[p]])
            o_ref[rows, p * _LANES:(p + 1) * _LANES] = r[:, :_LANES] / (r[:, _LANES:] + sink_terms[p])

    n_qb = q_ref.shape[0] // w
    scored, soft = {}, {}
    for step in range(n_qb + 2):
        if step < n_qb:
            scored[step] = score_stage(step)
        if 0 <= step - 2 < n_qb:
            qb = step - 2
            value_stage(qb, *soft.pop(qb), scored.pop(qb)[2])
        if 0 <= step - 1 < n_qb:
            qb = step - 1
            soft[qb] = softmax_stage(*scored[qb][:2])


def _swa_prompt_call(q, k, v, sinks, batch):
    n, nq = q.shape
    nk = k.shape[1]
    w = _WINDOW
    qb = _SWA_QBLOCKS
    nb = n // batch // (w * qb)
    cur = lambda b, i: (b * nb + i, 0)
    prev = lambda b, i: ((b * nb + i) * qb - jnp.minimum(i, 1), 0)
    return pl.pallas_call(
        _swa_prompt_kernel,
        grid=(batch, nb),
        in_specs=[
            pl.BlockSpec(memory_space=pltpu.SMEM),
            pl.BlockSpec((qb * w, nq), cur),
            pl.BlockSpec((w, nk), prev), pl.BlockSpec((qb * w, nk), cur),
            pl.BlockSpec((w, nk), prev), pl.BlockSpec((qb * w, nk), cur),
        ],
        out_specs=pl.BlockSpec((qb * w, nq), cur),
        out_shape=jax.ShapeDtypeStruct((n, nq), _F32),
        compiler_params=_params("parallel", "parallel"),
        name="swa_prompt",
    )(sinks, q, k, k, v, v)


def _swa_sample_kernel(sink_ref, q_ref, kn_ref, vn_ref, ck_ref, cv_ref, o_ref, kw_ref, vw_ref, *, t_new):
    p = ck_ref.shape[1]
    scale = np.float32(1.0 / np.sqrt(_A_HEAD_DIM))
    rows = _A_GROUP * _SUBLANES
    t = lax.broadcasted_iota(jnp.int32, (rows, p + _SUBLANES), 0) & (_SUBLANES - 1)
    c = lax.broadcasted_iota(jnp.int32, (rows, p + _SUBLANES), 1)
    mask = jnp.logical_and(c <= t + p, c > t + p - _WINDOW)
    g_of_row = lax.broadcasted_iota(jnp.int32, (rows, 1), 0) >> (_SUBLANES.bit_length() - 1)
    nseq = ck_ref.shape[0]
    q_t = [q_ref[t] for t in range(t_new)]
    kn_t = [kn_ref[t] for t in range(t_new)]
    vn_t = [vn_ref[t] for t in range(t_new)]
    sinks = []
    for j in range(_A_KV_HEADS):
        sink = jnp.zeros((rows, 1), _F32)
        for g in range(_A_GROUP):
            sink = jnp.where(g_of_row == g, sink_ref[j * _A_GROUP + g], sink)
        sinks.append(sink)
    new_k, new_v, qs, keys, vals = [], [], [], [], []
    for s in range(nseq):
        q8, kn8, vn8 = _seq_tile(q_t, s), _seq_tile(kn_t, s), _seq_tile(vn_t, s)
        ck, cv = ck_ref[s], cv_ref[s]
        kw_ref[s, 0:p - t_new, :] = ck[t_new:, :]
        kw_ref[s, p - t_new:p, :] = kn8[0:t_new, :]
        vw_ref[s, 0:p - t_new, :] = cv[t_new:, :]
        vw_ref[s, p - t_new:p, :] = vn8[0:t_new, :]
        for j in range(_A_KV_HEADS):
            ks = slice(j * _A_HEAD_DIM, (j + 1) * _A_HEAD_DIM)
            keys.append(jnp.concatenate([ck[:, ks], kn8[:, ks]], axis=0))
            vals.append(jnp.concatenate([cv[:, ks], vn8[:, ks]], axis=0))
            qs.append(jnp.concatenate(
                [q8[:, (j * _A_GROUP + g) * _A_HEAD_DIM:(j * _A_GROUP + g + 1) * _A_HEAD_DIM]
                 for g in range(_A_GROUP)], axis=0))
    scores = [_dot_nt(qj, k) * scale for qj, k in zip(qs, keys)]
    outs = [_sink_softmax_pv(sc, mask, sinks[i % _A_KV_HEADS], v) for i, (sc, v) in enumerate(zip(scores, vals))]
    per_seq = []
    for s in range(nseq):
        heads = []
        for j in range(_A_KV_HEADS):
            o = outs[s * _A_KV_HEADS + j]
            heads.extend(o[g * _SUBLANES:(g + 1) * _SUBLANES, :] for g in range(_A_GROUP))
        per_seq.append(jnp.concatenate(heads, axis=1))
    for t, tile in enumerate(_time_tiles(per_seq, t_new)):
        o_ref[t] = tile


def _swa_sample_call(q, k_new, v_new, cache_k, cache_v, sinks):
    t_new, nseq, nq = q.shape
    nk = k_new.shape[2]
    p = cache_k.shape[1]
    bs = _SAMPLE_SEQS
    toks = lambda i: (0, i, 0)
    seqs = lambda i: (i, 0, 0)
    return pl.pallas_call(
        functools.partial(_swa_sample_kernel, t_new=t_new),
        grid=(nseq // bs,),
        in_specs=[
            pl.BlockSpec(memory_space=pltpu.SMEM),
            pl.BlockSpec((t_new, bs, nq), toks),
            pl.BlockSpec((t_new, bs, nk), toks), pl.BlockSpec((t_new, bs, nk), toks),
            pl.BlockSpec((bs, p, nk), seqs), pl.BlockSpec((bs, p, nk), seqs),
        ],
        out_specs=[pl.BlockSpec((t_new, bs, nq), toks),
                   pl.BlockSpec((bs, p, nk), seqs), pl.BlockSpec((bs, p, nk), seqs)],
        out_shape=[jax.ShapeDtypeStruct((t_new, nseq, nq), _F32),
                   jax.ShapeDtypeStruct(cache_k.shape, _F32), jax.ShapeDtypeStruct(cache_v.shape, _F32)],
        compiler_params=_params("parallel"),
        name="swa_sample",
    )(sinks, q, k_new, v_new, cache_k, cache_v)


def _softplus(z):
    return jnp.maximum(z, 0.0) + jnp.log1p(jnp.exp(-jnp.abs(z)))


def _gelu_tanh(x):
    return 0.5 * x * (1.0 + jnp.tanh(np.float32(np.sqrt(2.0 / np.pi)) * (x + 0.044715 * (x * x * x))))


def _rg_gates(xc, wa_ref, ba_ref, wx_ref, bx_ref, sp_neg_lam):
    r = _sigmoid(_dot(xc, wa_ref[...]) + ba_ref[...])
    i = _sigmoid(_dot(xc, wx_ref[...]) + bx_ref[...])
    a = jnp.exp((-_RG_C) * r * sp_neg_lam)
    gap = jnp.maximum(1.0 - a * a, 0.0)
    mult = jnp.where(gap > 0.0, gap * lax.rsqrt(gap), 0.0)
    return a, mult * (i * xc)


def _rg_prompt_kernel(a_ref, b_ref, xg_ref, o_ref, h_ref):
    t_len, w = a_ref.shape
    row = lax.broadcasted_iota(jnp.int32, (_SUBLANES, w), 0)

    def group(g, h):
        r0 = pl.multiple_of(g * _SUBLANES, _SUBLANES)
        a = a_ref[pl.ds(r0, _SUBLANES), :]
        b = b_ref[pl.ds(r0, _SUBLANES), :]
        sh = 1
        while sh < _SUBLANES:
            a_prev = jnp.where(row >= sh, pltpu.roll(a, sh, axis=0), 1.0)
            b_prev = jnp.where(row >= sh, pltpu.roll(b, sh, axis=0), 0.0)
            b = a * b_prev + b
            a = a * a_prev
            sh *= 2
        hs = a * h + b
        o_ref[pl.ds(r0, _SUBLANES), :] = _gelu_tanh(xg_ref[pl.ds(r0, _SUBLANES), :]) * hs
        return hs[_SUBLANES - 1:_SUBLANES, :]

    h_last = lax.fori_loop(0, t_len // _SUBLANES, group, jnp.zeros((1, w), _F32), unroll=2)
    h_ref[0] = h_last


def _rg_prompt_call(a, b, xg, batch):
    n, w = a.shape
    t_len = n // batch
    seq = pl.BlockSpec((t_len, w), lambda s: (s, 0))
    return pl.pallas_call(
        _rg_prompt_kernel,
        grid=(batch,),
        in_specs=[seq, seq, seq],
        out_specs=[seq, pl.BlockSpec((1, 1, w), lambda s: (s, 0, 0))],
        out_shape=[jax.ShapeDtypeStruct((n, w), _F32), jax.ShapeDtypeStruct((batch, 1, w), _F32)],
        compiler_params=_params("parallel"),
        name="rglru_prompt",
    )(a, b, xg)


def _rg_sample_kernel(xr_ref, xg_ref, conv0_ref, h0_ref, cw_ref, cb_ref, wa_ref, ba_ref, wx_ref, bx_ref, lam_ref,
                      o_ref, conv_ref, h_ref):
    cw = cw_ref.shape[0]
    t_new = xr_ref.shape[0]
    sp = _softplus(-lam_ref[...])
    hist = [conv0_ref[j] for j in range(cw - 1)] + [xr_ref[t] for t in range(t_new)]
    h = h0_ref[...]
    for t in range(t_new):
        xc = cb_ref[...]
        for j in range(cw):
            xc = xc + hist[t + j] * cw_ref[j:j + 1, :]
        a, b = _rg_gates(xc, wa_ref, ba_ref, wx_ref, bx_ref, sp)
        h = a * h + b
        o_ref[t] = _gelu_tanh(xg_ref[t]) * h
    h_ref[...] = h
    for j in range(cw - 1):
        conv_ref[j] = hist[t_new + j]


def _rg_sample_call(xr_t, xg_t, conv0_t, h0, conv_w, conv_b, wa_bd, ba, wx_bd, bx, lam):
    t_new, nseq, w = xg_t.shape
    return pl.pallas_call(
        _rg_sample_kernel,
        out_shape=[jax.ShapeDtypeStruct((t_new, nseq, w), _F32), jax.ShapeDtypeStruct(conv0_t.shape, _F32),
                   jax.ShapeDtypeStruct((nseq, w), _F32)],
        compiler_params=pltpu.CompilerParams(vmem_limit_bytes=_VMEM_LIMIT),
        name="rglru_sample",
    )(xr_t, xg_t, conv0_t, h0, conv_w, conv_b, wa_bd, ba, wx_bd, bx, lam)


def _row_bcast(x, r, n):
    return jnp.broadcast_to(x[r:r + 1, :], (n, x.shape[1]))


def _chunk_cumsum(x):
    n_tiles = x.shape[0] // _SUBLANES
    row = lax.broadcasted_iota(jnp.int32, (_SUBLANES, x.shape[1]), 0)
    tiles = []
    carry = None
    for i in range(n_tiles):
        t = x[i * _SUBLANES:(i + 1) * _SUBLANES, :]
        sh = 1
        while sh < _SUBLANES:
            t = t + jnp.where(row >= sh, pltpu.roll(t, sh, axis=0), 0.0)
            sh *= 2
        if carry is not None:
            t = t + carry
        carry = _row_bcast(t, _SUBLANES - 1, _SUBLANES)
        tiles.append(t)
    return tiles[0] if n_tiles == 1 else jnp.concatenate(tiles, axis=0)


def _level_reference(b, m):
    n = b.shape[0]
    if 2 * m >= _SUBLANES:
        pieces = [_row_bcast(b, lo + m - 1, 2 * m) for lo in range(0, n, 2 * m)]
        return pieces[0] if len(pieces) == 1 else jnp.concatenate(pieces, axis=0)
    row = lax.broadcasted_iota(jnp.int32, (_SUBLANES, b.shape[1]), 0)
    tiles = []
    for i in range(n // _SUBLANES):
        t = b[i * _SUBLANES:(i + 1) * _SUBLANES, :]
        ref = None
        for lo in range(0, _SUBLANES, 2 * m):
            piece = _row_bcast(t, lo + m - 1, _SUBLANES)
            ref = piece if ref is None else jnp.where(row >= lo, piece, ref)
        tiles.append(ref)
    return tiles[0] if len(tiles) == 1 else jnp.concatenate(tiles, axis=0)


def _hgrn_gates(fz, lb):
    f = lb + (1.0 - lb) * _sigmoid(fz)
    return jnp.log(f), 1.0 - f


def _hgrn_chunk(q, fz, v, lb, state, n_valid, state_is_vk):
    n, kd = q.shape
    log_f, k = _hgrn_gates(fz, lb)
    if n_valid < n:
        valid = lax.broadcasted_iota(jnp.int32, (n, kd), 0) < n_valid
        log_f = jnp.where(valid, log_f, 0.0)
        k = jnp.where(valid, k, 0.0)
    b = _chunk_cumsum(log_f)
    b_last = _row_bcast(b, n - 1, n)

    q_in = q * jnp.exp(b)
    k_end = k * jnp.exp(b_last - b)
    if state_is_vk:
        o = _dot_nt(q_in, state)
        new_state = jnp.exp(b_last[0:1, :]) * state + _dot_tn(v, k_end)
    else:
        o = _dot(q_in, state)
        decay = jnp.exp(jnp.broadcast_to(b_last[0:1, :], (v.shape[1], kd))).T
        new_state = decay * state + _dot_tn(k_end, v)

    row = lax.broadcasted_iota(jnp.int32, (n, n), 0)
    col = lax.broadcasted_iota(jnp.int32, (n, n), 1)
    upper = lax.broadcasted_iota(jnp.int32, (n, kd), 0)
    scores = jnp.where(row == col, jnp.sum(q * k, axis=-1, keepdims=True), 0.0)
    m = 1
    while m < n_valid:
        e = jnp.exp(-jnp.abs(b - _level_reference(b, m)))
        z = jnp.where((upper & m) != 0, q, k) * e
        pair = jnp.logical_and((row & m) != 0, (row ^ m) >> (m.bit_length() - 1) == col >> (m.bit_length() - 1))
        scores = scores + jnp.where(pair, _dot_nt(z, z), 0.0)
        m *= 2
    return o + _dot(scores, v), new_state


def _lower_bound(logits, layer):
    m = jnp.max(logits, axis=0, keepdims=True)
    e = jnp.exp(logits - m)
    return jnp.sum(e[1:layer + 1, :], axis=0, keepdims=True) / jnp.sum(e, axis=0, keepdims=True)


def _hgrn_out(o, g, gnorm):
    return _rms(o, gnorm) * _silu(g)


def _hgrn_prompt_kernel(q_ref, f_ref, v_ref, g_ref, lbl_ref, gn_ref, o_ref, s_ref,
                        st_scr, st0_scr, *, layer):
    rows_blk, width = q_ref.shape
    kd = width // _HG_HEADS
    n_chunks = rows_blk // _HG_CHUNK
    heads = [slice(h * kd, (h + 1) * kd) for h in range(_HG_HEADS)]
    lb = _lower_bound(lbl_ref[...], layer)
    gn = gn_ref[...]

    @pl.when(pl.program_id(2) == 0)
    def _():
        st_scr[...] = jnp.zeros(st_scr.shape, _F32)

    st0_scr[...] = st_scr[...]

    def chunk_rows(c):
        return pl.ds(pl.multiple_of(c * _HG_CHUNK, _HG_CHUNK), _HG_CHUNK)

    def prepare(c):
        rows = chunk_rows(c)
        log_f, k = _hgrn_gates(f_ref[rows, :], lb)
        b = _chunk_cumsum(log_f)
        b_end = b[_HG_CHUNK - 1:_HG_CHUNK, :]
        dec = jnp.exp(b_end)
        e_b = jnp.exp(b)
        k_start = k / e_b
        q, v = q_ref[rows, :], v_ref[rows, :]
        qk = q * k
        own = jnp.concatenate([jnp.broadcast_to(jnp.sum(qk[:, l], axis=-1, keepdims=True), (_HG_CHUNK, kd))
                               for l in heads], axis=1) * v
        ops = ((q * e_b).astype(_BF16), k_start.astype(_BF16), (k_start * dec).astype(_BF16),
               v.astype(_BF16), dec, own)
        return ops, b_end

    row = lax.broadcasted_iota(jnp.int32, (_HG_CHUNK, _HG_CHUNK), 0)
    col = lax.broadcasted_iota(jnp.int32, (_HG_CHUNK, _HG_CHUNK), 1)

    def finish(c, h, o):
        rows = chunk_rows(c)
        o_ref[rows, heads[h]] = _hgrn_out(o, g_ref[rows, heads[h]], gn)

    def contract(c, ops):
        qs, ks, ke, vb, dec, own = ops
        states = [st_scr[h] for h in range(_HG_HEADS)]
        scores = [_dot_nt(qs[:, l], ks[:, l]) for l in heads]
        carried = [_dot_nt(qs[:, l], st) for l, st in zip(heads, states)]
        incs = [_dot_tn(vb[:, l], ke[:, l]) for l in heads]
        outs = [_dot(jnp.where(row > col, s, 0.0), vb[:, l]) for s, l in zip(scores, heads)]
        for h, l in enumerate(heads):
            st_scr[h] = states[h] * dec[:, l] + incs[h]
        for h, l in enumerate(heads):
            finish(c, h, outs[h] + carried[h] + own[:, l])

    def step(c, carry):
        ops, min_b = carry
        contract(c, ops)
        ops, b_end = prepare(c + 1)
        return ops, jnp.minimum(min_b, b_end)

    last_ops, min_b = lax.fori_loop(0, n_chunks - 1, step, prepare(0), unroll=5)
    contract(n_chunks - 1, last_ops)

    @pl.when(jnp.min(min_b) <= -_HG_SAFE_LOG_DECAY)
    def _():
        st_scr[...] = st0_scr[...]

        def chunk(c, carry):
            rows = chunk_rows(c)
            for h, l in enumerate(heads):
                o, st_scr[h] = _hgrn_chunk(q_ref[rows, l], f_ref[rows, l], v_ref[rows, l],
                                           lb[:, l], st_scr[h], _HG_CHUNK, True)
                finish(c, h, o)
            return carry
        lax.fori_loop(0, n_chunks, chunk, 0)

    @pl.when(pl.program_id(2) == pl.num_programs(2) - 1)
    def _():
        for h in range(_HG_HEADS):
            s_ref[0, h] = st_scr[h].T


def _hgrn_prompt_call(q, fz, v, g, lb_logits, gnorm, batch, layer):
    n, width = q.shape
    t_len = n // batch
    kd = width // _C_HEADS
    gw = _HG_HEADS * kd
    rows_blk = min(_HG_ROWS, t_len)
    nt = t_len // rows_blk
    blk = pl.BlockSpec((rows_blk, gw), lambda b, h, t: (b * nt + t, h))
    return pl.pallas_call(
        functools.partial(_hgrn_prompt_kernel, layer=layer),
        grid=(batch, _C_HEADS // _HG_HEADS, nt),
        in_specs=[blk, blk, blk, blk,
                  pl.BlockSpec((lb_logits.shape[0], gw), lambda b, h, t: (0, h)),
                  pl.BlockSpec((1, kd), lambda b, h, t: (0, 0))],
        out_specs=[blk, pl.BlockSpec((1, _HG_HEADS, kd, kd), lambda b, h, t: (b, h, 0, 0))],
        out_shape=[jax.ShapeDtypeStruct((n, width), _F32),
                   jax.ShapeDtypeStruct((batch, _C_HEADS, kd, kd), _F32)],
        scratch_shapes=[pltpu.VMEM((_HG_HEADS, kd, kd), _F32)] * 2,
        compiler_params=_params("parallel", "parallel", "arbitrary"),
        name="hgrn2_prompt",
    )(q, fz, v, g, lb_logits, gnorm)


def _hgrn_sample_kernel(q_ref, f_ref, v_ref, g_ref, lbl_ref, gn_ref, s0_ref, o_ref, s_ref, *, layer):
    t_new, nseq, width = q_ref.shape
    n_heads = s0_ref.shape[1]
    kd = width // n_heads
    lanes = [slice(h * kd, (h + 1) * kd) for h in range(n_heads)]
    lb = _lower_bound(lbl_ref[...], layer)
    gn = gn_ref[...]
    q = [q_ref[t] for t in range(t_new)]
    v = [v_ref[t] for t in range(t_new)]
    keys, b = [], []
    for t in range(t_new):
        log_f, k = _hgrn_gates(f_ref[t], lb)
        keys.append(k)
        b.append(log_f if t == 0 else b[-1] + log_f)

    def per_head_sum(w):
        return jnp.concatenate([jnp.broadcast_to(jnp.sum(w[:, l], axis=-1, keepdims=True), (nseq, kd))
                                for l in lanes], axis=1)

    within = []
    for t in range(t_new):
        acc = per_head_sum(q[t] * keys[t]) * v[t]
        for s in range(t):
            acc = acc + per_head_sum(q[t] * keys[s] * jnp.exp(b[t] - b[s])) * v[s]
        within.append(acc)

    q_in = [q[t] * jnp.exp(b[t]) for t in range(t_new)]
    k_end = [keys[t] * jnp.exp(b[-1] - b[t]) for t in range(t_new)]
    decay = jnp.exp(b[-1])
    pairs = [(s, h) for h in range(n_heads) for s in range(nseq)]
    lhs = [_seq_tile([x[:, lanes[h]] for x in q_in], s) for s, h in pairs]
    k_seq = [_seq_tile([x[:, lanes[h]] for x in k_end], s) for s, h in pairs]
    v_seq = [_seq_tile([x[:, lanes[h]] for x in v], s) for s, h in pairs]
    carried = [_dot(x, s0_ref[s, h]) for x, (s, h) in zip(lhs, pairs)]
    incs = [_dot_tn(ks, vs) for ks, vs in zip(k_seq, v_seq)]
    carried_t = []
    for h in range(n_heads):
        dec_cols = jnp.concatenate([decay[:, lanes[h]], jnp.zeros((kd - nseq, kd), _F32)], axis=0).T
        for s in range(nseq):
            i = h * nseq + s
            s_ref[s, h] = jnp.broadcast_to(dec_cols[:, s:s + 1], (kd, kd)) * s0_ref[s, h] + incs[i]
        carried_t.append(_time_tiles(carried[h * nseq:(h + 1) * nseq], t_new))
    for t in range(t_new):
        o = within[t] + jnp.concatenate([carried_t[h][t] for h in range(n_heads)], axis=1)
        g = g_ref[t]
        o_ref[t] = jnp.concatenate([_hgrn_out(o[:, l], g[:, l], gn) for l in lanes], axis=1)


def _hgrn_sample_call(q, fz, v, g, lb_logits, gnorm, s0, layer):
    t_new, nseq, width = q.shape
    kd = width // _C_HEADS
    gw = _HG_HEADS * kd
    bs = _SAMPLE_SEQS
    blk = pl.BlockSpec((t_new, bs, gw), lambda i, h: (0, i, h))
    st = pl.BlockSpec((bs, _HG_HEADS, kd, kd), lambda i, h: (i, h, 0, 0))
    return pl.pallas_call(
        functools.partial(_hgrn_sample_kernel, layer=layer),
        grid=(nseq // bs, _C_HEADS // _HG_HEADS),
        in_specs=[blk, blk, blk, blk,
                  pl.BlockSpec((lb_logits.shape[0], gw), lambda i, h: (0, h)),
                  pl.BlockSpec((1, kd), lambda i, h: (0, 0)), st],
        out_specs=[blk, st],
        out_shape=[jax.ShapeDtypeStruct((t_new, nseq, width), _F32), jax.ShapeDtypeStruct(s0.shape, _F32)],
        compiler_params=_params("parallel", "parallel"),
        name="hgrn2_sample",
    )(q, fz, v, g, lb_logits, gnorm, s0)


def _rope_tables(pos):
    half = _ROT_DIM // 2
    inv_freq = _ROPE_THETA ** (-jnp.arange(0, _ROT_DIM, 2, dtype=_F32) / _ROT_DIM)
    ang = pos.astype(_F32)[:, None] * inv_freq[None, :]
    cos, sin = jnp.cos(ang), jnp.sin(ang)
    ones = jnp.ones((pos.shape[0], _A_HEAD_DIM - _ROT_DIM), _F32)
    zeros = jnp.zeros((pos.shape[0], _A_HEAD_DIM - half), _F32)
    zeros_h = jnp.zeros((pos.shape[0], half), _F32)
    reps = _LANES // _A_HEAD_DIM
    cos_t = jnp.tile(jnp.concatenate([cos, cos, ones], axis=1), (1, reps))
    sin_lo = jnp.tile(jnp.concatenate([-sin, zeros], axis=1), (1, reps))
    sin_hi = jnp.tile(jnp.concatenate([zeros_h, sin, ones * 0.0], axis=1), (1, reps))
    return cos_t, sin_lo, sin_hi


def _block_diag(w):
    nb, bd, _ = w.shape
    eye = jnp.eye(nb, dtype=w.dtype)
    return (w[:, :, None, :] * eye[:, None, :, None]).reshape(nb * bd, nb * bd)


def kernel(x_prompt, x_sample, c_prompt, c_sample, cache_k_win, cache_v_win, state_conv_rglru,
           state_h_rglru, state_s_hgrn, norm_pre, norm_post, ada_w, ada_b, ffn1_w_in, ffn1_w_out,
           ffn2_w_in, ffn2_w_out, even_w_in, even_w_out, attn_sinks, rg_conv_w, rg_conv_b, rg_wa,
           rg_ba, rg_wx, rg_bx, rg_lambda, odd_w_in, odd_w_out, hgrn_lb_logits, hgrn_gnorm):
    bp, tp, d = x_prompt.shape
    bs, ts, _ = x_sample.shape
    depth = norm_pre.shape[0]
    n_sub = depth * _N_SUB
    nk = _A_KV_HEADS * _A_HEAD_DIM
    win = cache_k_win.shape[2]
    cw = rg_conv_w.shape[1]
    bw = rg_conv_w.shape[2]
    kd = state_s_hgrn.shape[3]

    cast = lambda w: w.astype(_BF16)
    ffn1_in, ffn1_out, ffn2_in, ffn2_out = ffn1_w_in, ffn1_w_out, ffn2_w_in, ffn2_w_out
    ev_in, ev_out, od_in, od_out = even_w_in, even_w_out, odd_w_in, odd_w_out
    gpre = norm_pre.reshape(n_sub, 1, d)
    gpost = norm_post.reshape(n_sub, 1, d)

    mod = _ada_call(jnp.concatenate([c_prompt, c_sample], axis=0),
                    ada_w.reshape(n_sub, d, 3 * d), ada_b.reshape(n_sub, 1, 3 * d))
    mod_p = mod[:, :bp].reshape(n_sub, bp, 1, 3 * d)
    mod_s = mod[:, bp:].reshape(n_sub, 1, bs, 3 * d)

    tabs_p = _rope_tables(jnp.arange(tp, dtype=jnp.int32))
    tabs_s = tuple(jnp.repeat(t, bs, axis=0) for t in _rope_tables(_PAST_LEN + jnp.arange(ts, dtype=jnp.int32)))

    time_major = lambda a: a.reshape(ts, bs, a.shape[-1])
    xp = x_prompt.reshape(bp * tp, d)
    xs = x_sample.transpose(1, 0, 2).reshape(ts * bs, d)
    groups = {"p": dict(mod=mod_p, seq_rows=tp, tabs=tabs_p), "s": dict(mod=mod_s, seq_rows=ts * bs, tabs=tabs_s)}
    outs = {g: dict(k=[], v=[], conv=[], h=[], s=[]) for g in groups}

    for l in range(depth):
        s0, s1, s2 = l * _N_SUB, l * _N_SUB + 1, l * _N_SUB + 2
        xp, xs = _ffn_call(xp, xs, mod_p, mod_s, s0, l, gpre, ffn1_in, ffn1_out, gpost, 0.5, tp)
        acts = {}
        for name, x in (("p", xp), ("s", xs)):
            grp = groups[name]
            mod4, tps = grp["mod"], grp["seq_rows"]
            if l % 2 == 0:
                e = l // 2
                wa_bd, wx_bd = cast(_block_diag(rg_wa[e])), cast(_block_diag(rg_wx[e]))
                vecs = [a[e].reshape(1, bw) for a in (rg_conv_b, rg_ba, rg_bx, rg_lambda)]
                if name == "p":
                    gates = (rg_conv_w[e], vecs[0], wa_bd, vecs[1], wx_bd, vecs[2], vecs[3])
                    q, k, v, xg, a, b, tail = _even_in_call(x, mod4, s1, e, gpre, ev_in, grp["tabs"], tps, gates)
                    o_a = _swa_prompt_call(q, k, v, attn_sinks[e], bp)
                    o_b, h_last = _rg_prompt_call(a, b, xg, bp)
                    conv = tail[:, _SUBLANES - (cw - 1):]
                    outs[name]["k"].append(k.reshape(bp, tp, _A_KV_HEADS, _A_HEAD_DIM)[:, tp - win:])
                    outs[name]["v"].append(v.reshape(bp, tp, _A_KV_HEADS, _A_HEAD_DIM)[:, tp - win:])
                    outs[name]["h"].append(h_last.reshape(bp, bw))
                else:
                    q, k, v, xg, xr = _even_in_call(x, mod4, s1, e, gpre, ev_in, grp["tabs"], tps)
                    o_a, kw, vw = _swa_sample_call(time_major(q), time_major(k), time_major(v),
                                                   cache_k_win[e].reshape(bs, win, nk),
                                                   cache_v_win[e].reshape(bs, win, nk), attn_sinks[e])
                    o_b, conv_t, h_last = _rg_sample_call(time_major(xr), time_major(xg),
                                                          state_conv_rglru[e].transpose(1, 0, 2), state_h_rglru[e],
                                                          rg_conv_w[e], vecs[0], wa_bd, vecs[1], wx_bd, vecs[2], vecs[3])
                    o_a, o_b = o_a.reshape(ts * bs, -1), o_b.reshape(ts * bs, bw)
                    conv = conv_t.transpose(1, 0, 2)
                    outs[name]["k"].append(kw.reshape(bs, win, _A_KV_HEADS, _A_HEAD_DIM))
                    outs[name]["v"].append(vw.reshape(bs, win, _A_KV_HEADS, _A_HEAD_DIM))
                    outs[name]["h"].append(h_last)
                outs[name]["conv"].append(conv)
                acts[name] = [o_a, o_b]
                widx, w_mix = e, ev_out
            else:
                o = l // 2
                q, fz, v, g = _odd_in_call(x, mod4, s1, o, gpre, od_in, tps)
                gn = hgrn_gnorm[o].reshape(1, kd)
                if name == "p":
                    y, s_last = _hgrn_prompt_call(q, fz, v, g, hgrn_lb_logits, gn, bp, l)
                else:
                    y, s_last = _hgrn_sample_call(time_major(q), time_major(fz), time_major(v), time_major(g),
                                                  hgrn_lb_logits, gn, state_s_hgrn[o], l)
                    y = y.reshape(ts * bs, -1)
                outs[name]["s"].append(s_last)
                acts[name] = [y]
                widx, w_mix = o, od_out
        xp, xs = _ffn_call(xp, xs, mod_p, mod_s, s2, l, gpre, ffn2_in, ffn2_out, gpost, 0.5, tp,
                           (s1, widx, w_mix, acts["p"], acts["s"]))

    ys = {"p": xp.reshape(bp, tp, d), "s": xs.reshape(ts, bs, d).transpose(1, 0, 2)}
    res = []
    for name in ("p", "s"):
        o = outs[name]
        res.append((jnp.stack(o["k"]), jnp.stack(o["v"]), jnp.stack(o["conv"]), jnp.stack(o["h"]), jnp.stack(o["s"])))
    return (ys["p"], ys["s"]) + res[0] + res[1]
```

```python
import functools

import jax
import jax.numpy as jnp
import numpy as np
from jax import lax
from jax.experimental import pallas as pl
from jax.experimental.pallas import tpu as pltpu

_F32 = jnp.float32
_BF16 = jnp.bfloat16

_EPS = 1e-6
_A_HEADS = 8
_A_KV_HEADS = 2
_A_HEAD_DIM = 64
_A_GROUP = _A_HEADS // _A_KV_HEADS
_WINDOW = 128
_ROPE_THETA = 500000.0
_ROT_DIM = _A_HEAD_DIM // 4
_RG_C = 8.0
_C_HEADS = 8
_PAST_LEN = 16384
_N_SUB = 3

_LANES = 128
_SUBLANES = 8
_VMEM_BYTES = 64 * 1024 * 1024
_VMEM_LIMIT = _VMEM_BYTES * 3 // 4
_VMEM_LIMIT_FFN = _VMEM_BYTES * 7 // 8

_ROW_TILE = 512
_FF_CHUNK = 256
_RG_CHUNK = 256
_HG_CHUNK = 64
_HG_HEADS = 4
_HG_ROWS = 1024
_HG_SAFE_LOG_DECAY = 80.0
_SAMPLE_SEQS = 8
_SWA_QBLOCKS = 4


def _dot(a, b):
    return jnp.dot(a.astype(_BF16), b.astype(_BF16), preferred_element_type=_F32)


def _dot_nt(a, b):
    return lax.dot_general(a.astype(_BF16), b.astype(_BF16), (((1,), (1,)), ((), ())),
                           preferred_element_type=_F32)


def _dot_tn(a, b):
    return lax.dot_general(a.astype(_BF16), b.astype(_BF16), (((0,), (0,)), ((), ())),
                           preferred_element_type=_F32)


def _sigmoid(x):
    return 1.0 / (1.0 + jnp.exp(-x))


def _silu(x):
    return x * _sigmoid(x)


def _rms(x, gain):
    inv = lax.rsqrt(jnp.mean(x * x, axis=-1, keepdims=True) + _EPS)
    return x * inv * gain


def _per_seq(a, r):
    n = a.shape[0]
    return a if r in (1, n) else a.reshape(n // r, r, a.shape[1])


def _pre(x, mod, gain):
    n, d = x.shape
    h = _rms(_per_seq(x, mod.shape[0]), gain) * (1.0 + mod[:, d:2 * d]) + mod[:, :d]
    return h.reshape(n, d)


def _post(x, y, mod, gain, res_w):
    n, d = x.shape
    r = mod.shape[0]
    out = _per_seq(x, r) + (res_w * (1.0 + mod[:, 2 * d:])) * _rms(_per_seq(y, r), gain)
    return out.reshape(n, d)


def _params(*sem):
    return pltpu.CompilerParams(dimension_semantics=sem, vmem_limit_bytes=_VMEM_LIMIT)


def _ada_kernel(c_ref, w_ref, b_ref, o_ref):
    o_ref[...] = _dot(_silu(c_ref[...]), w_ref[...]) + b_ref[...]


def _ada_call(c_all, ada_w, ada_b):
    m, d = c_all.shape
    n_sub = ada_w.shape[0]
    n = ada_w.shape[-1]
    tn = n // 2
    return pl.pallas_call(
        _ada_kernel,
        grid=(n_sub, n // tn),
        in_specs=[
            pl.BlockSpec((m, d), lambda s, j: (0, 0)),
            pl.BlockSpec((None, d, tn), lambda s, j: (s, 0, j)),
            pl.BlockSpec((None, 1, tn), lambda s, j: (s, 0, j)),
        ],
        out_specs=pl.BlockSpec((None, m, tn), lambda s, j: (s, 0, j)),
        out_shape=jax.ShapeDtypeStruct((n_sub, m, n), _F32),
        compiler_params=_params("parallel", "parallel"),
        name="ada_mod",
    )(c_all, ada_w, ada_b)


def _ffn_kernel(xp_ref, xs_ref, modp_ref, mods_ref, gpre_ref, win_ref, wout_ref, gpost_ref, *refs,
                res_w, n_load, n_prompt, n_acts):
    op_ref, os_ref, win_bf, wout_bf = refs[-4:]
    i = pl.program_id(0)
    ld = win_ref.shape[1]
    dff = wout_bf.shape[0] * wout_bf.shape[1]

    @pl.when(i < n_load)
    def _():
        win_bf[i] = win_ref[...].astype(_BF16)
        wout_bf[i] = wout_ref[...].astype(_BF16)

    def w_in_cols(lo):
        return win_bf[lo // ld, :, lo % ld:lo % ld + _FF_CHUNK]

    def row_tile(x_ref, mod_ref, mmod_ref, act_refs, o_ref):
        x = x_ref[...]
        if n_acts:
            mgpost_ref, mw_ref = refs[2:4]
            y = None
            off = 0
            for a_ref in act_refs:
                k = a_ref.shape[1]
                t = _dot(a_ref[...], mw_ref[off:off + k, :])
                y = t if y is None else y + t
                off += k
            x = _post(x, y, mmod_ref[0], mgpost_ref[...], 1.0)
        mod = mod_ref[0]
        h = _pre(x, mod, gpre_ref[...]).astype(_BF16)
        acc = jnp.zeros(x.shape, _F32)
        for j in range(dff // _FF_CHUNK):
            lo = j * _FF_CHUNK
            g = _dot(h, w_in_cols(lo))
            u = _dot(h, w_in_cols(dff + lo))
            acc = acc + _dot(_silu(g) * u, wout_bf[j])
        o_ref[...] = _post(x, acc, mod, gpost_ref[...], res_w)

    acts = refs[4:4 + 2 * n_acts] if n_acts else ()

    @pl.when(jnp.logical_and(i >= n_load, i < n_load + n_prompt))
    def _():
        row_tile(xp_ref, modp_ref, refs[0] if n_acts else None, acts[:n_acts], op_ref)

    @pl.when(i == n_load + n_prompt)
    def _():
        row_tile(xs_ref, mods_ref, refs[1] if n_acts else None, acts[n_acts:], os_ref)


def _ffn_call(xp, xs, modp, mods, sub, layer, gpre, w_in, w_out, gpost, res_w, seq_rows, mixer=None):
    n, d = xp.shape
    tm = xs.shape[0]
    assert n % tm == 0 and seq_rows % tm == 0 and mods.shape[2] * (tm // mods.shape[2]) == tm
    dff = w_out.shape[1]
    n_load = dff // _FF_CHUNK
    n_prompt = n // tm
    tiles_per_seq = seq_rows // tm
    tile = lambda i: jnp.clip(i - n_load, 0, n_prompt - 1)
    rows_p = pl.BlockSpec((tm, d), lambda i: (tile(i), 0))
    once = dict(pipeline_mode=pl.Buffered(1))
    rows_s = pl.BlockSpec((tm, d), lambda i: (0, 0), **once)
    modp_spec = lambda s: pl.BlockSpec((None, 1, 1, 3 * d), lambda i: (s, tile(i) // tiles_per_seq, 0, 0))
    mods_spec = lambda s: pl.BlockSpec((None, 1, mods.shape[2], 3 * d), lambda i: (s, 0, 0, 0), **once)
    gain_spec = lambda s: pl.BlockSpec((None, 1, d), lambda i: (s, 0, 0))
    chunk = lambda i: jnp.minimum(i, n_load - 1)
    in_specs = [
        rows_p, rows_s, modp_spec(sub), mods_spec(sub), gain_spec(sub),
        pl.BlockSpec((None, d, 2 * dff // n_load), lambda i: (layer, 0, chunk(i))),
        pl.BlockSpec((None, _FF_CHUNK, d), lambda i: (layer, chunk(i), 0)),
        gain_spec(sub),
    ]
    args = [xp, xs, modp, mods, gpre, w_in, w_out, gpost]
    n_acts = 0
    if mixer is not None:
        msub, widx, mw, acts_p, acts_s = mixer
        n_acts = len(acts_p)
        in_specs += [modp_spec(msub), mods_spec(msub), gain_spec(msub),
                     pl.BlockSpec((None, mw.shape[1], d), lambda i: (widx, 0, 0), pipeline_mode=pl.Buffered(1))]
        in_specs += [pl.BlockSpec((tm, a.shape[1]), lambda i: (tile(i), 0)) for a in acts_p]
        in_specs += [pl.BlockSpec((tm, a.shape[1]), lambda i: (0, 0), **once) for a in acts_s]
        args += [modp, mods, gpost, mw] + list(acts_p) + list(acts_s)
    return pl.pallas_call(
        functools.partial(_ffn_kernel, res_w=res_w, n_load=n_load, n_prompt=n_prompt, n_acts=n_acts),
        grid=(n_load + n_prompt + 1,),
        in_specs=in_specs,
        out_specs=[rows_p, pl.BlockSpec((tm, d), lambda i: (0, 0))],
        out_shape=[jax.ShapeDtypeStruct((n, d), _F32), jax.ShapeDtypeStruct((tm, d), _F32)],
        scratch_shapes=[pltpu.VMEM((n_load, d, 2 * dff // n_load), _BF16),
                        pltpu.VMEM((n_load, _FF_CHUNK, d), _BF16)],
        compiler_params=pltpu.CompilerParams(dimension_semantics=("arbitrary",), vmem_limit_bytes=_VMEM_LIMIT_FFN),
        name="ffn_sublayer" if mixer is None else "mixer_out_ffn",
    )(*args)


def _rope(x, cos, sin_lo, sin_hi):
    outs = []
    for j in range(x.shape[1] // _LANES):
        xc = x[:, j * _LANES:(j + 1) * _LANES]
        nxt = pltpu.roll(xc, _LANES - _ROT_DIM // 2, axis=1)
        prv = pltpu.roll(xc, _ROT_DIM // 2, axis=1)
        outs.append(xc * cos + nxt * sin_lo + prv * sin_hi)
    return outs[0] if len(outs) == 1 else jnp.concatenate(outs, axis=1)


def _even_in_kernel(x_ref, mod_ref, gpre_ref, w_ref, cos_ref, slo_ref, shi_ref,
                    q_ref, k_ref, v_ref, xg_ref, xr_ref):
    h = _pre(x_ref[...], mod_ref[0], gpre_ref[...])
    y = _dot(h, w_ref[...])
    nq, nk, nw = q_ref.shape[1], k_ref.shape[1], xg_ref.shape[1]
    cos, slo, shi = cos_ref[...], slo_ref[...], shi_ref[...]
    q_ref[...] = _rope(y[:, :nq], cos, slo, shi)
    k_ref[...] = _rope(y[:, nq:nq + nk], cos, slo, shi)
    v_ref[...] = y[:, nq + nk:nq + 2 * nk]
    xg_ref[...] = y[:, nq + 2 * nk:nq + 2 * nk + nw]
    xr_ref[...] = y[:, nq + 2 * nk + nw:]


def _even_in_call(x, mod4, sub, e, gpre, w_in, rope_tabs, seq_rows):
    n, d = x.shape
    r = mod4.shape[2]
    tm = min(_ROW_TILE, n)
    tiles_per_seq = tab_tiles = seq_rows // tm
    nq = _A_HEADS * _A_HEAD_DIM
    nk = _A_KV_HEADS * _A_HEAD_DIM
    nw = (w_in.shape[-1] - nq - 2 * nk) // 2
    row = lambda i: (i, 0)
    tab = pl.BlockSpec((tm, _LANES), lambda i: (i % tab_tiles, 0))
    widths = [nq, nk, nk, nw, nw]
    return pl.pallas_call(
        _even_in_kernel,
        grid=(n // tm,),
        in_specs=[
            pl.BlockSpec((tm, d), row),
            pl.BlockSpec((None, 1, r, 3 * d), lambda i: (sub, i // tiles_per_seq, 0, 0)),
            pl.BlockSpec((None, 1, d), lambda i: (sub, 0, 0)),
            pl.BlockSpec((None, d, w_in.shape[-1]), lambda i: (e, 0, 0)),
            tab, tab, tab,
        ],
        out_specs=[pl.BlockSpec((tm, c), row) for c in widths],
        out_shape=[jax.ShapeDtypeStruct((n, c), _F32) for c in widths],
        compiler_params=_params("parallel"),
        name="even_in_proj",
    )(x, mod4, gpre, w_in, *rope_tabs)


def _odd_in_kernel(x_ref, mod_ref, gpre_ref, w_ref, q_ref, f_ref, v_ref, g_ref):
    h = _pre(x_ref[...], mod_ref[0], gpre_ref[...])
    y = _dot(h, w_ref[...])
    n = q_ref.shape[1]
    q_ref[...] = y[:, :n]
    f_ref[...] = y[:, n:2 * n]
    v_ref[...] = y[:, 2 * n:3 * n]
    g_ref[...] = y[:, 3 * n:]


def _odd_in_call(x, mod4, sub, o, gpre, w_in, seq_rows):
    n, d = x.shape
    r = mod4.shape[2]
    tm = min(_ROW_TILE, n)
    tiles_per_seq = seq_rows // tm
    nw = w_in.shape[-1] // 4
    row = lambda i: (i, 0)
    return pl.pallas_call(
        _odd_in_kernel,
        grid=(n // tm,),
        in_specs=[
            pl.BlockSpec((tm, d), row),
            pl.BlockSpec((None, 1, r, 3 * d), lambda i: (sub, i // tiles_per_seq, 0, 0)),
            pl.BlockSpec((None, 1, d), lambda i: (sub, 0, 0)),
            pl.BlockSpec((None, d, 4 * nw), lambda i: (o, 0, 0)),
        ],
        out_specs=[pl.BlockSpec((tm, nw), row)] * 4,
        out_shape=[jax.ShapeDtypeStruct((n, nw), _F32)] * 4,
        compiler_params=_params("parallel"),
        name="odd_in_proj",
    )(x, mod4, gpre, w_in)


def _seq_tile(time_tiles, s):
    row = lax.broadcasted_iota(jnp.int32, time_tiles[0].shape, 0)
    out = jnp.zeros(time_tiles[0].shape, time_tiles[0].dtype)
    for t, x in enumerate(time_tiles):
        shift = (t - s) % _SUBLANES
        out = jnp.where(row == t, pltpu.roll(x, shift, axis=0) if shift else x, out)
    return out


def _time_tiles(seq_tiles, n_t):
    row = lax.broadcasted_iota(jnp.int32, seq_tiles[0].shape, 0)
    outs = []
    for t in range(n_t):
        acc = jnp.zeros(seq_tiles[0].shape, seq_tiles[0].dtype)
        for s, x in enumerate(seq_tiles):
            shift = (s - t) % _SUBLANES
            acc = jnp.where(row == s, pltpu.roll(x, shift, axis=0) if shift else x, acc)
        outs.append(acc)
    return outs


def _sink_softmax_pv(s, mask, sink, v):
    s = jnp.where(mask, s, -jnp.inf)
    m = jnp.maximum(jnp.max(s, axis=-1, keepdims=True), sink)
    p = jnp.exp(s - m)
    denom = jnp.sum(p, axis=-1, keepdims=True) + jnp.exp(sink - m)
    return _dot(p, v) / denom


def _swa_prompt_kernel(sink_ref, q_ref, kp_ref, kc_ref, vp_ref, vc_ref, o_ref):
    w = _WINDOW
    hd = _A_HEAD_DIM
    assert _LANES == 2 * hd and _A_GROUP % 2 == 0
    log2e = np.float32(np.log2(np.e))
    scale = np.float32(1.0 / np.sqrt(hd)) * log2e
    low = lax.broadcasted_iota(jnp.int32, (2 * w, _LANES), 1) < hd
    ones_lo = jnp.where(low, 1.0, 0.0).astype(_BF16)
    ones_hi = jnp.where(low, 0.0, 1.0).astype(_BF16)
    low_q = lax.broadcasted_iota(jnp.int32, (w, _LANES), 1) < hd
    row = lax.broadcasted_iota(jnp.int32, (w, 4 * w), 0)
    col = lax.broadcasted_iota(jnp.int32, (w, 4 * w), 1) & (2 * w - 1)
    pairs = range(_A_HEADS // 2)
    kv_of = [(2 * p) // _A_GROUP for p in pairs]

    def score_stage(qb):
        rows = slice(qb * w, (qb + 1) * w)
        if qb == 0:
            k2 = jnp.concatenate([kp_ref[...], kc_ref[0:w, :]], axis=0)
            v2 = jnp.concatenate([vp_ref[...], vc_ref[0:w, :]], axis=0)
            first = jnp.where(pl.program_id(1) > 0, 0, w)
        else:
            k2 = kc_ref[(qb - 1) * w:(qb + 1) * w, :]
            v2 = vc_ref[(qb - 1) * w:(qb + 1) * w, :]
            first = 0
        mask = jnp.logical_and(col > jnp.maximum(row, first - 1), col <= row + w)
        keys, vals = [], []
        for j in range(_A_KV_HEADS):
            own_k = jnp.where(low, k2, 0.0) if j == 0 else jnp.where(low, 0.0, k2)
            own_v = jnp.where(low, v2, 0.0) if j == 0 else jnp.where(low, 0.0, v2)
            oth_k = pltpu.roll(own_k, hd, axis=1)
            oth_v = pltpu.roll(own_v, hd, axis=1)
            lo_k, hi_k = (own_k, oth_k) if j == 0 else (oth_k, own_k)
            lo_v, hi_v = (own_v, oth_v) if j == 0 else (oth_v, own_v)
            keys.append(jnp.concatenate([lo_k, hi_k], axis=0).astype(_BF16))
            vals.append(jnp.concatenate([jnp.concatenate([lo_v.astype(_BF16), ones_lo], axis=1),
                                         jnp.concatenate([hi_v.astype(_BF16), ones_hi], axis=1)], axis=0))
        scores = [_dot_nt(q_ref[rows, p * _LANES:(p + 1) * _LANES] * scale, keys[kv_of[p]]) for p in pairs]
        return scores, mask, vals

    def softmax_stage(scores, mask):
        probs, sink_terms = [], []
        for p in pairs:
            s = jnp.where(mask, scores[p], -jnp.inf)
            halves = []
            for i in range(2):
                sh = s[:, i * 2 * w:(i + 1) * 2 * w]
                sink = sink_ref[2 * p + i] * log2e
                m = jnp.maximum(jnp.max(sh, axis=-1, keepdims=True), sink)
                halves.append((jnp.exp2(sh - m), jnp.exp2(sink - m)))
            probs.append(jnp.concatenate([halves[0][0], halves[1][0]], axis=1).astype(_BF16))
            sink_terms.append(jnp.where(low_q, halves[0][1], halves[1][1]))
        return probs, sink_terms

    def value_stage(qb, probs, sink_terms, vals):
        rows = slice(qb * w, (qb + 1) * w)
        for p in pairs:
            r = _dot(probs[p], vals[kv_of[p]])
            o_ref[rows, p * _LANES:(p + 1) * _LANES] = r[:, :_LANES] / (r[:, _LANES:] + sink_terms[p])

    n_qb = q_ref.shape[0] // w
    scored, soft = {}, {}
    for step in range(n_qb + 2):
        if step < n_qb:
            scored[step] = score_stage(step)
        if 0 <= step - 2 < n_qb:
            qb = step - 2
            value_stage(qb, *soft.pop(qb), scored.pop(qb)[2])
        if 0 <= step - 1 < n_qb:
            qb = step - 1
            soft[qb] = softmax_stage(*scored[qb][:2])


def _swa_prompt_call(q, k, v, sinks, batch):
    n, nq = q.shape
    nk = k.shape[1]
    w = _WINDOW
    qb = _SWA_QBLOCKS
    nb = n // batch // (w * qb)
    cur = lambda b, i: (b * nb + i, 0)
    prev = lambda b, i: ((b * nb + i) * qb - jnp.minimum(i, 1), 0)
    return pl.pallas_call(
        _swa_prompt_kernel,
        grid=(batch, nb),
        in_specs=[
            pl.BlockSpec(memory_space=pltpu.SMEM),
            pl.BlockSpec((qb * w, nq), cur),
            pl.BlockSpec((w, nk), prev), pl.BlockSpec((qb * w, nk), cur),
            pl.BlockSpec((w, nk), prev), pl.BlockSpec((qb * w, nk), cur),
        ],
        out_specs=pl.BlockSpec((qb * w, nq), cur),
        out_shape=jax.ShapeDtypeStruct((n, nq), _F32),
        compiler_params=_params("parallel", "parallel"),
        name="swa_prompt",
    )(sinks, q, k, k, v, v)


def _swa_sample_kernel(sink_ref, q_ref, kn_ref, vn_ref, ck_ref, cv_ref, o_ref, kw_ref, vw_ref, *, t_new):
    p = ck_ref.shape[1]
    scale = np.float32(1.0 / np.sqrt(_A_HEAD_DIM))
    rows = _A_GROUP * _SUBLANES
    t = lax.broadcasted_iota(jnp.int32, (rows, p + _SUBLANES), 0) & (_SUBLANES - 1)
    c = lax.broadcasted_iota(jnp.int32, (rows, p + _SUBLANES), 1)
    mask = jnp.logical_and(c <= t + p, c > t + p - _WINDOW)
    g_of_row = lax.broadcasted_iota(jnp.int32, (rows, 1), 0) >> (_SUBLANES.bit_length() - 1)
    nseq = ck_ref.shape[0]
    q_t = [q_ref[t] for t in range(t_new)]
    kn_t = [kn_ref[t] for t in range(t_new)]
    vn_t = [vn_ref[t] for t in range(t_new)]
    sinks = []
    for j in range(_A_KV_HEADS):
        sink = jnp.zeros((rows, 1), _F32)
        for g in range(_A_GROUP):
            sink = jnp.where(g_of_row == g, sink_ref[j * _A_GROUP + g], sink)
        sinks.append(sink)
    new_k, new_v, qs, keys, vals = [], [], [], [], []
    for s in range(nseq):
        q8, kn8, vn8 = _seq_tile(q_t, s), _seq_tile(kn_t, s), _seq_tile(vn_t, s)
        ck, cv = ck_ref[s], cv_ref[s]
        kw_ref[s, 0:p - t_new, :] = ck[t_new:, :]
        kw_ref[s, p - t_new:p, :] = kn8[0:t_new, :]
        vw_ref[s, 0:p - t_new, :] = cv[t_new:, :]
        vw_ref[s, p - t_new:p, :] = vn8[0:t_new, :]
        for j in range(_A_KV_HEADS):
            ks = slice(j * _A_HEAD_DIM, (j + 1) * _A_HEAD_DIM)
            keys.append(jnp.concatenate([ck[:, ks], kn8[:, ks]], axis=0))
            vals.append(jnp.concatenate([cv[:, ks], vn8[:, ks]], axis=0))
            qs.append(jnp.concatenate(
                [q8[:, (j * _A_GROUP + g) * _A_HEAD_DIM:(j * _A_GROUP + g + 1) * _A_HEAD_DIM]
                 for g in range(_A_GROUP)], axis=0))
    scores = [_dot_nt(qj, k) * scale for qj, k in zip(qs, keys)]
    outs = [_sink_softmax_pv(sc, mask, sinks[i % _A_KV_HEADS], v) for i, (sc, v) in enumerate(zip(scores, vals))]
    per_seq = []
    for s in range(nseq):
        heads = []
        for j in range(_A_KV_HEADS):
            o = outs[s * _A_KV_HEADS + j]
            heads.extend(o[g * _SUBLANES:(g + 1) * _SUBLANES, :] for g in range(_A_GROUP))
        per_seq.append(jnp.concatenate(heads, axis=1))
    for t, tile in enumerate(_time_tiles(per_seq, t_new)):
        o_ref[t] = tile


def _swa_sample_call(q, k_new, v_new, cache_k, cache_v, sinks):
    t_new, nseq, nq = q.shape
    nk = k_new.shape[2]
    p = cache_k.shape[1]
    bs = _SAMPLE_SEQS
    toks = lambda i: (0, i, 0)
    seqs = lambda i: (i, 0, 0)
    return pl.pallas_call(
        functools.partial(_swa_sample_kernel, t_new=t_new),
        grid=(nseq // bs,),
        in_specs=[
            pl.BlockSpec(memory_space=pltpu.SMEM),
            pl.BlockSpec((t_new, bs, nq), toks),
            pl.BlockSpec((t_new, bs, nk), toks), pl.BlockSpec((t_new, bs, nk), toks),
            pl.BlockSpec((bs, p, nk), seqs), pl.BlockSpec((bs, p, nk), seqs),
        ],
        out_specs=[pl.BlockSpec((t_new, bs, nq), toks),
                   pl.BlockSpec((bs, p, nk), seqs), pl.BlockSpec((bs, p, nk), seqs)],
        out_shape=[jax.ShapeDtypeStruct((t_new, nseq, nq), _F32),
                   jax.ShapeDtypeStruct(cache_k.shape, _F32), jax.ShapeDtypeStruct(cache_v.shape, _F32)],
        compiler_params=_params("parallel"),
        name="swa_sample",
    )(sinks, q, k_new, v_new, cache_k, cache_v)


def _softplus(z):
    return jnp.maximum(z, 0.0) + jnp.log1p(jnp.exp(-jnp.abs(z)))


def _gelu_tanh(x):
    return 0.5 * x * (1.0 + jnp.tanh(np.float32(np.sqrt(2.0 / np.pi)) * (x + 0.044715 * (x * x * x))))


def _rg_gates(xc, wa_ref, ba_ref, wx_ref, bx_ref, sp_neg_lam):
    r = _sigmoid(_dot(xc, wa_ref[...]) + ba_ref[...])
    i = _sigmoid(_dot(xc, wx_ref[...]) + bx_ref[...])
    a = jnp.exp((-_RG_C) * r * sp_neg_lam)
    gap = jnp.maximum(1.0 - a * a, 0.0)
    mult = jnp.where(gap > 0.0, gap * lax.rsqrt(gap), 0.0)
    return a, mult * (i * xc)


def _rg_prompt_kernel(xr_ref, xg_ref, cw_ref, cb_ref, wa_ref, ba_ref, wx_ref, bx_ref, lam_ref,
                      o_ref, conv_ref, h_ref, xpad, a_s, b_s):
    t_len, w = xr_ref.shape
    cw = cw_ref.shape[0]
    xpad[0:_SUBLANES, :] = jnp.zeros((_SUBLANES, w), _F32)
    xpad[_SUBLANES:, :] = xr_ref[...]
    sp = _softplus(-lam_ref[...])
    for c in range(t_len // _RG_CHUNK):
        r0 = c * _RG_CHUNK
        xc = cb_ref[...]
        for j in range(cw):
            lo = _SUBLANES + r0 - (cw - 1) + j
            xc = xc + xpad[lo:lo + _RG_CHUNK, :] * cw_ref[j:j + 1, :]
        a, b = _rg_gates(xc, wa_ref, ba_ref, wx_ref, bx_ref, sp)
        a_s[r0:r0 + _RG_CHUNK, :] = a
        b_s[r0:r0 + _RG_CHUNK, :] = b

    row = lax.broadcasted_iota(jnp.int32, (_SUBLANES, w), 0)

    def group(g, h):
        r0 = pl.multiple_of(g * _SUBLANES, _SUBLANES)
        a = a_s[pl.ds(r0, _SUBLANES), :]
        b = b_s[pl.ds(r0, _SUBLANES), :]
        sh = 1
        while sh < _SUBLANES:
            a_prev = jnp.where(row >= sh, pltpu.roll(a, sh, axis=0), 1.0)
            b_prev = jnp.where(row >= sh, pltpu.roll(b, sh, axis=0), 0.0)
            b = a * b_prev + b
            a = a * a_prev
            sh *= 2
        hs = a * h + b
        o_ref[pl.ds(r0, _SUBLANES), :] = _gelu_tanh(xg_ref[pl.ds(r0, _SUBLANES), :]) * hs
        return hs[_SUBLANES - 1:_SUBLANES, :]

    h_last = lax.fori_loop(0, t_len // _SUBLANES, group, jnp.zeros((1, w), _F32), unroll=2)
    h_ref[0] = h_last
    conv_ref[0] = xr_ref[t_len - (cw - 1):t_len, :]


def _rg_prompt_call(xr, xg, conv_w, conv_b, wa_bd, ba, wx_bd, bx, lam, batch):
    n, w = xr.shape
    t_len = n // batch
    cw = conv_w.shape[0]
    seq = lambda b: (b, 0)
    const = lambda b: (0, 0)
    vec = pl.BlockSpec((1, w), const)
    mat = pl.BlockSpec((w, w), const)
    return pl.pallas_call(
        _rg_prompt_kernel,
        grid=(batch,),
        in_specs=[pl.BlockSpec((t_len, w), seq), pl.BlockSpec((t_len, w), seq),
                  pl.BlockSpec((cw, w), const), vec, mat, vec, mat, vec, vec],
        out_specs=[pl.BlockSpec((t_len, w), seq),
                   pl.BlockSpec((1, cw - 1, w), lambda b: (b, 0, 0)),
                   pl.BlockSpec((1, 1, w), lambda b: (b, 0, 0))],
        out_shape=[jax.ShapeDtypeStruct((n, w), _F32),
                   jax.ShapeDtypeStruct((batch, cw - 1, w), _F32),
                   jax.ShapeDtypeStruct((batch, 1, w), _F32)],
        scratch_shapes=[pltpu.VMEM((t_len + _SUBLANES, w), _F32),
                        pltpu.VMEM((t_len, w), _F32), pltpu.VMEM((t_len, w), _F32)],
        compiler_params=_params("parallel"),
        name="rglru_prompt",
    )(xr, xg, conv_w, conv_b, wa_bd, ba, wx_bd, bx, lam)


def _rg_sample_kernel(xr_ref, xg_ref, conv0_ref, h0_ref, cw_ref, cb_ref, wa_ref, ba_ref, wx_ref, bx_ref, lam_ref,
                      o_ref, conv_ref, h_ref):
    cw = cw_ref.shape[0]
    t_new = xr_ref.shape[0]
    sp = _softplus(-lam_ref[...])
    hist = [conv0_ref[j] for j in range(cw - 1)] + [xr_ref[t] for t in range(t_new)]
    h = h0_ref[...]
    for t in range(t_new):
        xc = cb_ref[...]
        for j in range(cw):
            xc = xc + hist[t + j] * cw_ref[j:j + 1, :]
        a, b = _rg_gates(xc, wa_ref, ba_ref, wx_ref, bx_ref, sp)
        h = a * h + b
        o_ref[t] = _gelu_tanh(xg_ref[t]) * h
    h_ref[...] = h
    for j in range(cw - 1):
        conv_ref[j] = hist[t_new + j]


def _rg_sample_call(xr_t, xg_t, conv0_t, h0, conv_w, conv_b, wa_bd, ba, wx_bd, bx, lam):
    t_new, nseq, w = xg_t.shape
    return pl.pallas_call(
        _rg_sample_kernel,
        out_shape=[jax.ShapeDtypeStruct((t_new, nseq, w), _F32), jax.ShapeDtypeStruct(conv0_t.shape, _F32),
                   jax.ShapeDtypeStruct((nseq, w), _F32)],
        compiler_params=pltpu.CompilerParams(vmem_limit_bytes=_VMEM_LIMIT),
        name="rglru_sample",
    )(xr_t, xg_t, conv0_t, h0, conv_w, conv_b, wa_bd, ba, wx_bd, bx, lam)


def _row_bcast(x, r, n):
    return jnp.broadcast_to(x[r:r + 1, :], (n, x.shape[1]))


def _chunk_cumsum(x):
    n_tiles = x.shape[0] // _SUBLANES
    row = lax.broadcasted_iota(jnp.int32, (_SUBLANES, x.shape[1]), 0)
    tiles = []
    carry = None
    for i in range(n_tiles):
        t = x[i * _SUBLANES:(i + 1) * _SUBLANES, :]
        sh = 1
        while sh < _SUBLANES:
            t = t + jnp.where(row >= sh, pltpu.roll(t, sh, axis=0), 0.0)
            sh *= 2
        if carry is not None:
            t = t + carry
        carry = _row_bcast(t, _SUBLANES - 1, _SUBLANES)
        tiles.append(t)
    return tiles[0] if n_tiles == 1 else jnp.concatenate(tiles, axis=0)


def _level_reference(b, m):
    n = b.shape[0]
    if 2 * m >= _SUBLANES:
        pieces = [_row_bcast(b, lo + m - 1, 2 * m) for lo in range(0, n, 2 * m)]
        return pieces[0] if len(pieces) == 1 else jnp.concatenate(pieces, axis=0)
    row = lax.broadcasted_iota(jnp.int32, (_SUBLANES, b.shape[1]), 0)
    tiles = []
    for i in range(n // _SUBLANES):
        t = b[i * _SUBLANES:(i + 1) * _SUBLANES, :]
        ref = None
        for lo in range(0, _SUBLANES, 2 * m):
            piece = _row_bcast(t, lo + m - 1, _SUBLANES)
            ref = piece if ref is None else jnp.where(row >= lo, piece, ref)
        tiles.append(ref)
    return tiles[0] if len(tiles) == 1 else jnp.concatenate(tiles, axis=0)


def _hgrn_gates(fz, lb):
    f = lb + (1.0 - lb) * _sigmoid(fz)
    return jnp.log(f), 1.0 - f


def _hgrn_chunk(q, fz, v, lb, state, n_valid, state_is_vk):
    n, kd = q.shape
    log_f, k = _hgrn_gates(fz, lb)
    if n_valid < n:
        valid = lax.broadcasted_iota(jnp.int32, (n, kd), 0) < n_valid
        log_f = jnp.where(valid, log_f, 0.0)
        k = jnp.where(valid, k, 0.0)
    b = _chunk_cumsum(log_f)
    b_last = _row_bcast(b, n - 1, n)

    q_in = q * jnp.exp(b)
    k_end = k * jnp.exp(b_last - b)
    if state_is_vk:
        o = _dot_nt(q_in, state)
        new_state = jnp.exp(b_last[0:1, :]) * state + _dot_tn(v, k_end)
    else:
        o = _dot(q_in, state)
        decay = jnp.exp(jnp.broadcast_to(b_last[0:1, :], (v.shape[1], kd))).T
        new_state = decay * state + _dot_tn(k_end, v)

    row = lax.broadcasted_iota(jnp.int32, (n, n), 0)
    col = lax.broadcasted_iota(jnp.int32, (n, n), 1)
    upper = lax.broadcasted_iota(jnp.int32, (n, kd), 0)
    scores = jnp.where(row == col, jnp.sum(q * k, axis=-1, keepdims=True), 0.0)
    m = 1
    while m < n_valid:
        e = jnp.exp(-jnp.abs(b - _level_reference(b, m)))
        z = jnp.where((upper & m) != 0, q, k) * e
        pair = jnp.logical_and((row & m) != 0, (row ^ m) >> (m.bit_length() - 1) == col >> (m.bit_length() - 1))
        scores = scores + jnp.where(pair, _dot_nt(z, z), 0.0)
        m *= 2
    return o + _dot(scores, v), new_state


def _lower_bound(logits, layer):
    m = jnp.max(logits, axis=0, keepdims=True)
    e = jnp.exp(logits - m)
    return jnp.sum(e[1:layer + 1, :], axis=0, keepdims=True) / jnp.sum(e, axis=0, keepdims=True)


def _hgrn_out(o, g, gnorm):
    return _rms(o, gnorm) * _silu(g)


def _hgrn_prompt_kernel(q_ref, f_ref, v_ref, g_ref, lbl_ref, gn_ref, o_ref, s_ref,
                        st_scr, st0_scr, *, layer):
    rows_blk, width = q_ref.shape
    kd = width // _HG_HEADS
    n_chunks = rows_blk // _HG_CHUNK
    heads = [slice(h * kd, (h + 1) * kd) for h in range(_HG_HEADS)]
    lb = _lower_bound(lbl_ref[...], layer)
    gn = gn_ref[...]

    @pl.when(pl.program_id(2) == 0)
    def _():
        st_scr[...] = jnp.zeros(st_scr.shape, _F32)

    st0_scr[...] = st_scr[...]

    def chunk_rows(c):
        return pl.ds(pl.multiple_of(c * _HG_CHUNK, _HG_CHUNK), _HG_CHUNK)

    def prepare(c):
        rows = chunk_rows(c)
        log_f, k = _hgrn_gates(f_ref[rows, :], lb)
        b = _chunk_cumsum(log_f)
        b_end = b[_HG_CHUNK - 1:_HG_CHUNK, :]
        dec = jnp.exp(b_end)
        e_b = jnp.exp(b)
        k_start = k / e_b
        q, v = q_ref[rows, :], v_ref[rows, :]
        qk = q * k
        own = jnp.concatenate([jnp.broadcast_to(jnp.sum(qk[:, l], axis=-1, keepdims=True), (_HG_CHUNK, kd))
                               for l in heads], axis=1) * v
        ops = ((q * e_b).astype(_BF16), k_start.astype(_BF16), (k_start * dec).astype(_BF16),
               v.astype(_BF16), dec, own)
        return ops, b_end

    row = lax.broadcasted_iota(jnp.int32, (_HG_CHUNK, _HG_CHUNK), 0)
    col = lax.broadcasted_iota(jnp.int32, (_HG_CHUNK, _HG_CHUNK), 1)

    def finish(c, h, o):
        rows = chunk_rows(c)
        o_ref[rows, heads[h]] = _hgrn_out(o, g_ref[rows, heads[h]], gn)

    def contract(c, ops):
        qs, ks, ke, vb, dec, own = ops
        states = [st_scr[h] for h in range(_HG_HEADS)]
        scores = [_dot_nt(qs[:, l], ks[:, l]) for l in heads]
        carried = [_dot_nt(qs[:, l], st) for l, st in zip(heads, states)]
        incs = [_dot_tn(vb[:, l], ke[:, l]) for l in heads]
        outs = [_dot(jnp.where(row > col, s, 0.0), vb[:, l]) for s, l in zip(scores, heads)]
        for h, l in enumerate(heads):
            st_scr[h] = states[h] * dec[:, l] + incs[h]
        for h, l in enumerate(heads):
            finish(c, h, outs[h] + carried[h] + own[:, l])

    def step(c, carry):
        ops, min_b = carry
        contract(c, ops)
        ops, b_end = prepare(c + 1)
        return ops, jnp.minimum(min_b, b_end)

    last_ops, min_b = lax.fori_loop(0, n_chunks - 1, step, prepare(0), unroll=5)
    contract(n_chunks - 1, last_ops)

    @pl.when(jnp.min(min_b) <= -_HG_SAFE_LOG_DECAY)
    def _():
        st_scr[...] = st0_scr[...]

        def chunk(c, carry):
            rows = chunk_rows(c)
            for h, l in enumerate(heads):
                o, st_scr[h] = _hgrn_chunk(q_ref[rows, l], f_ref[rows, l], v_ref[rows, l],
                                           lb[:, l], st_scr[h], _HG_CHUNK, True)
                finish(c, h, o)
            return carry
        lax.fori_loop(0, n_chunks, chunk, 0)

    @pl.when(pl.program_id(2) == pl.num_programs(2) - 1)
    def _():
        for h in range(_HG_HEADS):
            s_ref[0, h] = st_scr[h].T


def _hgrn_prompt_call(q, fz, v, g, lb_logits, gnorm, batch, layer):
    n, width = q.shape
    t_len = n // batch
    kd = width // _C_HEADS
    gw = _HG_HEADS * kd
    rows_blk = min(_HG_ROWS, t_len)
    nt = t_len // rows_blk
    blk = pl.BlockSpec((rows_blk, gw), lambda b, h, t: (b * nt + t, h))
    return pl.pallas_call(
        functools.partial(_hgrn_prompt_kernel, layer=layer),
        grid=(batch, _C_HEADS // _HG_HEADS, nt),
        in_specs=[blk, blk, blk, blk,
                  pl.BlockSpec((lb_logits.shape[0], gw), lambda b, h, t: (0, h)),
                  pl.BlockSpec((1, kd), lambda b, h, t: (0, 0))],
        out_specs=[blk, pl.BlockSpec((1, _HG_HEADS, kd, kd), lambda b, h, t: (b, h, 0, 0))],
        out_shape=[jax.ShapeDtypeStruct((n, width), _F32),
                   jax.ShapeDtypeStruct((batch, _C_HEADS, kd, kd), _F32)],
        scratch_shapes=[pltpu.VMEM((_HG_HEADS, kd, kd), _F32)] * 2,
        compiler_params=_params("parallel", "parallel", "arbitrary"),
        name="hgrn2_prompt",
    )(q, fz, v, g, lb_logits, gnorm)


def _hgrn_sample_kernel(q_ref, f_ref, v_ref, g_ref, lbl_ref, gn_ref, s0_ref, o_ref, s_ref, *, layer):
    t_new, nseq, width = q_ref.shape
    n_heads = s0_ref.shape[1]
    kd = width // n_heads
    lanes = [slice(h * kd, (h + 1) * kd) for h in range(n_heads)]
    lb = _lower_bound(lbl_ref[...], layer)
    gn = gn_ref[...]
    q = [q_ref[t] for t in range(t_new)]
    v = [v_ref[t] for t in range(t_new)]
    keys, b = [], []
    for t in range(t_new):
        log_f, k = _hgrn_gates(f_ref[t], lb)
        keys.append(k)
        b.append(log_f if t == 0 else b[-1] + log_f)

    def per_head_sum(w):
        return jnp.concatenate([jnp.broadcast_to(jnp.sum(w[:, l], axis=-1, keepdims=True), (nseq, kd))
                                for l in lanes], axis=1)

    within = []
    for t in range(t_new):
        acc = per_head_sum(q[t] * keys[t]) * v[t]
        for s in range(t):
            acc = acc + per_head_sum(q[t] * keys[s] * jnp.exp(b[t] - b[s])) * v[s]
        within.append(acc)

    q_in = [q[t] * jnp.exp(b[t]) for t in range(t_new)]
    k_end = [keys[t] * jnp.exp(b[-1] - b[t]) for t in range(t_new)]
    decay = jnp.exp(b[-1])
    pairs = [(s, h) for h in range(n_heads) for s in range(nseq)]
    lhs = [_seq_tile([x[:, lanes[h]] for x in q_in], s) for s, h in pairs]
    k_seq = [_seq_tile([x[:, lanes[h]] for x in k_end], s) for s, h in pairs]
    v_seq = [_seq_tile([x[:, lanes[h]] for x in v], s) for s, h in pairs]
    carried = [_dot(x, s0_ref[s, h]) for x, (s, h) in zip(lhs, pairs)]
    incs = [_dot_tn(ks, vs) for ks, vs in zip(k_seq, v_seq)]
    carried_t = []
    for h in range(n_heads):
        dec_cols = jnp.concatenate([decay[:, lanes[h]], jnp.zeros((kd - nseq, kd), _F32)], axis=0).T
        for s in range(nseq):
            i = h * nseq + s
            s_ref[s, h] = jnp.broadcast_to(dec_cols[:, s:s + 1], (kd, kd)) * s0_ref[s, h] + incs[i]
        carried_t.append(_time_tiles(carried[h * nseq:(h + 1) * nseq], t_new))
    for t in range(t_new):
        o = within[t] + jnp.concatenate([carried_t[h][t] for h in range(n_heads)], axis=1)
        g = g_ref[t]
        o_ref[t] = jnp.concatenate([_hgrn_out(o[:, l], g[:, l], gn) for l in lanes], axis=1)


def _hgrn_sample_call(q, fz, v, g, lb_logits, gnorm, s0, layer):
    t_new, nseq, width = q.shape
    kd = width // _C_HEADS
    gw = _HG_HEADS * kd
    bs = _SAMPLE_SEQS
    blk = pl.BlockSpec((t_new, bs, gw), lambda i, h: (0, i, h))
    st = pl.BlockSpec((bs, _HG_HEADS, kd, kd), lambda i, h: (i, h, 0, 0))
    return pl.pallas_call(
        functools.partial(_hgrn_sample_kernel, layer=layer),
        grid=(nseq // bs, _C_HEADS // _HG_HEADS),
        in_specs=[blk, blk, blk, blk,
                  pl.BlockSpec((lb_logits.shape[0], gw), lambda i, h: (0, h)),
                  pl.BlockSpec((1, kd), lambda i, h: (0, 0)), st],
        out_specs=[blk, st],
        out_shape=[jax.ShapeDtypeStruct((t_new, nseq, width), _F32), jax.ShapeDtypeStruct(s0.shape, _F32)],
        compiler_params=_params("parallel", "parallel"),
        name="hgrn2_sample",
    )(q, fz, v, g, lb_logits, gnorm, s0)


def _rope_tables(pos):
    half = _ROT_DIM // 2
    inv_freq = _ROPE_THETA ** (-jnp.arange(0, _ROT_DIM, 2, dtype=_F32) / _ROT_DIM)
    ang = pos.astype(_F32)[:, None] * inv_freq[None, :]
    cos, sin = jnp.cos(ang), jnp.sin(ang)
    ones = jnp.ones((pos.shape[0], _A_HEAD_DIM - _ROT_DIM), _F32)
    zeros = jnp.zeros((pos.shape[0], _A_HEAD_DIM - half), _F32)
    zeros_h = jnp.zeros((pos.shape[0], half), _F32)
    reps = _LANES // _A_HEAD_DIM
    cos_t = jnp.tile(jnp.concatenate([cos, cos, ones], axis=1), (1, reps))
    sin_lo = jnp.tile(jnp.concatenate([-sin, zeros], axis=1), (1, reps))
    sin_hi = jnp.tile(jnp.concatenate([zeros_h, sin, ones * 0.0], axis=1), (1, reps))
    return cos_t, sin_lo, sin_hi


def _block_diag(w):
    nb, bd, _ = w.shape
    eye = jnp.eye(nb, dtype=w.dtype)
    return (w[:, :, None, :] * eye[:, None, :, None]).reshape(nb * bd, nb * bd)


def kernel(x_prompt, x_sample, c_prompt, c_sample, cache_k_win, cache_v_win, state_conv_rglru,
           state_h_rglru, state_s_hgrn, norm_pre, norm_post, ada_w, ada_b, ffn1_w_in, ffn1_w_out,
           ffn2_w_in, ffn2_w_out, even_w_in, even_w_out, attn_sinks, rg_conv_w, rg_conv_b, rg_wa,
           rg_ba, rg_wx, rg_bx, rg_lambda, odd_w_in, odd_w_out, hgrn_lb_logits, hgrn_gnorm):
    bp, tp, d = x_prompt.shape
    bs, ts, _ = x_sample.shape
    depth = norm_pre.shape[0]
    n_sub = depth * _N_SUB
    nk = _A_KV_HEADS * _A_HEAD_DIM
    win = cache_k_win.shape[2]
    cw = rg_conv_w.shape[1]
    bw = rg_conv_w.shape[2]
    kd = state_s_hgrn.shape[3]

    cast = lambda w: w.astype(_BF16)
    ffn1_in, ffn1_out, ffn2_in, ffn2_out = ffn1_w_in, ffn1_w_out, ffn2_w_in, ffn2_w_out
    ev_in, ev_out, od_in, od_out = even_w_in, even_w_out, odd_w_in, odd_w_out
    gpre = norm_pre.reshape(n_sub, 1, d)
    gpost = norm_post.reshape(n_sub, 1, d)

    mod = _ada_call(jnp.concatenate([c_prompt, c_sample], axis=0),
                    ada_w.reshape(n_sub, d, 3 * d), ada_b.reshape(n_sub, 1, 3 * d))
    mod_p = mod[:, :bp].reshape(n_sub, bp, 1, 3 * d)
    mod_s = mod[:, bp:].reshape(n_sub, 1, bs, 3 * d)

    tabs_p = _rope_tables(jnp.arange(tp, dtype=jnp.int32))
    tabs_s = tuple(jnp.repeat(t, bs, axis=0) for t in _rope_tables(_PAST_LEN + jnp.arange(ts, dtype=jnp.int32)))

    time_major = lambda a: a.reshape(ts, bs, a.shape[-1])
    xp = x_prompt.reshape(bp * tp, d)
    xs = x_sample.transpose(1, 0, 2).reshape(ts * bs, d)
    groups = {"p": dict(mod=mod_p, seq_rows=tp, tabs=tabs_p), "s": dict(mod=mod_s, seq_rows=ts * bs, tabs=tabs_s)}
    outs = {g: dict(k=[], v=[], conv=[], h=[], s=[]) for g in groups}

    for l in range(depth):
        s0, s1, s2 = l * _N_SUB, l * _N_SUB + 1, l * _N_SUB + 2
        xp, xs = _ffn_call(xp, xs, mod_p, mod_s, s0, l, gpre, ffn1_in, ffn1_out, gpost, 0.5, tp)
        acts = {}
        for name, x in (("p", xp), ("s", xs)):
            grp = groups[name]
            mod4, tps = grp["mod"], grp["seq_rows"]
            if l % 2 == 0:
                e = l // 2
                q, k, v, xg, xr = _even_in_call(x, mod4, s1, e, gpre, ev_in, grp["tabs"], tps)
                wa_bd, wx_bd = cast(_block_diag(rg_wa[e])), cast(_block_diag(rg_wx[e]))
                vecs = [a[e].reshape(1, bw) for a in (rg_conv_b, rg_ba, rg_bx, rg_lambda)]
                if name == "p":
                    o_a = _swa_prompt_call(q, k, v, attn_sinks[e], bp)
                    o_b, conv, h_last = _rg_prompt_call(xr, xg, rg_conv_w[e], vecs[0], wa_bd, vecs[1], wx_bd,
                                                        vecs[2], vecs[3], bp)
                    outs[name]["k"].append(k.reshape(bp, tp, _A_KV_HEADS, _A_HEAD_DIM)[:, tp - win:])
                    outs[name]["v"].append(v.reshape(bp, tp, _A_KV_HEADS, _A_HEAD_DIM)[:, tp - win:])
                    outs[name]["h"].append(h_last.reshape(bp, bw))
                else:
                    o_a, kw, vw = _swa_sample_call(time_major(q), time_major(k), time_major(v),
                                                   cache_k_win[e].reshape(bs, win, nk),
                                                   cache_v_win[e].reshape(bs, win, nk), attn_sinks[e])
                    o_b, conv_t, h_last = _rg_sample_call(time_major(xr), time_major(xg),
                                                          state_conv_rglru[e].transpose(1, 0, 2), state_h_rglru[e],
                                                          rg_conv_w[e], vecs[0], wa_bd, vecs[1], wx_bd, vecs[2], vecs[3])
                    o_a, o_b = o_a.reshape(ts * bs, -1), o_b.reshape(ts * bs, bw)
                    conv = conv_t.transpose(1, 0, 2)
                    outs[name]["k"].append(kw.reshape(bs, win, _A_KV_HEADS, _A_HEAD_DIM))
                    outs[name]["v"].append(vw.reshape(bs, win, _A_KV_HEADS, _A_HEAD_DIM))
                    outs[name]["h"].append(h_last)
                outs[name]["conv"].append(conv)
                acts[name] = [o_a, o_b]
                widx, w_mix = e, ev_out
            else:
                o = l // 2
                q, fz, v, g = _odd_in_call(x, mod4, s1, o, gpre, od_in, tps)
                gn = hgrn_gnorm[o].reshape(1, kd)
                if name == "p":
                    y, s_last = _hgrn_prompt_call(q, fz, v, g, hgrn_lb_logits, gn, bp, l)
                else:
                    y, s_last = _hgrn_sample_call(time_major(q), time_major(fz), time_major(v), time_major(g),
                                                  hgrn_lb_logits, gn, state_s_hgrn[o], l)
                    y = y.reshape(ts * bs, -1)
                outs[name]["s"].append(s_last)
                acts[name] = [y]
                widx, w_mix = o, od_out
        xp, xs = _ffn_call(xp, xs, mod_p, mod_s, s2, l, gpre, ffn2_in, ffn2_out, gpost, 0.5, tp,
                           (s1, widx, w_mix, acts["p"], acts["s"]))

    ys = {"p": xp.reshape(bp, tp, d), "s": xs.reshape(ts, bs, d).transpose(1, 0, 2)}
    res = []
    for name in ("p", "s"):
        o = outs[name]
        res.append((jnp.stack(o["k"]), jnp.stack(o["v"]), jnp.stack(o["conv"]), jnp.stack(o["h"]), jnp.stack(o["s"])))
    return (ys["p"], ys["s"]) + res[0] + res[1]
```

```python
import functools

import jax
import jax.numpy as jnp
import numpy as np
from jax import lax
from jax.experimental import pallas as pl
from jax.experimental.pallas import tpu as pltpu

_F32 = jnp.float32
_BF16 = jnp.bfloat16

_EPS = 1e-6
_A_HEADS = 8
_A_KV_HEADS = 2
_A_HEAD_DIM = 64
_A_GROUP = _A_HEADS // _A_KV_HEADS
_WINDOW = 128
_ROPE_THETA = 500000.0
_ROT_DIM = _A_HEAD_DIM // 4
_RG_C = 8.0
_C_HEADS = 8
_PAST_LEN = 16384
_N_SUB = 3

_LANES = 128
_SUBLANES = 8
_VMEM_BYTES = 64 * 1024 * 1024
_VMEM_LIMIT = _VMEM_BYTES * 3 // 4
_VMEM_LIMIT_FFN = _VMEM_BYTES * 7 // 8

_ROW_TILE = 512
_FF_CHUNK = 256
_RG_CHUNK = 256
_HG_CHUNK = 64
_HG_HEADS = 4
_HG_ROWS = 1024
_HG_SAFE_LOG_DECAY = 80.0
_SAMPLE_SEQS = 8
_SWA_QBLOCKS = 4


def _dot(a, b):
    return jnp.dot(a.astype(_BF16), b.astype(_BF16), preferred_element_type=_F32)


def _dot_nt(a, b):
    return lax.dot_general(a.astype(_BF16), b.astype(_BF16), (((1,), (1,)), ((), ())),
                           preferred_element_type=_F32)


def _dot_tn(a, b):
    return lax.dot_general(a.astype(_BF16), b.astype(_BF16), (((0,), (0,)), ((), ())),
                           preferred_element_type=_F32)


def _sigmoid(x):
    return 1.0 / (1.0 + jnp.exp(-x))


def _silu(x):
    return x * _sigmoid(x)


def _rms(x, gain):
    inv = lax.rsqrt(jnp.mean(x * x, axis=-1, keepdims=True) + _EPS)
    return x * inv * gain


def _per_seq(a, r):
    n = a.shape[0]
    return a if r in (1, n) else a.reshape(n // r, r, a.shape[1])


def _pre(x, mod, gain):
    n, d = x.shape
    h = _rms(_per_seq(x, mod.shape[0]), gain) * (1.0 + mod[:, d:2 * d]) + mod[:, :d]
    return h.reshape(n, d)


def _post(x, y, mod, gain, res_w):
    n, d = x.shape
    r = mod.shape[0]
    out = _per_seq(x, r) + (res_w * (1.0 + mod[:, 2 * d:])) * _rms(_per_seq(y, r), gain)
    return out.reshape(n, d)


def _params(*sem):
    return pltpu.CompilerParams(dimension_semantics=sem, vmem_limit_bytes=_VMEM_LIMIT)


def _ada_kernel(c_ref, w_ref, b_ref, o_ref):
    o_ref[...] = _dot(_silu(c_ref[...]), w_ref[...]) + b_ref[...]


def _ada_call(c_all, ada_w, ada_b):
    m, d = c_all.shape
    n_sub = ada_w.shape[0]
    n = ada_w.shape[-1]
    tn = n // 2
    return pl.pallas_call(
        _ada_kernel,
        grid=(n_sub, n // tn),
        in_specs=[
            pl.BlockSpec((m, d), lambda s, j: (0, 0)),
            pl.BlockSpec((None, d, tn), lambda s, j: (s, 0, j)),
            pl.BlockSpec((None, 1, tn), lambda s, j: (s, 0, j)),
        ],
        out_specs=pl.BlockSpec((None, m, tn), lambda s, j: (s, 0, j)),
        out_shape=jax.ShapeDtypeStruct((n_sub, m, n), _F32),
        compiler_params=_params("parallel", "parallel"),
        name="ada_mod",
    )(c_all, ada_w, ada_b)


def _ffn_kernel(xp_ref, xs_ref, modp_ref, mods_ref, gpre_ref, win_ref, wout_ref, gpost_ref, *refs,
                res_w, n_load, n_prompt, n_acts):
    op_ref, os_ref, win_bf, wout_bf = refs[-4:]
    i = pl.program_id(0)
    ld = win_ref.shape[1]
    dff = wout_bf.shape[0] * wout_bf.shape[1]

    @pl.when(i < n_load)
    def _():
        win_bf[i] = win_ref[...].astype(_BF16)
        wout_bf[i] = wout_ref[...].astype(_BF16)

    def w_in_cols(lo):
        return win_bf[lo // ld, :, lo % ld:lo % ld + _FF_CHUNK]

    def row_tile(x_ref, mod_ref, mmod_ref, act_refs, o_ref):
        x = x_ref[...]
        if n_acts:
            mgpost_ref, mw_ref = refs[2:4]
            y = None
            off = 0
            for a_ref in act_refs:
                k = a_ref.shape[1]
                t = _dot(a_ref[...], mw_ref[off:off + k, :])
                y = t if y is None else y + t
                off += k
            x = _post(x, y, mmod_ref[0], mgpost_ref[...], 1.0)
        mod = mod_ref[0]
        h = _pre(x, mod, gpre_ref[...]).astype(_BF16)
        acc = jnp.zeros(x.shape, _F32)
        for j in range(dff // _FF_CHUNK):
            lo = j * _FF_CHUNK
            g = _dot(h, w_in_cols(lo))
            u = _dot(h, w_in_cols(dff + lo))
            acc = acc + _dot(_silu(g) * u, wout_bf[j])
        o_ref[...] = _post(x, acc, mod, gpost_ref[...], res_w)

    acts = refs[4:4 + 2 * n_acts] if n_acts else ()

    @pl.when(jnp.logical_and(i >= n_load, i < n_load + n_prompt))
    def _():
        row_tile(xp_ref, modp_ref, refs[0] if n_acts else None, acts[:n_acts], op_ref)

    @pl.when(i == n_load + n_prompt)
    def _():
        row_tile(xs_ref, mods_ref, refs[1] if n_acts else None, acts[n_acts:], os_ref)


def _ffn_call(xp, xs, modp, mods, sub, layer, gpre, w_in, w_out, gpost, res_w, seq_rows, mixer=None):
    n, d = xp.shape
    tm = xs.shape[0]
    assert n % tm == 0 and seq_rows % tm == 0 and mods.shape[2] * (tm // mods.shape[2]) == tm
    dff = w_out.shape[1]
    n_load = dff // _FF_CHUNK
    n_prompt = n // tm
    tiles_per_seq = seq_rows // tm
    tile = lambda i: jnp.clip(i - n_load, 0, n_prompt - 1)
    rows_p = pl.BlockSpec((tm, d), lambda i: (tile(i), 0))
    once = dict(pipeline_mode=pl.Buffered(1))
    rows_s = pl.BlockSpec((tm, d), lambda i: (0, 0), **once)
    modp_spec = lambda s: pl.BlockSpec((None, 1, 1, 3 * d), lambda i: (s, tile(i) // tiles_per_seq, 0, 0))
    mods_spec = lambda s: pl.BlockSpec((None, 1, mods.shape[2], 3 * d), lambda i: (s, 0, 0, 0), **once)
    gain_spec = lambda s: pl.BlockSpec((None, 1, d), lambda i: (s, 0, 0))
    chunk = lambda i: jnp.minimum(i, n_load - 1)
    in_specs = [
        rows_p, rows_s, modp_spec(sub), mods_spec(sub), gain_spec(sub),
        pl.BlockSpec((None, d, 2 * dff // n_load), lambda i: (layer, 0, chunk(i))),
        pl.BlockSpec((None, _FF_CHUNK, d), lambda i: (layer, chunk(i), 0)),
        gain_spec(sub),
    ]
    args = [xp, xs, modp, mods, gpre, w_in, w_out, gpost]
    n_acts = 0
    if mixer is not None:
        msub, widx, mw, acts_p, acts_s = mixer
        n_acts = len(acts_p)
        in_specs += [modp_spec(msub), mods_spec(msub), gain_spec(msub),
                     pl.BlockSpec((None, mw.shape[1], d), lambda i: (widx, 0, 0), pipeline_mode=pl.Buffered(1))]
        in_specs += [pl.BlockSpec((tm, a.shape[1]), lambda i: (tile(i), 0)) for a in acts_p]
        in_specs += [pl.BlockSpec((tm, a.shape[1]), lambda i: (0, 0), **once) for a in acts_s]
        args += [modp, mods, gpost, mw] + list(acts_p) + list(acts_s)
    return pl.pallas_call(
        functools.partial(_ffn_kernel, res_w=res_w, n_load=n_load, n_prompt=n_prompt, n_acts=n_acts),
        grid=(n_load + n_prompt + 1,),
        in_specs=in_specs,
        out_specs=[rows_p, pl.BlockSpec((tm, d), lambda i: (0, 0))],
        out_shape=[jax.ShapeDtypeStruct((n, d), _F32), jax.ShapeDtypeStruct((tm, d), _F32)],
        scratch_shapes=[pltpu.VMEM((n_load, d, 2 * dff // n_load), _BF16),
                        pltpu.VMEM((n_load, _FF_CHUNK, d), _BF16)],
        compiler_params=pltpu.CompilerParams(dimension_semantics=("arbitrary",), vmem_limit_bytes=_VMEM_LIMIT_FFN),
        name="ffn_sublayer" if mixer is None else "mixer_out_ffn",
    )(*args)


def _rope(x, cos, sin_lo, sin_hi):
    outs = []
    for j in range(x.shape[1] // _LANES):
        xc = x[:, j * _LANES:(j + 1) * _LANES]
        nxt = pltpu.roll(xc, _LANES - _ROT_DIM // 2, axis=1)
        prv = pltpu.roll(xc, _ROT_DIM // 2, axis=1)
        outs.append(xc * cos + nxt * sin_lo + prv * sin_hi)
    return outs[0] if len(outs) == 1 else jnp.concatenate(outs, axis=1)


def _even_in_kernel(x_ref, mod_ref, gpre_ref, w_ref, cos_ref, slo_ref, shi_ref,
                    q_ref, k_ref, v_ref, xg_ref, xr_ref):
    h = _pre(x_ref[...], mod_ref[0], gpre_ref[...])
    y = _dot(h, w_ref[...])
    nq, nk, nw = q_ref.shape[1], k_ref.shape[1], xg_ref.shape[1]
    cos, slo, shi = cos_ref[...], slo_ref[...], shi_ref[...]
    q_ref[...] = _rope(y[:, :nq], cos, slo, shi)
    k_ref[...] = _rope(y[:, nq:nq + nk], cos, slo, shi)
    v_ref[...] = y[:, nq + nk:nq + 2 * nk]
    xg_ref[...] = y[:, nq + 2 * nk:nq + 2 * nk + nw]
    xr_ref[...] = y[:, nq + 2 * nk + nw:]


def _even_in_call(x, mod4, sub, e, gpre, w_in, rope_tabs, seq_rows):
    n, d = x.shape
    r = mod4.shape[2]
    tm = min(_ROW_TILE, n)
    tiles_per_seq = tab_tiles = seq_rows // tm
    nq = _A_HEADS * _A_HEAD_DIM
    nk = _A_KV_HEADS * _A_HEAD_DIM
    nw = (w_in.shape[-1] - nq - 2 * nk) // 2
    row = lambda i: (i, 0)
    tab = pl.BlockSpec((tm, _LANES), lambda i: (i % tab_tiles, 0))
    widths = [nq, nk, nk, nw, nw]
    return pl.pallas_call(
        _even_in_kernel,
        grid=(n // tm,),
        in_specs=[
            pl.BlockSpec((tm, d), row),
            pl.BlockSpec((None, 1, r, 3 * d), lambda i: (sub, i // tiles_per_seq, 0, 0)),
            pl.BlockSpec((None, 1, d), lambda i: (sub, 0, 0)),
            pl.BlockSpec((None, d, w_in.shape[-1]), lambda i: (e, 0, 0)),
            tab, tab, tab,
        ],
        out_specs=[pl.BlockSpec((tm, c), row) for c in widths],
        out_shape=[jax.ShapeDtypeStruct((n, c), _F32) for c in widths],
        compiler_params=_params("parallel"),
        name="even_in_proj",
    )(x, mod4, gpre, w_in, *rope_tabs)


def _odd_in_kernel(x_ref, mod_ref, gpre_ref, w_ref, q_ref, f_ref, v_ref, g_ref):
    h = _pre(x_ref[...], mod_ref[0], gpre_ref[...])
    y = _dot(h, w_ref[...])
    n = q_ref.shape[1]
    q_ref[...] = y[:, :n]
    f_ref[...] = y[:, n:2 * n]
    v_ref[...] = y[:, 2 * n:3 * n]
    g_ref[...] = y[:, 3 * n:]


def _odd_in_call(x, mod4, sub, o, gpre, w_in, seq_rows):
    n, d = x.shape
    r = mod4.shape[2]
    tm = min(_ROW_TILE, n)
    tiles_per_seq = seq_rows // tm
    nw = w_in.shape[-1] // 4
    row = lambda i: (i, 0)
    return pl.pallas_call(
        _odd_in_kernel,
        grid=(n // tm,),
        in_specs=[
            pl.BlockSpec((tm, d), row),
            pl.BlockSpec((None, 1, r, 3 * d), lambda i: (sub, i // tiles_per_seq, 0, 0)),
            pl.BlockSpec((None, 1, d), lambda i: (sub, 0, 0)),
            pl.BlockSpec((None, d, 4 * nw), lambda i: (o, 0, 0)),
        ],
        out_specs=[pl.BlockSpec((tm, nw), row)] * 4,
        out_shape=[jax.ShapeDtypeStruct((n, nw), _F32)] * 4,
        compiler_params=_params("parallel"),
        name="odd_in_proj",
    )(x, mod4, gpre, w_in)


def _seq_tile(time_tiles, s):
    row = lax.broadcasted_iota(jnp.int32, time_tiles[0].shape, 0)
    out = jnp.zeros(time_tiles[0].shape, time_tiles[0].dtype)
    for t, x in enumerate(time_tiles):
        shift = (t - s) % _SUBLANES
        out = jnp.where(row == t, pltpu.roll(x, shift, axis=0) if shift else x, out)
    return out


def _time_tiles(seq_tiles, n_t):
    row = lax.broadcasted_iota(jnp.int32, seq_tiles[0].shape, 0)
    outs = []
    for t in range(n_t):
        acc = jnp.zeros(seq_tiles[0].shape, seq_tiles[0].dtype)
        for s, x in enumerate(seq_tiles):
            shift = (s - t) % _SUBLANES
            acc = jnp.where(row == s, pltpu.roll(x, shift, axis=0) if shift else x, acc)
        outs.append(acc)
    return outs


def _sink_softmax_pv(s, mask, sink, v):
    s = jnp.where(mask, s, -jnp.inf)
    m = jnp.maximum(jnp.max(s, axis=-1, keepdims=True), sink)
    p = jnp.exp(s - m)
    denom = jnp.sum(p, axis=-1, keepdims=True) + jnp.exp(sink - m)
    return _dot(p, v) / denom


def _swa_prompt_kernel(sink_ref, q_ref, kp_ref, kc_ref, vp_ref, vc_ref, o_ref):
    w = _WINDOW
    hd = _A_HEAD_DIM
    assert _LANES == 2 * hd and _A_GROUP % 2 == 0
    log2e = np.float32(np.log2(np.e))
    scale = np.float32(1.0 / np.sqrt(hd)) * log2e
    low = lax.broadcasted_iota(jnp.int32, (2 * w, _LANES), 1) < hd
    ones_lo = jnp.where(low, 1.0, 0.0).astype(_BF16)
    ones_hi = jnp.where(low, 0.0, 1.0).astype(_BF16)
    low_q = lax.broadcasted_iota(jnp.int32, (w, _LANES), 1) < hd
    row = lax.broadcasted_iota(jnp.int32, (w, 4 * w), 0)
    col = lax.broadcasted_iota(jnp.int32, (w, 4 * w), 1) & (2 * w - 1)
    pairs = range(_A_HEADS // 2)
    kv_of = [(2 * p) // _A_GROUP for p in pairs]

    def score_stage(qb):
        rows = slice(qb * w, (qb + 1) * w)
        if qb == 0:
            k2 = jnp.concatenate([kp_ref[...], kc_ref[0:w, :]], axis=0)
            v2 = jnp.concatenate([vp_ref[...], vc_ref[0:w, :]], axis=0)
            first = jnp.where(pl.program_id(1) > 0, 0, w)
        else:
            k2 = kc_ref[(qb - 1) * w:(qb + 1) * w, :]
            v2 = vc_ref[(qb - 1) * w:(qb + 1) * w, :]
            first = 0
        mask = jnp.logical_and(col > jnp.maximum(row, first - 1), col <= row + w)
        keys, vals = [], []
        for j in range(_A_KV_HEADS):
            own_k = jnp.where(low, k2, 0.0) if j == 0 else jnp.where(low, 0.0, k2)
            own_v = jnp.where(low, v2, 0.0) if j == 0 else jnp.where(low, 0.0, v2)
            oth_k = pltpu.roll(own_k, hd, axis=1)
            oth_v = pltpu.roll(own_v, hd, axis=1)
            lo_k, hi_k = (own_k, oth_k) if j == 0 else (oth_k, own_k)
            lo_v, hi_v = (own_v, oth_v) if j == 0 else (oth_v, own_v)
            keys.append(jnp.concatenate([lo_k, hi_k], axis=0).astype(_BF16))
            vals.append(jnp.concatenate([jnp.concatenate([lo_v.astype(_BF16), ones_lo], axis=1),
                                         jnp.concatenate([hi_v.astype(_BF16), ones_hi], axis=1)], axis=0))
        scores = [_dot_nt(q_ref[rows, p * _LANES:(p + 1) * _LANES] * scale, keys[kv_of[p]]) for p in pairs]
        return scores, mask, vals

    def softmax_stage(scores, mask):
        probs, sink_terms = [], []
        for p in pairs:
            s = jnp.where(mask, scores[p], -jnp.inf)
            halves = []
            for i in range(2):
                sh = s[:, i * 2 * w:(i + 1) * 2 * w]
                sink = sink_ref[2 * p + i] * log2e
                m = jnp.maximum(jnp.max(sh, axis=-1, keepdims=True), sink)
                halves.append((jnp.exp2(sh - m), jnp.exp2(sink - m)))
            probs.append(jnp.concatenate([halves[0][0], halves[1][0]], axis=1).astype(_BF16))
            sink_terms.append(jnp.where(low_q, halves[0][1], halves[1][1]))
        return probs, sink_terms

    def value_stage(qb, probs, sink_terms, vals):
        rows = slice(qb * w, (qb + 1) * w)
        for p in pairs:
            r = _dot(probs[p], vals[kv_of[p]])
            o_ref[rows, p * _LANES:(p + 1) * _LANES] = r[:, :_LANES] / (r[:, _LANES:] + sink_terms[p])

    n_qb = q_ref.shape[0] // w
    scored, soft = {}, {}
    for step in range(n_qb + 2):
        if step < n_qb:
            scored[step] = score_stage(step)
        if 0 <= step - 2 < n_qb:
            qb = step - 2
            value_stage(qb, *soft.pop(qb), scored.pop(qb)[2])
        if 0 <= step - 1 < n_qb:
            qb = step - 1
            soft[qb] = softmax_stage(*scored[qb][:2])


def _swa_prompt_call(q, k, v, sinks, batch):
    n, nq = q.shape
    nk = k.shape[1]
    w = _WINDOW
    qb = _SWA_QBLOCKS
    nb = n // batch // (w * qb)
    cur = lambda b, i: (b * nb + i, 0)
    prev = lambda b, i: ((b * nb + i) * qb - jnp.minimum(i, 1), 0)
    return pl.pallas_call(
        _swa_prompt_kernel,
        grid=(batch, nb),
        in_specs=[
            pl.BlockSpec(memory_space=pltpu.SMEM),
            pl.BlockSpec((qb * w, nq), cur),
            pl.BlockSpec((w, nk), prev), pl.BlockSpec((qb * w, nk), cur),
            pl.BlockSpec((w, nk), prev), pl.BlockSpec((qb * w, nk), cur),
        ],
        out_specs=pl.BlockSpec((qb * w, nq), cur),
        out_shape=jax.ShapeDtypeStruct((n, nq), _F32),
        compiler_params=_params("parallel", "parallel"),
        name="swa_prompt",
    )(sinks, q, k, k, v, v)


def _swa_sample_kernel(sink_ref, q_ref, kn_ref, vn_ref, ck_ref, cv_ref, o_ref, kw_ref, vw_ref, *, t_new):
    p = ck_ref.shape[1]
    scale = np.float32(1.0 / np.sqrt(_A_HEAD_DIM))
    rows = _A_GROUP * _SUBLANES
    t = lax.broadcasted_iota(jnp.int32, (rows, p + _SUBLANES), 0) & (_SUBLANES - 1)
    c = lax.broadcasted_iota(jnp.int32, (rows, p + _SUBLANES), 1)
    mask = jnp.logical_and(c <= t + p, c > t + p - _WINDOW)
    g_of_row = lax.broadcasted_iota(jnp.int32, (rows, 1), 0) >> (_SUBLANES.bit_length() - 1)
    nseq = ck_ref.shape[0]
    q_t = [q_ref[t] for t in range(t_new)]
    kn_t = [kn_ref[t] for t in range(t_new)]
    vn_t = [vn_ref[t] for t in range(t_new)]
    sinks = []
    for j in range(_A_KV_HEADS):
        sink = jnp.zeros((rows, 1), _F32)
        for g in range(_A_GROUP):
            sink = jnp.where(g_of_row == g, sink_ref[j * _A_GROUP + g], sink)
        sinks.append(sink)
    new_k, new_v, qs, keys, vals = [], [], [], [], []
    for s in range(nseq):
        q8, kn8, vn8 = _seq_tile(q_t, s), _seq_tile(kn_t, s), _seq_tile(vn_t, s)
        ck, cv = ck_ref[s], cv_ref[s]
        kw_ref[s, 0:p - t_new, :] = ck[t_new:, :]
        kw_ref[s, p - t_new:p, :] = kn8[0:t_new, :]
        vw_ref[s, 0:p - t_new, :] = cv[t_new:, :]
        vw_ref[s, p - t_new:p, :] = vn8[0:t_new, :]
        for j in range(_A_KV_HEADS):
            ks = slice(j * _A_HEAD_DIM, (j + 1) * _A_HEAD_DIM)
            keys.append(jnp.concatenate([ck[:, ks], kn8[:, ks]], axis=0))
            vals.append(jnp.concatenate([cv[:, ks], vn8[:, ks]], axis=0))
            qs.append(jnp.concatenate(
                [q8[:, (j * _A_GROUP + g) * _A_HEAD_DIM:(j * _A_GROUP + g + 1) * _A_HEAD_DIM]
                 for g in range(_A_GROUP)], axis=0))
    scores = [_dot_nt(qj, k) * scale for qj, k in zip(qs, keys)]
    outs = [_sink_softmax_pv(sc, mask, sinks[i % _A_KV_HEADS], v) for i, (sc, v) in enumerate(zip(scores, vals))]
    per_seq = []
    for s in range(nseq):
        heads = []
        for j in range(_A_KV_HEADS):
            o = outs[s * _A_KV_HEADS + j]
            heads.extend(o[g * _SUBLANES:(g + 1) * _SUBLANES, :] for g in range(_A_GROUP))
        per_seq.append(jnp.concatenate(heads, axis=1))
    for t, tile in enumerate(_time_tiles(per_seq, t_new)):
        o_ref[t] = tile


def _swa_sample_call(q, k_new, v_new, cache_k, cache_v, sinks):
    t_new, nseq, nq = q.shape
    nk = k_new.shape[2]
    p = cache_k.shape[1]
    bs = _SAMPLE_SEQS
    toks = lambda i: (0, i, 0)
    seqs = lambda i: (i, 0, 0)
    return pl.pallas_call(
        functools.partial(_swa_sample_kernel, t_new=t_new),
        grid=(nseq // bs,),
        in_specs=[
            pl.BlockSpec(memory_space=pltpu.SMEM),
            pl.BlockSpec((t_new, bs, nq), toks),
            pl.BlockSpec((t_new, bs, nk), toks), pl.BlockSpec((t_new, bs, nk), toks),
            pl.BlockSpec((bs, p, nk), seqs), pl.BlockSpec((bs, p, nk), seqs),
        ],
        out_specs=[pl.BlockSpec((t_new, bs, nq), toks),
                   pl.BlockSpec((bs, p, nk), seqs), pl.BlockSpec((bs, p, nk), seqs)],
        out_shape=[jax.ShapeDtypeStruct((t_new, nseq, nq), _F32),
                   jax.ShapeDtypeStruct(cache_k.shape, _F32), jax.ShapeDtypeStruct(cache_v.shape, _F32)],
        compiler_params=_params("parallel"),
        name="swa_sample",
    )(sinks, q, k_new, v_new, cache_k, cache_v)


def _softplus(z):
    return jnp.maximum(z, 0.0) + jnp.log1p(jnp.exp(-jnp.abs(z)))


def _gelu_tanh(x):
    return 0.5 * x * (1.0 + jnp.tanh(np.float32(np.sqrt(2.0 / np.pi)) * (x + 0.044715 * (x * x * x))))


def _rg_gates(xc, wa_ref, ba_ref, wx_ref, bx_ref, sp_neg_lam):
    r = _sigmoid(_dot(xc, wa_ref[...]) + ba_ref[...])
    i = _sigmoid(_dot(xc, wx_ref[...]) + bx_ref[...])
    a = jnp.exp((-_RG_C) * r * sp_neg_lam)
    gap = jnp.maximum(1.0 - a * a, 0.0)
    mult = jnp.where(gap > 0.0, gap * lax.rsqrt(gap), 0.0)
    return a, mult * (i * xc)


def _rg_prompt_kernel(xr_ref, xg_ref, cw_ref, cb_ref, wa_ref, ba_ref, wx_ref, bx_ref, lam_ref,
                      o_ref, conv_ref, h_ref, xpad, a_s, b_s):
    t_len, w = xr_ref.shape
    cw = cw_ref.shape[0]
    xpad[0:_SUBLANES, :] = jnp.zeros((_SUBLANES, w), _F32)
    xpad[_SUBLANES:, :] = xr_ref[...]
    sp = _softplus(-lam_ref[...])
    for c in range(t_len // _RG_CHUNK):
        r0 = c * _RG_CHUNK
        xc = cb_ref[...]
        for j in range(cw):
            lo = _SUBLANES + r0 - (cw - 1) + j
            xc = xc + xpad[lo:lo + _RG_CHUNK, :] * cw_ref[j:j + 1, :]
        a, b = _rg_gates(xc, wa_ref, ba_ref, wx_ref, bx_ref, sp)
        a_s[r0:r0 + _RG_CHUNK, :] = a
        b_s[r0:r0 + _RG_CHUNK, :] = b

    row = lax.broadcasted_iota(jnp.int32, (_SUBLANES, w), 0)

    def group(g, h):
        r0 = pl.multiple_of(g * _SUBLANES, _SUBLANES)
        a = a_s[pl.ds(r0, _SUBLANES), :]
        b = b_s[pl.ds(r0, _SUBLANES), :]
        sh = 1
        while sh < _SUBLANES:
            a_prev = jnp.where(row >= sh, pltpu.roll(a, sh, axis=0), 1.0)
            b_prev = jnp.where(row >= sh, pltpu.roll(b, sh, axis=0), 0.0)
            b = a * b_prev + b
            a = a * a_prev
            sh *= 2
        hs = a * h + b
        o_ref[pl.ds(r0, _SUBLANES), :] = _gelu_tanh(xg_ref[pl.ds(r0, _SUBLANES), :]) * hs
        return hs[_SUBLANES - 1:_SUBLANES, :]

    h_last = lax.fori_loop(0, t_len // _SUBLANES, group, jnp.zeros((1, w), _F32), unroll=2)
    h_ref[0] = h_last
    conv_ref[0] = xr_ref[t_len - (cw - 1):t_len, :]


def _rg_prompt_call(xr, xg, conv_w, conv_b, wa_bd, ba, wx_bd, bx, lam, batch):
    n, w = xr.shape
    t_len = n // batch
    cw = conv_w.shape[0]
    seq = lambda b: (b, 0)
    const = lambda b: (0, 0)
    vec = pl.BlockSpec((1, w), const)
    mat = pl.BlockSpec((w, w), const)
    return pl.pallas_call(
        _rg_prompt_kernel,
        grid=(batch,),
        in_specs=[pl.BlockSpec((t_len, w), seq), pl.BlockSpec((t_len, w), seq),
                  pl.BlockSpec((cw, w), const), vec, mat, vec, mat, vec, vec],
        out_specs=[pl.BlockSpec((t_len, w), seq),
                   pl.BlockSpec((1, cw - 1, w), lambda b: (b, 0, 0)),
                   pl.BlockSpec((1, 1, w), lambda b: (b, 0, 0))],
        out_shape=[jax.ShapeDtypeStruct((n, w), _F32),
                   jax.ShapeDtypeStruct((batch, cw - 1, w), _F32),
                   jax.ShapeDtypeStruct((batch, 1, w), _F32)],
        scratch_shapes=[pltpu.VMEM((t_len + _SUBLANES, w), _F32),
                        pltpu.VMEM((t_len, w), _F32), pltpu.VMEM((t_len, w), _F32)],
        compiler_params=_params("parallel"),
        name="rglru_prompt",
    )(xr, xg, conv_w, conv_b, wa_bd, ba, wx_bd, bx, lam)


def _rg_sample_kernel(xr_ref, xg_ref, conv0_ref, h0_ref, cw_ref, cb_ref, wa_ref, ba_ref, wx_ref, bx_ref, lam_ref,
                      o_ref, conv_ref, h_ref):
    cw = cw_ref.shape[0]
    t_new = xr_ref.shape[0]
    sp = _softplus(-lam_ref[...])
    hist = [conv0_ref[j] for j in range(cw - 1)] + [xr_ref[t] for t in range(t_new)]
    h = h0_ref[...]
    for t in range(t_new):
        xc = cb_ref[...]
        for j in range(cw):
            xc = xc + hist[t + j] * cw_ref[j:j + 1, :]
        a, b = _rg_gates(xc, wa_ref, ba_ref, wx_ref, bx_ref, sp)
        h = a * h + b
        o_ref[t] = _gelu_tanh(xg_ref[t]) * h
    h_ref[...] = h
    for j in range(cw - 1):
        conv_ref[j] = hist[t_new + j]


def _rg_sample_call(xr_t, xg_t, conv0_t, h0, conv_w, conv_b, wa_bd, ba, wx_bd, bx, lam):
    t_new, nseq, w = xg_t.shape
    return pl.pallas_call(
        _rg_sample_kernel,
        out_shape=[jax.ShapeDtypeStruct((t_new, nseq, w), _F32), jax.ShapeDtypeStruct(conv0_t.shape, _F32),
                   jax.ShapeDtypeStruct((nseq, w), _F32)],
        compiler_params=pltpu.CompilerParams(vmem_limit_bytes=_VMEM_LIMIT),
        name="rglru_sample",
    )(xr_t, xg_t, conv0_t, h0, conv_w, conv_b, wa_bd, ba, wx_bd, bx, lam)


def _row_bcast(x, r, n):
    return jnp.broadcast_to(x[r:r + 1, :], (n, x.shape[1]))


def _chunk_cumsum(x):
    n_tiles = x.shape[0] // _SUBLANES
    row = lax.broadcasted_iota(jnp.int32, (_SUBLANES, x.shape[1]), 0)
    tiles = []
    carry = None
    for i in range(n_tiles):
        t = x[i * _SUBLANES:(i + 1) * _SUBLANES, :]
        sh = 1
        while sh < _SUBLANES:
            t = t + jnp.where(row >= sh, pltpu.roll(t, sh, axis=0), 0.0)
            sh *= 2
        if carry is not None:
            t = t + carry
        carry = _row_bcast(t, _SUBLANES - 1, _SUBLANES)
        tiles.append(t)
    return tiles[0] if n_tiles == 1 else jnp.concatenate(tiles, axis=0)


def _level_reference(b, m):
    n = b.shape[0]
    if 2 * m >= _SUBLANES:
        pieces = [_row_bcast(b, lo + m - 1, 2 * m) for lo in range(0, n, 2 * m)]
        return pieces[0] if len(pieces) == 1 else jnp.concatenate(pieces, axis=0)
    row = lax.broadcasted_iota(jnp.int32, (_SUBLANES, b.shape[1]), 0)
    tiles = []
    for i in range(n // _SUBLANES):
        t = b[i * _SUBLANES:(i + 1) * _SUBLANES, :]
        ref = None
        for lo in range(0, _SUBLANES, 2 * m):
            piece = _row_bcast(t, lo + m - 1, _SUBLANES)
            ref = piece if ref is None else jnp.where(row >= lo, piece, ref)
        tiles.append(ref)
    return tiles[0] if len(tiles) == 1 else jnp.concatenate(tiles, axis=0)


def _hgrn_gates(fz, lb):
    f = lb + (1.0 - lb) * _sigmoid(fz)
    return jnp.log(f), 1.0 - f


def _hgrn_chunk(q, fz, v, lb, state):
    n, kd = q.shape
    log_f, k = _hgrn_gates(fz, lb)
    b = _chunk_cumsum(log_f)
    b_last = _row_bcast(b, n - 1, n)

    o = _dot_nt(q * jnp.exp(b), state)
    new_state = jnp.exp(b_last[0:1, :]) * state + _dot_tn(v, k * jnp.exp(b_last - b))

    row = lax.broadcasted_iota(jnp.int32, (n, n), 0)
    col = lax.broadcasted_iota(jnp.int32, (n, n), 1)
    upper = lax.broadcasted_iota(jnp.int32, (n, kd), 0)
    scores = jnp.where(row == col, jnp.sum(q * k, axis=-1, keepdims=True), 0.0)
    m = 1
    while m < n:
        e = jnp.exp(-jnp.abs(b - _level_reference(b, m)))
        z = jnp.where((upper & m) != 0, q, k) * e
        pair = jnp.logical_and((row & m) != 0, (row ^ m) >> (m.bit_length() - 1) == col >> (m.bit_length() - 1))
        scores = scores + jnp.where(pair, _dot_nt(z, z), 0.0)
        m *= 2
    return o + _dot(scores, v), new_state


def _lower_bound(logits, layer):
    m = jnp.max(logits, axis=0, keepdims=True)
    e = jnp.exp(logits - m)
    return jnp.sum(e[1:layer + 1, :], axis=0, keepdims=True) / jnp.sum(e, axis=0, keepdims=True)


def _hgrn_out(o, g, gnorm):
    return _rms(o, gnorm) * _silu(g)


def _hgrn_prompt_kernel(q_ref, f_ref, v_ref, g_ref, lbl_ref, gn_ref, o_ref, s_ref,
                        st_scr, st0_scr, *, layer):
    rows_blk, width = q_ref.shape
    kd = width // _HG_HEADS
    n_chunks = rows_blk // _HG_CHUNK
    heads = [slice(h * kd, (h + 1) * kd) for h in range(_HG_HEADS)]
    lb = _lower_bound(lbl_ref[...], layer)
    gn = gn_ref[...]

    @pl.when(pl.program_id(2) == 0)
    def _():
        st_scr[...] = jnp.zeros(st_scr.shape, _F32)

    st0_scr[...] = st_scr[...]

    def chunk_rows(c):
        return pl.ds(pl.multiple_of(c * _HG_CHUNK, _HG_CHUNK), _HG_CHUNK)

    def prepare(c):
        rows = chunk_rows(c)
        log_f, k = _hgrn_gates(f_ref[rows, :], lb)
        b = _chunk_cumsum(log_f)
        b_end = b[_HG_CHUNK - 1:_HG_CHUNK, :]
        dec = jnp.exp(b_end)
        e_b = jnp.exp(b)
        k_start = k / e_b
        q, v = q_ref[rows, :], v_ref[rows, :]
        qk = q * k
        own = jnp.concatenate([jnp.broadcast_to(jnp.sum(qk[:, l], axis=-1, keepdims=True), (_HG_CHUNK, kd))
                               for l in heads], axis=1) * v
        ops = ((q * e_b).astype(_BF16), k_start.astype(_BF16), (k_start * dec).astype(_BF16),
               v.astype(_BF16), dec, own)
        return ops, b_end

    row = lax.broadcasted_iota(jnp.int32, (_HG_CHUNK, _HG_CHUNK), 0)
    col = lax.broadcasted_iota(jnp.int32, (_HG_CHUNK, _HG_CHUNK), 1)

    def finish(c, h, o):
        rows = chunk_rows(c)
        o_ref[rows, heads[h]] = _hgrn_out(o, g_ref[rows, heads[h]], gn)

    def contract(c, ops):
        qs, ks, ke, vb, dec, own = ops
        states = [st_scr[h] for h in range(_HG_HEADS)]
        scores = [_dot_nt(qs[:, l], ks[:, l]) for l in heads]
        carried = [_dot_nt(qs[:, l], st) for l, st in zip(heads, states)]
        incs = [_dot_tn(vb[:, l], ke[:, l]) for l in heads]
        outs = [_dot(jnp.where(row > col, s, 0.0), vb[:, l]) for s, l in zip(scores, heads)]
        for h, l in enumerate(heads):
            st_scr[h] = states[h] * dec[:, l] + incs[h]
        for h, l in enumerate(heads):
            finish(c, h, outs[h] + carried[h] + own[:, l])

    def step(c, carry):
        ops, min_b = carry
        contract(c, ops)
        ops, b_end = prepare(c + 1)
        return ops, jnp.minimum(min_b, b_end)

    last_ops, min_b = lax.fori_loop(0, n_chunks - 1, step, prepare(0), unroll=True)
    contract(n_chunks - 1, last_ops)

    @pl.when(jnp.min(min_b) <= -_HG_SAFE_LOG_DECAY)
    def _():
        st_scr[...] = st0_scr[...]

        def chunk(c, carry):
            rows = chunk_rows(c)
            for h, l in enumerate(heads):
                o, st_scr[h] = _hgrn_chunk(q_ref[rows, l], f_ref[rows, l], v_ref[rows, l], lb[:, l], st_scr[h])
                finish(c, h, o)
            return carry
        lax.fori_loop(0, n_chunks, chunk, 0)

    @pl.when(pl.program_id(2) == pl.num_programs(2) - 1)
    def _():
        for h in range(_HG_HEADS):
            s_ref[0, h] = st_scr[h].T


def _hgrn_prompt_call(q, fz, v, g, lb_logits, gnorm, batch, layer):
    n, width = q.shape
    t_len = n // batch
    kd = width // _C_HEADS
    gw = _HG_HEADS * kd
    rows_blk = min(_HG_ROWS, t_len)
    nt = t_len // rows_blk
    blk = pl.BlockSpec((rows_blk, gw), lambda b, h, t: (b * nt + t, h))
    return pl.pallas_call(
        functools.partial(_hgrn_prompt_kernel, layer=layer),
        grid=(batch, _C_HEADS // _HG_HEADS, nt),
        in_specs=[blk, blk, blk, blk,
                  pl.BlockSpec((lb_logits.shape[0], gw), lambda b, h, t: (0, h)),
                  pl.BlockSpec((1, kd), lambda b, h, t: (0, 0))],
        out_specs=[blk, pl.BlockSpec((1, _HG_HEADS, kd, kd), lambda b, h, t: (b, h, 0, 0))],
        out_shape=[jax.ShapeDtypeStruct((n, width), _F32),
                   jax.ShapeDtypeStruct((batch, _C_HEADS, kd, kd), _F32)],
        scratch_shapes=[pltpu.VMEM((_HG_HEADS, kd, kd), _F32)] * 2,
        compiler_params=_params("parallel", "parallel", "arbitrary"),
        name="hgrn2_prompt",
    )(q, fz, v, g, lb_logits, gnorm)


def _hgrn_sample_kernel(q_ref, f_ref, v_ref, g_ref, lbl_ref, gn_ref, s0_ref, o_ref, s_ref, *, layer):
    t_new, nseq, width = q_ref.shape
    n_heads = s0_ref.shape[1]
    kd = width // n_heads
    lanes = [slice(h * kd, (h + 1) * kd) for h in range(n_heads)]
    lb = _lower_bound(lbl_ref[...], layer)
    gn = gn_ref[...]
    q = [q_ref[t] for t in range(t_new)]
    v = [v_ref[t] for t in range(t_new)]
    keys, b = [], []
    for t in range(t_new):
        log_f, k = _hgrn_gates(f_ref[t], lb)
        keys.append(k)
        b.append(log_f if t == 0 else b[-1] + log_f)

    def per_head_sum(w):
        return jnp.concatenate([jnp.broadcast_to(jnp.sum(w[:, l], axis=-1, keepdims=True), (nseq, kd))
                                for l in lanes], axis=1)

    within = []
    for t in range(t_new):
        acc = per_head_sum(q[t] * keys[t]) * v[t]
        for s in range(t):
            acc = acc + per_head_sum(q[t] * keys[s] * jnp.exp(b[t] - b[s])) * v[s]
        within.append(acc)

    q_in = [q[t] * jnp.exp(b[t]) for t in range(t_new)]
    k_end = [keys[t] * jnp.exp(b[-1] - b[t]) for t in range(t_new)]
    decay = jnp.exp(b[-1])
    pairs = [(s, h) for h in range(n_heads) for s in range(nseq)]
    lhs = [_seq_tile([x[:, lanes[h]] for x in q_in], s) for s, h in pairs]
    k_seq = [_seq_tile([x[:, lanes[h]] for x in k_end], s) for s, h in pairs]
    v_seq = [_seq_tile([x[:, lanes[h]] for x in v], s) for s, h in pairs]
    carried = [_dot(x, s0_ref[s, h]) for x, (s, h) in zip(lhs, pairs)]
    incs = [_dot_tn(ks, vs) for ks, vs in zip(k_seq, v_seq)]
    carried_t = []
    for h in range(n_heads):
        dec_cols = jnp.concatenate([decay[:, lanes[h]], jnp.zeros((kd - nseq, kd), _F32)], axis=0).T
        for s in range(nseq):
            i = h * nseq + s
            s_ref[s, h] = jnp.broadcast_to(dec_cols[:, s:s + 1], (kd, kd)) * s0_ref[s, h] + incs[i]
        carried_t.append(_time_tiles(carried[h * nseq:(h + 1) * nseq], t_new))
    for t in range(t_new):
        o = within[t] + jnp.concatenate([carried_t[h][t] for h in range(n_heads)], axis=1)
        g = g_ref[t]
        o_ref[t] = jnp.concatenate([_hgrn_out(o[:, l], g[:, l], gn) for l in lanes], axis=1)


def _hgrn_sample_call(q, fz, v, g, lb_logits, gnorm, s0, layer):
    t_new, nseq, width = q.shape
    kd = width // _C_HEADS
    gw = _HG_HEADS * kd
    bs = _SAMPLE_SEQS
    blk = pl.BlockSpec((t_new, bs, gw), lambda i, h: (0, i, h))
    st = pl.BlockSpec((bs, _HG_HEADS, kd, kd), lambda i, h: (i, h, 0, 0))
    return pl.pallas_call(
        functools.partial(_hgrn_sample_kernel, layer=layer),
        grid=(nseq // bs, _C_HEADS // _HG_HEADS),
        in_specs=[blk, blk, blk, blk,
                  pl.BlockSpec((lb_logits.shape[0], gw), lambda i, h: (0, h)),
                  pl.BlockSpec((1, kd), lambda i, h: (0, 0)), st],
        out_specs=[blk, st],
        out_shape=[jax.ShapeDtypeStruct((t_new, nseq, width), _F32), jax.ShapeDtypeStruct(s0.shape, _F32)],
        compiler_params=_params("parallel", "parallel"),
        name="hgrn2_sample",
    )(q, fz, v, g, lb_logits, gnorm, s0)


def _rope_tables(pos):
    half = _ROT_DIM // 2
    inv_freq = _ROPE_THETA ** (-jnp.arange(0, _ROT_DIM, 2, dtype=_F32) / _ROT_DIM)
    ang = pos.astype(_F32)[:, None] * inv_freq[None, :]
    cos, sin = jnp.cos(ang), jnp.sin(ang)
    ones = jnp.ones((pos.shape[0], _A_HEAD_DIM - _ROT_DIM), _F32)
    zeros = jnp.zeros((pos.shape[0], _A_HEAD_DIM - half), _F32)
    zeros_h = jnp.zeros((pos.shape[0], half), _F32)
    reps = _LANES // _A_HEAD_DIM
    cos_t = jnp.tile(jnp.concatenate([cos, cos, ones], axis=1), (1, reps))
    sin_lo = jnp.tile(jnp.concatenate([-sin, zeros], axis=1), (1, reps))
    sin_hi = jnp.tile(jnp.concatenate([zeros_h, sin, ones * 0.0], axis=1), (1, reps))
    return cos_t, sin_lo, sin_hi


def _block_diag(w):
    nb, bd, _ = w.shape
    eye = jnp.eye(nb, dtype=w.dtype)
    return (w[:, :, None, :] * eye[:, None, :, None]).reshape(nb * bd, nb * bd)


def kernel(x_prompt, x_sample, c_prompt, c_sample, cache_k_win, cache_v_win, state_conv_rglru,
           state_h_rglru, state_s_hgrn, norm_pre, norm_post, ada_w, ada_b, ffn1_w_in, ffn1_w_out,
           ffn2_w_in, ffn2_w_out, even_w_in, even_w_out, attn_sinks, rg_conv_w, rg_conv_b, rg_wa,
           rg_ba, rg_wx, rg_bx, rg_lambda, odd_w_in, odd_w_out, hgrn_lb_logits, hgrn_gnorm):
    bp, tp, d = x_prompt.shape
    bs, ts, _ = x_sample.shape
    depth = norm_pre.shape[0]
    n_sub = depth * _N_SUB
    nk = _A_KV_HEADS * _A_HEAD_DIM
    win = cache_k_win.shape[2]
    cw = rg_conv_w.shape[1]
    bw = rg_conv_w.shape[2]
    kd = state_s_hgrn.shape[3]

    cast = lambda w: w.astype(_BF16)
    ffn1_in, ffn1_out, ffn2_in, ffn2_out = ffn1_w_in, ffn1_w_out, ffn2_w_in, ffn2_w_out
    ev_in, ev_out, od_in, od_out = even_w_in, even_w_out, odd_w_in, odd_w_out
    gpre = norm_pre.reshape(n_sub, 1, d)
    gpost = norm_post.reshape(n_sub, 1, d)

    mod = _ada_call(jnp.concatenate([c_prompt, c_sample], axis=0),
                    ada_w.reshape(n_sub, d, 3 * d), ada_b.reshape(n_sub, 1, 3 * d))
    mod_p = mod[:, :bp].reshape(n_sub, bp, 1, 3 * d)
    mod_s = mod[:, bp:].reshape(n_sub, 1, bs, 3 * d)

    tabs_p = _rope_tables(jnp.arange(tp, dtype=jnp.int32))
    tabs_s = tuple(jnp.repeat(t, bs, axis=0) for t in _rope_tables(_PAST_LEN + jnp.arange(ts, dtype=jnp.int32)))

    time_major = lambda a: a.reshape(ts, bs, a.shape[-1])
    xp = x_prompt.reshape(bp * tp, d)
    xs = x_sample.transpose(1, 0, 2).reshape(ts * bs, d)
    groups = {"p": dict(mod=mod_p, seq_rows=tp, tabs=tabs_p), "s": dict(mod=mod_s, seq_rows=ts * bs, tabs=tabs_s)}
    outs = {g: dict(k=[], v=[], conv=[], h=[], s=[]) for g in groups}

    for l in range(depth):
        s0, s1, s2 = l * _N_SUB, l * _N_SUB + 1, l * _N_SUB + 2
        xp, xs = _ffn_call(xp, xs, mod_p, mod_s, s0, l, gpre, ffn1_in, ffn1_out, gpost, 0.5, tp)
        acts = {}
        for name, x in (("p", xp), ("s", xs)):
            grp = groups[name]
            mod4, tps = grp["mod"], grp["seq_rows"]
            if l % 2 == 0:
                e = l // 2
                q, k, v, xg, xr = _even_in_call(x, mod4, s1, e, gpre, ev_in, grp["tabs"], tps)
                wa_bd, wx_bd = cast(_block_diag(rg_wa[e])), cast(_block_diag(rg_wx[e]))
                vecs = [a[e].reshape(1, bw) for a in (rg_conv_b, rg_ba, rg_bx, rg_lambda)]
                if name == "p":
                    o_a = _swa_prompt_call(q, k, v, attn_sinks[e], bp)
                    o_b, conv, h_last = _rg_prompt_call(xr, xg, rg_conv_w[e], vecs[0], wa_bd, vecs[1], wx_bd,
                                                        vecs[2], vecs[3], bp)
                    outs[name]["k"].append(k.reshape(bp, tp, _A_KV_HEADS, _A_HEAD_DIM)[:, tp - win:])
                    outs[name]["v"].append(v.reshape(bp, tp, _A_KV_HEADS, _A_HEAD_DIM)[:, tp - win:])
                    outs[name]["h"].append(h_last.reshape(bp, bw))
                else:
                    o_a, kw, vw = _swa_sample_call(time_major(q), time_major(k), time_major(v),
                                                   cache_k_win[e].reshape(bs, win, nk),
                                                   cache_v_win[e].reshape(bs, win, nk), attn_sinks[e])
                    o_b, conv_t, h_last = _rg_sample_call(time_major(xr), time_major(xg),
                                                          state_conv_rglru[e].transpose(1, 0, 2), state_h_rglru[e],
                                                          rg_conv_w[e], vecs[0], wa_bd, vecs[1], wx_bd, vecs[2], vecs[3])
                    o_a, o_b = o_a.reshape(ts * bs, -1), o_b.reshape(ts * bs, bw)
                    conv = conv_t.transpose(1, 0, 2)
                    outs[name]["k"].append(kw.reshape(bs, win, _A_KV_HEADS, _A_HEAD_DIM))
                    outs[name]["v"].append(vw.reshape(bs, win, _A_KV_HEADS, _A_HEAD_DIM))
                    outs[name]["h"].append(h_last)
                outs[name]["conv"].append(conv)
                acts[name] = [o_a, o_b]
                widx, w_mix = e, ev_out
            else:
                o = l // 2
                q, fz, v, g = _odd_in_call(x, mod4, s1, o, gpre, od_in, tps)
                gn = hgrn_gnorm[o].reshape(1, kd)
                if name == "p":
                    y, s_last = _hgrn_prompt_call(q, fz, v, g, hgrn_lb_logits, gn, bp, l)
                else:
                    y, s_last = _hgrn_sample_call(time_major(q), time_major(fz), time_major(v), time_major(g),
                                                  hgrn_lb_logits, gn, state_s_hgrn[o], l)
                    y = y.reshape(ts * bs, -1)
                outs[name]["s"].append(s_last)
                acts[name] = [y]
                widx, w_mix = o, od_out
        xp, xs = _ffn_call(xp, xs, mod_p, mod_s, s2, l, gpre, ffn2_in, ffn2_out, gpost, 0.5, tp,
                           (s1, widx, w_mix, acts["p"], acts["s"]))

    ys = {"p": xp.reshape(bp, tp, d), "s": xs.reshape(ts, bs, d).transpose(1, 0, 2)}
    res = []
    for name in ("p", "s"):
        o = outs[name]
        res.append((jnp.stack(o["k"]), jnp.stack(o["v"]), jnp.stack(o["conv"]), jnp.stack(o["h"]), jnp.stack(o["s"])))
    return (ys["p"], ys["s"]) + res[0] + res[1]
```

```python
import functools

import jax
import jax.numpy as jnp
import numpy as np
from jax import lax
from jax.experimental import pallas as pl
from jax.experimental.pallas import tpu as pltpu

_F32 = jnp.float32
_BF16 = jnp.bfloat16

_EPS = 1e-6
_A_HEADS = 8
_A_KV_HEADS = 2
_A_HEAD_DIM = 64
_A_GROUP = _A_HEADS // _A_KV_HEADS
_WINDOW = 128
_ROPE_THETA = 500000.0
_ROT_DIM = _A_HEAD_DIM // 4
_RG_C = 8.0
_C_HEADS = 8
_PAST_LEN = 16384
_N_SUB = 3

_LANES = 128
_SUBLANES = 8
_VMEM_BYTES = 64 * 1024 * 1024
_VMEM_LIMIT = _VMEM_BYTES * 3 // 4
_VMEM_LIMIT_FFN = _VMEM_BYTES * 7 // 8

_ROW_TILE = 512
_FF_CHUNK = 256
_RG_CHUNK = 256
_HG_CHUNK = 64
_HG_HEADS = 4
_HG_ROWS = 1024
_HG_SAFE_LOG_DECAY = 80.0
_SAMPLE_SEQS = 8
_SWA_QBLOCKS = 4


def _dot(a, b):
    return jnp.dot(a.astype(_BF16), b.astype(_BF16), preferred_element_type=_F32)


def _dot_nt(a, b):
    return lax.dot_general(a.astype(_BF16), b.astype(_BF16), (((1,), (1,)), ((), ())),
                           preferred_element_type=_F32)


def _dot_tn(a, b):
    return lax.dot_general(a.astype(_BF16), b.astype(_BF16), (((0,), (0,)), ((), ())),
                           preferred_element_type=_F32)


def _sigmoid(x):
    return 1.0 / (1.0 + jnp.exp(-x))


def _silu(x):
    return x * _sigmoid(x)


def _rms(x, gain):
    inv = lax.rsqrt(jnp.mean(x * x, axis=-1, keepdims=True) + _EPS)
    return x * inv * gain


def _per_seq(a, r):
    n = a.shape[0]
    return a if r in (1, n) else a.reshape(n // r, r, a.shape[1])


def _pre(x, mod, gain):
    n, d = x.shape
    h = _rms(_per_seq(x, mod.shape[0]), gain * (1.0 + mod[:, d:2 * d])) + mod[:, :d]
    return h.reshape(n, d)


def _post(x, y, mod, gain, res_w):
    n, d = x.shape
    r = mod.shape[0]
    out = _per_seq(x, r) + _rms(_per_seq(y, r), (res_w * (1.0 + mod[:, 2 * d:])) * gain)
    return out.reshape(n, d)


def _params(*sem):
    return pltpu.CompilerParams(dimension_semantics=sem, vmem_limit_bytes=_VMEM_LIMIT)


def _ada_kernel(c_ref, w_ref, b_ref, o_ref):
    o_ref[...] = _dot(_silu(c_ref[...]), w_ref[...]) + b_ref[...]


def _ada_call(c_all, ada_w, ada_b):
    m, d = c_all.shape
    n_sub = ada_w.shape[0]
    n = ada_w.shape[-1]
    tn = n // 2
    return pl.pallas_call(
        _ada_kernel,
        grid=(n_sub, n // tn),
        in_specs=[
            pl.BlockSpec((m, d), lambda s, j: (0, 0)),
            pl.BlockSpec((None, d, tn), lambda s, j: (s, 0, j)),
            pl.BlockSpec((None, 1, tn), lambda s, j: (s, 0, j)),
        ],
        out_specs=pl.BlockSpec((None, m, tn), lambda s, j: (s, 0, j)),
        out_shape=jax.ShapeDtypeStruct((n_sub, m, n), _F32),
        compiler_params=_params("parallel", "parallel"),
        name="ada_mod",
    )(c_all, ada_w, ada_b)


def _ffn_kernel(xp_ref, xs_ref, modp_ref, mods_ref, gpre_ref, win_ref, wout_ref, gpost_ref, *refs,
                res_w, n_load, n_prompt, n_acts):
    op_ref, os_ref, win_bf, wout_bf = refs[-4:]
    i = pl.program_id(0)
    ld = win_ref.shape[1]
    dff = wout_bf.shape[0] * wout_bf.shape[1]

    @pl.when(i < n_load)
    def _():
        win_bf[i] = win_ref[...].astype(_BF16)
        wout_bf[i] = wout_ref[...].astype(_BF16)

    def w_in_cols(lo):
        return win_bf[lo // ld, :, lo % ld:lo % ld + _FF_CHUNK]

    def row_tile(x_ref, mod_ref, mmod_ref, act_refs, o_ref):
        x = x_ref[...]
        if n_acts:
            mgpost_ref, mw_ref = refs[2:4]
            y = None
            off = 0
            for a_ref in act_refs:
                k = a_ref.shape[1]
                t = _dot(a_ref[...], mw_ref[off:off + k, :])
                y = t if y is None else y + t
                off += k
            x = _post(x, y, mmod_ref[0], mgpost_ref[...], 1.0)
        mod = mod_ref[0]
        h = _pre(x, mod, gpre_ref[...]).astype(_BF16)
        acc = jnp.zeros(x.shape, _F32)
        for j in range(dff // _FF_CHUNK):
            lo = j * _FF_CHUNK
            g = _dot(h, w_in_cols(lo))
            u = _dot(h, w_in_cols(dff + lo))
            acc = acc + _dot(_silu(g) * u, wout_bf[j])
        o_ref[...] = _post(x, acc, mod, gpost_ref[...], res_w)

    acts = refs[4:4 + 2 * n_acts] if n_acts else ()

    @pl.when(jnp.logical_and(i >= n_load, i < n_load + n_prompt))
    def _():
        row_tile(xp_ref, modp_ref, refs[0] if n_acts else None, acts[:n_acts], op_ref)

    @pl.when(i == n_load + n_prompt)
    def _():
        row_tile(xs_ref, mods_ref, refs[1] if n_acts else None, acts[n_acts:], os_ref)


def _ffn_call(xp, xs, modp, mods, sub, layer, gpre, w_in, w_out, gpost, res_w, seq_rows, mixer=None):
    n, d = xp.shape
    tm = xs.shape[0]
    assert n % tm == 0 and seq_rows % tm == 0 and mods.shape[2] * (tm // mods.shape[2]) == tm
    dff = w_out.shape[1]
    n_load = dff // _FF_CHUNK
    n_prompt = n // tm
    tiles_per_seq = seq_rows // tm
    tile = lambda i: jnp.clip(i - n_load, 0, n_prompt - 1)
    rows_p = pl.BlockSpec((tm, d), lambda i: (tile(i), 0))
    once = dict(pipeline_mode=pl.Buffered(1))
    rows_s = pl.BlockSpec((tm, d), lambda i: (0, 0), **once)
    modp_spec = lambda s: pl.BlockSpec((None, 1, 1, 3 * d), lambda i: (s, tile(i) // tiles_per_seq, 0, 0))
    mods_spec = lambda s: pl.BlockSpec((None, 1, mods.shape[2], 3 * d), lambda i: (s, 0, 0, 0), **once)
    gain_spec = lambda s: pl.BlockSpec((None, 1, d), lambda i: (s, 0, 0))
    chunk = lambda i: jnp.minimum(i, n_load - 1)
    in_specs = [
        rows_p, rows_s, modp_spec(sub), mods_spec(sub), gain_spec(sub),
        pl.BlockSpec((None, d, 2 * dff // n_load), lambda i: (layer, 0, chunk(i))),
        pl.BlockSpec((None, _FF_CHUNK, d), lambda i: (layer, chunk(i), 0)),
        gain_spec(sub),
    ]
    args = [xp, xs, modp, mods, gpre, w_in, w_out, gpost]
    n_acts = 0
    if mixer is not None:
        msub, widx, mw, acts_p, acts_s = mixer
        n_acts = len(acts_p)
        in_specs += [modp_spec(msub), mods_spec(msub), gain_spec(msub),
                     pl.BlockSpec((None, mw.shape[1], d), lambda i: (widx, 0, 0), pipeline_mode=pl.Buffered(1))]
        in_specs += [pl.BlockSpec((tm, a.shape[1]), lambda i: (tile(i), 0)) for a in acts_p]
        in_specs += [pl.BlockSpec((tm, a.shape[1]), lambda i: (0, 0), **once) for a in acts_s]
        args += [modp, mods, gpost, mw] + list(acts_p) + list(acts_s)
    return pl.pallas_call(
        functools.partial(_ffn_kernel, res_w=res_w, n_load=n_load, n_prompt=n_prompt, n_acts=n_acts),
        grid=(n_load + n_prompt + 1,),
        in_specs=in_specs,
        out_specs=[rows_p, pl.BlockSpec((tm, d), lambda i: (0, 0))],
        out_shape=[jax.ShapeDtypeStruct((n, d), _F32), jax.ShapeDtypeStruct((tm, d), _F32)],
        scratch_shapes=[pltpu.VMEM((n_load, d, 2 * dff // n_load), _BF16),
                        pltpu.VMEM((n_load, _FF_CHUNK, d), _BF16)],
        compiler_params=pltpu.CompilerParams(dimension_semantics=("arbitrary",), vmem_limit_bytes=_VMEM_LIMIT_FFN),
        name="ffn_sublayer" if mixer is None else "mixer_out_ffn",
    )(*args)


def _rope(x, cos, sin_lo, sin_hi):
    outs = []
    for j in range(x.shape[1] // _LANES):
        xc = x[:, j * _LANES:(j + 1) * _LANES]
        nxt = pltpu.roll(xc, _LANES - _ROT_DIM // 2, axis=1)
        prv = pltpu.roll(xc, _ROT_DIM // 2, axis=1)
        outs.append(xc * cos + nxt * sin_lo + prv * sin_hi)
    return outs[0] if len(outs) == 1 else jnp.concatenate(outs, axis=1)


def _even_in_kernel(x_ref, mod_ref, gpre_ref, w_ref, cos_ref, slo_ref, shi_ref,
                    q_ref, k_ref, v_ref, xg_ref, xr_ref):
    h = _pre(x_ref[...], mod_ref[0], gpre_ref[...])
    y = _dot(h, w_ref[...])
    nq, nk, nw = q_ref.shape[1], k_ref.shape[1], xg_ref.shape[1]
    cos, slo, shi = cos_ref[...], slo_ref[...], shi_ref[...]
    q_ref[...] = _rope(y[:, :nq], cos, slo, shi)
    k_ref[...] = _rope(y[:, nq:nq + nk], cos, slo, shi)
    v_ref[...] = y[:, nq + nk:nq + 2 * nk]
    xg_ref[...] = y[:, nq + 2 * nk:nq + 2 * nk + nw]
    xr_ref[...] = y[:, nq + 2 * nk + nw:]


def _even_in_call(x, mod4, sub, e, gpre, w_in, rope_tabs, seq_rows):
    n, d = x.shape
    r = mod4.shape[2]
    tm = min(_ROW_TILE, n)
    tiles_per_seq = tab_tiles = seq_rows // tm
    nq = _A_HEADS * _A_HEAD_DIM
    nk = _A_KV_HEADS * _A_HEAD_DIM
    nw = (w_in.shape[-1] - nq - 2 * nk) // 2
    row = lambda i: (i, 0)
    tab = pl.BlockSpec((tm, _LANES), lambda i: (i % tab_tiles, 0))
    widths = [nq, nk, nk, nw, nw]
    return pl.pallas_call(
        _even_in_kernel,
        grid=(n // tm,),
        in_specs=[
            pl.BlockSpec((tm, d), row),
            pl.BlockSpec((None, 1, r, 3 * d), lambda i: (sub, i // tiles_per_seq, 0, 0)),
            pl.BlockSpec((None, 1, d), lambda i: (sub, 0, 0)),
            pl.BlockSpec((None, d, w_in.shape[-1]), lambda i: (e, 0, 0)),
            tab, tab, tab,
        ],
        out_specs=[pl.BlockSpec((tm, c), row) for c in widths],
        out_shape=[jax.ShapeDtypeStruct((n, c), _F32) for c in widths],
        compiler_params=_params("parallel"),
        name="even_in_proj",
    )(x, mod4, gpre, w_in, *rope_tabs)


def _odd_in_kernel(x_ref, mod_ref, gpre_ref, w_ref, q_ref, f_ref, v_ref, g_ref):
    h = _pre(x_ref[...], mod_ref[0], gpre_ref[...])
    y = _dot(h, w_ref[...])
    n = q_ref.shape[1]
    q_ref[...] = y[:, :n]
    f_ref[...] = y[:, n:2 * n]
    v_ref[...] = y[:, 2 * n:3 * n]
    g_ref[...] = y[:, 3 * n:]


def _odd_in_call(x, mod4, sub, o, gpre, w_in, seq_rows):
    n, d = x.shape
    r = mod4.shape[2]
    tm = min(_ROW_TILE, n)
    tiles_per_seq = seq_rows // tm
    nw = w_in.shape[-1] // 4
    row = lambda i: (i, 0)
    return pl.pallas_call(
        _odd_in_kernel,
        grid=(n // tm,),
        in_specs=[
            pl.BlockSpec((tm, d), row),
            pl.BlockSpec((None, 1, r, 3 * d), lambda i: (sub, i // tiles_per_seq, 0, 0)),
            pl.BlockSpec((None, 1, d), lambda i: (sub, 0, 0)),
            pl.BlockSpec((None, d, 4 * nw), lambda i: (o, 0, 0)),
        ],
        out_specs=[pl.BlockSpec((tm, nw), row)] * 4,
        out_shape=[jax.ShapeDtypeStruct((n, nw), _F32)] * 4,
        compiler_params=_params("parallel"),
        name="odd_in_proj",
    )(x, mod4, gpre, w_in)


def _seq_tile(time_tiles, s):
    row = lax.broadcasted_iota(jnp.int32, time_tiles[0].shape, 0)
    out = jnp.zeros(time_tiles[0].shape, time_tiles[0].dtype)
    for t, x in enumerate(time_tiles):
        shift = (t - s) % _SUBLANES
        out = jnp.where(row == t, pltpu.roll(x, shift, axis=0) if shift else x, out)
    return out


def _time_tiles(seq_tiles, n_t):
    row = lax.broadcasted_iota(jnp.int32, seq_tiles[0].shape, 0)
    outs = []
    for t in range(n_t):
        acc = jnp.zeros(seq_tiles[0].shape, seq_tiles[0].dtype)
        for s, x in enumerate(seq_tiles):
            shift = (s - t) % _SUBLANES
            acc = jnp.where(row == s, pltpu.roll(x, shift, axis=0) if shift else x, acc)
        outs.append(acc)
    return outs


def _sink_softmax_pv(s, mask, sink, v):
    s = jnp.where(mask, s, -jnp.inf)
    m = jnp.maximum(jnp.max(s, axis=-1, keepdims=True), sink)
    p = jnp.exp(s - m)
    denom = jnp.sum(p, axis=-1, keepdims=True) + jnp.exp(sink - m)
    return _dot(p, v) / denom


def _swa_prompt_kernel(sink_ref, q_ref, kp_ref, kc_ref, vp_ref, vc_ref, o_ref):
    w = _WINDOW
    hd = _A_HEAD_DIM
    assert _LANES == 2 * hd and _A_GROUP % 2 == 0
    log2e = np.float32(np.log2(np.e))
    scale = np.float32(1.0 / np.sqrt(hd)) * log2e
    low = lax.broadcasted_iota(jnp.int32, (2 * w, _LANES), 1) < hd
    ones_lo = jnp.where(low, 1.0, 0.0).astype(_BF16)
    ones_hi = jnp.where(low, 0.0, 1.0).astype(_BF16)
    low_q = lax.broadcasted_iota(jnp.int32, (w, _LANES), 1) < hd
    row = lax.broadcasted_iota(jnp.int32, (w, 4 * w), 0)
    col = lax.broadcasted_iota(jnp.int32, (w, 4 * w), 1) & (2 * w - 1)
    pairs = range(_A_HEADS // 2)
    kv_of = [(2 * p) // _A_GROUP for p in pairs]

    def score_stage(qb):
        rows = slice(qb * w, (qb + 1) * w)
        if qb == 0:
            k2 = jnp.concatenate([kp_ref[...], kc_ref[0:w, :]], axis=0)
            v2 = jnp.concatenate([vp_ref[...], vc_ref[0:w, :]], axis=0)
            first = jnp.where(pl.program_id(1) > 0, 0, w)
        else:
            k2 = kc_ref[(qb - 1) * w:(qb + 1) * w, :]
            v2 = vc_ref[(qb - 1) * w:(qb + 1) * w, :]
            first = 0
        mask = jnp.logical_and(col > jnp.maximum(row, first - 1), col <= row + w)
        keys, vals = [], []
        for j in range(_A_KV_HEADS):
            own_k = jnp.where(low, k2, 0.0) if j == 0 else jnp.where(low, 0.0, k2)
            own_v = jnp.where(low, v2, 0.0) if j == 0 else jnp.where(low, 0.0, v2)
            oth_k = pltpu.roll(own_k, hd, axis=1)
            oth_v = pltpu.roll(own_v, hd, axis=1)
            lo_k, hi_k = (own_k, oth_k) if j == 0 else (oth_k, own_k)
            lo_v, hi_v = (own_v, oth_v) if j == 0 else (oth_v, own_v)
            keys.append(jnp.concatenate([lo_k, hi_k], axis=0).astype(_BF16))
            vals.append(jnp.concatenate([jnp.concatenate([lo_v.astype(_BF16), ones_lo], axis=1),
                                         jnp.concatenate([hi_v.astype(_BF16), ones_hi], axis=1)], axis=0))
        scores = [_dot_nt(q_ref[rows, p * _LANES:(p + 1) * _LANES] * scale, keys[kv_of[p]]) for p in pairs]
        return scores, mask, vals

    def softmax_stage(scores, mask):
        probs, sink_terms = [], []
        for p in pairs:
            s = jnp.where(mask, scores[p], -jnp.inf)
            halves = []
            for i in range(2):
                sh = s[:, i * 2 * w:(i + 1) * 2 * w]
                sink = sink_ref[2 * p + i] * log2e
                m = jnp.maximum(jnp.max(sh, axis=-1, keepdims=True), sink)
                halves.append((jnp.exp2(sh - m), jnp.exp2(sink - m)))
            probs.append(jnp.concatenate([halves[0][0], halves[1][0]], axis=1).astype(_BF16))
            sink_terms.append(jnp.where(low_q, halves[0][1], halves[1][1]))
        return probs, sink_terms

    def value_stage(qb, probs, sink_terms, vals):
        rows = slice(qb * w, (qb + 1) * w)
        for p in pairs:
            r = _dot(probs[p], vals[kv_of[p]])
            o_ref[rows, p * _LANES:(p + 1) * _LANES] = r[:, :_LANES] / (r[:, _LANES:] + sink_terms[p])

    n_qb = q_ref.shape[0] // w
    scored, soft = {}, {}
    for step in range(n_qb + 2):
        if 0 <= step - 1 < n_qb:
            qb = step - 1
            soft[qb] = softmax_stage(*scored[qb][:2])
        if step < n_qb:
            scored[step] = score_stage(step)
        if 0 <= step - 2 < n_qb:
            qb = step - 2
            value_stage(qb, *soft.pop(qb), scored.pop(qb)[2])


def _swa_prompt_call(q, k, v, sinks, batch):
    n, nq = q.shape
    nk = k.shape[1]
    w = _WINDOW
    qb = _SWA_QBLOCKS
    nb = n // batch // (w * qb)
    cur = lambda b, i: (b * nb + i, 0)
    prev = lambda b, i: ((b * nb + i) * qb - jnp.minimum(i, 1), 0)
    return pl.pallas_call(
        _swa_prompt_kernel,
        grid=(batch, nb),
        in_specs=[
            pl.BlockSpec(memory_space=pltpu.SMEM),
            pl.BlockSpec((qb * w, nq), cur),
            pl.BlockSpec((w, nk), prev), pl.BlockSpec((qb * w, nk), cur),
            pl.BlockSpec((w, nk), prev), pl.BlockSpec((qb * w, nk), cur),
        ],
        out_specs=pl.BlockSpec((qb * w, nq), cur),
        out_shape=jax.ShapeDtypeStruct((n, nq), _F32),
        compiler_params=_params("parallel", "parallel"),
        name="swa_prompt",
    )(sinks, q, k, k, v, v)


def _swa_sample_kernel(sink_ref, q_ref, kn_ref, vn_ref, ck_ref, cv_ref, o_ref, kw_ref, vw_ref, *, t_new):
    p = ck_ref.shape[1]
    scale = np.float32(1.0 / np.sqrt(_A_HEAD_DIM))
    rows = _A_GROUP * _SUBLANES
    t = lax.broadcasted_iota(jnp.int32, (rows, p + _SUBLANES), 0) & (_SUBLANES - 1)
    c = lax.broadcasted_iota(jnp.int32, (rows, p + _SUBLANES), 1)
    mask = jnp.logical_and(c <= t + p, c > t + p - _WINDOW)
    g_of_row = lax.broadcasted_iota(jnp.int32, (rows, 1), 0) >> (_SUBLANES.bit_length() - 1)
    nseq = ck_ref.shape[0]
    q_t = [q_ref[t] for t in range(t_new)]
    kn_t = [kn_ref[t] for t in range(t_new)]
    vn_t = [vn_ref[t] for t in range(t_new)]
    sinks = []
    for j in range(_A_KV_HEADS):
        sink = jnp.zeros((rows, 1), _F32)
        for g in range(_A_GROUP):
            sink = jnp.where(g_of_row == g, sink_ref[j * _A_GROUP + g], sink)
        sinks.append(sink)
    new_k, new_v, qs, keys, vals = [], [], [], [], []
    for s in range(nseq):
        q8, kn8, vn8 = _seq_tile(q_t, s), _seq_tile(kn_t, s), _seq_tile(vn_t, s)
        ck, cv = ck_ref[s], cv_ref[s]
        kw_ref[s, 0:p - t_new, :] = ck[t_new:, :]
        kw_ref[s, p - t_new:p, :] = kn8[0:t_new, :]
        vw_ref[s, 0:p - t_new, :] = cv[t_new:, :]
        vw_ref[s, p - t_new:p, :] = vn8[0:t_new, :]
        for j in range(_A_KV_HEADS):
            ks = slice(j * _A_HEAD_DIM, (j + 1) * _A_HEAD_DIM)
            keys.append(jnp.concatenate([ck[:, ks], kn8[:, ks]], axis=0))
            vals.append(jnp.concatenate([cv[:, ks], vn8[:, ks]], axis=0))
            qs.append(jnp.concatenate(
                [q8[:, (j * _A_GROUP + g) * _A_HEAD_DIM:(j * _A_GROUP + g + 1) * _A_HEAD_DIM]
                 for g in range(_A_GROUP)], axis=0))
    scores = [_dot_nt(qj, k) * scale for qj, k in zip(qs, keys)]
    outs = [_sink_softmax_pv(sc, mask, sinks[i % _A_KV_HEADS], v) for i, (sc, v) in enumerate(zip(scores, vals))]
    per_seq = []
    for s in range(nseq):
        heads = []
        for j in range(_A_KV_HEADS):
            o = outs[s * _A_KV_HEADS + j]
            heads.extend(o[g * _SUBLANES:(g + 1) * _SUBLANES, :] for g in range(_A_GROUP))
        per_seq.append(jnp.concatenate(heads, axis=1))
    for t, tile in enumerate(_time_tiles(per_seq, t_new)):
        o_ref[t] = tile


def _swa_sample_call(q, k_new, v_new, cache_k, cache_v, sinks):
    t_new, nseq, nq = q.shape
    nk = k_new.shape[2]
    p = cache_k.shape[1]
    bs = _SAMPLE_SEQS
    toks = lambda i: (0, i, 0)
    seqs = lambda i: (i, 0, 0)
    return pl.pallas_call(
        functools.partial(_swa_sample_kernel, t_new=t_new),
        grid=(nseq // bs,),
        in_specs=[
            pl.BlockSpec(memory_space=pltpu.SMEM),
            pl.BlockSpec((t_new, bs, nq), toks),
            pl.BlockSpec((t_new, bs, nk), toks), pl.BlockSpec((t_new, bs, nk), toks),
            pl.BlockSpec((bs, p, nk), seqs), pl.BlockSpec((bs, p, nk), seqs),
        ],
        out_specs=[pl.BlockSpec((t_new, bs, nq), toks),
                   pl.BlockSpec((bs, p, nk), seqs), pl.BlockSpec((bs, p, nk), seqs)],
        out_shape=[jax.ShapeDtypeStruct((t_new, nseq, nq), _F32),
                   jax.ShapeDtypeStruct(cache_k.shape, _F32), jax.ShapeDtypeStruct(cache_v.shape, _F32)],
        compiler_params=_params("parallel"),
        name="swa_sample",
    )(sinks, q, k_new, v_new, cache_k, cache_v)


def _softplus(z):
    return jnp.maximum(z, 0.0) + jnp.log1p(jnp.exp(-jnp.abs(z)))


def _gelu_tanh(x):
    return 0.5 * x * (1.0 + jnp.tanh(np.float32(np.sqrt(2.0 / np.pi)) * (x + 0.044715 * (x * x * x))))


def _rg_gates(xc, wa_ref, ba_ref, wx_ref, bx_ref, sp_neg_lam):
    r = _sigmoid(_dot(xc, wa_ref[...]) + ba_ref[...])
    i = _sigmoid(_dot(xc, wx_ref[...]) + bx_ref[...])
    a = jnp.exp((-_RG_C) * r * sp_neg_lam)
    gap = jnp.maximum(1.0 - a * a, 0.0)
    mult = jnp.where(gap > 0.0, gap * lax.rsqrt(gap), 0.0)
    return a, mult * (i * xc)


def _rg_prompt_kernel(xr_ref, xg_ref, cw_ref, cb_ref, wa_ref, ba_ref, wx_ref, bx_ref, lam_ref,
                      o_ref, conv_ref, h_ref, xpad, a_s, b_s):
    t_len, w = xr_ref.shape
    cw = cw_ref.shape[0]
    xpad[0:_SUBLANES, :] = jnp.zeros((_SUBLANES, w), _F32)
    xpad[_SUBLANES:, :] = xr_ref[...]
    sp = _softplus(-lam_ref[...])
    for c in range(t_len // _RG_CHUNK):
        r0 = c * _RG_CHUNK
        xc = cb_ref[...]
        for j in range(cw):
            lo = _SUBLANES + r0 - (cw - 1) + j
            xc = xc + xpad[lo:lo + _RG_CHUNK, :] * cw_ref[j:j + 1, :]
        a, b = _rg_gates(xc, wa_ref, ba_ref, wx_ref, bx_ref, sp)
        a_s[r0:r0 + _RG_CHUNK, :] = a
        b_s[r0:r0 + _RG_CHUNK, :] = b

    row = lax.broadcasted_iota(jnp.int32, (_SUBLANES, w), 0)

    def group(g, h):
        r0 = pl.multiple_of(g * _SUBLANES, _SUBLANES)
        a = a_s[pl.ds(r0, _SUBLANES), :]
        b = b_s[pl.ds(r0, _SUBLANES), :]
        sh = 1
        while sh < _SUBLANES:
            a_prev = jnp.where(row >= sh, pltpu.roll(a, sh, axis=0), 1.0)
            b_prev = jnp.where(row >= sh, pltpu.roll(b, sh, axis=0), 0.0)
            b = a * b_prev + b
            a = a * a_prev
            sh *= 2
        hs = a * h + b
        o_ref[pl.ds(r0, _SUBLANES), :] = _gelu_tanh(xg_ref[pl.ds(r0, _SUBLANES), :]) * hs
        return hs[_SUBLANES - 1:_SUBLANES, :]

    h_last = lax.fori_loop(0, t_len // _SUBLANES, group, jnp.zeros((1, w), _F32), unroll=8)
    h_ref[0] = h_last
    conv_ref[0] = xr_ref[t_len - (cw - 1):t_len, :]


def _rg_prompt_call(xr, xg, conv_w, conv_b, wa_bd, ba, wx_bd, bx, lam, batch):
    n, w = xr.shape
    t_len = n // batch
    cw = conv_w.shape[0]
    seq = lambda b: (b, 0)
    const = lambda b: (0, 0)
    vec = pl.BlockSpec((1, w), const)
    mat = pl.BlockSpec((w, w), const)
    return pl.pallas_call(
        _rg_prompt_kernel,
        grid=(batch,),
        in_specs=[pl.BlockSpec((t_len, w), seq), pl.BlockSpec((t_len, w), seq),
                  pl.BlockSpec((cw, w), const), vec, mat, vec, mat, vec, vec],
        out_specs=[pl.BlockSpec((t_len, w), seq),
                   pl.BlockSpec((1, cw - 1, w), lambda b: (b, 0, 0)),
                   pl.BlockSpec((1, 1, w), lambda b: (b, 0, 0))],
        out_shape=[jax.ShapeDtypeStruct((n, w), _F32),
                   jax.ShapeDtypeStruct((batch, cw - 1, w), _F32),
                   jax.ShapeDtypeStruct((batch, 1, w), _F32)],
        scratch_shapes=[pltpu.VMEM((t_len + _SUBLANES, w), _F32),
                        pltpu.VMEM((t_len, w), _F32), pltpu.VMEM((t_len, w), _F32)],
        compiler_params=_params("parallel"),
        name="rglru_prompt",
    )(xr, xg, conv_w, conv_b, wa_bd, ba, wx_bd, bx, lam)


def _rg_sample_kernel(xr_ref, xg_ref, conv0_ref, h0_ref, cw_ref, cb_ref, wa_ref, ba_ref, wx_ref, bx_ref, lam_ref,
                      o_ref, conv_ref, h_ref):
    cw = cw_ref.shape[0]
    t_new = xr_ref.shape[0]
    sp = _softplus(-lam_ref[...])
    hist = [conv0_ref[j] for j in range(cw - 1)] + [xr_ref[t] for t in range(t_new)]
    h = h0_ref[...]
    for t in range(t_new):
        xc = cb_ref[...]
        for j in range(cw):
            xc = xc + hist[t + j] * cw_ref[j:j + 1, :]
        a, b = _rg_gates(xc, wa_ref, ba_ref, wx_ref, bx_ref, sp)
        h = a * h + b
        o_ref[t] = _gelu_tanh(xg_ref[t]) * h
    h_ref[...] = h
    for j in range(cw - 1):
        conv_ref[j] = hist[t_new + j]


def _rg_sample_call(xr_t, xg_t, conv0_t, h0, conv_w, conv_b, wa_bd, ba, wx_bd, bx, lam):
    t_new, nseq, w = xg_t.shape
    return pl.pallas_call(
        _rg_sample_kernel,
        out_shape=[jax.ShapeDtypeStruct((t_new, nseq, w), _F32), jax.ShapeDtypeStruct(conv0_t.shape, _F32),
                   jax.ShapeDtypeStruct((nseq, w), _F32)],
        compiler_params=pltpu.CompilerParams(vmem_limit_bytes=_VMEM_LIMIT),
        name="rglru_sample",
    )(xr_t, xg_t, conv0_t, h0, conv_w, conv_b, wa_bd, ba, wx_bd, bx, lam)


def _row_bcast(x, r, n):
    return jnp.broadcast_to(x[r:r + 1, :], (n, x.shape[1]))


def _chunk_cumsum(x):
    n_tiles = x.shape[0] // _SUBLANES
    row = lax.broadcasted_iota(jnp.int32, (_SUBLANES, x.shape[1]), 0)
    tiles = []
    carry = None
    for i in range(n_tiles):
        t = x[i * _SUBLANES:(i + 1) * _SUBLANES, :]
        sh = 1
        while sh < _SUBLANES:
            t = t + jnp.where(row >= sh, pltpu.roll(t, sh, axis=0), 0.0)
            sh *= 2
        if carry is not None:
            t = t + carry
        carry = _row_bcast(t, _SUBLANES - 1, _SUBLANES)
        tiles.append(t)
    return tiles[0] if n_tiles == 1 else jnp.concatenate(tiles, axis=0)


def _level_reference(b, m):
    n = b.shape[0]
    if 2 * m >= _SUBLANES:
        pieces = [_row_bcast(b, lo + m - 1, 2 * m) for lo in range(0, n, 2 * m)]
        return pieces[0] if len(pieces) == 1 else jnp.concatenate(pieces, axis=0)
    row = lax.broadcasted_iota(jnp.int32, (_SUBLANES, b.shape[1]), 0)
    tiles = []
    for i in range(n // _SUBLANES):
        t = b[i * _SUBLANES:(i + 1) * _SUBLANES, :]
        ref = None
        for lo in range(0, _SUBLANES, 2 * m):
            piece = _row_bcast(t, lo + m - 1, _SUBLANES)
            ref = piece if ref is None else jnp.where(row >= lo, piece, ref)
        tiles.append(ref)
    return tiles[0] if len(tiles) == 1 else jnp.concatenate(tiles, axis=0)


def _hgrn_gates(fz, lb):
    f = lb + (1.0 - lb) * _sigmoid(fz)
    return jnp.log(f), 1.0 - f


def _hgrn_chunk(q, fz, v, lb, state):
    n, kd = q.shape
    log_f, k = _hgrn_gates(fz, lb)
    b = _chunk_cumsum(log_f)
    b_last = _row_bcast(b, n - 1, n)

    o = _dot_nt(q * jnp.exp(b), state)
    new_state = jnp.exp(b_last[0:1, :]) * state + _dot_tn(v, k * jnp.exp(b_last - b))

    row = lax.broadcasted_iota(jnp.int32, (n, n), 0)
    col = lax.broadcasted_iota(jnp.int32, (n, n), 1)
    upper = lax.broadcasted_iota(jnp.int32, (n, kd), 0)
    scores = jnp.where(row == col, jnp.sum(q * k, axis=-1, keepdims=True), 0.0)
    m = 1
    while m < n:
        e = jnp.exp(-jnp.abs(b - _level_reference(b, m)))
        z = jnp.where((upper & m) != 0, q, k) * e
        pair = jnp.logical_and((row & m) != 0, (row ^ m) >> (m.bit_length() - 1) == col >> (m.bit_length() - 1))
        scores = scores + jnp.where(pair, _dot_nt(z, z), 0.0)
        m *= 2
    return o + _dot(scores, v), new_state


def _lower_bound(logits, layer):
    m = jnp.max(logits, axis=0, keepdims=True)
    e = jnp.exp(logits - m)
    return jnp.sum(e[1:layer + 1, :], axis=0, keepdims=True) / jnp.sum(e, axis=0, keepdims=True)


def _hgrn_out(o, g, gnorm):
    return _rms(o, gnorm) * _silu(g)


def _hgrn_prompt_kernel(q_ref, f_ref, v_ref, g_ref, lbl_ref, gn_ref, o_ref, s_ref,
                        st_scr, st0_scr, *, layer):
    rows_blk, width = q_ref.shape
    kd = width // _HG_HEADS
    n_chunks = rows_blk // _HG_CHUNK
    heads = [slice(h * kd, (h + 1) * kd) for h in range(_HG_HEADS)]
    lb = _lower_bound(lbl_ref[...], layer)
    gn = gn_ref[...]

    @pl.when(pl.program_id(2) == 0)
    def _():
        st_scr[...] = jnp.zeros(st_scr.shape, _F32)

    st0_scr[...] = st_scr[...]

    def chunk_rows(c):
        return pl.ds(pl.multiple_of(c * _HG_CHUNK, _HG_CHUNK), _HG_CHUNK)

    def prepare(c):
        rows = chunk_rows(c)
        log_f, k = _hgrn_gates(f_ref[rows, :], lb)
        b = _chunk_cumsum(log_f)
        b_end = b[_HG_CHUNK - 1:_HG_CHUNK, :]
        dec = jnp.exp(b_end)
        e_b = jnp.exp(b)
        k_start = k / e_b
        q, v = q_ref[rows, :], v_ref[rows, :]
        qk = q * k
        own = jnp.concatenate([jnp.broadcast_to(jnp.sum(qk[:, l], axis=-1, keepdims=True), (_HG_CHUNK, kd))
                               for l in heads], axis=1) * v
        ops = ((q * e_b).astype(_BF16), k_start.astype(_BF16), (k_start * dec).astype(_BF16),
               v.astype(_BF16), dec, own)
        return ops, b_end

    row = lax.broadcasted_iota(jnp.int32, (_HG_CHUNK, _HG_CHUNK), 0)
    col = lax.broadcasted_iota(jnp.int32, (_HG_CHUNK, _HG_CHUNK), 1)

    def finish(c, h, o):
        rows = chunk_rows(c)
        o_ref[rows, heads[h]] = _hgrn_out(o, g_ref[rows, heads[h]], gn)

    def contract(c, ops):
        qs, ks, ke, vb, dec, own = ops
        states = [st_scr[h] for h in range(_HG_HEADS)]
        scores = [_dot_nt(qs[:, l], ks[:, l]) for l in heads]
        carried = [_dot_nt(qs[:, l], st) for l, st in zip(heads, states)]
        incs = [_dot_tn(vb[:, l], ke[:, l]) for l in heads]
        outs = [_dot(jnp.where(row > col, s, 0.0), vb[:, l]) for s, l in zip(scores, heads)]
        for h, l in enumerate(heads):
            st_scr[h] = states[h] * dec[:, l] + incs[h]
        for h, l in enumerate(heads):
            finish(c, h, outs[h] + carried[h] + own[:, l])

    def step(c, carry):
        ops, min_b = carry
        contract(c, ops)
        ops, b_end = prepare(c + 1)
        return ops, jnp.minimum(min_b, b_end)

    last_ops, min_b = lax.fori_loop(0, n_chunks - 1, step, prepare(0), unroll=True)
    contract(n_chunks - 1, last_ops)

    @pl.when(jnp.min(min_b) <= -_HG_SAFE_LOG_DECAY)
    def _():
        st_scr[...] = st0_scr[...]

        def chunk(c, carry):
            rows = chunk_rows(c)
            for h, l in enumerate(heads):
                o, st_scr[h] = _hgrn_chunk(q_ref[rows, l], f_ref[rows, l], v_ref[rows, l], lb[:, l], st_scr[h])
                finish(c, h, o)
            return carry
        lax.fori_loop(0, n_chunks, chunk, 0)

    @pl.when(pl.program_id(2) == pl.num_programs(2) - 1)
    def _():
        for h in range(_HG_HEADS):
            s_ref[0, h] = st_scr[h].T


def _hgrn_prompt_call(q, fz, v, g, lb_logits, gnorm, batch, layer):
    n, width = q.shape
    t_len = n // batch
    kd = width // _C_HEADS
    gw = _HG_HEADS * kd
    rows_blk = min(_HG_ROWS, t_len)
    nt = t_len // rows_blk
    blk = pl.BlockSpec((rows_blk, gw), lambda b, h, t: (b * nt + t, h))
    return pl.pallas_call(
        functools.partial(_hgrn_prompt_kernel, layer=layer),
        grid=(batch, _C_HEADS // _HG_HEADS, nt),
        in_specs=[blk, blk, blk, blk,
                  pl.BlockSpec((lb_logits.shape[0], gw), lambda b, h, t: (0, h)),
                  pl.BlockSpec((1, kd), lambda b, h, t: (0, 0))],
        out_specs=[blk, pl.BlockSpec((1, _HG_HEADS, kd, kd), lambda b, h, t: (b, h, 0, 0))],
        out_shape=[jax.ShapeDtypeStruct((n, width), _F32),
                   jax.ShapeDtypeStruct((batch, _C_HEADS, kd, kd), _F32)],
        scratch_shapes=[pltpu.VMEM((_HG_HEADS, kd, kd), _F32)] * 2,
        compiler_params=_params("parallel", "parallel", "arbitrary"),
        name="hgrn2_prompt",
    )(q, fz, v, g, lb_logits, gnorm)


def _hgrn_sample_kernel(q_ref, f_ref, v_ref, g_ref, lbl_ref, gn_ref, s0_ref, o_ref, s_ref, *, layer):
    t_new, nseq, width = q_ref.shape
    n_heads = s0_ref.shape[1]
    kd = width // n_heads
    lanes = [slice(h * kd, (h + 1) * kd) for h in range(n_heads)]
    lb = _lower_bound(lbl_ref[...], layer)
    gn = gn_ref[...]
    q = [q_ref[t] for t in range(t_new)]
    v = [v_ref[t] for t in range(t_new)]
    keys, b = [], []
    for t in range(t_new):
        log_f, k = _hgrn_gates(f_ref[t], lb)
        keys.append(k)
        b.append(log_f if t == 0 else b[-1] + log_f)

    def per_head_sum(w):
        return jnp.concatenate([jnp.broadcast_to(jnp.sum(w[:, l], axis=-1, keepdims=True), (nseq, kd))
                                for l in lanes], axis=1)

    within = []
    for t in range(t_new):
        acc = per_head_sum(q[t] * keys[t]) * v[t]
        for s in range(t):
            acc = acc + per_head_sum(q[t] * keys[s] * jnp.exp(b[t] - b[s])) * v[s]
        within.append(acc)

    q_in = [q[t] * jnp.exp(b[t]) for t in range(t_new)]
    k_end = [keys[t] * jnp.exp(b[-1] - b[t]) for t in range(t_new)]
    decay = jnp.exp(b[-1])
    pairs = [(s, h) for h in range(n_heads) for s in range(nseq)]
    lhs = [_seq_tile([x[:, lanes[h]] for x in q_in], s) for s, h in pairs]
    k_seq = [_seq_tile([x[:, lanes[h]] for x in k_end], s) for s, h in pairs]
    v_seq = [_seq_tile([x[:, lanes[h]] for x in v], s) for s, h in pairs]
    carried = [_dot(x, s0_ref[s, h]) for x, (s, h) in zip(lhs, pairs)]
    incs = [_dot_tn(ks, vs) for ks, vs in zip(k_seq, v_seq)]
    carried_t = []
    for h in range(n_heads):
        dec_cols = jnp.concatenate([decay[:, lanes[h]], jnp.zeros((kd - nseq, kd), _F32)], axis=0).T
        for s in range(nseq):
            i = h * nseq + s
            s_ref[s, h] = jnp.broadcast_to(dec_cols[:, s:s + 1], (kd, kd)) * s0_ref[s, h] + incs[i]
        carried_t.append(_time_tiles(carried[h * nseq:(h + 1) * nseq], t_new))
    for t in range(t_new):
        o = within[t] + jnp.concatenate([carried_t[h][t] for h in range(n_heads)], axis=1)
        g = g_ref[t]
        o_ref[t] = jnp.concatenate([_hgrn_out(o[:, l], g[:, l], gn) for l in lanes], axis=1)


def _hgrn_sample_call(q, fz, v, g, lb_logits, gnorm, s0, layer):
    t_new, nseq, width = q.shape
    kd = width // _C_HEADS
    gw = _HG_HEADS * kd
    bs = _SAMPLE_SEQS
    blk = pl.BlockSpec((t_new, bs, gw), lambda i, h: (0, i, h))
    st = pl.BlockSpec((bs, _HG_HEADS, kd, kd), lambda i, h: (i, h, 0, 0))
    return pl.pallas_call(
        functools.partial(_hgrn_sample_kernel, layer=layer),
        grid=(nseq // bs, _C_HEADS // _HG_HEADS),
        in_specs=[blk, blk, blk, blk,
                  pl.BlockSpec((lb_logits.shape[0], gw), lambda i, h: (0, h)),
                  pl.BlockSpec((1, kd), lambda i, h: (0, 0)), st],
        out_specs=[blk, st],
        out_shape=[jax.ShapeDtypeStruct((t_new, nseq, width), _F32), jax.ShapeDtypeStruct(s0.shape, _F32)],
        compiler_params=_params("parallel", "parallel"),
        name="hgrn2_sample",
    )(q, fz, v, g, lb_logits, gnorm, s0)


def _rope_tables(pos):
    half = _ROT_DIM // 2
    inv_freq = _ROPE_THETA ** (-jnp.arange(0, _ROT_DIM, 2, dtype=_F32) / _ROT_DIM)
    ang = pos.astype(_F32)[:, None] * inv_freq[None, :]
    cos, sin = jnp.cos(ang), jnp.sin(ang)
    ones = jnp.ones((pos.shape[0], _A_HEAD_DIM - _ROT_DIM), _F32)
    zeros = jnp.zeros((pos.shape[0], _A_HEAD_DIM - half), _F32)
    zeros_h = jnp.zeros((pos.shape[0], half), _F32)
    reps = _LANES // _A_HEAD_DIM
    cos_t = jnp.tile(jnp.concatenate([cos, cos, ones], axis=1), (1, reps))
    sin_lo = jnp.tile(jnp.concatenate([-sin, zeros], axis=1), (1, reps))
    sin_hi = jnp.tile(jnp.concatenate([zeros_h, sin, ones * 0.0], axis=1), (1, reps))
    return cos_t, sin_lo, sin_hi


def _block_diag(w):
    nb, bd, _ = w.shape
    eye = jnp.eye(nb, dtype=w.dtype)
    return (w[:, :, None, :] * eye[:, None, :, None]).reshape(nb * bd, nb * bd)


def kernel(x_prompt, x_sample, c_prompt, c_sample, cache_k_win, cache_v_win, state_conv_rglru,
           state_h_rglru, state_s_hgrn, norm_pre, norm_post, ada_w, ada_b, ffn1_w_in, ffn1_w_out,
           ffn2_w_in, ffn2_w_out, even_w_in, even_w_out, attn_sinks, rg_conv_w, rg_conv_b, rg_wa,
           rg_ba, rg_wx, rg_bx, rg_lambda, odd_w_in, odd_w_out, hgrn_lb_logits, hgrn_gnorm):
    bp, tp, d = x_prompt.shape
    bs, ts, _ = x_sample.shape
    depth = norm_pre.shape[0]
    n_sub = depth * _N_SUB
    nk = _A_KV_HEADS * _A_HEAD_DIM
    win = cache_k_win.shape[2]
    cw = rg_conv_w.shape[1]
    bw = rg_conv_w.shape[2]
    kd = state_s_hgrn.shape[3]

    cast = lambda w: w.astype(_BF16)
    ffn1_in, ffn1_out, ffn2_in, ffn2_out = ffn1_w_in, ffn1_w_out, ffn2_w_in, ffn2_w_out
    ev_in, ev_out, od_in, od_out = even_w_in, even_w_out, odd_w_in, odd_w_out
    gpre = norm_pre.reshape(n_sub, 1, d)
    gpost = norm_post.reshape(n_sub, 1, d)

    mod = _ada_call(jnp.concatenate([c_prompt, c_sample], axis=0),
                    ada_w.reshape(n_sub, d, 3 * d), ada_b.reshape(n_sub, 1, 3 * d))
    mod_p = mod[:, :bp].reshape(n_sub, bp, 1, 3 * d)
    mod_s = mod[:, bp:].reshape(n_sub, 1, bs, 3 * d)

    tabs_p = _rope_tables(jnp.arange(tp, dtype=jnp.int32))
    tabs_s = tuple(jnp.repeat(t, bs, axis=0) for t in _rope_tables(_PAST_LEN + jnp.arange(ts, dtype=jnp.int32)))

    time_major = lambda a: a.reshape(ts, bs, a.shape[-1])
    xp = x_prompt.reshape(bp * tp, d)
    xs = x_sample.transpose(1, 0, 2).reshape(ts * bs, d)
    groups = {"p": dict(mod=mod_p, seq_rows=tp, tabs=tabs_p), "s": dict(mod=mod_s, seq_rows=ts * bs, tabs=tabs_s)}
    outs = {g: dict(k=[], v=[], conv=[], h=[], s=[]) for g in groups}

    for l in range(depth):
        s0, s1, s2 = l * _N_SUB, l * _N_SUB + 1, l * _N_SUB + 2
        xp, xs = _ffn_call(xp, xs, mod_p, mod_s, s0, l, gpre, ffn1_in, ffn1_out, gpost, 0.5, tp)
        acts = {}
        for name, x in (("p", xp), ("s", xs)):
            grp = groups[name]
            mod4, tps = grp["mod"], grp["seq_rows"]
            if l % 2 == 0:
                e = l // 2
                q, k, v, xg, xr = _even_in_call(x, mod4, s1, e, gpre, ev_in, grp["tabs"], tps)
                wa_bd, wx_bd = cast(_block_diag(rg_wa[e])), cast(_block_diag(rg_wx[e]))
                vecs = [a[e].reshape(1, bw) for a in (rg_conv_b, rg_ba, rg_bx, rg_lambda)]
                if name == "p":
                    o_a = _swa_prompt_call(q, k, v, attn_sinks[e], bp)
                    o_b, conv, h_last = _rg_prompt_call(xr, xg, rg_conv_w[e], vecs[0], wa_bd, vecs[1], wx_bd,
                                                        vecs[2], vecs[3], bp)
                    outs[name]["k"].append(k.reshape(bp, tp, _A_KV_HEADS, _A_HEAD_DIM)[:, tp - win:])
                    outs[name]["v"].append(v.reshape(bp, tp, _A_KV_HEADS, _A_HEAD_DIM)[:, tp - win:])
                    outs[name]["h"].append(h_last.reshape(bp, bw))
                else:
                    o_a, kw, vw = _swa_sample_call(time_major(q), time_major(k), time_major(v),
                                                   cache_k_win[e].reshape(bs, win, nk),
                                                   cache_v_win[e].reshape(bs, win, nk), attn_sinks[e])
                    o_b, conv_t, h_last = _rg_sample_call(time_major(xr), time_major(xg),
                                                          state_conv_rglru[e].transpose(1, 0, 2), state_h_rglru[e],
                                                          rg_conv_w[e], vecs[0], wa_bd, vecs[1], wx_bd, vecs[2], vecs[3])
                    o_a, o_b = o_a.reshape(ts * bs, -1), o_b.reshape(ts * bs, bw)
                    conv = conv_t.transpose(1, 0, 2)
                    outs[name]["k"].append(kw.reshape(bs, win, _A_KV_HEADS, _A_HEAD_DIM))
                    outs[name]["v"].append(vw.reshape(bs, win, _A_KV_HEADS, _A_HEAD_DIM))
                    outs[name]["h"].append(h_last)
                outs[name]["conv"].append(conv)
                acts[name] = [o_a, o_b]
                widx, w_mix = e, ev_out
            else:
                o = l // 2
                q, fz, v, g = _odd_in_call(x, mod4, s1, o, gpre, od_in, tps)
                gn = hgrn_gnorm[o].reshape(1, kd)
                if name == "p":
                    y, s_last = _hgrn_prompt_call(q, fz, v, g, hgrn_lb_logits, gn, bp, l)
                else:
                    y, s_last = _hgrn_sample_call(time_major(q), time_major(fz), time_major(v), time_major(g),
                                                  hgrn_lb_logits, gn, state_s_hgrn[o], l)
                    y = y.reshape(ts * bs, -1)
                outs[name]["s"].append(s_last)
                acts[name] = [y]
                widx, w_mix = o, od_out
        xp, xs = _ffn_call(xp, xs, mod_p, mod_s, s2, l, gpre, ffn2_in, ffn2_out, gpost, 0.5, tp,
                           (s1, widx, w_mix, acts["p"], acts["s"]))

    ys = {"p": xp.reshape(bp, tp, d), "s": xs.reshape(ts, bs, d).transpose(1, 0, 2)}
    res = []
    for name in ("p", "s"):
        o = outs[name]
        res.append((jnp.stack(o["k"]), jnp.stack(o["v"]), jnp.stack(o["conv"]), jnp.stack(o["h"]), jnp.stack(o["s"])))
    return (ys["p"], ys["s"]) + res[0] + res[1]
```

```python
import functools

import jax
import jax.numpy as jnp
import numpy as np
from jax import lax
from jax.experimental import pallas as pl
from jax.experimental.pallas import tpu as pltpu

_F32 = jnp.float32
_BF16 = jnp.bfloat16

_EPS = 1e-6
_A_HEADS = 8
_A_KV_HEADS = 2
_A_HEAD_DIM = 64
_A_GROUP = _A_HEADS // _A_KV_HEADS
_WINDOW = 128
_ROPE_THETA = 500000.0
_ROT_DIM = _A_HEAD_DIM // 4
_RG_C = 8.0
_C_HEADS = 8
_PAST_LEN = 16384
_N_SUB = 3

_LANES = 128
_SUBLANES = 8
_VMEM_BYTES = 64 * 1024 * 1024
_VMEM_LIMIT = _VMEM_BYTES * 3 // 4
_VMEM_LIMIT_FFN = _VMEM_BYTES * 7 // 8

_ROW_TILE = 512
_FF_CHUNK = 256
_RG_CHUNK = 256
_HG_CHUNK = 64
_HG_HEADS = 4
_HG_ROWS = 1024
_HG_SAFE_LOG_DECAY = 80.0
_SAMPLE_SEQS = 8
_SWA_QBLOCKS = 4


def _dot(a, b):
    return jnp.dot(a.astype(_BF16), b.astype(_BF16), preferred_element_type=_F32)


def _dot_nt(a, b):
    return lax.dot_general(a.astype(_BF16), b.astype(_BF16), (((1,), (1,)), ((), ())),
                           preferred_element_type=_F32)


def _dot_tn(a, b):
    return lax.dot_general(a.astype(_BF16), b.astype(_BF16), (((0,), (0,)), ((), ())),
                           preferred_element_type=_F32)


def _sigmoid(x):
    return 1.0 / (1.0 + jnp.exp(-x))


def _silu(x):
    return x * _sigmoid(x)


def _rms(x, gain):
    inv = lax.rsqrt(jnp.mean(x * x, axis=-1, keepdims=True) + _EPS)
    return x * inv * gain


def _per_seq(a, r):
    n = a.shape[0]
    return a if r in (1, n) else a.reshape(n // r, r, a.shape[1])


def _pre(x, mod, gain):
    n, d = x.shape
    h = _rms(_per_seq(x, mod.shape[0]), gain * (1.0 + mod[:, d:2 * d])) + mod[:, :d]
    return h.reshape(n, d)


def _post(x, y, mod, gain, res_w):
    n, d = x.shape
    r = mod.shape[0]
    out = _per_seq(x, r) + _rms(_per_seq(y, r), (res_w * (1.0 + mod[:, 2 * d:])) * gain)
    return out.reshape(n, d)


def _params(*sem):
    return pltpu.CompilerParams(dimension_semantics=sem, vmem_limit_bytes=_VMEM_LIMIT)


def _ada_kernel(c_ref, w_ref, b_ref, op_ref, os_ref):
    mod = _dot(_silu(c_ref[...]), w_ref[...]) + b_ref[...]
    n_p = op_ref.shape[0]
    op_ref[...] = mod[:n_p, :]
    os_ref[...] = mod[n_p:, :]


def _ada_call(c_all, n_prompt, ada_w, ada_b):
    m, d = c_all.shape
    assert n_prompt % _SUBLANES == 0
    n_sub = ada_w.shape[0]
    n = ada_w.shape[-1]
    tn = n // 2
    out = lambda rows: (pl.BlockSpec((None, rows, tn), lambda s, j: (s, 0, j)),
                        jax.ShapeDtypeStruct((n_sub, rows, n), _F32))
    (spec_p, shape_p), (spec_s, shape_s) = out(n_prompt), out(m - n_prompt)
    return pl.pallas_call(
        _ada_kernel,
        grid=(n_sub, n // tn),
        in_specs=[
            pl.BlockSpec((m, d), lambda s, j: (0, 0)),
            pl.BlockSpec((None, d, tn), lambda s, j: (s, 0, j)),
            pl.BlockSpec((None, 1, tn), lambda s, j: (s, 0, j)),
        ],
        out_specs=[spec_p, spec_s],
        out_shape=[shape_p, shape_s],
        compiler_params=_params("parallel", "parallel"),
        name="ada_mod",
    )(c_all, ada_w, ada_b)


def _ffn_kernel(xp_ref, xs_ref, modp_ref, mods_ref, gpre_ref, win_ref, wout_ref, gpost_ref, *refs,
                res_w, n_load, n_prompt, n_acts):
    op_ref, os_ref, win_bf, wout_bf = refs[-4:]
    i = pl.program_id(0)
    ld = win_ref.shape[1]
    dff = wout_bf.shape[0] * wout_bf.shape[1]

    @pl.when(i < n_load)
    def _():
        win_bf[i] = win_ref[...].astype(_BF16)
        wout_bf[i] = wout_ref[...].astype(_BF16)

    def w_in_cols(lo):
        return win_bf[lo // ld, :, lo % ld:lo % ld + _FF_CHUNK]

    def row_tile(x_ref, mod_ref, mmod_ref, act_refs, o_ref):
        x = x_ref[...]
        if n_acts:
            mgpost_ref, mw_ref = refs[2:4]
            y = None
            off = 0
            for a_ref in act_refs:
                k = a_ref.shape[1]
                t = _dot(a_ref[...], mw_ref[off:off + k, :])
                y = t if y is None else y + t
                off += k
            x = _post(x, y, mmod_ref[0], mgpost_ref[...], 1.0)
        mod = mod_ref[0]
        h = _pre(x, mod, gpre_ref[...]).astype(_BF16)
        acc = jnp.zeros(x.shape, _F32)
        for j in range(dff // _FF_CHUNK):
            lo = j * _FF_CHUNK
            g = _dot(h, w_in_cols(lo))
            u = _dot(h, w_in_cols(dff + lo))
            acc = acc + _dot(_silu(g) * u, wout_bf[j])
        o_ref[...] = _post(x, acc, mod, gpost_ref[...], res_w)

    acts = refs[4:4 + 2 * n_acts] if n_acts else ()

    @pl.when(jnp.logical_and(i >= n_load, i < n_load + n_prompt))
    def _():
        row_tile(xp_ref, modp_ref, refs[0] if n_acts else None, acts[:n_acts], op_ref)

    @pl.when(i == n_load + n_prompt)
    def _():
        row_tile(xs_ref, mods_ref, refs[1] if n_acts else None, acts[n_acts:], os_ref)


def _ffn_call(xp, xs, modp, mods, sub, layer, gpre, w_in, w_out, gpost, res_w, seq_rows, mixer=None):
    n, d = xp.shape
    tm = xs.shape[0]
    assert n % tm == 0 and seq_rows % tm == 0 and mods.shape[2] * (tm // mods.shape[2]) == tm
    dff = w_out.shape[1]
    n_load = dff // _FF_CHUNK
    n_prompt = n // tm
    tiles_per_seq = seq_rows // tm
    tile = lambda i: jnp.clip(i - n_load, 0, n_prompt - 1)
    rows_p = pl.BlockSpec((tm, d), lambda i: (tile(i), 0))
    once = dict(pipeline_mode=pl.Buffered(1))
    rows_s = pl.BlockSpec((tm, d), lambda i: (0, 0), **once)
    modp_spec = lambda s: pl.BlockSpec((None, 1, 1, 3 * d), lambda i: (s, tile(i) // tiles_per_seq, 0, 0))
    mods_spec = lambda s: pl.BlockSpec((None, 1, mods.shape[2], 3 * d), lambda i: (s, 0, 0, 0), **once)
    gain_spec = lambda s: pl.BlockSpec((None, 1, d), lambda i: (s, 0, 0))
    chunk = lambda i: jnp.minimum(i, n_load - 1)
    in_specs = [
        rows_p, rows_s, modp_spec(sub), mods_spec(sub), gain_spec(sub),
        pl.BlockSpec((None, d, 2 * dff // n_load), lambda i: (layer, 0, chunk(i))),
        pl.BlockSpec((None, _FF_CHUNK, d), lambda i: (layer, chunk(i), 0)),
        gain_spec(sub),
    ]
    args = [xp, xs, modp, mods, gpre, w_in, w_out, gpost]
    n_acts = 0
    if mixer is not None:
        msub, widx, mw, acts_p, acts_s = mixer
        n_acts = len(acts_p)
        in_specs += [modp_spec(msub), mods_spec(msub), gain_spec(msub),
                     pl.BlockSpec((None, mw.shape[1], d), lambda i: (widx, 0, 0), pipeline_mode=pl.Buffered(1))]
        in_specs += [pl.BlockSpec((tm, a.shape[1]), lambda i: (tile(i), 0)) for a in acts_p]
        in_specs += [pl.BlockSpec((tm, a.shape[1]), lambda i: (0, 0), **once) for a in acts_s]
        args += [modp, mods, gpost, mw] + list(acts_p) + list(acts_s)
    return pl.pallas_call(
        functools.partial(_ffn_kernel, res_w=res_w, n_load=n_load, n_prompt=n_prompt, n_acts=n_acts),
        grid=(n_load + n_prompt + 1,),
        in_specs=in_specs,
        out_specs=[rows_p, pl.BlockSpec((tm, d), lambda i: (0, 0))],
        out_shape=[jax.ShapeDtypeStruct((n, d), _F32), jax.ShapeDtypeStruct((tm, d), _F32)],
        scratch_shapes=[pltpu.VMEM((n_load, d, 2 * dff // n_load), _BF16),
                        pltpu.VMEM((n_load, _FF_CHUNK, d), _BF16)],
        compiler_params=pltpu.CompilerParams(dimension_semantics=("arbitrary",), vmem_limit_bytes=_VMEM_LIMIT_FFN),
        name="ffn_sublayer" if mixer is None else "mixer_out_ffn",
    )(*args)


def _rope(x, cos, sin_lo, sin_hi):
    outs = []
    for j in range(x.shape[1] // _LANES):
        xc = x[:, j * _LANES:(j + 1) * _LANES]
        nxt = pltpu.roll(xc, _LANES - _ROT_DIM // 2, axis=1)
        prv = pltpu.roll(xc, _ROT_DIM // 2, axis=1)
        outs.append(xc * cos + nxt * sin_lo + prv * sin_hi)
    return outs[0] if len(outs) == 1 else jnp.concatenate(outs, axis=1)


def _even_in_kernel(x_ref, mod_ref, gpre_ref, w_ref, cos_ref, slo_ref, shi_ref,
                    q_ref, k_ref, v_ref, xg_ref, xr_ref):
    h = _pre(x_ref[...], mod_ref[0], gpre_ref[...])
    y = _dot(h, w_ref[...])
    nq, nk, nw = q_ref.shape[1], k_ref.shape[1], xg_ref.shape[1]
    cos, slo, shi = cos_ref[...], slo_ref[...], shi_ref[...]
    q_ref[...] = _rope(y[:, :nq], cos, slo, shi)
    k_ref[...] = _rope(y[:, nq:nq + nk], cos, slo, shi)
    v_ref[...] = y[:, nq + nk:nq + 2 * nk]
    xg_ref[...] = y[:, nq + 2 * nk:nq + 2 * nk + nw]
    xr_ref[...] = y[:, nq + 2 * nk + nw:]


def _even_in_call(x, mod4, sub, e, gpre, w_in, rope_tabs, seq_rows):
    n, d = x.shape
    r = mod4.shape[2]
    tm = min(_ROW_TILE, n)
    tiles_per_seq = tab_tiles = seq_rows // tm
    nq = _A_HEADS * _A_HEAD_DIM
    nk = _A_KV_HEADS * _A_HEAD_DIM
    nw = (w_in.shape[-1] - nq - 2 * nk) // 2
    row = lambda i: (i, 0)
    tab = pl.BlockSpec((tm, _LANES), lambda i: (i % tab_tiles, 0))
    widths = [nq, nk, nk, nw, nw]
    return pl.pallas_call(
        _even_in_kernel,
        grid=(n // tm,),
        in_specs=[
            pl.BlockSpec((tm, d), row),
            pl.BlockSpec((None, 1, r, 3 * d), lambda i: (sub, i // tiles_per_seq, 0, 0)),
            pl.BlockSpec((None, 1, d), lambda i: (sub, 0, 0)),
            pl.BlockSpec((None, d, w_in.shape[-1]), lambda i: (e, 0, 0)),
            tab, tab, tab,
        ],
        out_specs=[pl.BlockSpec((tm, c), row) for c in widths],
        out_shape=[jax.ShapeDtypeStruct((n, c), _F32) for c in widths],
        compiler_params=_params("parallel"),
        name="even_in_proj",
    )(x, mod4, gpre, w_in, *rope_tabs)


def _odd_in_kernel(x_ref, mod_ref, gpre_ref, w_ref, q_ref, f_ref, v_ref, g_ref):
    h = _pre(x_ref[...], mod_ref[0], gpre_ref[...])
    y = _dot(h, w_ref[...])
    n = q_ref.shape[1]
    q_ref[...] = y[:, :n]
    f_ref[...] = y[:, n:2 * n]
    v_ref[...] = y[:, 2 * n:3 * n]
    g_ref[...] = y[:, 3 * n:]


def _odd_in_call(x, mod4, sub, o, gpre, w_in, seq_rows):
    n, d = x.shape
    r = mod4.shape[2]
    tm = min(_ROW_TILE, n)
    tiles_per_seq = seq_rows // tm
    nw = w_in.shape[-1] // 4
    row = lambda i: (i, 0)
    return pl.pallas_call(
        _odd_in_kernel,
        grid=(n // tm,),
        in_specs=[
            pl.BlockSpec((tm, d), row),
            pl.BlockSpec((None, 1, r, 3 * d), lambda i: (sub, i // tiles_per_seq, 0, 0)),
            pl.BlockSpec((None, 1, d), lambda i: (sub, 0, 0)),
            pl.BlockSpec((None, d, 4 * nw), lambda i: (o, 0, 0)),
        ],
        out_specs=[pl.BlockSpec((tm, nw), row)] * 4,
        out_shape=[jax.ShapeDtypeStruct((n, nw), _F32)] * 4,
        compiler_params=_params("parallel"),
        name="odd_in_proj",
    )(x, mod4, gpre, w_in)


def _seq_tile(time_tiles, s):
    row = lax.broadcasted_iota(jnp.int32, time_tiles[0].shape, 0)
    out = jnp.zeros(time_tiles[0].shape, time_tiles[0].dtype)
    for t, x in enumerate(time_tiles):
        shift = (t - s) % _SUBLANES
        out = jnp.where(row == t, pltpu.roll(x, shift, axis=0) if shift else x, out)
    return out


def _time_tiles(seq_tiles, n_t):
    row = lax.broadcasted_iota(jnp.int32, seq_tiles[0].shape, 0)
    outs = []
    for t in range(n_t):
        acc = jnp.zeros(seq_tiles[0].shape, seq_tiles[0].dtype)
        for s, x in enumerate(seq_tiles):
            shift = (s - t) % _SUBLANES
            acc = jnp.where(row == s, pltpu.roll(x, shift, axis=0) if shift else x, acc)
        outs.append(acc)
    return outs


def _sink_softmax_pv(s, mask, sink, v):
    s = jnp.where(mask, s, -jnp.inf)
    m = jnp.maximum(jnp.max(s, axis=-1, keepdims=True), sink)
    p = jnp.exp(s - m)
    denom = jnp.sum(p, axis=-1, keepdims=True) + jnp.exp(sink - m)
    return _dot(p, v) / denom


def _swa_prompt_kernel(sink_ref, q_ref, kp_ref, kc_ref, vp_ref, vc_ref, o_ref):
    w = _WINDOW
    hd = _A_HEAD_DIM
    assert _LANES == 2 * hd and _A_GROUP % 2 == 0
    log2e = np.float32(np.log2(np.e))
    scale = np.float32(1.0 / np.sqrt(hd)) * log2e
    low = lax.broadcasted_iota(jnp.int32, (2 * w, _LANES), 1) < hd
    ones_lo = jnp.where(low, 1.0, 0.0).astype(_BF16)
    ones_hi = jnp.where(low, 0.0, 1.0).astype(_BF16)
    low_q = lax.broadcasted_iota(jnp.int32, (w, _LANES), 1) < hd
    row = lax.broadcasted_iota(jnp.int32, (w, 4 * w), 0)
    col = lax.broadcasted_iota(jnp.int32, (w, 4 * w), 1) & (2 * w - 1)
    pairs = range(_A_HEADS // 2)
    kv_of = [(2 * p) // _A_GROUP for p in pairs]

    def score_stage(qb):
        rows = slice(qb * w, (qb + 1) * w)
        if qb == 0:
            k2 = jnp.concatenate([kp_ref[...], kc_ref[0:w, :]], axis=0)
            v2 = jnp.concatenate([vp_ref[...], vc_ref[0:w, :]], axis=0)
            first = jnp.where(pl.program_id(1) > 0, 0, w)
        else:
            k2 = kc_ref[(qb - 1) * w:(qb + 1) * w, :]
            v2 = vc_ref[(qb - 1) * w:(qb + 1) * w, :]
            first = 0
        mask = jnp.logical_and(col > jnp.maximum(row, first - 1), col <= row + w)
        keys, vals = [], []
        for j in range(_A_KV_HEADS):
            own_k = jnp.where(low, k2, 0.0) if j == 0 else jnp.where(low, 0.0, k2)
            own_v = jnp.where(low, v2, 0.0) if j == 0 else jnp.where(low, 0.0, v2)
            oth_k = pltpu.roll(own_k, hd, axis=1)
            oth_v = pltpu.roll(own_v, hd, axis=1)
            lo_k, hi_k = (own_k, oth_k) if j == 0 else (oth_k, own_k)
            lo_v, hi_v = (own_v, oth_v) if j == 0 else (oth_v, own_v)
            keys.append(jnp.concatenate([lo_k, hi_k], axis=0).astype(_BF16))
            vals.append(jnp.concatenate([jnp.concatenate([lo_v.astype(_BF16), ones_lo], axis=1),
                                         jnp.concatenate([hi_v.astype(_BF16), ones_hi], axis=1)], axis=0))
        scores = [_dot_nt(q_ref[rows, p * _LANES:(p + 1) * _LANES] * scale, keys[kv_of[p]]) for p in pairs]
        return scores, mask, vals

    def softmax_stage(scores, mask):
        probs, sink_terms = [], []
        for p in pairs:
            s = jnp.where(mask, scores[p], -jnp.inf)
            halves = []
            for i in range(2):
                sh = s[:, i * 2 * w:(i + 1) * 2 * w]
                sink = sink_ref[2 * p + i] * log2e
                m = jnp.maximum(jnp.max(sh, axis=-1, keepdims=True), sink)
                halves.append((jnp.exp2(sh - m), jnp.exp2(sink - m)))
            probs.append(jnp.concatenate([halves[0][0], halves[1][0]], axis=1).astype(_BF16))
            sink_terms.append(jnp.where(low_q, halves[0][1], halves[1][1]))
        return probs, sink_terms

    def value_stage(qb, probs, sink_terms, vals):
        rows = slice(qb * w, (qb + 1) * w)
        for p in pairs:
            r = _dot(probs[p], vals[kv_of[p]])
            o_ref[rows, p * _LANES:(p + 1) * _LANES] = r[:, :_LANES] / (r[:, _LANES:] + sink_terms[p])

    n_qb = q_ref.shape[0] // w
    scored, soft = {}, {}
    for step in range(n_qb + 2):
        if 0 <= step - 1 < n_qb:
            qb = step - 1
            soft[qb] = softmax_stage(*scored[qb][:2])
        if step < n_qb:
            scored[step] = score_stage(step)
        if 0 <= step - 2 < n_qb:
            qb = step - 2
            value_stage(qb, *soft.pop(qb), scored.pop(qb)[2])


def _swa_prompt_call(q, k, v, sinks, batch):
    n, nq = q.shape
    nk = k.shape[1]
    w = _WINDOW
    qb = _SWA_QBLOCKS
    nb = n // batch // (w * qb)
    cur = lambda b, i: (b * nb + i, 0)
    prev = lambda b, i: ((b * nb + i) * qb - jnp.minimum(i, 1), 0)
    return pl.pallas_call(
        _swa_prompt_kernel,
        grid=(batch, nb),
        in_specs=[
            pl.BlockSpec(memory_space=pltpu.SMEM),
            pl.BlockSpec((qb * w, nq), cur),
            pl.BlockSpec((w, nk), prev), pl.BlockSpec((qb * w, nk), cur),
            pl.BlockSpec((w, nk), prev), pl.BlockSpec((qb * w, nk), cur),
        ],
        out_specs=pl.BlockSpec((qb * w, nq), cur),
        out_shape=jax.ShapeDtypeStruct((n, nq), _F32),
        compiler_params=_params("parallel", "parallel"),
        name="swa_prompt",
    )(sinks, q, k, k, v, v)


def _swa_sample_kernel(sink_ref, q_ref, kn_ref, vn_ref, ck_ref, cv_ref, o_ref, kw_ref, vw_ref, *, t_new):
    p = ck_ref.shape[1]
    scale = np.float32(1.0 / np.sqrt(_A_HEAD_DIM))
    rows = _A_GROUP * _SUBLANES
    t = lax.broadcasted_iota(jnp.int32, (rows, p + _SUBLANES), 0) & (_SUBLANES - 1)
    c = lax.broadcasted_iota(jnp.int32, (rows, p + _SUBLANES), 1)
    mask = jnp.logical_and(c <= t + p, c > t + p - _WINDOW)
    g_of_row = lax.broadcasted_iota(jnp.int32, (rows, 1), 0) >> (_SUBLANES.bit_length() - 1)
    nseq = ck_ref.shape[0]
    q_t = [q_ref[t] for t in range(t_new)]
    kn_t = [kn_ref[t] for t in range(t_new)]
    vn_t = [vn_ref[t] for t in range(t_new)]
    sinks = []
    for j in range(_A_KV_HEADS):
        sink = jnp.zeros((rows, 1), _F32)
        for g in range(_A_GROUP):
            sink = jnp.where(g_of_row == g, sink_ref[j * _A_GROUP + g], sink)
        sinks.append(sink)
    new_k, new_v, qs, keys, vals = [], [], [], [], []
    for s in range(nseq):
        q8, kn8, vn8 = _seq_tile(q_t, s), _seq_tile(kn_t, s), _seq_tile(vn_t, s)
        ck, cv = ck_ref[s], cv_ref[s]
        kw_ref[s, 0:p - t_new, :] = ck[t_new:, :]
        kw_ref[s, p - t_new:p, :] = kn8[0:t_new, :]
        vw_ref[s, 0:p - t_new, :] = cv[t_new:, :]
        vw_ref[s, p - t_new:p, :] = vn8[0:t_new, :]
        for j in range(_A_KV_HEADS):
            ks = slice(j * _A_HEAD_DIM, (j + 1) * _A_HEAD_DIM)
            keys.append(jnp.concatenate([ck[:, ks], kn8[:, ks]], axis=0))
            vals.append(jnp.concatenate([cv[:, ks], vn8[:, ks]], axis=0))
            qs.append(jnp.concatenate(
                [q8[:, (j * _A_GROUP + g) * _A_HEAD_DIM:(j * _A_GROUP + g + 1) * _A_HEAD_DIM]
                 for g in range(_A_GROUP)], axis=0))
    scores = [_dot_nt(qj, k) * scale for qj, k in zip(qs, keys)]
    outs = [_sink_softmax_pv(sc, mask, sinks[i % _A_KV_HEADS], v) for i, (sc, v) in enumerate(zip(scores, vals))]
    per_seq = []
    for s in range(nseq):
        heads = []
        for j in range(_A_KV_HEADS):
            o = outs[s * _A_KV_HEADS + j]
            heads.extend(o[g * _SUBLANES:(g + 1) * _SUBLANES, :] for g in range(_A_GROUP))
        per_seq.append(jnp.concatenate(heads, axis=1))
    for t, tile in enumerate(_time_tiles(per_seq, t_new)):
        o_ref[t] = tile


def _swa_sample_call(q, k_new, v_new, cache_k, cache_v, sinks):
    t_new, nseq, nq = q.shape
    nk = k_new.shape[2]
    p = cache_k.shape[1]
    bs = _SAMPLE_SEQS
    toks = lambda i: (0, i, 0)
    seqs = lambda i: (i, 0, 0)
    return pl.pallas_call(
        functools.partial(_swa_sample_kernel, t_new=t_new),
        grid=(nseq // bs,),
        in_specs=[
            pl.BlockSpec(memory_space=pltpu.SMEM),
            pl.BlockSpec((t_new, bs, nq), toks),
            pl.BlockSpec((t_new, bs, nk), toks), pl.BlockSpec((t_new, bs, nk), toks),
            pl.BlockSpec((bs, p, nk), seqs), pl.BlockSpec((bs, p, nk), seqs),
        ],
        out_specs=[pl.BlockSpec((t_new, bs, nq), toks),
                   pl.BlockSpec((bs, p, nk), seqs), pl.BlockSpec((bs, p, nk), seqs)],
        out_shape=[jax.ShapeDtypeStruct((t_new, nseq, nq), _F32),
                   jax.ShapeDtypeStruct(cache_k.shape, _F32), jax.ShapeDtypeStruct(cache_v.shape, _F32)],
        compiler_params=_params("parallel"),
        name="swa_sample",
    )(sinks, q, k_new, v_new, cache_k, cache_v)


def _softplus(z):
    return jnp.maximum(z, 0.0) + jnp.log1p(jnp.exp(-jnp.abs(z)))


def _gelu_tanh(x):
    return 0.5 * x * (1.0 + jnp.tanh(np.float32(np.sqrt(2.0 / np.pi)) * (x + 0.044715 * (x * x * x))))


def _rg_gates(xc, wa_ref, ba_ref, wx_ref, bx_ref, sp_neg_lam):
    r = _sigmoid(_dot(xc, wa_ref[...]) + ba_ref[...])
    i = _sigmoid(_dot(xc, wx_ref[...]) + bx_ref[...])
    a = jnp.exp((-_RG_C) * r * sp_neg_lam)
    gap = jnp.maximum(1.0 - a * a, 0.0)
    mult = jnp.where(gap > 0.0, gap * lax.rsqrt(gap), 0.0)
    return a, mult * (i * xc)


def _rg_prompt_kernel(xr_ref, xg_ref, cw_ref, cb_ref, wa_ref, ba_ref, wx_ref, bx_ref, lam_ref,
                      o_ref, conv_ref, h_ref, xpad, a_s, b_s):
    t_len, w = xr_ref.shape
    cw = cw_ref.shape[0]
    xpad[0:_SUBLANES, :] = jnp.zeros((_SUBLANES, w), _F32)
    xpad[_SUBLANES:, :] = xr_ref[...]
    sp = _softplus(-lam_ref[...])
    for c in range(t_len // _RG_CHUNK):
        r0 = c * _RG_CHUNK
        xc = cb_ref[...]
        for j in range(cw):
            lo = _SUBLANES + r0 - (cw - 1) + j
            xc = xc + xpad[lo:lo + _RG_CHUNK, :] * cw_ref[j:j + 1, :]
        a, b = _rg_gates(xc, wa_ref, ba_ref, wx_ref, bx_ref, sp)
        a_s[r0:r0 + _RG_CHUNK, :] = a
        b_s[r0:r0 + _RG_CHUNK, :] = b

    row = lax.broadcasted_iota(jnp.int32, (_SUBLANES, w), 0)

    def group(g, h):
        r0 = pl.multiple_of(g * _SUBLANES, _SUBLANES)
        a = a_s[pl.ds(r0, _SUBLANES), :]
        b = b_s[pl.ds(r0, _SUBLANES), :]
        sh = 1
        while sh < _SUBLANES:
            a_prev = jnp.where(row >= sh, pltpu.roll(a, sh, axis=0), 1.0)
            b_prev = jnp.where(row >= sh, pltpu.roll(b, sh, axis=0), 0.0)
            b = a * b_prev + b
            a = a * a_prev
            sh *= 2
        hs = a * h + b
        o_ref[pl.ds(r0, _SUBLANES), :] = _gelu_tanh(xg_ref[pl.ds(r0, _SUBLANES), :]) * hs
        return hs[_SUBLANES - 1:_SUBLANES, :]

    h_last = lax.fori_loop(0, t_len // _SUBLANES, group, jnp.zeros((1, w), _F32), unroll=8)
    h_ref[0] = h_last
    conv_ref[0] = xr_ref[t_len - (cw - 1):t_len, :]


def _rg_prompt_call(xr, xg, conv_w, conv_b, wa_bd, ba, wx_bd, bx, lam, batch):
    n, w = xr.shape
    t_len = n // batch
    cw = conv_w.shape[0]
    seq = lambda b: (b, 0)
    const = lambda b: (0, 0)
    vec = pl.BlockSpec((1, w), const)
    mat = pl.BlockSpec((w, w), const)
    return pl.pallas_call(
        _rg_prompt_kernel,
        grid=(batch,),
        in_specs=[pl.BlockSpec((t_len, w), seq), pl.BlockSpec((t_len, w), seq),
                  pl.BlockSpec((cw, w), const), vec, mat, vec, mat, vec, vec],
        out_specs=[pl.BlockSpec((t_len, w), seq),
                   pl.BlockSpec((1, cw - 1, w), lambda b: (b, 0, 0)),
                   pl.BlockSpec((1, 1, w), lambda b: (b, 0, 0))],
        out_shape=[jax.ShapeDtypeStruct((n, w), _F32),
                   jax.ShapeDtypeStruct((batch, cw - 1, w), _F32),
                   jax.ShapeDtypeStruct((batch, 1, w), _F32)],
        scratch_shapes=[pltpu.VMEM((t_len + _SUBLANES, w), _F32),
                        pltpu.VMEM((t_len, w), _F32), pltpu.VMEM((t_len, w), _F32)],
        compiler_params=_params("parallel"),
        name="rglru_prompt",
    )(xr, xg, conv_w, conv_b, wa_bd, ba, wx_bd, bx, lam)


def _rg_sample_kernel(xr_ref, xg_ref, conv0_ref, h0_ref, cw_ref, cb_ref, wa_ref, ba_ref, wx_ref, bx_ref, lam_ref,
                      o_ref, conv_ref, h_ref):
    cw = cw_ref.shape[0]
    t_new = xr_ref.shape[0]
    sp = _softplus(-lam_ref[...])
    hist = [conv0_ref[j] for j in range(cw - 1)] + [xr_ref[t] for t in range(t_new)]
    h = h0_ref[...]
    for t in range(t_new):
        xc = cb_ref[...]
        for j in range(cw):
            xc = xc + hist[t + j] * cw_ref[j:j + 1, :]
        a, b = _rg_gates(xc, wa_ref, ba_ref, wx_ref, bx_ref, sp)
        h = a * h + b
        o_ref[t] = _gelu_tanh(xg_ref[t]) * h
    h_ref[...] = h
    for j in range(cw - 1):
        conv_ref[j] = hist[t_new + j]


def _rg_sample_call(xr_t, xg_t, conv0_t, h0, conv_w, conv_b, wa_bd, ba, wx_bd, bx, lam):
    t_new, nseq, w = xg_t.shape
    return pl.pallas_call(
        _rg_sample_kernel,
        out_shape=[jax.ShapeDtypeStruct((t_new, nseq, w), _F32), jax.ShapeDtypeStruct(conv0_t.shape, _F32),
                   jax.ShapeDtypeStruct((nseq, w), _F32)],
        compiler_params=pltpu.CompilerParams(vmem_limit_bytes=_VMEM_LIMIT),
        name="rglru_sample",
    )(xr_t, xg_t, conv0_t, h0, conv_w, conv_b, wa_bd, ba, wx_bd, bx, lam)


def _row_bcast(x, r, n):
    return jnp.broadcast_to(x[r:r + 1, :], (n, x.shape[1]))


def _chunk_cumsum(x):
    n_tiles = x.shape[0] // _SUBLANES
    row = lax.broadcasted_iota(jnp.int32, (_SUBLANES, x.shape[1]), 0)
    tiles = []
    carry = None
    for i in range(n_tiles):
        t = x[i * _SUBLANES:(i + 1) * _SUBLANES, :]
        sh = 1
        while sh < _SUBLANES:
            t = t + jnp.where(row >= sh, pltpu.roll(t, sh, axis=0), 0.0)
            sh *= 2
        if carry is not None:
            t = t + carry
        carry = _row_bcast(t, _SUBLANES - 1, _SUBLANES)
        tiles.append(t)
    return tiles[0] if n_tiles == 1 else jnp.concatenate(tiles, axis=0)


def _level_reference(b, m):
    n = b.shape[0]
    if 2 * m >= _SUBLANES:
        pieces = [_row_bcast(b, lo + m - 1, 2 * m) for lo in range(0, n, 2 * m)]
        return pieces[0] if len(pieces) == 1 else jnp.concatenate(pieces, axis=0)
    row = lax.broadcasted_iota(jnp.int32, (_SUBLANES, b.shape[1]), 0)
    tiles = []
    for i in range(n // _SUBLANES):
        t = b[i * _SUBLANES:(i + 1) * _SUBLANES, :]
        ref = None
        for lo in range(0, _SUBLANES, 2 * m):
            piece = _row_bcast(t, lo + m - 1, _SUBLANES)
            ref = piece if ref is None else jnp.where(row >= lo, piece, ref)
        tiles.append(ref)
    return tiles[0] if len(tiles) == 1 else jnp.concatenate(tiles, axis=0)


def _hgrn_gates(fz, lb):
    f = lb + (1.0 - lb) * _sigmoid(fz)
    return jnp.log(f), 1.0 - f


def _hgrn_chunk(q, fz, v, lb, state):
    n, kd = q.shape
    log_f, k = _hgrn_gates(fz, lb)
    b = _chunk_cumsum(log_f)
    b_last = _row_bcast(b, n - 1, n)

    o = _dot_nt(q * jnp.exp(b), state)
    new_state = jnp.exp(b_last[0:1, :]) * state + _dot_tn(v, k * jnp.exp(b_last - b))

    row = lax.broadcasted_iota(jnp.int32, (n, n), 0)
    col = lax.broadcasted_iota(jnp.int32, (n, n), 1)
    upper = lax.broadcasted_iota(jnp.int32, (n, kd), 0)
    scores = jnp.where(row == col, jnp.sum(q * k, axis=-1, keepdims=True), 0.0)
    m = 1
    while m < n:
        e = jnp.exp(-jnp.abs(b - _level_reference(b, m)))
        z = jnp.where((upper & m) != 0, q, k) * e
        pair = jnp.logical_and((row & m) != 0, (row ^ m) >> (m.bit_length() - 1) == col >> (m.bit_length() - 1))
        scores = scores + jnp.where(pair, _dot_nt(z, z), 0.0)
        m *= 2
    return o + _dot(scores, v), new_state


def _lower_bound(logits, layer):
    m = jnp.max(logits, axis=0, keepdims=True)
    e = jnp.exp(logits - m)
    return jnp.sum(e[1:layer + 1, :], axis=0, keepdims=True) / jnp.sum(e, axis=0, keepdims=True)


def _hgrn_out(o, g, gnorm):
    return _rms(o, gnorm) * _silu(g)


def _hgrn_prompt_kernel(q_ref, f_ref, v_ref, g_ref, lbl_ref, gn_ref, o_ref, s_ref,
                        st_scr, st0_scr, *, layer):
    rows_blk, width = q_ref.shape
    kd = width // _HG_HEADS
    n_chunks = rows_blk // _HG_CHUNK
    heads = [slice(h * kd, (h + 1) * kd) for h in range(_HG_HEADS)]
    lb = _lower_bound(lbl_ref[...], layer)
    gn = gn_ref[...]

    @pl.when(pl.program_id(2) == 0)
    def _():
        st_scr[...] = jnp.zeros(st_scr.shape, _F32)

    st0_scr[...] = st_scr[...]

    def chunk_rows(c):
        return pl.ds(pl.multiple_of(c * _HG_CHUNK, _HG_CHUNK), _HG_CHUNK)

    def prepare(c):
        rows = chunk_rows(c)
        log_f, k = _hgrn_gates(f_ref[rows, :], lb)
        b = _chunk_cumsum(log_f)
        b_end = b[_HG_CHUNK - 1:_HG_CHUNK, :]
        dec = jnp.exp(b_end)
        e_b = jnp.exp(b)
        k_start = k / e_b
        q, v = q_ref[rows, :], v_ref[rows, :]
        qk = q * k
        own = jnp.concatenate([jnp.broadcast_to(jnp.sum(qk[:, l], axis=-1, keepdims=True), (_HG_CHUNK, kd))
                               for l in heads], axis=1) * v
        ops = ((q * e_b).astype(_BF16), k_start.astype(_BF16), (k_start * dec).astype(_BF16),
               v.astype(_BF16), dec, own)
        return ops, b_end

    row = lax.broadcasted_iota(jnp.int32, (_HG_CHUNK, _HG_CHUNK), 0)
    col = lax.broadcasted_iota(jnp.int32, (_HG_CHUNK, _HG_CHUNK), 1)

    def finish(c, h, o):
        rows = chunk_rows(c)
        o_ref[rows, heads[h]] = _hgrn_out(o, g_ref[rows, heads[h]], gn)

    def contract(c, ops):
        qs, ks, ke, vb, dec, own = ops
        states = [st_scr[h] for h in range(_HG_HEADS)]
        scores = [_dot_nt(qs[:, l], ks[:, l]) for l in heads]
        carried = [_dot_nt(qs[:, l], st) for l, st in zip(heads, states)]
        incs = [_dot_tn(vb[:, l], ke[:, l]) for l in heads]
        outs = [_dot(jnp.where(row > col, s, 0.0), vb[:, l]) for s, l in zip(scores, heads)]
        for h, l in enumerate(heads):
            st_scr[h] = states[h] * dec[:, l] + incs[h]
        for h, l in enumerate(heads):
            finish(c, h, outs[h] + carried[h] + own[:, l])

    def step(c, carry):
        ops, min_b = carry
        contract(c, ops)
        ops, b_end = prepare(c + 1)
        return ops, jnp.minimum(min_b, b_end)

    last_ops, min_b = lax.fori_loop(0, n_chunks - 1, step, prepare(0), unroll=True)
    contract(n_chunks - 1, last_ops)

    @pl.when(jnp.min(min_b) <= -_HG_SAFE_LOG_DECAY)
    def _():
        st_scr[...] = st0_scr[...]

        def chunk(c, carry):
            rows = chunk_rows(c)
            for h, l in enumerate(heads):
                o, st_scr[h] = _hgrn_chunk(q_ref[rows, l], f_ref[rows, l], v_ref[rows, l], lb[:, l], st_scr[h])
                finish(c, h, o)
            return carry
        lax.fori_loop(0, n_chunks, chunk, 0)

    @pl.when(pl.program_id(2) == pl.num_programs(2) - 1)
    def _():
        for h in range(_HG_HEADS):
            s_ref[0, h] = st_scr[h].T


def _hgrn_prompt_call(q, fz, v, g, lb_logits, gnorm, batch, layer):
    n, width = q.shape
    t_len = n // batch
    kd = width // _C_HEADS
    gw = _HG_HEADS * kd
    rows_blk = min(_HG_ROWS, t_len)
    nt = t_len // rows_blk
    blk = pl.BlockSpec((rows_blk, gw), lambda b, h, t: (b * nt + t, h))
    return pl.pallas_call(
        functools.partial(_hgrn_prompt_kernel, layer=layer),
        grid=(batch, _C_HEADS // _HG_HEADS, nt),
        in_specs=[blk, blk, blk, blk,
                  pl.BlockSpec((lb_logits.shape[0], gw), lambda b, h, t: (0, h)),
                  pl.BlockSpec((1, kd), lambda b, h, t: (0, 0))],
        out_specs=[blk, pl.BlockSpec((1, _HG_HEADS, kd, kd), lambda b, h, t: (b, h, 0, 0))],
        out_shape=[jax.ShapeDtypeStruct((n, width), _F32),
                   jax.ShapeDtypeStruct((batch, _C_HEADS, kd, kd), _F32)],
        scratch_shapes=[pltpu.VMEM((_HG_HEADS, kd, kd), _F32)] * 2,
        compiler_params=_params("parallel", "parallel", "arbitrary"),
        name="hgrn2_prompt",
    )(q, fz, v, g, lb_logits, gnorm)


def _hgrn_sample_kernel(q_ref, f_ref, v_ref, g_ref, lbl_ref, gn_ref, s0_ref, o_ref, s_ref, *, layer):
    t_new, nseq, width = q_ref.shape
    n_heads = s0_ref.shape[1]
    kd = width // n_heads
    lanes = [slice(h * kd, (h + 1) * kd) for h in range(n_heads)]
    lb = _lower_bound(lbl_ref[...], layer)
    gn = gn_ref[...]
    q = [q_ref[t] for t in range(t_new)]
    v = [v_ref[t] for t in range(t_new)]
    keys, b = [], []
    for t in range(t_new):
        log_f, k = _hgrn_gates(f_ref[t], lb)
        keys.append(k)
        b.append(log_f if t == 0 else b[-1] + log_f)

    def per_head_sum(w):
        return jnp.concatenate([jnp.broadcast_to(jnp.sum(w[:, l], axis=-1, keepdims=True), (nseq, kd))
                                for l in lanes], axis=1)

    within = []
    for t in range(t_new):
        acc = per_head_sum(q[t] * keys[t]) * v[t]
        for s in range(t):
            acc = acc + per_head_sum(q[t] * keys[s] * jnp.exp(b[t] - b[s])) * v[s]
        within.append(acc)

    q_in = [q[t] * jnp.exp(b[t]) for t in range(t_new)]
    k_end = [keys[t] * jnp.exp(b[-1] - b[t]) for t in range(t_new)]
    decay = jnp.exp(b[-1])
    pairs = [(s, h) for h in range(n_heads) for s in range(nseq)]
    lhs = [_seq_tile([x[:, lanes[h]] for x in q_in], s) for s, h in pairs]
    k_seq = [_seq_tile([x[:, lanes[h]] for x in k_end], s) for s, h in pairs]
    v_seq = [_seq_tile([x[:, lanes[h]] for x in v], s) for s, h in pairs]
    carried = [_dot(x, s0_ref[s, h]) for x, (s, h) in zip(lhs, pairs)]
    incs = [_dot_tn(ks, vs) for ks, vs in zip(k_seq, v_seq)]
    carried_t = []
    for h in range(n_heads):
        dec_cols = jnp.concatenate([decay[:, lanes[h]], jnp.zeros((kd - nseq, kd), _F32)], axis=0).T
        for s in range(nseq):
            i = h * nseq + s
            s_ref[s, h] = jnp.broadcast_to(dec_cols[:, s:s + 1], (kd, kd)) * s0_ref[s, h] + incs[i]
        carried_t.append(_time_tiles(carried[h * nseq:(h + 1) * nseq], t_new))
    for t in range(t_new):
        o = within[t] + jnp.concatenate([carried_t[h][t] for h in range(n_heads)], axis=1)
        g = g_ref[t]
        o_ref[t] = jnp.concatenate([_hgrn_out(o[:, l], g[:, l], gn) for l in lanes], axis=1)


def _hgrn_sample_call(q, fz, v, g, lb_logits, gnorm, s0, layer):
    t_new, nseq, width = q.shape
    kd = width // _C_HEADS
    gw = _HG_HEADS * kd
    bs = _SAMPLE_SEQS
    blk = pl.BlockSpec((t_new, bs, gw), lambda i, h: (0, i, h))
    st = pl.BlockSpec((bs, _HG_HEADS, kd, kd), lambda i, h: (i, h, 0, 0))
    return pl.pallas_call(
        functools.partial(_hgrn_sample_kernel, layer=layer),
        grid=(nseq // bs, _C_HEADS // _HG_HEADS),
        in_specs=[blk, blk, blk, blk,
                  pl.BlockSpec((lb_logits.shape[0], gw), lambda i, h: (0, h)),
                  pl.BlockSpec((1, kd), lambda i, h: (0, 0)), st],
        out_specs=[blk, st],
        out_shape=[jax.ShapeDtypeStruct((t_new, nseq, width), _F32), jax.ShapeDtypeStruct(s0.shape, _F32)],
        compiler_params=_params("parallel", "parallel"),
        name="hgrn2_sample",
    )(q, fz, v, g, lb_logits, gnorm, s0)


def _rope_tables(pos):
    half = _ROT_DIM // 2
    inv_freq = _ROPE_THETA ** (-jnp.arange(0, _ROT_DIM, 2, dtype=_F32) / _ROT_DIM)
    ang = pos.astype(_F32)[:, None] * inv_freq[None, :]
    cos, sin = jnp.cos(ang), jnp.sin(ang)
    ones = jnp.ones((pos.shape[0], _A_HEAD_DIM - _ROT_DIM), _F32)
    zeros = jnp.zeros((pos.shape[0], _A_HEAD_DIM - half), _F32)
    zeros_h = jnp.zeros((pos.shape[0], half), _F32)
    reps = _LANES // _A_HEAD_DIM
    cos_t = jnp.tile(jnp.concatenate([cos, cos, ones], axis=1), (1, reps))
    sin_lo = jnp.tile(jnp.concatenate([-sin, zeros], axis=1), (1, reps))
    sin_hi = jnp.tile(jnp.concatenate([zeros_h, sin, ones * 0.0], axis=1), (1, reps))
    return cos_t, sin_lo, sin_hi


def _block_diag(w):
    nb, bd, _ = w.shape
    eye = jnp.eye(nb, dtype=w.dtype)
    return (w[:, :, None, :] * eye[:, None, :, None]).reshape(nb * bd, nb * bd)


def kernel(x_prompt, x_sample, c_prompt, c_sample, cache_k_win, cache_v_win, state_conv_rglru,
           state_h_rglru, state_s_hgrn, norm_pre, norm_post, ada_w, ada_b, ffn1_w_in, ffn1_w_out,
           ffn2_w_in, ffn2_w_out, even_w_in, even_w_out, attn_sinks, rg_conv_w, rg_conv_b, rg_wa,
           rg_ba, rg_wx, rg_bx, rg_lambda, odd_w_in, odd_w_out, hgrn_lb_logits, hgrn_gnorm):
    bp, tp, d = x_prompt.shape
    bs, ts, _ = x_sample.shape
    depth = norm_pre.shape[0]
    n_sub = depth * _N_SUB
    nk = _A_KV_HEADS * _A_HEAD_DIM
    win = cache_k_win.shape[2]
    cw = rg_conv_w.shape[1]
    bw = rg_conv_w.shape[2]
    kd = state_s_hgrn.shape[3]

    cast = lambda w: w.astype(_BF16)
    ffn1_in, ffn1_out, ffn2_in, ffn2_out = ffn1_w_in, ffn1_w_out, ffn2_w_in, ffn2_w_out
    ev_in, ev_out, od_in, od_out = even_w_in, even_w_out, odd_w_in, odd_w_out
    gpre = norm_pre.reshape(n_sub, 1, d)
    gpost = norm_post.reshape(n_sub, 1, d)

    mod_p, mod_s = _ada_call(jnp.concatenate([c_prompt, c_sample], axis=0), bp,
                             ada_w.reshape(n_sub, d, 3 * d), ada_b.reshape(n_sub, 1, 3 * d))
    mod_p = mod_p.reshape(n_sub, bp, 1, 3 * d)
    mod_s = mod_s.reshape(n_sub, 1, bs, 3 * d)

    tabs_p = _rope_tables(jnp.arange(tp, dtype=jnp.int32))
    tabs_s = tuple(jnp.repeat(t, bs, axis=0) for t in _rope_tables(_PAST_LEN + jnp.arange(ts, dtype=jnp.int32)))

    time_major = lambda a: a.reshape(ts, bs, a.shape[-1])
    xp = x_prompt.reshape(bp * tp, d)
    xs = x_sample.transpose(1, 0, 2).reshape(ts * bs, d)
    groups = {"p": dict(mod=mod_p, seq_rows=tp, tabs=tabs_p), "s": dict(mod=mod_s, seq_rows=ts * bs, tabs=tabs_s)}
    outs = {g: dict(k=[], v=[], conv=[], h=[], s=[]) for g in groups}

    for l in range(depth):
        s0, s1, s2 = l * _N_SUB, l * _N_SUB + 1, l * _N_SUB + 2
        xp, xs = _ffn_call(xp, xs, mod_p, mod_s, s0, l, gpre, ffn1_in, ffn1_out, gpost, 0.5, tp)
        acts = {}
        for name, x in (("p", xp), ("s", xs)):
            grp = groups[name]
            mod4, tps = grp["mod"], grp["seq_rows"]
            if l % 2 == 0:
                e = l // 2
                q, k, v, xg, xr = _even_in_call(x, mod4, s1, e, gpre, ev_in, grp["tabs"], tps)
                wa_bd, wx_bd = cast(_block_diag(rg_wa[e])), cast(_block_diag(rg_wx[e]))
                vecs = [a[e].reshape(1, bw) for a in (rg_conv_b, rg_ba, rg_bx, rg_lambda)]
                if name == "p":
                    o_a = _swa_prompt_call(q, k, v, attn_sinks[e], bp)
                    o_b, conv, h_last = _rg_prompt_call(xr, xg, rg_conv_w[e], vecs[0], wa_bd, vecs[1], wx_bd,
                                                        vecs[2], vecs[3], bp)
                    last = lambda a: a.reshape(bp, tp, nk)[:, tp - win:].reshape(bp, win, _A_KV_HEADS, _A_HEAD_DIM)
                    outs[name]["k"].append(last(k))
                    outs[name]["v"].append(last(v))
                    outs[name]["h"].append(h_last.reshape(bp, bw))
                else:
                    o_a, kw, vw = _swa_sample_call(time_major(q), time_major(k), time_major(v),
                                                   cache_k_win[e].reshape(bs, win, nk),
                                                   cache_v_win[e].reshape(bs, win, nk), attn_sinks[e])
                    o_b, conv_t, h_last = _rg_sample_call(time_major(xr), time_major(xg),
                                                          state_conv_rglru[e].transpose(1, 0, 2), state_h_rglru[e],
                                                          rg_conv_w[e], vecs[0], wa_bd, vecs[1], wx_bd, vecs[2], vecs[3])
                    o_a, o_b = o_a.reshape(ts * bs, -1), o_b.reshape(ts * bs, bw)
                    conv = conv_t.transpose(1, 0, 2)
                    outs[name]["k"].append(kw.reshape(bs, win, _A_KV_HEADS, _A_HEAD_DIM))
                    outs[name]["v"].append(vw.reshape(bs, win, _A_KV_HEADS, _A_HEAD_DIM))
                    outs[name]["h"].append(h_last)
                outs[name]["conv"].append(conv)
                acts[name] = [o_a, o_b]
                widx, w_mix = e, ev_out
            else:
                o = l // 2
                q, fz, v, g = _odd_in_call(x, mod4, s1, o, gpre, od_in, tps)
                gn = hgrn_gnorm[o].reshape(1, kd)
                if name == "p":
                    y, s_last = _hgrn_prompt_call(q, fz, v, g, hgrn_lb_logits, gn, bp, l)
                else:
                    y, s_last = _hgrn_sample_call(time_major(q), time_major(fz), time_major(v), time_major(g),
                                                  hgrn_lb_logits, gn, state_s_hgrn[o], l)
                    y = y.reshape(ts * bs, -1)
                outs[name]["s"].append(s_last)
                acts[name] = [y]
                widx, w_mix = o, od_out
        xp, xs = _ffn_call(xp, xs, mod_p, mod_s, s2, l, gpre, ffn2_in, ffn2_out, gpost, 0.5, tp,
                           (s1, widx, w_mix, acts["p"], acts["s"]))

    ys = {"p": xp.reshape(bp, tp, d), "s": xs.reshape(ts, bs, d).transpose(1, 0, 2)}
    res = []
    for name in ("p", "s"):
        o = outs[name]
        res.append((jnp.stack(o["k"]), jnp.stack(o["v"]), jnp.stack(o["conv"]), jnp.stack(o["h"]), jnp.stack(o["s"])))
    return (ys["p"], ys["s"]) + res[0] + res[1]
```

```python
import functools

import jax
import jax.numpy as jnp
import numpy as np
from jax import lax
from jax.experimental import pallas as pl
from jax.experimental.pallas import tpu as pltpu

_F32 = jnp.float32
_BF16 = jnp.bfloat16

_EPS = 1e-6
_A_HEADS = 8
_A_KV_HEADS = 2
_A_HEAD_DIM = 64
_A_GROUP = _A_HEADS // _A_KV_HEADS
_WINDOW = 128
_ROPE_THETA = 500000.0
_ROT_DIM = _A_HEAD_DIM // 4
_RG_C = 8.0
_C_HEADS = 8
_PAST_LEN = 16384
_N_SUB = 3

_LANES = 128
_SUBLANES = 8
_VMEM_BYTES = 64 * 1024 * 1024
_VMEM_LIMIT = _VMEM_BYTES * 3 // 4
_VMEM_LIMIT_FFN = _VMEM_BYTES * 29 // 32

_ROW_TILE = 512
_FF_CHUNK = 256
_RG_CHUNK = 256
_HG_CHUNK = 64
_HG_HEADS = 4
_HG_ROWS = 1024
_HG_SAFE_LOG_DECAY = 80.0
_SAMPLE_SEQS = 8
_SWA_QBLOCKS = 4


def _dot(a, b):
    return jnp.dot(a.astype(_BF16), b.astype(_BF16), preferred_element_type=_F32)


def _dot_nt(a, b):
    return lax.dot_general(a.astype(_BF16), b.astype(_BF16), (((1,), (1,)), ((), ())),
                           preferred_element_type=_F32)


def _dot_tn(a, b):
    return lax.dot_general(a.astype(_BF16), b.astype(_BF16), (((0,), (0,)), ((), ())),
                           preferred_element_type=_F32)


def _sigmoid(x):
    return 1.0 / (1.0 + jnp.exp(-x))


def _silu(x):
    return x * _sigmoid(x)


def _rms(x, gain):
    inv = lax.rsqrt(jnp.mean(x * x, axis=-1, keepdims=True) + _EPS)
    return x * inv * gain


def _per_seq(a, r):
    n = a.shape[0]
    return a if r in (1, n) else a.reshape(n // r, r, a.shape[1])


def _pre(x, mod, gain):
    n, d = x.shape
    h = _rms(_per_seq(x, mod.shape[0]), gain * (1.0 + mod[:, d:2 * d])) + mod[:, :d]
    return h.reshape(n, d)


def _post(x, y, mod, gain, res_w):
    n, d = x.shape
    r = mod.shape[0]
    out = _per_seq(x, r) + _rms(_per_seq(y, r), (res_w * (1.0 + mod[:, 2 * d:])) * gain)
    return out.reshape(n, d)


def _params(*sem):
    return pltpu.CompilerParams(dimension_semantics=sem, vmem_limit_bytes=_VMEM_LIMIT)


def _ada_kernel(c_ref, w_ref, b_ref, op_ref, os_ref):
    mod = _dot(_silu(c_ref[...]), w_ref[...]) + b_ref[...]
    n_p = op_ref.shape[0]
    op_ref[...] = mod[:n_p, :]
    os_ref[...] = mod[n_p:, :]


def _ada_call(c_all, n_prompt, ada_w, ada_b):
    m, d = c_all.shape
    assert n_prompt % _SUBLANES == 0
    n_sub = ada_w.shape[0]
    n = ada_w.shape[-1]
    tn = n // 2
    out = lambda rows: (pl.BlockSpec((None, rows, tn), lambda s, j: (s, 0, j)),
                        jax.ShapeDtypeStruct((n_sub, rows, n), _F32))
    (spec_p, shape_p), (spec_s, shape_s) = out(n_prompt), out(m - n_prompt)
    return pl.pallas_call(
        _ada_kernel,
        grid=(n_sub, n // tn),
        in_specs=[
            pl.BlockSpec((m, d), lambda s, j: (0, 0)),
            pl.BlockSpec((None, d, tn), lambda s, j: (s, 0, j)),
            pl.BlockSpec((None, 1, tn), lambda s, j: (s, 0, j)),
        ],
        out_specs=[spec_p, spec_s],
        out_shape=[shape_p, shape_s],
        compiler_params=_params("parallel", "parallel"),
        name="ada_mod",
    )(c_all, ada_w, ada_b)


def _ffn_kernel(xp_ref, xs_ref, modp_ref, mods_ref, gpre_ref, wg_ref, wu_ref, wout_ref, gpost_ref, *refs,
                res_w, n_load, n_prompt, n_acts):
    op_ref, os_ref, wg_bf, wu_bf, wout_bf, x0_scr, acc_scr = refs[-7:]
    i = pl.program_id(0)
    acts = refs[4:4 + 2 * n_acts] if n_acts else ()

    def mixed(x_ref, mmod_ref, act_refs):
        x = x_ref[...]
        if n_acts:
            mgpost_ref, mw_ref = refs[2:4]
            y = None
            off = 0
            for a_ref in act_refs:
                k = a_ref.shape[1]
                t = _dot(a_ref[...], mw_ref[off:off + k, :])
                y = t if y is None else y + t
                off += k
            x = _post(x, y, mmod_ref[0], mgpost_ref[...], 1.0)
        return x

    def chunk(h, j):
        return _dot(_silu(_dot(h, wg_bf[j])) * _dot(h, wu_bf[j]), wout_bf[j])

    def row_tile(x_ref, mod_ref, mmod_ref, act_refs, o_ref):
        x = mixed(x_ref, mmod_ref, act_refs)
        mod = mod_ref[0]
        h = _pre(x, mod, gpre_ref[...]).astype(_BF16)
        acc = jnp.zeros(x.shape, _F32)
        for j in range(n_load):
            acc = acc + chunk(h, j)
        o_ref[...] = _post(x, acc, mod, gpost_ref[...], res_w)

    @pl.when(i < n_load)
    def _():
        wg_bf[i] = wg_ref[...].astype(_BF16)
        wu_bf[i] = wu_ref[...].astype(_BF16)
        wout_bf[i] = wout_ref[...].astype(_BF16)

        @pl.when(i == 0)
        def _():
            x0_scr[...] = mixed(xp_ref, refs[0] if n_acts else None, acts[:n_acts])
            acc_scr[...] = jnp.zeros(acc_scr.shape, _F32)

        mod = modp_ref[0]
        acc_scr[...] += chunk(_pre(x0_scr[...], mod, gpre_ref[...]).astype(_BF16), i)

        @pl.when(i == n_load - 1)
        def _():
            op_ref[...] = _post(x0_scr[...], acc_scr[...], mod, gpost_ref[...], res_w)

    @pl.when(jnp.logical_and(i >= n_load, i < n_load + n_prompt - 1))
    def _():
        row_tile(xp_ref, modp_ref, refs[0] if n_acts else None, acts[:n_acts], op_ref)

    @pl.when(i == n_load + n_prompt - 1)
    def _():
        row_tile(xs_ref, mods_ref, refs[1] if n_acts else None, acts[n_acts:], os_ref)


def _ffn_call(xp, xs, modp, mods, sub, layer, gpre, w_in, w_out, gpost, res_w, seq_rows, mixer=None):
    n, d = xp.shape
    tm = xs.shape[0]
    assert n % tm == 0 and seq_rows % tm == 0 and mods.shape[2] * (tm // mods.shape[2]) == tm
    dff = w_out.shape[1]
    n_load = dff // _FF_CHUNK
    n_prompt = n // tm
    tiles_per_seq = seq_rows // tm
    tile = lambda i: jnp.clip(i - n_load + 1, 0, n_prompt - 1)
    rows_p = pl.BlockSpec((tm, d), lambda i: (tile(i), 0))
    once = dict(pipeline_mode=pl.Buffered(1))
    rows_s = pl.BlockSpec((tm, d), lambda i: (0, 0), **once)
    modp_spec = lambda s: pl.BlockSpec((None, 1, 1, 3 * d), lambda i: (s, tile(i) // tiles_per_seq, 0, 0))
    mods_spec = lambda s: pl.BlockSpec((None, 1, mods.shape[2], 3 * d), lambda i: (s, 0, 0, 0), **once)
    gain_spec = lambda s: pl.BlockSpec((None, 1, d), lambda i: (s, 0, 0))
    chunk = lambda i: jnp.minimum(i, n_load - 1)
    in_specs = [
        rows_p, rows_s, modp_spec(sub), mods_spec(sub), gain_spec(sub),
        pl.BlockSpec((None, d, _FF_CHUNK), lambda i: (layer, 0, chunk(i))),
        pl.BlockSpec((None, d, _FF_CHUNK), lambda i: (layer, 0, n_load + chunk(i))),
        pl.BlockSpec((None, _FF_CHUNK, d), lambda i: (layer, chunk(i), 0)),
        gain_spec(sub),
    ]
    args = [xp, xs, modp, mods, gpre, w_in, w_in, w_out, gpost]
    n_acts = 0
    if mixer is not None:
        msub, widx, mw, acts_p, acts_s = mixer
        n_acts = len(acts_p)
        in_specs += [modp_spec(msub), mods_spec(msub), gain_spec(msub),
                     pl.BlockSpec((None, mw.shape[1], d), lambda i: (widx, 0, 0), pipeline_mode=pl.Buffered(1))]
        in_specs += [pl.BlockSpec((tm, a.shape[1]), lambda i: (tile(i), 0)) for a in acts_p]
        in_specs += [pl.BlockSpec((tm, a.shape[1]), lambda i: (0, 0), **once) for a in acts_s]
        args += [modp, mods, gpost, mw] + list(acts_p) + list(acts_s)
    return pl.pallas_call(
        functools.partial(_ffn_kernel, res_w=res_w, n_load=n_load, n_prompt=n_prompt, n_acts=n_acts),
        grid=(n_load + n_prompt,),
        in_specs=in_specs,
        out_specs=[rows_p, pl.BlockSpec((tm, d), lambda i: (0, 0))],
        out_shape=[jax.ShapeDtypeStruct((n, d), _F32), jax.ShapeDtypeStruct((tm, d), _F32)],
        scratch_shapes=[pltpu.VMEM((n_load, d, _FF_CHUNK), _BF16), pltpu.VMEM((n_load, d, _FF_CHUNK), _BF16),
                        pltpu.VMEM((n_load, _FF_CHUNK, d), _BF16),
                        pltpu.VMEM((tm, d), _F32), pltpu.VMEM((tm, d), _F32)],
        compiler_params=pltpu.CompilerParams(dimension_semantics=("arbitrary",), vmem_limit_bytes=_VMEM_LIMIT_FFN),
        name="ffn_sublayer" if mixer is None else "mixer_out_ffn",
    )(*args)


def _rope(x, cos, sin_lo, sin_hi):
    outs = []
    for j in range(x.shape[1] // _LANES):
        xc = x[:, j * _LANES:(j + 1) * _LANES]
        nxt = pltpu.roll(xc, _LANES - _ROT_DIM // 2, axis=1)
        prv = pltpu.roll(xc, _ROT_DIM // 2, axis=1)
        outs.append(xc * cos + nxt * sin_lo + prv * sin_hi)
    return outs[0] if len(outs) == 1 else jnp.concatenate(outs, axis=1)


def _even_in_kernel(x_ref, mod_ref, gpre_ref, w_ref, cos_ref, slo_ref, shi_ref,
                    q_ref, k_ref, v_ref, xg_ref, xr_ref):
    h = _pre(x_ref[...], mod_ref[0], gpre_ref[...])
    y = _dot(h, w_ref[...])
    nq, nk, nw = q_ref.shape[1], k_ref.shape[1], xg_ref.shape[1]
    cos, slo, shi = cos_ref[...], slo_ref[...], shi_ref[...]
    q_ref[...] = _rope(y[:, :nq], cos, slo, shi)
    k_ref[...] = _rope(y[:, nq:nq + nk], cos, slo, shi)
    v_ref[...] = y[:, nq + nk:nq + 2 * nk]
    xg_ref[...] = y[:, nq + 2 * nk:nq + 2 * nk + nw]
    xr_ref[...] = y[:, nq + 2 * nk + nw:]


def _even_in_call(x, mod4, sub, e, gpre, w_in, rope_tabs, seq_rows):
    n, d = x.shape
    r = mod4.shape[2]
    tm = min(_ROW_TILE, n)
    tiles_per_seq = tab_tiles = seq_rows // tm
    nq = _A_HEADS * _A_HEAD_DIM
    nk = _A_KV_HEADS * _A_HEAD_DIM
    nw = (w_in.shape[-1] - nq - 2 * nk) // 2
    row = lambda i: (i, 0)
    tab = pl.BlockSpec((tm, _LANES), lambda i: (i % tab_tiles, 0))
    widths = [nq, nk, nk, nw, nw]
    return pl.pallas_call(
        _even_in_kernel,
        grid=(n // tm,),
        in_specs=[
            pl.BlockSpec((tm, d), row),
            pl.BlockSpec((None, 1, r, 3 * d), lambda i: (sub, i // tiles_per_seq, 0, 0)),
            pl.BlockSpec((None, 1, d), lambda i: (sub, 0, 0)),
            pl.BlockSpec((None, d, w_in.shape[-1]), lambda i: (e, 0, 0)),
            tab, tab, tab,
        ],
        out_specs=[pl.BlockSpec((tm, c), row) for c in widths],
        out_shape=[jax.ShapeDtypeStruct((n, c), _F32) for c in widths],
        compiler_params=_params("parallel"),
        name="even_in_proj",
    )(x, mod4, gpre, w_in, *rope_tabs)


def _odd_in_kernel(x_ref, mod_ref, gpre_ref, w_ref, q_ref, f_ref, v_ref, g_ref):
    h = _pre(x_ref[...], mod_ref[0], gpre_ref[...])
    y = _dot(h, w_ref[...])
    n = q_ref.shape[1]
    q_ref[...] = y[:, :n]
    f_ref[...] = y[:, n:2 * n]
    v_ref[...] = y[:, 2 * n:3 * n]
    g_ref[...] = y[:, 3 * n:]


def _odd_in_call(x, mod4, sub, o, gpre, w_in, seq_rows):
    n, d = x.shape
    r = mod4.shape[2]
    tm = min(_ROW_TILE, n)
    tiles_per_seq = seq_rows // tm
    nw = w_in.shape[-1] // 4
    row = lambda i: (i, 0)
    return pl.pallas_call(
        _odd_in_kernel,
        grid=(n // tm,),
        in_specs=[
            pl.BlockSpec((tm, d), row),
            pl.BlockSpec((None, 1, r, 3 * d), lambda i: (sub, i // tiles_per_seq, 0, 0)),
            pl.BlockSpec((None, 1, d), lambda i: (sub, 0, 0)),
            pl.BlockSpec((None, d, 4 * nw), lambda i: (o, 0, 0)),
        ],
        out_specs=[pl.BlockSpec((tm, nw), row)] * 4,
        out_shape=[jax.ShapeDtypeStruct((n, nw), _F32)] * 4,
        compiler_params=_params("parallel"),
        name="odd_in_proj",
    )(x, mod4, gpre, w_in)


def _seq_tile(time_tiles, s):
    row = lax.broadcasted_iota(jnp.int32, time_tiles[0].shape, 0)
    out = jnp.zeros(time_tiles[0].shape, time_tiles[0].dtype)
    for t, x in enumerate(time_tiles):
        shift = (t - s) % _SUBLANES
        out = jnp.where(row == t, pltpu.roll(x, shift, axis=0) if shift else x, out)
    return out


def _time_tiles(seq_tiles, n_t):
    row = lax.broadcasted_iota(jnp.int32, seq_tiles[0].shape, 0)
    outs = []
    for t in range(n_t):
        acc = jnp.zeros(seq_tiles[0].shape, seq_tiles[0].dtype)
        for s, x in enumerate(seq_tiles):
            shift = (s - t) % _SUBLANES
            acc = jnp.where(row == s, pltpu.roll(x, shift, axis=0) if shift else x, acc)
        outs.append(acc)
    return outs


def _sink_softmax_pv(s, mask, sink, v):
    s = jnp.where(mask, s, -jnp.inf)
    m = jnp.maximum(jnp.max(s, axis=-1, keepdims=True), sink)
    p = jnp.exp(s - m)
    denom = jnp.sum(p, axis=-1, keepdims=True) + jnp.exp(sink - m)
    return _dot(p, v) / denom


def _swa_prompt_kernel(sink_ref, q_ref, kp_ref, kc_ref, vp_ref, vc_ref, o_ref):
    w = _WINDOW
    hd = _A_HEAD_DIM
    assert _LANES == 2 * hd and _A_GROUP % 2 == 0
    log2e = np.float32(np.log2(np.e))
    scale = np.float32(1.0 / np.sqrt(hd)) * log2e
    low = lax.broadcasted_iota(jnp.int32, (2 * w, _LANES), 1) < hd
    ones_lo = jnp.where(low, 1.0, 0.0).astype(_BF16)
    ones_hi = jnp.where(low, 0.0, 1.0).astype(_BF16)
    low_q = lax.broadcasted_iota(jnp.int32, (w, _LANES), 1) < hd
    row = lax.broadcasted_iota(jnp.int32, (w, 4 * w), 0)
    col = lax.broadcasted_iota(jnp.int32, (w, 4 * w), 1) & (2 * w - 1)
    pairs = range(_A_HEADS // 2)
    kv_of = [(2 * p) // _A_GROUP for p in pairs]

    def score_stage(qb):
        rows = slice(qb * w, (qb + 1) * w)
        if qb == 0:
            k2 = jnp.concatenate([kp_ref[...], kc_ref[0:w, :]], axis=0)
            v2 = jnp.concatenate([vp_ref[...], vc_ref[0:w, :]], axis=0)
            first = jnp.where(pl.program_id(1) > 0, 0, w)
        else:
            k2 = kc_ref[(qb - 1) * w:(qb + 1) * w, :]
            v2 = vc_ref[(qb - 1) * w:(qb + 1) * w, :]
            first = 0
        mask = jnp.logical_and(col > jnp.maximum(row, first - 1), col <= row + w)
        keys, vals = [], []
        for j in range(_A_KV_HEADS):
            own_k = jnp.where(low, k2, 0.0) if j == 0 else jnp.where(low, 0.0, k2)
            own_v = jnp.where(low, v2, 0.0) if j == 0 else jnp.where(low, 0.0, v2)
            oth_k = pltpu.roll(own_k, hd, axis=1)
            oth_v = pltpu.roll(own_v, hd, axis=1)
            lo_k, hi_k = (own_k, oth_k) if j == 0 else (oth_k, own_k)
            lo_v, hi_v = (own_v, oth_v) if j == 0 else (oth_v, own_v)
            keys.append(jnp.concatenate([lo_k, hi_k], axis=0).astype(_BF16))
            vals.append(jnp.concatenate([jnp.concatenate([lo_v.astype(_BF16), ones_lo], axis=1),
                                         jnp.concatenate([hi_v.astype(_BF16), ones_hi], axis=1)], axis=0))
        scores = [_dot_nt(q_ref[rows, p * _LANES:(p + 1) * _LANES] * scale, keys[kv_of[p]]) for p in pairs]
        return scores, mask, vals

    def softmax_stage(scores, mask):
        probs, sink_terms = [], []
        for p in pairs:
            s = jnp.where(mask, scores[p], -jnp.inf)
            halves = []
            for i in range(2):
                sh = s[:, i * 2 * w:(i + 1) * 2 * w]
                sink = sink_ref[2 * p + i] * log2e
                m = jnp.maximum(jnp.max(sh, axis=-1, keepdims=True), sink)
                halves.append((jnp.exp2(sh - m), jnp.exp2(sink - m)))
            probs.append(jnp.concatenate([halves[0][0], halves[1][0]], axis=1).astype(_BF16))
            sink_terms.append(jnp.where(low_q, halves[0][1], halves[1][1]))
        return probs, sink_terms

    def value_stage(qb, probs, sink_terms, vals):
        rows = slice(qb * w, (qb + 1) * w)
        for p in pairs:
            r = _dot(probs[p], vals[kv_of[p]])
            o_ref[rows, p * _LANES:(p + 1) * _LANES] = r[:, :_LANES] / (r[:, _LANES:] + sink_terms[p])

    n_qb = q_ref.shape[0] // w
    scored, soft = {}, {}
    for step in range(n_qb + 2):
        if 0 <= step - 1 < n_qb:
            qb = step - 1
            soft[qb] = softmax_stage(*scored[qb][:2])
        if step < n_qb:
            scored[step] = score_stage(step)
        if 0 <= step - 2 < n_qb:
            qb = step - 2
            value_stage(qb, *soft.pop(qb), scored.pop(qb)[2])


def _swa_prompt_call(q, k, v, sinks, batch):
    n, nq = q.shape
    nk = k.shape[1]
    w = _WINDOW
    qb = _SWA_QBLOCKS
    nb = n // batch // (w * qb)
    cur = lambda b, i: (b * nb + i, 0)
    prev = lambda b, i: ((b * nb + i) * qb - jnp.minimum(i, 1), 0)
    return pl.pallas_call(
        _swa_prompt_kernel,
        grid=(batch, nb),
        in_specs=[
            pl.BlockSpec(memory_space=pltpu.SMEM),
            pl.BlockSpec((qb * w, nq), cur),
            pl.BlockSpec((w, nk), prev), pl.BlockSpec((qb * w, nk), cur),
            pl.BlockSpec((w, nk), prev), pl.BlockSpec((qb * w, nk), cur),
        ],
        out_specs=pl.BlockSpec((qb * w, nq), cur),
        out_shape=jax.ShapeDtypeStruct((n, nq), _F32),
        compiler_params=_params("parallel", "parallel"),
        name="swa_prompt",
    )(sinks, q, k, k, v, v)


def _swa_sample_kernel(sink_ref, q_ref, kn_ref, vn_ref, ck_ref, cv_ref, o_ref, kw_ref, vw_ref, *, t_new):
    p = ck_ref.shape[1]
    scale = np.float32(1.0 / np.sqrt(_A_HEAD_DIM))
    rows = _A_GROUP * _SUBLANES
    t = lax.broadcasted_iota(jnp.int32, (rows, p + _SUBLANES), 0) & (_SUBLANES - 1)
    c = lax.broadcasted_iota(jnp.int32, (rows, p + _SUBLANES), 1)
    mask = jnp.logical_and(c <= t + p, c > t + p - _WINDOW)
    g_of_row = lax.broadcasted_iota(jnp.int32, (rows, 1), 0) >> (_SUBLANES.bit_length() - 1)
    nseq = ck_ref.shape[0]
    q_t = [q_ref[t] for t in range(t_new)]
    kn_t = [kn_ref[t] for t in range(t_new)]
    vn_t = [vn_ref[t] for t in range(t_new)]
    sinks = []
    for j in range(_A_KV_HEADS):
        sink = jnp.zeros((rows, 1), _F32)
        for g in range(_A_GROUP):
            sink = jnp.where(g_of_row == g, sink_ref[j * _A_GROUP + g], sink)
        sinks.append(sink)
    new_k, new_v, qs, keys, vals = [], [], [], [], []
    for s in range(nseq):
        q8, kn8, vn8 = _seq_tile(q_t, s), _seq_tile(kn_t, s), _seq_tile(vn_t, s)
        ck, cv = ck_ref[s], cv_ref[s]
        kw_ref[s, 0:p - t_new, :] = ck[t_new:, :]
        kw_ref[s, p - t_new:p, :] = kn8[0:t_new, :]
        vw_ref[s, 0:p - t_new, :] = cv[t_new:, :]
        vw_ref[s, p - t_new:p, :] = vn8[0:t_new, :]
        for j in range(_A_KV_HEADS):
            ks = slice(j * _A_HEAD_DIM, (j + 1) * _A_HEAD_DIM)
            keys.append(jnp.concatenate([ck[:, ks], kn8[:, ks]], axis=0))
            vals.append(jnp.concatenate([cv[:, ks], vn8[:, ks]], axis=0))
            qs.append(jnp.concatenate(
                [q8[:, (j * _A_GROUP + g) * _A_HEAD_DIM:(j * _A_GROUP + g + 1) * _A_HEAD_DIM]
                 for g in range(_A_GROUP)], axis=0))
    scores = [_dot_nt(qj, k) * scale for qj, k in zip(qs, keys)]
    outs = [_sink_softmax_pv(sc, mask, sinks[i % _A_KV_HEADS], v) for i, (sc, v) in enumerate(zip(scores, vals))]
    per_seq = []
    for s in range(nseq):
        heads = []
        for j in range(_A_KV_HEADS):
            o = outs[s * _A_KV_HEADS + j]
            heads.extend(o[g * _SUBLANES:(g + 1) * _SUBLANES, :] for g in range(_A_GROUP))
        per_seq.append(jnp.concatenate(heads, axis=1))
    for t, tile in enumerate(_time_tiles(per_seq, t_new)):
        o_ref[t] = tile


def _swa_sample_call(q, k_new, v_new, cache_k, cache_v, sinks):
    t_new, nseq, nq = q.shape
    nk = k_new.shape[2]
    p = cache_k.shape[1]
    bs = _SAMPLE_SEQS
    toks = lambda i: (0, i, 0)
    seqs = lambda i: (i, 0, 0)
    return pl.pallas_call(
        functools.partial(_swa_sample_kernel, t_new=t_new),
        grid=(nseq // bs,),
        in_specs=[
            pl.BlockSpec(memory_space=pltpu.SMEM),
            pl.BlockSpec((t_new, bs, nq), toks),
            pl.BlockSpec((t_new, bs, nk), toks), pl.BlockSpec((t_new, bs, nk), toks),
            pl.BlockSpec((bs, p, nk), seqs), pl.BlockSpec((bs, p, nk), seqs),
        ],
        out_specs=[pl.BlockSpec((t_new, bs, nq), toks),
                   pl.BlockSpec((bs, p, nk), seqs), pl.BlockSpec((bs, p, nk), seqs)],
        out_shape=[jax.ShapeDtypeStruct((t_new, nseq, nq), _F32),
                   jax.ShapeDtypeStruct(cache_k.shape, _F32), jax.ShapeDtypeStruct(cache_v.shape, _F32)],
        compiler_params=_params("parallel"),
        name="swa_sample",
    )(sinks, q, k_new, v_new, cache_k, cache_v)


def _softplus(z):
    return jnp.maximum(z, 0.0) + jnp.log1p(jnp.exp(-jnp.abs(z)))


def _gelu_tanh(x):
    return 0.5 * x * (1.0 + jnp.tanh(np.float32(np.sqrt(2.0 / np.pi)) * (x + 0.044715 * (x * x * x))))


def _rg_gates(xc, wa_ref, ba_ref, wx_ref, bx_ref, sp_neg_lam):
    r = _sigmoid(_dot(xc, wa_ref[...]) + ba_ref[...])
    i = _sigmoid(_dot(xc, wx_ref[...]) + bx_ref[...])
    a = jnp.exp((-_RG_C) * r * sp_neg_lam)
    gap = jnp.maximum(1.0 - a * a, 0.0)
    mult = jnp.where(gap > 0.0, gap * lax.rsqrt(gap), 0.0)
    return a, mult * (i * xc)


def _rg_prompt_kernel(xr_ref, xg_ref, cw_ref, cb_ref, wa_ref, ba_ref, wx_ref, bx_ref, lam_ref,
                      o_ref, conv_ref, h_ref, xpad, a_s, b_s):
    t_len, w = xr_ref.shape
    cw = cw_ref.shape[0]
    xpad[0:_SUBLANES, :] = jnp.zeros((_SUBLANES, w), _F32)
    xpad[_SUBLANES:, :] = xr_ref[...]
    sp = _softplus(-lam_ref[...])
    for c in range(t_len // _RG_CHUNK):
        r0 = c * _RG_CHUNK
        xc = cb_ref[...]
        for j in range(cw):
            lo = _SUBLANES + r0 - (cw - 1) + j
            xc = xc + xpad[lo:lo + _RG_CHUNK, :] * cw_ref[j:j + 1, :]
        a, b = _rg_gates(xc, wa_ref, ba_ref, wx_ref, bx_ref, sp)
        a_s[r0:r0 + _RG_CHUNK, :] = a
        b_s[r0:r0 + _RG_CHUNK, :] = b

    row = lax.broadcasted_iota(jnp.int32, (_SUBLANES, w), 0)

    def group(g, h):
        r0 = pl.multiple_of(g * _SUBLANES, _SUBLANES)
        a = a_s[pl.ds(r0, _SUBLANES), :]
        b = b_s[pl.ds(r0, _SUBLANES), :]
        sh = 1
        while sh < _SUBLANES:
            a_prev = jnp.where(row >= sh, pltpu.roll(a, sh, axis=0), 1.0)
            b_prev = jnp.where(row >= sh, pltpu.roll(b, sh, axis=0), 0.0)
            b = a * b_prev + b
            a = a * a_prev
            sh *= 2
        hs = a * h + b
        o_ref[pl.ds(r0, _SUBLANES), :] = _gelu_tanh(xg_ref[pl.ds(r0, _SUBLANES), :]) * hs
        return hs[_SUBLANES - 1:_SUBLANES, :]

    h_last = lax.fori_loop(0, t_len // _SUBLANES, group, jnp.zeros((1, w), _F32), unroll=8)
    h_ref[0] = h_last
    conv_ref[0] = xr_ref[t_len - (cw - 1):t_len, :]


def _rg_prompt_call(xr, xg, conv_w, conv_b, wa_bd, ba, wx_bd, bx, lam, batch):
    n, w = xr.shape
    t_len = n // batch
    cw = conv_w.shape[0]
    seq = lambda b: (b, 0)
    const = lambda b: (0, 0)
    vec = pl.BlockSpec((1, w), const)
    mat = pl.BlockSpec((w, w), const)
    return pl.pallas_call(
        _rg_prompt_kernel,
        grid=(batch,),
        in_specs=[pl.BlockSpec((t_len, w), seq), pl.BlockSpec((t_len, w), seq),
                  pl.BlockSpec((cw, w), const), vec, mat, vec, mat, vec, vec],
        out_specs=[pl.BlockSpec((t_len, w), seq),
                   pl.BlockSpec((1, cw - 1, w), lambda b: (b, 0, 0)),
                   pl.BlockSpec((1, 1, w), lambda b: (b, 0, 0))],
        out_shape=[jax.ShapeDtypeStruct((n, w), _F32),
                   jax.ShapeDtypeStruct((batch, cw - 1, w), _F32),
                   jax.ShapeDtypeStruct((batch, 1, w), _F32)],
        scratch_shapes=[pltpu.VMEM((t_len + _SUBLANES, w), _F32),
                        pltpu.VMEM((t_len, w), _F32), pltpu.VMEM((t_len, w), _F32)],
        compiler_params=_params("parallel"),
        name="rglru_prompt",
    )(xr, xg, conv_w, conv_b, wa_bd, ba, wx_bd, bx, lam)


def _rg_sample_kernel(xr_ref, xg_ref, conv0_ref, h0_ref, cw_ref, cb_ref, wa_ref, ba_ref, wx_ref, bx_ref, lam_ref,
                      o_ref, conv_ref, h_ref):
    cw = cw_ref.shape[0]
    t_new = xr_ref.shape[0]
    sp = _softplus(-lam_ref[...])
    hist = [conv0_ref[j] for j in range(cw - 1)] + [xr_ref[t] for t in range(t_new)]
    h = h0_ref[...]
    for t in range(t_new):
        xc = cb_ref[...]
        for j in range(cw):
            xc = xc + hist[t + j] * cw_ref[j:j + 1, :]
        a, b = _rg_gates(xc, wa_ref, ba_ref, wx_ref, bx_ref, sp)
        h = a * h + b
        o_ref[t] = _gelu_tanh(xg_ref[t]) * h
    h_ref[...] = h
    for j in range(cw - 1):
        conv_ref[j] = hist[t_new + j]


def _rg_sample_call(xr_t, xg_t, conv0_t, h0, conv_w, conv_b, wa_bd, ba, wx_bd, bx, lam):
    t_new, nseq, w = xg_t.shape
    return pl.pallas_call(
        _rg_sample_kernel,
        out_shape=[jax.ShapeDtypeStruct((t_new, nseq, w), _F32), jax.ShapeDtypeStruct(conv0_t.shape, _F32),
                   jax.ShapeDtypeStruct((nseq, w), _F32)],
        compiler_params=pltpu.CompilerParams(vmem_limit_bytes=_VMEM_LIMIT),
        name="rglru_sample",
    )(xr_t, xg_t, conv0_t, h0, conv_w, conv_b, wa_bd, ba, wx_bd, bx, lam)


def _row_bcast(x, r, n):
    return jnp.broadcast_to(x[r:r + 1, :], (n, x.shape[1]))


def _chunk_cumsum(x):
    n_tiles = x.shape[0] // _SUBLANES
    row = lax.broadcasted_iota(jnp.int32, (_SUBLANES, x.shape[1]), 0)
    tiles = []
    carry = None
    for i in range(n_tiles):
        t = x[i * _SUBLANES:(i + 1) * _SUBLANES, :]
        sh = 1
        while sh < _SUBLANES:
            t = t + jnp.where(row >= sh, pltpu.roll(t, sh, axis=0), 0.0)
            sh *= 2
        if carry is not None:
            t = t + carry
        carry = _row_bcast(t, _SUBLANES - 1, _SUBLANES)
        tiles.append(t)
    return tiles[0] if n_tiles == 1 else jnp.concatenate(tiles, axis=0)


def _level_reference(b, m):
    n = b.shape[0]
    if 2 * m >= _SUBLANES:
        pieces = [_row_bcast(b, lo + m - 1, 2 * m) for lo in range(0, n, 2 * m)]
        return pieces[0] if len(pieces) == 1 else jnp.concatenate(pieces, axis=0)
    row = lax.broadcasted_iota(jnp.int32, (_SUBLANES, b.shape[1]), 0)
    tiles = []
    for i in range(n // _SUBLANES):
        t = b[i * _SUBLANES:(i + 1) * _SUBLANES, :]
        ref = None
        for lo in range(0, _SUBLANES, 2 * m):
            piece = _row_bcast(t, lo + m - 1, _SUBLANES)
            ref = piece if ref is None else jnp.where(row >= lo, piece, ref)
        tiles.append(ref)
    return tiles[0] if len(tiles) == 1 else jnp.concatenate(tiles, axis=0)


def _hgrn_gates(fz, lb):
    f = lb + (1.0 - lb) * _sigmoid(fz)
    return jnp.log(f), 1.0 - f


def _hgrn_chunk(q, fz, v, lb, state):
    n, kd = q.shape
    log_f, k = _hgrn_gates(fz, lb)
    b = _chunk_cumsum(log_f)
    b_last = _row_bcast(b, n - 1, n)

    o = _dot_nt(q * jnp.exp(b), state)
    new_state = jnp.exp(b_last[0:1, :]) * state + _dot_tn(v, k * jnp.exp(b_last - b))

    row = lax.broadcasted_iota(jnp.int32, (n, n), 0)
    col = lax.broadcasted_iota(jnp.int32, (n, n), 1)
    upper = lax.broadcasted_iota(jnp.int32, (n, kd), 0)
    scores = jnp.where(row == col, jnp.sum(q * k, axis=-1, keepdims=True), 0.0)
    m = 1
    while m < n:
        e = jnp.exp(-jnp.abs(b - _level_reference(b, m)))
        z = jnp.where((upper & m) != 0, q, k) * e
        pair = jnp.logical_and((row & m) != 0, (row ^ m) >> (m.bit_length() - 1) == col >> (m.bit_length() - 1))
        scores = scores + jnp.where(pair, _dot_nt(z, z), 0.0)
        m *= 2
    return o + _dot(scores, v), new_state


def _lower_bound(logits, layer):
    m = jnp.max(logits, axis=0, keepdims=True)
    e = jnp.exp(logits - m)
    return jnp.sum(e[1:layer + 1, :], axis=0, keepdims=True) / jnp.sum(e, axis=0, keepdims=True)


def _hgrn_out(o, g, gnorm):
    return _rms(o, gnorm) * _silu(g)


def _hgrn_prompt_kernel(q_ref, f_ref, v_ref, g_ref, lbl_ref, gn_ref, o_ref, s_ref,
                        st_scr, st0_scr, *, layer):
    rows_blk, width = q_ref.shape
    kd = width // _HG_HEADS
    n_chunks = rows_blk // _HG_CHUNK
    heads = [slice(h * kd, (h + 1) * kd) for h in range(_HG_HEADS)]
    lb = _lower_bound(lbl_ref[...], layer)
    gn = gn_ref[...]

    @pl.when(pl.program_id(2) == 0)
    def _():
        st_scr[...] = jnp.zeros(st_scr.shape, _F32)

    st0_scr[...] = st_scr[...]

    def chunk_rows(c):
        return pl.ds(pl.multiple_of(c * _HG_CHUNK, _HG_CHUNK), _HG_CHUNK)

    def prepare(c):
        rows = chunk_rows(c)
        log_f, k = _hgrn_gates(f_ref[rows, :], lb)
        b = _chunk_cumsum(log_f)
        b_end = b[_HG_CHUNK - 1:_HG_CHUNK, :]
        dec = jnp.exp(b_end)
        e_b = jnp.exp(b)
        k_start = k / e_b
        q, v = q_ref[rows, :], v_ref[rows, :]
        qk = q * k
        own = jnp.concatenate([jnp.broadcast_to(jnp.sum(qk[:, l], axis=-1, keepdims=True), (_HG_CHUNK, kd))
                               for l in heads], axis=1) * v
        ops = ((q * e_b).astype(_BF16), k_start.astype(_BF16), (k_start * dec).astype(_BF16),
               v.astype(_BF16), dec, own)
        return ops, b_end

    row = lax.broadcasted_iota(jnp.int32, (_HG_CHUNK, _HG_CHUNK), 0)
    col = lax.broadcasted_iota(jnp.int32, (_HG_CHUNK, _HG_CHUNK), 1)

    def finish(c, h, o):
        rows = chunk_rows(c)
        o_ref[rows, heads[h]] = _hgrn_out(o, g_ref[rows, heads[h]], gn)

    def contract(c, ops):
        qs, ks, ke, vb, dec, own = ops
        states = [st_scr[h] for h in range(_HG_HEADS)]
        scores = [_dot_nt(qs[:, l], ks[:, l]) for l in heads]
        carried = [_dot_nt(qs[:, l], st) for l, st in zip(heads, states)]
        incs = [_dot_tn(vb[:, l], ke[:, l]) for l in heads]
        outs = [_dot(jnp.where(row > col, s, 0.0), vb[:, l]) for s, l in zip(scores, heads)]
        for h, l in enumerate(heads):
            st_scr[h] = states[h] * dec[:, l] + incs[h]
        for h, l in enumerate(heads):
            finish(c, h, outs[h] + carried[h] + own[:, l])

    def step(c, carry):
        ops, min_b = carry
        contract(c, ops)
        ops, b_end = prepare(c + 1)
        return ops, jnp.minimum(min_b, b_end)

    last_ops, min_b = lax.fori_loop(0, n_chunks - 1, step, prepare(0), unroll=True)
    contract(n_chunks - 1, last_ops)

    @pl.when(jnp.min(min_b) <= -_HG_SAFE_LOG_DECAY)
    def _():
        st_scr[...] = st0_scr[...]

        def chunk(c, carry):
            rows = chunk_rows(c)
            for h, l in enumerate(heads):
                o, st_scr[h] = _hgrn_chunk(q_ref[rows, l], f_ref[rows, l], v_ref[rows, l], lb[:, l], st_scr[h])
                finish(c, h, o)
            return carry
        lax.fori_loop(0, n_chunks, chunk, 0)

    @pl.when(pl.program_id(2) == pl.num_programs(2) - 1)
    def _():
        for h in range(_HG_HEADS):
            s_ref[0, h] = st_scr[h].T


def _hgrn_prompt_call(q, fz, v, g, lb_logits, gnorm, batch, layer):
    n, width = q.shape
    t_len = n // batch
    kd = width // _C_HEADS
    gw = _HG_HEADS * kd
    rows_blk = min(_HG_ROWS, t_len)
    nt = t_len // rows_blk
    blk = pl.BlockSpec((rows_blk, gw), lambda b, h, t: (b * nt + t, h))
    return pl.pallas_call(
        functools.partial(_hgrn_prompt_kernel, layer=layer),
        grid=(batch, _C_HEADS // _HG_HEADS, nt),
        in_specs=[blk, blk, blk, blk,
                  pl.BlockSpec((lb_logits.shape[0], gw), lambda b, h, t: (0, h)),
                  pl.BlockSpec((1, kd), lambda b, h, t: (0, 0))],
        out_specs=[blk, pl.BlockSpec((1, _HG_HEADS, kd, kd), lambda b, h, t: (b, h, 0, 0))],
        out_shape=[jax.ShapeDtypeStruct((n, width), _F32),
                   jax.ShapeDtypeStruct((batch, _C_HEADS, kd, kd), _F32)],
        scratch_shapes=[pltpu.VMEM((_HG_HEADS, kd, kd), _F32)] * 2,
        compiler_params=_params("parallel", "parallel", "arbitrary"),
        name="hgrn2_prompt",
    )(q, fz, v, g, lb_logits, gnorm)


def _hgrn_sample_kernel(q_ref, f_ref, v_ref, g_ref, lbl_ref, gn_ref, s0_ref, o_ref, s_ref, *, layer):
    t_new, nseq, width = q_ref.shape
    n_heads = s0_ref.shape[1]
    kd = width // n_heads
    lanes = [slice(h * kd, (h + 1) * kd) for h in range(n_heads)]
    lb = _lower_bound(lbl_ref[...], layer)
    gn = gn_ref[...]
    q = [q_ref[t] for t in range(t_new)]
    v = [v_ref[t] for t in range(t_new)]
    keys, b = [], []
    for t in range(t_new):
        log_f, k = _hgrn_gates(f_ref[t], lb)
        keys.append(k)
        b.append(log_f if t == 0 else b[-1] + log_f)

    def per_head_sum(w):
        return jnp.concatenate([jnp.broadcast_to(jnp.sum(w[:, l], axis=-1, keepdims=True), (nseq, kd))
                                for l in lanes], axis=1)

    within = []
    for t in range(t_new):
        acc = per_head_sum(q[t] * keys[t]) * v[t]
        for s in range(t):
            acc = acc + per_head_sum(q[t] * keys[s] * jnp.exp(b[t] - b[s])) * v[s]
        within.append(acc)

    q_in = [q[t] * jnp.exp(b[t]) for t in range(t_new)]
    k_end = [keys[t] * jnp.exp(b[-1] - b[t]) for t in range(t_new)]
    decay = jnp.exp(b[-1])
    pairs = [(s, h) for h in range(n_heads) for s in range(nseq)]
    lhs = [_seq_tile([x[:, lanes[h]] for x in q_in], s) for s, h in pairs]
    k_seq = [_seq_tile([x[:, lanes[h]] for x in k_end], s) for s, h in pairs]
    v_seq = [_seq_tile([x[:, lanes[h]] for x in v], s) for s, h in pairs]
    carried = [_dot(x, s0_ref[s, h]) for x, (s, h) in zip(lhs, pairs)]
    incs = [_dot_tn(ks, vs) for ks, vs in zip(k_seq, v_seq)]
    carried_t = []
    for h in range(n_heads):
        dec_cols = jnp.concatenate([decay[:, lanes[h]], jnp.zeros((kd - nseq, kd), _F32)], axis=0).T
        for s in range(nseq):
            i = h * nseq + s
            s_ref[s, h] = jnp.broadcast_to(dec_cols[:, s:s + 1], (kd, kd)) * s0_ref[s, h] + incs[i]
        carried_t.append(_time_tiles(carried[h * nseq:(h + 1) * nseq], t_new))
    for t in range(t_new):
        o = within[t] + jnp.concatenate([carried_t[h][t] for h in range(n_heads)], axis=1)
        g = g_ref[t]
        o_ref[t] = jnp.concatenate([_hgrn_out(o[:, l], g[:, l], gn) for l in lanes], axis=1)


def _hgrn_sample_call(q, fz, v, g, lb_logits, gnorm, s0, layer):
    t_new, nseq, width = q.shape
    kd = width // _C_HEADS
    gw = _HG_HEADS * kd
    bs = _SAMPLE_SEQS
    blk = pl.BlockSpec((t_new, bs, gw), lambda i, h: (0, i, h))
    st = pl.BlockSpec((bs, _HG_HEADS, kd, kd), lambda i, h: (i, h, 0, 0))
    return pl.pallas_call(
        functools.partial(_hgrn_sample_kernel, layer=layer),
        grid=(nseq // bs, _C_HEADS // _HG_HEADS),
        in_specs=[blk, blk, blk, blk,
                  pl.BlockSpec((lb_logits.shape[0], gw), lambda i, h: (0, h)),
                  pl.BlockSpec((1, kd), lambda i, h: (0, 0)), st],
        out_specs=[blk, st],
        out_shape=[jax.ShapeDtypeStruct((t_new, nseq, width), _F32), jax.ShapeDtypeStruct(s0.shape, _F32)],
        compiler_params=_params("parallel", "parallel"),
        name="hgrn2_sample",
    )(q, fz, v, g, lb_logits, gnorm, s0)


def _rope_tables(pos):
    half = _ROT_DIM // 2
    inv_freq = _ROPE_THETA ** (-jnp.arange(0, _ROT_DIM, 2, dtype=_F32) / _ROT_DIM)
    ang = pos.astype(_F32)[:, None] * inv_freq[None, :]
    cos, sin = jnp.cos(ang), jnp.sin(ang)
    ones = jnp.ones((pos.shape[0], _A_HEAD_DIM - _ROT_DIM), _F32)
    zeros = jnp.zeros((pos.shape[0], _A_HEAD_DIM - half), _F32)
    zeros_h = jnp.zeros((pos.shape[0], half), _F32)
    reps = _LANES // _A_HEAD_DIM
    cos_t = jnp.tile(jnp.concatenate([cos, cos, ones], axis=1), (1, reps))
    sin_lo = jnp.tile(jnp.concatenate([-sin, zeros], axis=1), (1, reps))
    sin_hi = jnp.tile(jnp.concatenate([zeros_h, sin, ones * 0.0], axis=1), (1, reps))
    return cos_t, sin_lo, sin_hi


def _block_diag(w):
    nb, bd, _ = w.shape
    eye = jnp.eye(nb, dtype=w.dtype)
    return (w[:, :, None, :] * eye[:, None, :, None]).reshape(nb * bd, nb * bd)


def kernel(x_prompt, x_sample, c_prompt, c_sample, cache_k_win, cache_v_win, state_conv_rglru,
           state_h_rglru, state_s_hgrn, norm_pre, norm_post, ada_w, ada_b, ffn1_w_in, ffn1_w_out,
           ffn2_w_in, ffn2_w_out, even_w_in, even_w_out, attn_sinks, rg_conv_w, rg_conv_b, rg_wa,
           rg_ba, rg_wx, rg_bx, rg_lambda, odd_w_in, odd_w_out, hgrn_lb_logits, hgrn_gnorm):
    bp, tp, d = x_prompt.shape
    bs, ts, _ = x_sample.shape
    depth = norm_pre.shape[0]
    n_sub = depth * _N_SUB
    nk = _A_KV_HEADS * _A_HEAD_DIM
    win = cache_k_win.shape[2]
    cw = rg_conv_w.shape[1]
    bw = rg_conv_w.shape[2]
    kd = state_s_hgrn.shape[3]

    cast = lambda w: w.astype(_BF16)
    ffn1_in, ffn1_out, ffn2_in, ffn2_out = ffn1_w_in, ffn1_w_out, ffn2_w_in, ffn2_w_out
    ev_in, ev_out, od_in, od_out = even_w_in, even_w_out, odd_w_in, odd_w_out
    gpre = norm_pre.reshape(n_sub, 1, d)
    gpost = norm_post.reshape(n_sub, 1, d)

    mod_p, mod_s = _ada_call(jnp.concatenate([c_prompt, c_sample], axis=0), bp,
                             ada_w.reshape(n_sub, d, 3 * d), ada_b.reshape(n_sub, 1, 3 * d))
    mod_p = mod_p.reshape(n_sub, bp, 1, 3 * d)
    mod_s = mod_s.reshape(n_sub, 1, bs, 3 * d)

    tabs_p = _rope_tables(jnp.arange(tp, dtype=jnp.int32))
    tabs_s = tuple(jnp.repeat(t, bs, axis=0) for t in _rope_tables(_PAST_LEN + jnp.arange(ts, dtype=jnp.int32)))

    time_major = lambda a: a.reshape(ts, bs, a.shape[-1])
    xp = x_prompt.reshape(bp * tp, d)
    xs = x_sample.transpose(1, 0, 2).reshape(ts * bs, d)
    groups = {"p": dict(mod=mod_p, seq_rows=tp, tabs=tabs_p), "s": dict(mod=mod_s, seq_rows=ts * bs, tabs=tabs_s)}
    outs = {g: dict(k=[], v=[], conv=[], h=[], s=[]) for g in groups}

    for l in range(depth):
        s0, s1, s2 = l * _N_SUB, l * _N_SUB + 1, l * _N_SUB + 2
        xp, xs = _ffn_call(xp, xs, mod_p, mod_s, s0, l, gpre, ffn1_in, ffn1_out, gpost, 0.5, tp)
        acts = {}
        for name, x in (("p", xp), ("s", xs)):
            grp = groups[name]
            mod4, tps = grp["mod"], grp["seq_rows"]
            if l % 2 == 0:
                e = l // 2
                q, k, v, xg, xr = _even_in_call(x, mod4, s1, e, gpre, ev_in, grp["tabs"], tps)
                wa_bd, wx_bd = cast(_block_diag(rg_wa[e])), cast(_block_diag(rg_wx[e]))
                vecs = [a[e].reshape(1, bw) for a in (rg_conv_b, rg_ba, rg_bx, rg_lambda)]
                if name == "p":
                    o_a = _swa_prompt_call(q, k, v, attn_sinks[e], bp)
                    o_b, conv, h_last = _rg_prompt_call(xr, xg, rg_conv_w[e], vecs[0], wa_bd, vecs[1], wx_bd,
                                                        vecs[2], vecs[3], bp)
                    last = lambda a: a.reshape(bp, tp, nk)[:, tp - win:].reshape(bp, win, _A_KV_HEADS, _A_HEAD_DIM)
                    outs[name]["k"].append(last(k))
                    outs[name]["v"].append(last(v))
                    outs[name]["h"].append(h_last.reshape(bp, bw))
                else:
                    o_a, kw, vw = _swa_sample_call(time_major(q), time_major(k), time_major(v),
                                                   cache_k_win[e].reshape(bs, win, nk),
                                                   cache_v_win[e].reshape(bs, win, nk), attn_sinks[e])
                    o_b, conv_t, h_last = _rg_sample_call(time_major(xr), time_major(xg),
                                                          state_conv_rglru[e].transpose(1, 0, 2), state_h_rglru[e],
                                                          rg_conv_w[e], vecs[0], wa_bd, vecs[1], wx_bd, vecs[2], vecs[3])
                    o_a, o_b = o_a.reshape(ts * bs, -1), o_b.reshape(ts * bs, bw)
                    conv = conv_t.transpose(1, 0, 2)
                    outs[name]["k"].append(kw.reshape(bs, win, _A_KV_HEADS, _A_HEAD_DIM))
                    outs[name]["v"].append(vw.reshape(bs, win, _A_KV_HEADS, _A_HEAD_DIM))
                    outs[name]["h"].append(h_last)
                outs[name]["conv"].append(conv)
                acts[name] = [o_a, o_b]
                widx, w_mix = e, ev_out
            else:
                o = l // 2
                q, fz, v, g = _odd_in_call(x, mod4, s1, o, gpre, od_in, tps)
                gn = hgrn_gnorm[o].reshape(1, kd)
                if name == "p":
                    y, s_last = _hgrn_prompt_call(q, fz, v, g, hgrn_lb_logits, gn, bp, l)
                else:
                    y, s_last = _hgrn_sample_call(time_major(q), time_major(fz), time_major(v), time_major(g),
                                                  hgrn_lb_logits, gn, state_s_hgrn[o], l)
                    y = y.reshape(ts * bs, -1)
                outs[name]["s"].append(s_last)
                acts[name] = [y]
                widx, w_mix = o, od_out
        xp, xs = _ffn_call(xp, xs, mod_p, mod_s, s2, l, gpre, ffn2_in, ffn2_out, gpost, 0.5, tp,
                           (s1, widx, w_mix, acts["p"], acts["s"]))

    ys = {"p": xp.reshape(bp, tp, d), "s": xs.reshape(ts, bs, d).transpose(1, 0, 2)}
    res = []
    for name in ("p", "s"):
        o = outs[name]
        res.append((jnp.stack(o["k"]), jnp.stack(o["v"]), jnp.stack(o["conv"]), jnp.stack(o["h"]), jnp.stack(o["s"])))
    return (ys["p"], ys["s"]) + res[0] + res[1]
```

```python
import functools

import jax
import jax.numpy as jnp
import numpy as np
from jax import lax
from jax.experimental import pallas as pl
from jax.experimental.pallas import tpu as pltpu

_F32 = jnp.float32
_BF16 = jnp.bfloat16

_EPS = 1e-6
_A_HEADS = 8
_A_KV_HEADS = 2
_A_HEAD_DIM = 64
_A_GROUP = _A_HEADS // _A_KV_HEADS
_WINDOW = 128
_ROPE_THETA = 500000.0
_ROT_DIM = _A_HEAD_DIM // 4
_RG_C = 8.0
_C_HEADS = 8
_PAST_LEN = 16384
_N_SUB = 3

_LANES = 128
_SUBLANES = 8
_VMEM_BYTES = 64 * 1024 * 1024
_VMEM_LIMIT = _VMEM_BYTES * 3 // 4
_VMEM_LIMIT_FFN = _VMEM_BYTES * 29 // 32

_ROW_TILE = 512
_FF_CHUNK = 256
_RG_CHUNK = 256
_HG_CHUNK = 64
_HG_HEADS = 4
_HG_ROWS = 1024
_HG_SAFE_LOG_DECAY = 80.0
_SAMPLE_SEQS = 8
_SWA_QBLOCKS = 4


def _dot(a, b):
    return jnp.dot(a.astype(_BF16), b.astype(_BF16), preferred_element_type=_F32)


def _dot_nt(a, b):
    return lax.dot_general(a.astype(_BF16), b.astype(_BF16), (((1,), (1,)), ((), ())),
                           preferred_element_type=_F32)


def _dot_tn(a, b):
    return lax.dot_general(a.astype(_BF16), b.astype(_BF16), (((0,), (0,)), ((), ())),
                           preferred_element_type=_F32)


def _sigmoid(x):
    return 1.0 / (1.0 + jnp.exp(-x))


def _silu(x):
    return x * _sigmoid(x)


def _rms(x, gain):
    inv = lax.rsqrt(jnp.mean(x * x, axis=-1, keepdims=True) + _EPS)
    return x * inv * gain


def _per_seq(a, r):
    n = a.shape[0]
    return a if r in (1, n) else a.reshape(n // r, r, a.shape[1])


def _pre(x, mod, gain):
    n, d = x.shape
    h = _rms(_per_seq(x, mod.shape[0]), gain * (1.0 + mod[:, d:2 * d])) + mod[:, :d]
    return h.reshape(n, d)


def _post(x, y, mod, gain, res_w):
    n, d = x.shape
    r = mod.shape[0]
    out = _per_seq(x, r) + _rms(_per_seq(y, r), (res_w * (1.0 + mod[:, 2 * d:])) * gain)
    return out.reshape(n, d)


def _params(*sem):
    return pltpu.CompilerParams(dimension_semantics=sem, vmem_limit_bytes=_VMEM_LIMIT)


def _ada_kernel(c_ref, w_ref, b_ref, op_ref, os_ref):
    mod = _dot(_silu(c_ref[...]), w_ref[...]) + b_ref[...]
    n_p = op_ref.shape[0]
    op_ref[...] = mod[:n_p, :]
    os_ref[...] = mod[n_p:, :]


def _ada_call(c_all, n_prompt, ada_w, ada_b):
    m, d = c_all.shape
    assert n_prompt % _SUBLANES == 0
    n_sub = ada_w.shape[0]
    n = ada_w.shape[-1]
    tn = n // 2
    out = lambda rows: (pl.BlockSpec((None, rows, tn), lambda s, j: (s, 0, j)),
                        jax.ShapeDtypeStruct((n_sub, rows, n), _F32))
    (spec_p, shape_p), (spec_s, shape_s) = out(n_prompt), out(m - n_prompt)
    return pl.pallas_call(
        _ada_kernel,
        grid=(n_sub, n // tn),
        in_specs=[
            pl.BlockSpec((m, d), lambda s, j: (0, 0)),
            pl.BlockSpec((None, d, tn), lambda s, j: (s, 0, j)),
            pl.BlockSpec((None, 1, tn), lambda s, j: (s, 0, j)),
        ],
        out_specs=[spec_p, spec_s],
        out_shape=[shape_p, shape_s],
        compiler_params=_params("parallel", "parallel"),
        name="ada_mod",
    )(c_all, ada_w, ada_b)


def _ffn_kernel(xp_ref, xs_ref, modp_ref, mods_ref, gpre_ref, wg_ref, wu_ref, wout_ref, gpost_ref, *refs,
                res_w, n_load, n_prompt, n_acts):
    op_ref, os_ref, wg_bf, wu_bf, wout_bf, x0_scr, acc_scr = refs[-7:]
    i = pl.program_id(0)
    acts = refs[4:4 + 2 * n_acts] if n_acts else ()

    def mixed(x_ref, mmod_ref, act_refs):
        x = x_ref[...]
        if n_acts:
            mgpost_ref, mw_ref = refs[2:4]
            y = None
            off = 0
            for a_ref in act_refs:
                k = a_ref.shape[1]
                t = _dot(a_ref[...], mw_ref[off:off + k, :])
                y = t if y is None else y + t
                off += k
            x = _post(x, y, mmod_ref[0], mgpost_ref[...], 1.0)
        return x

    def chunk(h, j):
        return _dot(_silu(_dot(h, wg_bf[j])) * _dot(h, wu_bf[j]), wout_bf[j])

    def row_tile(x_ref, mod_ref, mmod_ref, act_refs, o_ref):
        x = mixed(x_ref, mmod_ref, act_refs)
        mod = mod_ref[0]
        h = _pre(x, mod, gpre_ref[...]).astype(_BF16)
        acc = jnp.zeros(x.shape, _F32)
        for j in range(n_load):
            acc = acc + chunk(h, j)
        o_ref[...] = _post(x, acc, mod, gpost_ref[...], res_w)

    @pl.when(i < n_load)
    def _():
        @pl.when(i == 0)
        def _():
            x0_scr[...] = mixed(xp_ref, refs[0] if n_acts else None, acts[:n_acts])
            acc_scr[...] = jnp.zeros(acc_scr.shape, _F32)

        mod = modp_ref[0]
        h0 = _pre(x0_scr[...], mod, gpre_ref[...]).astype(_BF16)

        @pl.when(i > 0)
        def _():
            acc_scr[...] += chunk(h0, i - 1)

        wg_bf[i] = wg_ref[...].astype(_BF16)
        wu_bf[i] = wu_ref[...].astype(_BF16)
        wout_bf[i] = wout_ref[...].astype(_BF16)

        @pl.when(i == n_load - 1)
        def _():
            acc = acc_scr[...] + chunk(h0, n_load - 1)
            op_ref[...] = _post(x0_scr[...], acc, mod, gpost_ref[...], res_w)

    @pl.when(jnp.logical_and(i >= n_load, i < n_load + n_prompt - 1))
    def _():
        row_tile(xp_ref, modp_ref, refs[0] if n_acts else None, acts[:n_acts], op_ref)

    @pl.when(i == n_load + n_prompt - 1)
    def _():
        row_tile(xs_ref, mods_ref, refs[1] if n_acts else None, acts[n_acts:], os_ref)


def _ffn_call(xp, xs, modp, mods, sub, layer, gpre, w_in, w_out, gpost, res_w, seq_rows, mixer=None):
    n, d = xp.shape
    tm = xs.shape[0]
    assert n % tm == 0 and seq_rows % tm == 0 and mods.shape[2] * (tm // mods.shape[2]) == tm
    dff = w_out.shape[1]
    n_load = dff // _FF_CHUNK
    n_prompt = n // tm
    tiles_per_seq = seq_rows // tm
    tile = lambda i: jnp.clip(i - n_load + 1, 0, n_prompt - 1)
    rows_p = pl.BlockSpec((tm, d), lambda i: (tile(i), 0))
    once = dict(pipeline_mode=pl.Buffered(1))
    rows_s = pl.BlockSpec((tm, d), lambda i: (0, 0), **once)
    modp_spec = lambda s: pl.BlockSpec((None, 1, 1, 3 * d), lambda i: (s, tile(i) // tiles_per_seq, 0, 0))
    mods_spec = lambda s: pl.BlockSpec((None, 1, mods.shape[2], 3 * d), lambda i: (s, 0, 0, 0), **once)
    gain_spec = lambda s: pl.BlockSpec((None, 1, d), lambda i: (s, 0, 0))
    chunk = lambda i: jnp.minimum(i, n_load - 1)
    in_specs = [
        rows_p, rows_s, modp_spec(sub), mods_spec(sub), gain_spec(sub),
        pl.BlockSpec((None, d, _FF_CHUNK), lambda i: (layer, 0, chunk(i))),
        pl.BlockSpec((None, d, _FF_CHUNK), lambda i: (layer, 0, n_load + chunk(i))),
        pl.BlockSpec((None, _FF_CHUNK, d), lambda i: (layer, chunk(i), 0)),
        gain_spec(sub),
    ]
    args = [xp, xs, modp, mods, gpre, w_in, w_in, w_out, gpost]
    n_acts = 0
    if mixer is not None:
        msub, widx, mw, acts_p, acts_s = mixer
        n_acts = len(acts_p)
        in_specs += [modp_spec(msub), mods_spec(msub), gain_spec(msub),
                     pl.BlockSpec((None, mw.shape[1], d), lambda i: (widx, 0, 0), pipeline_mode=pl.Buffered(1))]
        in_specs += [pl.BlockSpec((tm, a.shape[1]), lambda i: (tile(i), 0)) for a in acts_p]
        in_specs += [pl.BlockSpec((tm, a.shape[1]), lambda i: (0, 0), **once) for a in acts_s]
        args += [modp, mods, gpost, mw] + list(acts_p) + list(acts_s)
    return pl.pallas_call(
        functools.partial(_ffn_kernel, res_w=res_w, n_load=n_load, n_prompt=n_prompt, n_acts=n_acts),
        grid=(n_load + n_prompt,),
        in_specs=in_specs,
        out_specs=[rows_p, pl.BlockSpec((tm, d), lambda i: (0, 0))],
        out_shape=[jax.ShapeDtypeStruct((n, d), _F32), jax.ShapeDtypeStruct((tm, d), _F32)],
        scratch_shapes=[pltpu.VMEM((n_load, d, _FF_CHUNK), _BF16), pltpu.VMEM((n_load, d, _FF_CHUNK), _BF16),
                        pltpu.VMEM((n_load, _FF_CHUNK, d), _BF16),
                        pltpu.VMEM((tm, d), _F32), pltpu.VMEM((tm, d), _F32)],
        compiler_params=pltpu.CompilerParams(dimension_semantics=("arbitrary",), vmem_limit_bytes=_VMEM_LIMIT_FFN),
        name="ffn_sublayer" if mixer is None else "mixer_out_ffn",
    )(*args)


def _rope(x, cos, sin_lo, sin_hi):
    outs = []
    for j in range(x.shape[1] // _LANES):
        xc = x[:, j * _LANES:(j + 1) * _LANES]
        nxt = pltpu.roll(xc, _LANES - _ROT_DIM // 2, axis=1)
        prv = pltpu.roll(xc, _ROT_DIM // 2, axis=1)
        outs.append(xc * cos + nxt * sin_lo + prv * sin_hi)
    return outs[0] if len(outs) == 1 else jnp.concatenate(outs, axis=1)


def _even_in_kernel(x_ref, mod_ref, gpre_ref, w_ref, cos_ref, slo_ref, shi_ref,
                    q_ref, k_ref, v_ref, xg_ref, xr_ref):
    h = _pre(x_ref[...], mod_ref[0], gpre_ref[...])
    y = _dot(h, w_ref[...])
    nq, nk, nw = q_ref.shape[1], k_ref.shape[1], xg_ref.shape[1]
    cos, slo, shi = cos_ref[...], slo_ref[...], shi_ref[...]
    q_ref[...] = _rope(y[:, :nq], cos, slo, shi)
    k_ref[...] = _rope(y[:, nq:nq + nk], cos, slo, shi)
    v_ref[...] = y[:, nq + nk:nq + 2 * nk]
    xg_ref[...] = y[:, nq + 2 * nk:nq + 2 * nk + nw]
    xr_ref[...] = y[:, nq + 2 * nk + nw:]


def _even_in_call(x, mod4, sub, e, gpre, w_in, rope_tabs, seq_rows):
    n, d = x.shape
    r = mod4.shape[2]
    tm = min(_ROW_TILE, n)
    tiles_per_seq = tab_tiles = seq_rows // tm
    nq = _A_HEADS * _A_HEAD_DIM
    nk = _A_KV_HEADS * _A_HEAD_DIM
    nw = (w_in.shape[-1] - nq - 2 * nk) // 2
    row = lambda i: (i, 0)
    tab = pl.BlockSpec((tm, _LANES), lambda i: (i % tab_tiles, 0))
    widths = [nq, nk, nk, nw, nw]
    return pl.pallas_call(
        _even_in_kernel,
        grid=(n // tm,),
        in_specs=[
            pl.BlockSpec((tm, d), row),
            pl.BlockSpec((None, 1, r, 3 * d), lambda i: (sub, i // tiles_per_seq, 0, 0)),
            pl.BlockSpec((None, 1, d), lambda i: (sub, 0, 0)),
            pl.BlockSpec((None, d, w_in.shape[-1]), lambda i: (e, 0, 0)),
            tab, tab, tab,
        ],
        out_specs=[pl.BlockSpec((tm, c), row) for c in widths],
        out_shape=[jax.ShapeDtypeStruct((n, c), _F32) for c in widths],
        compiler_params=_params("parallel"),
        name="even_in_proj",
    )(x, mod4, gpre, w_in, *rope_tabs)


def _odd_in_kernel(x_ref, mod_ref, gpre_ref, w_ref, q_ref, f_ref, v_ref, g_ref):
    h = _pre(x_ref[...], mod_ref[0], gpre_ref[...])
    y = _dot(h, w_ref[...])
    n = q_ref.shape[1]
    q_ref[...] = y[:, :n]
    f_ref[...] = y[:, n:2 * n]
    v_ref[...] = y[:, 2 * n:3 * n]
    g_ref[...] = y[:, 3 * n:]


def _odd_in_call(x, mod4, sub, o, gpre, w_in, seq_rows):
    n, d = x.shape
    r = mod4.shape[2]
    tm = min(_ROW_TILE, n)
    tiles_per_seq = seq_rows // tm
    nw = w_in.shape[-1] // 4
    row = lambda i: (i, 0)
    return pl.pallas_call(
        _odd_in_kernel,
        grid=(n // tm,),
        in_specs=[
            pl.BlockSpec((tm, d), row),
            pl.BlockSpec((None, 1, r, 3 * d), lambda i: (sub, i // tiles_per_seq, 0, 0)),
            pl.BlockSpec((None, 1, d), lambda i: (sub, 0, 0)),
            pl.BlockSpec((None, d, 4 * nw), lambda i: (o, 0, 0)),
        ],
        out_specs=[pl.BlockSpec((tm, nw), row)] * 4,
        out_shape=[jax.ShapeDtypeStruct((n, nw), _F32)] * 4,
        compiler_params=_params("parallel"),
        name="odd_in_proj",
    )(x, mod4, gpre, w_in)


def _seq_tile(time_tiles, s):
    row = lax.broadcasted_iota(jnp.int32, time_tiles[0].shape, 0)
    out = jnp.zeros(time_tiles[0].shape, time_tiles[0].dtype)
    for t, x in enumerate(time_tiles):
        shift = (t - s) % _SUBLANES
        out = jnp.where(row == t, pltpu.roll(x, shift, axis=0) if shift else x, out)
    return out


def _time_tiles(seq_tiles, n_t):
    row = lax.broadcasted_iota(jnp.int32, seq_tiles[0].shape, 0)
    outs = []
    for t in range(n_t):
        acc = jnp.zeros(seq_tiles[0].shape, seq_tiles[0].dtype)
        for s, x in enumerate(seq_tiles):
            shift = (s - t) % _SUBLANES
            acc = jnp.where(row == s, pltpu.roll(x, shift, axis=0) if shift else x, acc)
        outs.append(acc)
    return outs


def _sink_softmax_pv(s, mask, sink, v):
    s = jnp.where(mask, s, -jnp.inf)
    m = jnp.maximum(jnp.max(s, axis=-1, keepdims=True), sink)
    p = jnp.exp(s - m)
    denom = jnp.sum(p, axis=-1, keepdims=True) + jnp.exp(sink - m)
    return _dot(p, v) / denom


def _swa_prompt_kernel(sink_ref, q_ref, kp_ref, kc_ref, vp_ref, vc_ref, o_ref):
    w = _WINDOW
    hd = _A_HEAD_DIM
    assert _LANES == 2 * hd and _A_GROUP % 2 == 0
    log2e = np.float32(np.log2(np.e))
    scale = np.float32(1.0 / np.sqrt(hd)) * log2e
    low = lax.broadcasted_iota(jnp.int32, (2 * w, _LANES), 1) < hd
    ones_lo = jnp.where(low, 1.0, 0.0).astype(_BF16)
    ones_hi = jnp.where(low, 0.0, 1.0).astype(_BF16)
    low_q = lax.broadcasted_iota(jnp.int32, (w, _LANES), 1) < hd
    row = lax.broadcasted_iota(jnp.int32, (w, 4 * w), 0)
    col = lax.broadcasted_iota(jnp.int32, (w, 4 * w), 1) & (2 * w - 1)
    pairs = range(_A_HEADS // 2)
    kv_of = [(2 * p) // _A_GROUP for p in pairs]

    def score_stage(qb):
        rows = slice(qb * w, (qb + 1) * w)
        if qb == 0:
            k2 = jnp.concatenate([kp_ref[...], kc_ref[0:w, :]], axis=0)
            v2 = jnp.concatenate([vp_ref[...], vc_ref[0:w, :]], axis=0)
            first = jnp.where(pl.program_id(1) > 0, 0, w)
        else:
            k2 = kc_ref[(qb - 1) * w:(qb + 1) * w, :]
            v2 = vc_ref[(qb - 1) * w:(qb + 1) * w, :]
            first = 0
        mask = jnp.logical_and(col > jnp.maximum(row, first - 1), col <= row + w)
        keys, vals = [], []
        for j in range(_A_KV_HEADS):
            own_k = jnp.where(low, k2, 0.0) if j == 0 else jnp.where(low, 0.0, k2)
            own_v = jnp.where(low, v2, 0.0) if j == 0 else jnp.where(low, 0.0, v2)
            oth_k = pltpu.roll(own_k, hd, axis=1)
            oth_v = pltpu.roll(own_v, hd, axis=1)
            lo_k, hi_k = (own_k, oth_k) if j == 0 else (oth_k, own_k)
            lo_v, hi_v = (own_v, oth_v) if j == 0 else (oth_v, own_v)
            keys.append(jnp.concatenate([lo_k, hi_k], axis=0).astype(_BF16))
            vals.append(jnp.concatenate([jnp.concatenate([lo_v.astype(_BF16), ones_lo], axis=1),
                                         jnp.concatenate([hi_v.astype(_BF16), ones_hi], axis=1)], axis=0))
        scores = [_dot_nt(q_ref[rows, p * _LANES:(p + 1) * _LANES] * scale, keys[kv_of[p]]) for p in pairs]
        return scores, mask, vals

    def softmax_stage(scores, mask):
        probs, sink_terms = [], []
        for p in pairs:
            s = jnp.where(mask, scores[p], -jnp.inf)
            halves = []
            for i in range(2):
                sh = s[:, i * 2 * w:(i + 1) * 2 * w]
                sink = sink_ref[2 * p + i] * log2e
                m = jnp.maximum(jnp.max(sh, axis=-1, keepdims=True), sink)
                halves.append((jnp.exp2(sh - m), jnp.exp2(sink - m)))
            probs.append(jnp.concatenate([halves[0][0], halves[1][0]], axis=1).astype(_BF16))
            sink_terms.append(jnp.where(low_q, halves[0][1], halves[1][1]))
        return probs, sink_terms

    def value_stage(qb, probs, sink_terms, vals):
        rows = slice(qb * w, (qb + 1) * w)
        for p in pairs:
            r = _dot(probs[p], vals[kv_of[p]])
            o_ref[rows, p * _LANES:(p + 1) * _LANES] = r[:, :_LANES] / (r[:, _LANES:] + sink_terms[p])

    n_qb = q_ref.shape[0] // w
    scored, soft = {}, {}
    for step in range(n_qb + 2):
        if 0 <= step - 1 < n_qb:
            qb = step - 1
            soft[qb] = softmax_stage(*scored[qb][:2])
        if step < n_qb:
            scored[step] = score_stage(step)
        if 0 <= step - 2 < n_qb:
            qb = step - 2
            value_stage(qb, *soft.pop(qb), scored.pop(qb)[2])


def _swa_prompt_call(q, k, v, sinks, batch):
    n, nq = q.shape
    nk = k.shape[1]
    w = _WINDOW
    qb = _SWA_QBLOCKS
    nb = n // batch // (w * qb)
    cur = lambda b, i: (b * nb + i, 0)
    prev = lambda b, i: ((b * nb + i) * qb - jnp.minimum(i, 1), 0)
    return pl.pallas_call(
        _swa_prompt_kernel,
        grid=(batch, nb),
        in_specs=[
            pl.BlockSpec(memory_space=pltpu.SMEM),
            pl.BlockSpec((qb * w, nq), cur),
            pl.BlockSpec((w, nk), prev), pl.BlockSpec((qb * w, nk), cur),
            pl.BlockSpec((w, nk), prev), pl.BlockSpec((qb * w, nk), cur),
        ],
        out_specs=pl.BlockSpec((qb * w, nq), cur),
        out_shape=jax.ShapeDtypeStruct((n, nq), _F32),
        compiler_params=_params("parallel", "parallel"),
        name="swa_prompt",
    )(sinks, q, k, k, v, v)


def _swa_sample_kernel(sink_ref, q_ref, kn_ref, vn_ref, ck_ref, cv_ref, o_ref, kw_ref, vw_ref, *, t_new):
    p = ck_ref.shape[1]
    scale = np.float32(1.0 / np.sqrt(_A_HEAD_DIM))
    rows = _A_GROUP * _SUBLANES
    t = lax.broadcasted_iota(jnp.int32, (rows, p + _SUBLANES), 0) & (_SUBLANES - 1)
    c = lax.broadcasted_iota(jnp.int32, (rows, p + _SUBLANES), 1)
    mask = jnp.logical_and(c <= t + p, c > t + p - _WINDOW)
    g_of_row = lax.broadcasted_iota(jnp.int32, (rows, 1), 0) >> (_SUBLANES.bit_length() - 1)
    nseq = ck_ref.shape[0]
    q_t = [q_ref[t] for t in range(t_new)]
    kn_t = [kn_ref[t] for t in range(t_new)]
    vn_t = [vn_ref[t] for t in range(t_new)]
    sinks = []
    for j in range(_A_KV_HEADS):
        sink = jnp.zeros((rows, 1), _F32)
        for g in range(_A_GROUP):
            sink = jnp.where(g_of_row == g, sink_ref[j * _A_GROUP + g], sink)
        sinks.append(sink)
    new_k, new_v, qs, keys, vals = [], [], [], [], []
    for s in range(nseq):
        q8, kn8, vn8 = _seq_tile(q_t, s), _seq_tile(kn_t, s), _seq_tile(vn_t, s)
        ck, cv = ck_ref[s], cv_ref[s]
        kw_ref[s, 0:p - t_new, :] = ck[t_new:, :]
        kw_ref[s, p - t_new:p, :] = kn8[0:t_new, :]
        vw_ref[s, 0:p - t_new, :] = cv[t_new:, :]
        vw_ref[s, p - t_new:p, :] = vn8[0:t_new, :]
        for j in range(_A_KV_HEADS):
            ks = slice(j * _A_HEAD_DIM, (j + 1) * _A_HEAD_DIM)
            keys.append(jnp.concatenate([ck[:, ks], kn8[:, ks]], axis=0))
            vals.append(jnp.concatenate([cv[:, ks], vn8[:, ks]], axis=0))
            qs.append(jnp.concatenate(
                [q8[:, (j * _A_GROUP + g) * _A_HEAD_DIM:(j * _A_GROUP + g + 1) * _A_HEAD_DIM]
                 for g in range(_A_GROUP)], axis=0))
    scores = [_dot_nt(qj, k) * scale for qj, k in zip(qs, keys)]
    outs = [_sink_softmax_pv(sc, mask, sinks[i % _A_KV_HEADS], v) for i, (sc, v) in enumerate(zip(scores, vals))]
    per_seq = []
    for s in range(nseq):
        heads = []
        for j in range(_A_KV_HEADS):
            o = outs[s * _A_KV_HEADS + j]
            heads.extend(o[g * _SUBLANES:(g + 1) * _SUBLANES, :] for g in range(_A_GROUP))
        per_seq.append(jnp.concatenate(heads, axis=1))
    for t, tile in enumerate(_time_tiles(per_seq, t_new)):
        o_ref[t] = tile


def _swa_sample_call(q, k_new, v_new, cache_k, cache_v, sinks):
    t_new, nseq, nq = q.shape
    nk = k_new.shape[2]
    p = cache_k.shape[1]
    bs = _SAMPLE_SEQS
    toks = lambda i: (0, i, 0)
    seqs = lambda i: (i, 0, 0)
    return pl.pallas_call(
        functools.partial(_swa_sample_kernel, t_new=t_new),
        grid=(nseq // bs,),
        in_specs=[
            pl.BlockSpec(memory_space=pltpu.SMEM),
            pl.BlockSpec((t_new, bs, nq), toks),
            pl.BlockSpec((t_new, bs, nk), toks), pl.BlockSpec((t_new, bs, nk), toks),
            pl.BlockSpec((bs, p, nk), seqs), pl.BlockSpec((bs, p, nk), seqs),
        ],
        out_specs=[pl.BlockSpec((t_new, bs, nq), toks),
                   pl.BlockSpec((bs, p, nk), seqs), pl.BlockSpec((bs, p, nk), seqs)],
        out_shape=[jax.ShapeDtypeStruct((t_new, nseq, nq), _F32),
                   jax.ShapeDtypeStruct(cache_k.shape, _F32), jax.ShapeDtypeStruct(cache_v.shape, _F32)],
        compiler_params=_params("parallel"),
        name="swa_sample",
    )(sinks, q, k_new, v_new, cache_k, cache_v)


def _softplus(z):
    return jnp.maximum(z, 0.0) + jnp.log1p(jnp.exp(-jnp.abs(z)))


def _gelu_tanh(x):
    return 0.5 * x * (1.0 + jnp.tanh(np.float32(np.sqrt(2.0 / np.pi)) * (x + 0.044715 * (x * x * x))))


def _rg_gates(xc, wa_ref, ba_ref, wx_ref, bx_ref, sp_neg_lam):
    r = _sigmoid(_dot(xc, wa_ref[...]) + ba_ref[...])
    i = _sigmoid(_dot(xc, wx_ref[...]) + bx_ref[...])
    a = jnp.exp((-_RG_C) * r * sp_neg_lam)
    gap = jnp.maximum(1.0 - a * a, 0.0)
    mult = jnp.where(gap > 0.0, gap * lax.rsqrt(gap), 0.0)
    return a, mult * (i * xc)


def _rg_prompt_kernel(xr_ref, xg_ref, cw_ref, cb_ref, wa_ref, ba_ref, wx_ref, bx_ref, lam_ref,
                      o_ref, conv_ref, h_ref, xpad, a_s, b_s):
    t_len, w = xr_ref.shape
    cw = cw_ref.shape[0]
    xpad[0:_SUBLANES, :] = jnp.zeros((_SUBLANES, w), _F32)
    xpad[_SUBLANES:, :] = xr_ref[...]
    sp = _softplus(-lam_ref[...])
    for c in range(t_len // _RG_CHUNK):
        r0 = c * _RG_CHUNK
        xc = cb_ref[...]
        for j in range(cw):
            lo = _SUBLANES + r0 - (cw - 1) + j
            xc = xc + xpad[lo:lo + _RG_CHUNK, :] * cw_ref[j:j + 1, :]
        a, b = _rg_gates(xc, wa_ref, ba_ref, wx_ref, bx_ref, sp)
        a_s[r0:r0 + _RG_CHUNK, :] = a
        b_s[r0:r0 + _RG_CHUNK, :] = b

    row = lax.broadcasted_iota(jnp.int32, (_SUBLANES, w), 0)

    def group(g, h):
        r0 = pl.multiple_of(g * _SUBLANES, _SUBLANES)
        a = a_s[pl.ds(r0, _SUBLANES), :]
        b = b_s[pl.ds(r0, _SUBLANES), :]
        sh = 1
        while sh < _SUBLANES:
            a_prev = jnp.where(row >= sh, pltpu.roll(a, sh, axis=0), 1.0)
            b_prev = jnp.where(row >= sh, pltpu.roll(b, sh, axis=0), 0.0)
            b = a * b_prev + b
            a = a * a_prev
            sh *= 2
        hs = a * h + b
        o_ref[pl.ds(r0, _SUBLANES), :] = _gelu_tanh(xg_ref[pl.ds(r0, _SUBLANES), :]) * hs
        return hs[_SUBLANES - 1:_SUBLANES, :]

    h_last = lax.fori_loop(0, t_len // _SUBLANES, group, jnp.zeros((1, w), _F32), unroll=8)
    h_ref[0] = h_last
    conv_ref[0] = xr_ref[t_len - (cw - 1):t_len, :]


def _rg_prompt_call(xr, xg, conv_w, conv_b, wa_bd, ba, wx_bd, bx, lam, batch):
    n, w = xr.shape
    t_len = n // batch
    cw = conv_w.shape[0]
    seq = lambda b: (b, 0)
    const = lambda b: (0, 0)
    vec = pl.BlockSpec((1, w), const)
    mat = pl.BlockSpec((w, w), const)
    return pl.pallas_call(
        _rg_prompt_kernel,
        grid=(batch,),
        in_specs=[pl.BlockSpec((t_len, w), seq), pl.BlockSpec((t_len, w), seq),
                  pl.BlockSpec((cw, w), const), vec, mat, vec, mat, vec, vec],
        out_specs=[pl.BlockSpec((t_len, w), seq),
                   pl.BlockSpec((1, cw - 1, w), lambda b: (b, 0, 0)),
                   pl.BlockSpec((1, 1, w), lambda b: (b, 0, 0))],
        out_shape=[jax.ShapeDtypeStruct((n, w), _F32),
                   jax.ShapeDtypeStruct((batch, cw - 1, w), _F32),
                   jax.ShapeDtypeStruct((batch, 1, w), _F32)],
        scratch_shapes=[pltpu.VMEM((t_len + _SUBLANES, w), _F32),
                        pltpu.VMEM((t_len, w), _F32), pltpu.VMEM((t_len, w), _F32)],
        compiler_params=_params("parallel"),
        name="rglru_prompt",
    )(xr, xg, conv_w, conv_b, wa_bd, ba, wx_bd, bx, lam)


def _rg_sample_kernel(xr_ref, xg_ref, conv0_ref, h0_ref, cw_ref, cb_ref, wa_ref, ba_ref, wx_ref, bx_ref, lam_ref,
                      o_ref, conv_ref, h_ref):
    cw = cw_ref.shape[0]
    t_new = xr_ref.shape[0]
    sp = _softplus(-lam_ref[...])
    hist = [conv0_ref[j] for j in range(cw - 1)] + [xr_ref[t] for t in range(t_new)]
    h = h0_ref[...]
    for t in range(t_new):
        xc = cb_ref[...]
        for j in range(cw):
            xc = xc + hist[t + j] * cw_ref[j:j + 1, :]
        a, b = _rg_gates(xc, wa_ref, ba_ref, wx_ref, bx_ref, sp)
        h = a * h + b
        o_ref[t] = _gelu_tanh(xg_ref[t]) * h
    h_ref[...] = h
    for j in range(cw - 1):
        conv_ref[j] = hist[t_new + j]


def _rg_sample_call(xr_t, xg_t, conv0_t, h0, conv_w, conv_b, wa_bd, ba, wx_bd, bx, lam):
    t_new, nseq, w = xg_t.shape
    return pl.pallas_call(
        _rg_sample_kernel,
        out_shape=[jax.ShapeDtypeStruct((t_new, nseq, w), _F32), jax.ShapeDtypeStruct(conv0_t.shape, _F32),
                   jax.ShapeDtypeStruct((nseq, w), _F32)],
        compiler_params=pltpu.CompilerParams(vmem_limit_bytes=_VMEM_LIMIT),
        name="rglru_sample",
    )(xr_t, xg_t, conv0_t, h0, conv_w, conv_b, wa_bd, ba, wx_bd, bx, lam)


def _row_bcast(x, r, n):
    return jnp.broadcast_to(x[r:r + 1, :], (n, x.shape[1]))


def _chunk_cumsum(x):
    n_tiles = x.shape[0] // _SUBLANES
    row = lax.broadcasted_iota(jnp.int32, (_SUBLANES, x.shape[1]), 0)
    tiles = []
    carry = None
    for i in range(n_tiles):
        t = x[i * _SUBLANES:(i + 1) * _SUBLANES, :]
        sh = 1
        while sh < _SUBLANES:
            t = t + jnp.where(row >= sh, pltpu.roll(t, sh, axis=0), 0.0)
            sh *= 2
        if carry is not None:
            t = t + carry
        carry = _row_bcast(t, _SUBLANES - 1, _SUBLANES)
        tiles.append(t)
    return tiles[0] if n_tiles == 1 else jnp.concatenate(tiles, axis=0)


def _level_reference(b, m):
    n = b.shape[0]
    if 2 * m >= _SUBLANES:
        pieces = [_row_bcast(b, lo + m - 1, 2 * m) for lo in range(0, n, 2 * m)]
        return pieces[0] if len(pieces) == 1 else jnp.concatenate(pieces, axis=0)
    row = lax.broadcasted_iota(jnp.int32, (_SUBLANES, b.shape[1]), 0)
    tiles = []
    for i in range(n // _SUBLANES):
        t = b[i * _SUBLANES:(i + 1) * _SUBLANES, :]
        ref = None
        for lo in range(0, _SUBLANES, 2 * m):
            piece = _row_bcast(t, lo + m - 1, _SUBLANES)
            ref = piece if ref is None else jnp.where(row >= lo, piece, ref)
        tiles.append(ref)
    return tiles[0] if len(tiles) == 1 else jnp.concatenate(tiles, axis=0)


def _hgrn_gates(fz, lb):
    f = lb + (1.0 - lb) * _sigmoid(fz)
    return jnp.log(f), 1.0 - f


def _hgrn_chunk(q, fz, v, lb, state):
    n, kd = q.shape
    log_f, k = _hgrn_gates(fz, lb)
    b = _chunk_cumsum(log_f)
    b_last = _row_bcast(b, n - 1, n)

    o = _dot_nt(q * jnp.exp(b), state)
    new_state = jnp.exp(b_last[0:1, :]) * state + _dot_tn(v, k * jnp.exp(b_last - b))

    row = lax.broadcasted_iota(jnp.int32, (n, n), 0)
    col = lax.broadcasted_iota(jnp.int32, (n, n), 1)
    upper = lax.broadcasted_iota(jnp.int32, (n, kd), 0)
    scores = jnp.where(row == col, jnp.sum(q * k, axis=-1, keepdims=True), 0.0)
    m = 1
    while m < n:
        e = jnp.exp(-jnp.abs(b - _level_reference(b, m)))
        z = jnp.where((upper & m) != 0, q, k) * e
        pair = jnp.logical_and((row & m) != 0, (row ^ m) >> (m.bit_length() - 1) == col >> (m.bit_length() - 1))
        scores = scores + jnp.where(pair, _dot_nt(z, z), 0.0)
        m *= 2
    return o + _dot(scores, v), new_state


def _lower_bound(logits, layer):
    m = jnp.max(logits, axis=0, keepdims=True)
    e = jnp.exp(logits - m)
    return jnp.sum(e[1:layer + 1, :], axis=0, keepdims=True) / jnp.sum(e, axis=0, keepdims=True)


def _hgrn_out(o, g, gnorm):
    return _rms(o, gnorm) * _silu(g)


def _hgrn_prompt_kernel(q_ref, f_ref, v_ref, g_ref, lbl_ref, gn_ref, o_ref, s_ref,
                        st_scr, st0_scr, *, layer):
    rows_blk, width = q_ref.shape
    kd = width // _HG_HEADS
    n_chunks = rows_blk // _HG_CHUNK
    heads = [slice(h * kd, (h + 1) * kd) for h in range(_HG_HEADS)]
    lb = _lower_bound(lbl_ref[...], layer)
    gn = gn_ref[...]

    @pl.when(pl.program_id(2) == 0)
    def _():
        st_scr[...] = jnp.zeros(st_scr.shape, _F32)

    st0_scr[...] = st_scr[...]

    def chunk_rows(c):
        return pl.ds(pl.multiple_of(c * _HG_CHUNK, _HG_CHUNK), _HG_CHUNK)

    def prepare(c):
        rows = chunk_rows(c)
        log_f, k = _hgrn_gates(f_ref[rows, :], lb)
        b = _chunk_cumsum(log_f)
        b_end = b[_HG_CHUNK - 1:_HG_CHUNK, :]
        dec = jnp.exp(b_end)
        e_b = jnp.exp(b)
        k_start = k / e_b
        q, v = q_ref[rows, :], v_ref[rows, :]
        qk = q * k
        own = jnp.concatenate([jnp.broadcast_to(jnp.sum(qk[:, l], axis=-1, keepdims=True), (_HG_CHUNK, kd))
                               for l in heads], axis=1) * v
        ops = ((q * e_b).astype(_BF16), k_start.astype(_BF16), (k_start * dec).astype(_BF16),
               v.astype(_BF16), dec, own)
        return ops, b_end

    row = lax.broadcasted_iota(jnp.int32, (_HG_CHUNK, _HG_CHUNK), 0)
    col = lax.broadcasted_iota(jnp.int32, (_HG_CHUNK, _HG_CHUNK), 1)

    def finish(c, h, o):
        rows = chunk_rows(c)
        o_ref[rows, heads[h]] = _hgrn_out(o, g_ref[rows, heads[h]], gn)

    def contract(c, ops):
        qs, ks, ke, vb, dec, own = ops
        states = [st_scr[h] for h in range(_HG_HEADS)]
        scores = [_dot_nt(qs[:, l], ks[:, l]) for l in heads]
        carried = [_dot_nt(qs[:, l], st) for l, st in zip(heads, states)]
        incs = [_dot_tn(vb[:, l], ke[:, l]) for l in heads]
        outs = [_dot(jnp.where(row > col, s, 0.0), vb[:, l]) for s, l in zip(scores, heads)]
        for h, l in enumerate(heads):
            st_scr[h] = states[h] * dec[:, l] + incs[h]
        for h, l in enumerate(heads):
            finish(c, h, outs[h] + carried[h] + own[:, l])

    def step(c, carry):
        ops, min_b = carry
        contract(c, ops)
        ops, b_end = prepare(c + 1)
        return ops, jnp.minimum(min_b, b_end)

    last_ops, min_b = lax.fori_loop(0, n_chunks - 1, step, prepare(0), unroll=True)
    contract(n_chunks - 1, last_ops)

    @pl.when(jnp.min(min_b) <= -_HG_SAFE_LOG_DECAY)
    def _():
        st_scr[...] = st0_scr[...]

        def chunk(c, carry):
            rows = chunk_rows(c)
            for h, l in enumerate(heads):
                o, st_scr[h] = _hgrn_chunk(q_ref[rows, l], f_ref[rows, l], v_ref[rows, l], lb[:, l], st_scr[h])
                finish(c, h, o)
            return carry
        lax.fori_loop(0, n_chunks, chunk, 0)

    @pl.when(pl.program_id(2) == pl.num_programs(2) - 1)
    def _():
        for h in range(_HG_HEADS):
            s_ref[0, h] = st_scr[h].T


def _hgrn_prompt_call(q, fz, v, g, lb_logits, gnorm, batch, layer):
    n, width = q.shape
    t_len = n // batch
    kd = width // _C_HEADS
    gw = _HG_HEADS * kd
    rows_blk = min(_HG_ROWS, t_len)
    nt = t_len // rows_blk
    blk = pl.BlockSpec((rows_blk, gw), lambda b, h, t: (b * nt + t, h))
    return pl.pallas_call(
        functools.partial(_hgrn_prompt_kernel, layer=layer),
        grid=(batch, _C_HEADS // _HG_HEADS, nt),
        in_specs=[blk, blk, blk, blk,
                  pl.BlockSpec((lb_logits.shape[0], gw), lambda b, h, t: (0, h)),
                  pl.BlockSpec((1, kd), lambda b, h, t: (0, 0))],
        out_specs=[blk, pl.BlockSpec((1, _HG_HEADS, kd, kd), lambda b, h, t: (b, h, 0, 0))],
        out_shape=[jax.ShapeDtypeStruct((n, width), _F32),
                   jax.ShapeDtypeStruct((batch, _C_HEADS, kd, kd), _F32)],
        scratch_shapes=[pltpu.VMEM((_HG_HEADS, kd, kd), _F32)] * 2,
        compiler_params=_params("parallel", "parallel", "arbitrary"),
        name="hgrn2_prompt",
    )(q, fz, v, g, lb_logits, gnorm)


def _hgrn_sample_kernel(q_ref, f_ref, v_ref, g_ref, lbl_ref, gn_ref, s0_ref, o_ref, s_ref, *, layer):
    t_new, nseq, width = q_ref.shape
    n_heads = s0_ref.shape[1]
    kd = width // n_heads
    lanes = [slice(h * kd, (h + 1) * kd) for h in range(n_heads)]
    lb = _lower_bound(lbl_ref[...], layer)
    gn = gn_ref[...]
    q = [q_ref[t] for t in range(t_new)]
    v = [v_ref[t] for t in range(t_new)]
    keys, b = [], []
    for t in range(t_new):
        log_f, k = _hgrn_gates(f_ref[t], lb)
        keys.append(k)
        b.append(log_f if t == 0 else b[-1] + log_f)

    def per_head_sum(w):
        return jnp.concatenate([jnp.broadcast_to(jnp.sum(w[:, l], axis=-1, keepdims=True), (nseq, kd))
                                for l in lanes], axis=1)

    within = []
    for t in range(t_new):
        acc = per_head_sum(q[t] * keys[t]) * v[t]
        for s in range(t):
            acc = acc + per_head_sum(q[t] * keys[s] * jnp.exp(b[t] - b[s])) * v[s]
        within.append(acc)

    q_in = [q[t] * jnp.exp(b[t]) for t in range(t_new)]
    k_end = [keys[t] * jnp.exp(b[-1] - b[t]) for t in range(t_new)]
    decay = jnp.exp(b[-1])
    pairs = [(s, h) for h in range(n_heads) for s in range(nseq)]
    lhs = [_seq_tile([x[:, lanes[h]] for x in q_in], s) for s, h in pairs]
    k_seq = [_seq_tile([x[:, lanes[h]] for x in k_end], s) for s, h in pairs]
    v_seq = [_seq_tile([x[:, lanes[h]] for x in v], s) for s, h in pairs]
    carried = [_dot(x, s0_ref[s, h]) for x, (s, h) in zip(lhs, pairs)]
    incs = [_dot_tn(ks, vs) for ks, vs in zip(k_seq, v_seq)]
    carried_t = []
    for h in range(n_heads):
        dec_cols = jnp.concatenate([decay[:, lanes[h]], jnp.zeros((kd - nseq, kd), _F32)], axis=0).T
        for s in range(nseq):
            i = h * nseq + s
            s_ref[s, h] = jnp.broadcast_to(dec_cols[:, s:s + 1], (kd, kd)) * s0_ref[s, h] + incs[i]
        carried_t.append(_time_tiles(carried[h * nseq:(h + 1) * nseq], t_new))
    for t in range(t_new):
        o = within[t] + jnp.concatenate([carried_t[h][t] for h in range(n_heads)], axis=1)
        g = g_ref[t]
        o_ref[t] = jnp.concatenate([_hgrn_out(o[:, l], g[:, l], gn) for l in lanes], axis=1)


def _hgrn_sample_call(q, fz, v, g, lb_logits, gnorm, s0, layer):
    t_new, nseq, width = q.shape
    kd = width // _C_HEADS
    gw = _HG_HEADS * kd
    bs = _SAMPLE_SEQS
    blk = pl.BlockSpec((t_new, bs, gw), lambda i, h: (0, i, h))
    st = pl.BlockSpec((bs, _HG_HEADS, kd, kd), lambda i, h: (i, h, 0, 0))
    return pl.pallas_call(
        functools.partial(_hgrn_sample_kernel, layer=layer),
        grid=(nseq // bs, _C_HEADS // _HG_HEADS),
        in_specs=[blk, blk, blk, blk,
                  pl.BlockSpec((lb_logits.shape[0], gw), lambda i, h: (0, h)),
                  pl.BlockSpec((1, kd), lambda i, h: (0, 0)), st],
        out_specs=[blk, st],
        out_shape=[jax.ShapeDtypeStruct((t_new, nseq, width), _F32), jax.ShapeDtypeStruct(s0.shape, _F32)],
        compiler_params=_params("parallel", "parallel"),
        name="hgrn2_sample",
    )(q, fz, v, g, lb_logits, gnorm, s0)


def _rope_tables(pos):
    half = _ROT_DIM // 2
    inv_freq = _ROPE_THETA ** (-jnp.arange(0, _ROT_DIM, 2, dtype=_F32) / _ROT_DIM)
    ang = pos.astype(_F32)[:, None] * inv_freq[None, :]
    cos, sin = jnp.cos(ang), jnp.sin(ang)
    ones = jnp.ones((pos.shape[0], _A_HEAD_DIM - _ROT_DIM), _F32)
    zeros = jnp.zeros((pos.shape[0], _A_HEAD_DIM - half), _F32)
    zeros_h = jnp.zeros((pos.shape[0], half), _F32)
    reps = _LANES // _A_HEAD_DIM
    cos_t = jnp.tile(jnp.concatenate([cos, cos, ones], axis=1), (1, reps))
    sin_lo = jnp.tile(jnp.concatenate([-sin, zeros], axis=1), (1, reps))
    sin_hi = jnp.tile(jnp.concatenate([zeros_h, sin, ones * 0.0], axis=1), (1, reps))
    return cos_t, sin_lo, sin_hi


def _block_diag(w):
    nb, bd, _ = w.shape
    eye = jnp.eye(nb, dtype=w.dtype)
    return (w[:, :, None, :] * eye[:, None, :, None]).reshape(nb * bd, nb * bd)


def kernel(x_prompt, x_sample, c_prompt, c_sample, cache_k_win, cache_v_win, state_conv_rglru,
           state_h_rglru, state_s_hgrn, norm_pre, norm_post, ada_w, ada_b, ffn1_w_in, ffn1_w_out,
           ffn2_w_in, ffn2_w_out, even_w_in, even_w_out, attn_sinks, rg_conv_w, rg_conv_b, rg_wa,
           rg_ba, rg_wx, rg_bx, rg_lambda, odd_w_in, odd_w_out, hgrn_lb_logits, hgrn_gnorm):
    bp, tp, d = x_prompt.shape
    bs, ts, _ = x_sample.shape
    depth = norm_pre.shape[0]
    n_sub = depth * _N_SUB
    nk = _A_KV_HEADS * _A_HEAD_DIM
    win = cache_k_win.shape[2]
    cw = rg_conv_w.shape[1]
    bw = rg_conv_w.shape[2]
    kd = state_s_hgrn.shape[3]

    cast = lambda w: w.astype(_BF16)
    ffn1_in, ffn1_out, ffn2_in, ffn2_out = ffn1_w_in, ffn1_w_out, ffn2_w_in, ffn2_w_out
    ev_in, ev_out, od_in, od_out = even_w_in, even_w_out, odd_w_in, odd_w_out
    gpre = norm_pre.reshape(n_sub, 1, d)
    gpost = norm_post.reshape(n_sub, 1, d)

    mod_p, mod_s = _ada_call(jnp.concatenate([c_prompt, c_sample], axis=0), bp,
                             ada_w.reshape(n_sub, d, 3 * d), ada_b.reshape(n_sub, 1, 3 * d))
    mod_p = mod_p.reshape(n_sub, bp, 1, 3 * d)
    mod_s = mod_s.reshape(n_sub, 1, bs, 3 * d)

    tabs_p = _rope_tables(jnp.arange(tp, dtype=jnp.int32))
    tabs_s = tuple(jnp.repeat(t, bs, axis=0) for t in _rope_tables(_PAST_LEN + jnp.arange(ts, dtype=jnp.int32)))

    time_major = lambda a: a.reshape(ts, bs, a.shape[-1])
    xp = x_prompt.reshape(bp * tp, d)
    xs = x_sample.transpose(1, 0, 2).reshape(ts * bs, d)
    groups = {"p": dict(mod=mod_p, seq_rows=tp, tabs=tabs_p), "s": dict(mod=mod_s, seq_rows=ts * bs, tabs=tabs_s)}
    outs = {g: dict(k=[], v=[], conv=[], h=[], s=[]) for g in groups}

    for l in range(depth):
        s0, s1, s2 = l * _N_SUB, l * _N_SUB + 1, l * _N_SUB + 2
        xp, xs = _ffn_call(xp, xs, mod_p, mod_s, s0, l, gpre, ffn1_in, ffn1_out, gpost, 0.5, tp)
        acts = {}
        for name, x in (("p", xp), ("s", xs)):
            grp = groups[name]
            mod4, tps = grp["mod"], grp["seq_rows"]
            if l % 2 == 0:
                e = l // 2
                q, k, v, xg, xr = _even_in_call(x, mod4, s1, e, gpre, ev_in, grp["tabs"], tps)
                wa_bd, wx_bd = cast(_block_diag(rg_wa[e])), cast(_block_diag(rg_wx[e]))
                vecs = [a[e].reshape(1, bw) for a in (rg_conv_b, rg_ba, rg_bx, rg_lambda)]
                if name == "p":
                    o_a = _swa_prompt_call(q, k, v, attn_sinks[e], bp)
                    o_b, conv, h_last = _rg_prompt_call(xr, xg, rg_conv_w[e], vecs[0], wa_bd, vecs[1], wx_bd,
                                                        vecs[2], vecs[3], bp)
                    last = lambda a: a.reshape(bp, tp, nk)[:, tp - win:].reshape(bp, win, _A_KV_HEADS, _A_HEAD_DIM)
                    outs[name]["k"].append(last(k))
                    outs[name]["v"].append(last(v))
                    outs[name]["h"].append(h_last.reshape(bp, bw))
                else:
                    o_a, kw, vw = _swa_sample_call(time_major(q), time_major(k), time_major(v),
                                                   cache_k_win[e].reshape(bs, win, nk),
                                                   cache_v_win[e].reshape(bs, win, nk), attn_sinks[e])
                    o_b, conv_t, h_last = _rg_sample_call(time_major(xr), time_major(xg),
                                                          state_conv_rglru[e].transpose(1, 0, 2), state_h_rglru[e],
                                                          rg_conv_w[e], vecs[0], wa_bd, vecs[1], wx_bd, vecs[2], vecs[3])
                    o_a, o_b = o_a.reshape(ts * bs, -1), o_b.reshape(ts * bs, bw)
                    conv = conv_t.transpose(1, 0, 2)
                    outs[name]["k"].append(kw.reshape(bs, win, _A_KV_HEADS, _A_HEAD_DIM))
                    outs[name]["v"].append(vw.reshape(bs, win, _A_KV_HEADS, _A_HEAD_DIM))
                    outs[name]["h"].append(h_last)
                outs[name]["conv"].append(conv)
                acts[name] = [o_a, o_b]
                widx, w_mix = e, ev_out
            else:
                o = l // 2
                q, fz, v, g = _odd_in_call(x, mod4, s1, o, gpre, od_in, tps)
                gn = hgrn_gnorm[o].reshape(1, kd)
                if name == "p":
                    y, s_last = _hgrn_prompt_call(q, fz, v, g, hgrn_lb_logits, gn, bp, l)
                else:
                    y, s_last = _hgrn_sample_call(time_major(q), time_major(fz), time_major(v), time_major(g),
                                                  hgrn_lb_logits, gn, state_s_hgrn[o], l)
                    y = y.reshape(ts * bs, -1)
                outs[name]["s"].append(s_last)
                acts[name] = [y]
                widx, w_mix = o, od_out
        xp, xs = _ffn_call(xp, xs, mod_p, mod_s, s2, l, gpre, ffn2_in, ffn2_out, gpost, 0.5, tp,
                           (s1, widx, w_mix, acts["p"], acts["s"]))

    ys = {"p": xp.reshape(bp, tp, d), "s": xs.reshape(ts, bs, d).transpose(1, 0, 2)}
    res = []
    for name in ("p", "s"):
        o = outs[name]
        res.append((jnp.stack(o["k"]), jnp.stack(o["v"]), jnp.stack(o["conv"]), jnp.stack(o["h"]), jnp.stack(o["s"])))
    return (ys["p"], ys["s"]) + res[0] + res[1]
```

```python
import functools

import jax
import jax.numpy as jnp
import numpy as np
from jax import lax
from jax.experimental import pallas as pl
from jax.experimental.pallas import tpu as pltpu

_F32 = jnp.float32
_BF16 = jnp.bfloat16

_EPS = 1e-6
_A_HEADS = 8
_A_KV_HEADS = 2
_A_HEAD_DIM = 64
_A_GROUP = _A_HEADS // _A_KV_HEADS
_WINDOW = 128
_ROPE_THETA = 500000.0
_ROT_DIM = _A_HEAD_DIM // 4
_RG_C = 8.0
_C_HEADS = 8
_PAST_LEN = 16384
_N_SUB = 3

_LANES = 128
_SUBLANES = 8
_VMEM_BYTES = 64 * 1024 * 1024
_VMEM_LIMIT = _VMEM_BYTES * 3 // 4
_VMEM_LIMIT_FFN = _VMEM_BYTES * 29 // 32

_ROW_TILE = 512
_FF_CHUNK = 256
_RG_CHUNK = 256
_HG_CHUNK = 64
_HG_HEADS = 4
_HG_ROWS = 1024
_HG_SAFE_LOG_DECAY = 80.0
_SAMPLE_SEQS = 8
_STATE_RING = 3
_SWA_QBLOCKS = 4


def _dot(a, b):
    return jnp.dot(a.astype(_BF16), b.astype(_BF16), preferred_element_type=_F32)


def _dot_nt(a, b):
    return lax.dot_general(a.astype(_BF16), b.astype(_BF16), (((1,), (1,)), ((), ())),
                           preferred_element_type=_F32)


def _dot_tn(a, b):
    return lax.dot_general(a.astype(_BF16), b.astype(_BF16), (((0,), (0,)), ((), ())),
                           preferred_element_type=_F32)


def _sigmoid(x):
    return 1.0 / (1.0 + jnp.exp(-x))


def _silu(x):
    return x * _sigmoid(x)


def _rms(x, gain):
    inv = lax.rsqrt(jnp.mean(x * x, axis=-1, keepdims=True) + _EPS)
    return x * inv * gain


def _per_seq(a, r):
    n = a.shape[0]
    return a if r in (1, n) else a.reshape(n // r, r, a.shape[1])


def _pre(x, mod, gain):
    n, d = x.shape
    h = _rms(_per_seq(x, mod.shape[0]), gain * (1.0 + mod[:, d:2 * d])) + mod[:, :d]
    return h.reshape(n, d)


def _post(x, y, mod, gain, res_w):
    n, d = x.shape
    r = mod.shape[0]
    out = _per_seq(x, r) + _rms(_per_seq(y, r), (res_w * (1.0 + mod[:, 2 * d:])) * gain)
    return out.reshape(n, d)


def _params(*sem):
    return pltpu.CompilerParams(dimension_semantics=sem, vmem_limit_bytes=_VMEM_LIMIT)


def _ada_kernel(c_ref, w_ref, b_ref, op_ref, os_ref):
    mod = _dot(_silu(c_ref[...]), w_ref[...]) + b_ref[...]
    n_p = op_ref.shape[0]
    op_ref[...] = mod[:n_p, :]
    os_ref[...] = mod[n_p:, :]


def _ada_call(c_all, n_prompt, ada_w, ada_b):
    m, d = c_all.shape
    assert n_prompt % _SUBLANES == 0
    n_sub = ada_w.shape[0]
    n = ada_w.shape[-1]
    tn = n // 2
    out = lambda rows: (pl.BlockSpec((None, rows, tn), lambda s, j: (s, 0, j)),
                        jax.ShapeDtypeStruct((n_sub, rows, n), _F32))
    (spec_p, shape_p), (spec_s, shape_s) = out(n_prompt), out(m - n_prompt)
    return pl.pallas_call(
        _ada_kernel,
        grid=(n_sub, n // tn),
        in_specs=[
            pl.BlockSpec((m, d), lambda s, j: (0, 0)),
            pl.BlockSpec((None, d, tn), lambda s, j: (s, 0, j)),
            pl.BlockSpec((None, 1, tn), lambda s, j: (s, 0, j)),
        ],
        out_specs=[spec_p, spec_s],
        out_shape=[shape_p, shape_s],
        compiler_params=_params("parallel", "parallel"),
        name="ada_mod",
    )(c_all, ada_w, ada_b)


def _ffn_kernel(xp_ref, xs_ref, modp_ref, mods_ref, gpre_ref, wg_ref, wu_ref, wout_ref, gpost_ref, *refs,
                res_w, n_load, n_prompt, n_acts):
    op_ref, os_ref, wg_bf, wu_bf, wout_bf, x0_scr, acc_scr = refs[-7:]
    i = pl.program_id(0)
    acts = refs[4:4 + 2 * n_acts] if n_acts else ()

    def mixed(x_ref, mmod_ref, act_refs):
        x = x_ref[...]
        if n_acts:
            mgpost_ref, mw_ref = refs[2:4]
            y = None
            off = 0
            for a_ref in act_refs:
                k = a_ref.shape[1]
                t = _dot(a_ref[...], mw_ref[off:off + k, :])
                y = t if y is None else y + t
                off += k
            x = _post(x, y, mmod_ref[0], mgpost_ref[...], 1.0)
        return x

    def chunk(h, j):
        return _dot(_silu(_dot(h, wg_bf[j])) * _dot(h, wu_bf[j]), wout_bf[j])

    def row_tile(x_ref, mod_ref, mmod_ref, act_refs, o_ref):
        x = mixed(x_ref, mmod_ref, act_refs)
        mod = mod_ref[0]
        h = _pre(x, mod, gpre_ref[...]).astype(_BF16)
        acc = jnp.zeros(x.shape, _F32)
        for j in range(n_load):
            acc = acc + chunk(h, j)
        o_ref[...] = _post(x, acc, mod, gpost_ref[...], res_w)

    @pl.when(i < n_load)
    def _():
        wg_bf[i] = wg_ref[...].astype(_BF16)
        wu_bf[i] = wu_ref[...].astype(_BF16)
        wout_bf[i] = wout_ref[...].astype(_BF16)

        @pl.when(i == 0)
        def _():
            x0_scr[...] = mixed(xp_ref, refs[0] if n_acts else None, acts[:n_acts])
            acc_scr[...] = jnp.zeros(acc_scr.shape, _F32)

        mod = modp_ref[0]
        acc_scr[...] += chunk(_pre(x0_scr[...], mod, gpre_ref[...]).astype(_BF16), i)

        @pl.when(i == n_load - 1)
        def _():
            op_ref[...] = _post(x0_scr[...], acc_scr[...], mod, gpost_ref[...], res_w)

    @pl.when(jnp.logical_and(i >= n_load, i < n_load + n_prompt - 1))
    def _():
        row_tile(xp_ref, modp_ref, refs[0] if n_acts else None, acts[:n_acts], op_ref)

    @pl.when(i == n_load + n_prompt - 1)
    def _():
        row_tile(xs_ref, mods_ref, refs[1] if n_acts else None, acts[n_acts:], os_ref)


def _ffn_call(xp, xs, modp, mods, sub, layer, gpre, w_in, w_out, gpost, res_w, seq_rows, mixer=None):
    n, d = xp.shape
    tm = xs.shape[0]
    assert n % tm == 0 and seq_rows % tm == 0 and mods.shape[2] * (tm // mods.shape[2]) == tm
    dff = w_out.shape[1]
    n_load = dff // _FF_CHUNK
    n_prompt = n // tm
    tiles_per_seq = seq_rows // tm
    tile = lambda i: jnp.clip(i - n_load + 1, 0, n_prompt - 1)
    rows_p = pl.BlockSpec((tm, d), lambda i: (tile(i), 0))
    once = dict(pipeline_mode=pl.Buffered(1))
    rows_s = pl.BlockSpec((tm, d), lambda i: (0, 0), **once)
    modp_spec = lambda s: pl.BlockSpec((None, 1, 1, 3 * d), lambda i: (s, tile(i) // tiles_per_seq, 0, 0))
    mods_spec = lambda s: pl.BlockSpec((None, 1, mods.shape[2], 3 * d), lambda i: (s, 0, 0, 0), **once)
    gain_spec = lambda s: pl.BlockSpec((None, 1, d), lambda i: (s, 0, 0))
    chunk = lambda i: jnp.minimum(i, n_load - 1)
    in_specs = [
        rows_p, rows_s, modp_spec(sub), mods_spec(sub), gain_spec(sub),
        pl.BlockSpec((None, d, _FF_CHUNK), lambda i: (layer, 0, chunk(i))),
        pl.BlockSpec((None, d, _FF_CHUNK), lambda i: (layer, 0, n_load + chunk(i))),
        pl.BlockSpec((None, _FF_CHUNK, d), lambda i: (layer, chunk(i), 0)),
        gain_spec(sub),
    ]
    args = [xp, xs, modp, mods, gpre, w_in, w_in, w_out, gpost]
    n_acts = 0
    if mixer is not None:
        msub, widx, mw, acts_p, acts_s = mixer
        n_acts = len(acts_p)
        in_specs += [modp_spec(msub), mods_spec(msub), gain_spec(msub),
                     pl.BlockSpec((None, mw.shape[1], d), lambda i: (widx, 0, 0), pipeline_mode=pl.Buffered(1))]
        in_specs += [pl.BlockSpec((tm, a.shape[1]), lambda i: (tile(i), 0)) for a in acts_p]
        in_specs += [pl.BlockSpec((tm, a.shape[1]), lambda i: (0, 0), **once) for a in acts_s]
        args += [modp, mods, gpost, mw] + list(acts_p) + list(acts_s)
    return pl.pallas_call(
        functools.partial(_ffn_kernel, res_w=res_w, n_load=n_load, n_prompt=n_prompt, n_acts=n_acts),
        grid=(n_load + n_prompt,),
        in_specs=in_specs,
        out_specs=[rows_p, pl.BlockSpec((tm, d), lambda i: (0, 0))],
        out_shape=[jax.ShapeDtypeStruct((n, d), _F32), jax.ShapeDtypeStruct((tm, d), _F32)],
        scratch_shapes=[pltpu.VMEM((n_load, d, _FF_CHUNK), _BF16), pltpu.VMEM((n_load, d, _FF_CHUNK), _BF16),
                        pltpu.VMEM((n_load, _FF_CHUNK, d), _BF16),
                        pltpu.VMEM((tm, d), _F32), pltpu.VMEM((tm, d), _F32)],
        compiler_params=pltpu.CompilerParams(dimension_semantics=("arbitrary",), vmem_limit_bytes=_VMEM_LIMIT_FFN),
        name="ffn_sublayer" if mixer is None else "mixer_out_ffn",
    )(*args)


def _rope(x, cos, sin_lo, sin_hi):
    outs = []
    for j in range(x.shape[1] // _LANES):
        xc = x[:, j * _LANES:(j + 1) * _LANES]
        nxt = pltpu.roll(xc, _LANES - _ROT_DIM // 2, axis=1)
        prv = pltpu.roll(xc, _ROT_DIM // 2, axis=1)
        outs.append(xc * cos + nxt * sin_lo + prv * sin_hi)
    return outs[0] if len(outs) == 1 else jnp.concatenate(outs, axis=1)


def _even_in_kernel(x_ref, mod_ref, gpre_ref, w_ref, cos_ref, slo_ref, shi_ref,
                    q_ref, k_ref, v_ref, xg_ref, xr_ref):
    h = _pre(x_ref[...], mod_ref[0], gpre_ref[...])
    y = _dot(h, w_ref[...])
    nq, nk, nw = q_ref.shape[1], k_ref.shape[1], xg_ref.shape[1]
    cos, slo, shi = cos_ref[...], slo_ref[...], shi_ref[...]
    q_ref[...] = _rope(y[:, :nq], cos, slo, shi)
    k_ref[...] = _rope(y[:, nq:nq + nk], cos, slo, shi)
    v_ref[...] = y[:, nq + nk:nq + 2 * nk]
    xg_ref[...] = y[:, nq + 2 * nk:nq + 2 * nk + nw]
    xr_ref[...] = y[:, nq + 2 * nk + nw:]


def _even_in_call(x, mod4, sub, e, gpre, w_in, rope_tabs, seq_rows):
    n, d = x.shape
    r = mod4.shape[2]
    tm = min(_ROW_TILE, n)
    tiles_per_seq = tab_tiles = seq_rows // tm
    nq = _A_HEADS * _A_HEAD_DIM
    nk = _A_KV_HEADS * _A_HEAD_DIM
    nw = (w_in.shape[-1] - nq - 2 * nk) // 2
    row = lambda i: (i, 0)
    tab = pl.BlockSpec((tm, _LANES), lambda i: (i % tab_tiles, 0))
    widths = [nq, nk, nk, nw, nw]
    return pl.pallas_call(
        _even_in_kernel,
        grid=(n // tm,),
        in_specs=[
            pl.BlockSpec((tm, d), row),
            pl.BlockSpec((None, 1, r, 3 * d), lambda i: (sub, i // tiles_per_seq, 0, 0)),
            pl.BlockSpec((None, 1, d), lambda i: (sub, 0, 0)),
            pl.BlockSpec((None, d, w_in.shape[-1]), lambda i: (e, 0, 0)),
            tab, tab, tab,
        ],
        out_specs=[pl.BlockSpec((tm, c), row) for c in widths],
        out_shape=[jax.ShapeDtypeStruct((n, c), _F32) for c in widths],
        compiler_params=_params("parallel"),
        name="even_in_proj",
    )(x, mod4, gpre, w_in, *rope_tabs)


def _odd_in_kernel(x_ref, mod_ref, gpre_ref, w_ref, q_ref, f_ref, v_ref, g_ref):
    h = _pre(x_ref[...], mod_ref[0], gpre_ref[...])
    y = _dot(h, w_ref[...])
    n = q_ref.shape[1]
    q_ref[...] = y[:, :n]
    f_ref[...] = y[:, n:2 * n]
    v_ref[...] = y[:, 2 * n:3 * n]
    g_ref[...] = y[:, 3 * n:]


def _odd_in_call(x, mod4, sub, o, gpre, w_in, seq_rows):
    n, d = x.shape
    r = mod4.shape[2]
    tm = min(_ROW_TILE, n)
    tiles_per_seq = seq_rows // tm
    nw = w_in.shape[-1] // 4
    row = lambda i: (i, 0)
    return pl.pallas_call(
        _odd_in_kernel,
        grid=(n // tm,),
        in_specs=[
            pl.BlockSpec((tm, d), row),
            pl.BlockSpec((None, 1, r, 3 * d), lambda i: (sub, i // tiles_per_seq, 0, 0)),
            pl.BlockSpec((None, 1, d), lambda i: (sub, 0, 0)),
            pl.BlockSpec((None, d, 4 * nw), lambda i: (o, 0, 0)),
        ],
        out_specs=[pl.BlockSpec((tm, nw), row)] * 4,
        out_shape=[jax.ShapeDtypeStruct((n, nw), _F32)] * 4,
        compiler_params=_params("parallel"),
        name="odd_in_proj",
    )(x, mod4, gpre, w_in)


def _seq_tile(time_tiles, s):
    row = lax.broadcasted_iota(jnp.int32, time_tiles[0].shape, 0)
    out = jnp.zeros(time_tiles[0].shape, time_tiles[0].dtype)
    for t, x in enumerate(time_tiles):
        shift = (t - s) % _SUBLANES
        out = jnp.where(row == t, pltpu.roll(x, shift, axis=0) if shift else x, out)
    return out


def _time_tiles(seq_tiles, n_t):
    row = lax.broadcasted_iota(jnp.int32, seq_tiles[0].shape, 0)
    outs = []
    for t in range(n_t):
        acc = jnp.zeros(seq_tiles[0].shape, seq_tiles[0].dtype)
        for s, x in enumerate(seq_tiles):
            shift = (s - t) % _SUBLANES
            acc = jnp.where(row == s, pltpu.roll(x, shift, axis=0) if shift else x, acc)
        outs.append(acc)
    return outs


def _sink_softmax_pv(s, mask, sink, v):
    s = jnp.where(mask, s, -jnp.inf)
    m = jnp.maximum(jnp.max(s, axis=-1, keepdims=True), sink)
    p = jnp.exp(s - m)
    denom = jnp.sum(p, axis=-1, keepdims=True) + jnp.exp(sink - m)
    return _dot(p, v) / denom


def _swa_prompt_kernel(sink_ref, q_ref, kp_ref, kc_ref, vp_ref, vc_ref, o_ref):
    w = _WINDOW
    hd = _A_HEAD_DIM
    assert _LANES == 2 * hd and _A_GROUP % 2 == 0
    log2e = np.float32(np.log2(np.e))
    scale = np.float32(1.0 / np.sqrt(hd)) * log2e
    low = lax.broadcasted_iota(jnp.int32, (2 * w, _LANES), 1) < hd
    ones_lo = jnp.where(low, 1.0, 0.0).astype(_BF16)
    ones_hi = jnp.where(low, 0.0, 1.0).astype(_BF16)
    low_q = lax.broadcasted_iota(jnp.int32, (w, _LANES), 1) < hd
    row = lax.broadcasted_iota(jnp.int32, (w, 4 * w), 0)
    col = lax.broadcasted_iota(jnp.int32, (w, 4 * w), 1) & (2 * w - 1)
    pairs = range(_A_HEADS // 2)
    kv_of = [(2 * p) // _A_GROUP for p in pairs]

    def score_stage(qb):
        rows = slice(qb * w, (qb + 1) * w)
        if qb == 0:
            k2 = jnp.concatenate([kp_ref[...], kc_ref[0:w, :]], axis=0)
            v2 = jnp.concatenate([vp_ref[...], vc_ref[0:w, :]], axis=0)
            first = jnp.where(pl.program_id(1) > 0, 0, w)
        else:
            k2 = kc_ref[(qb - 1) * w:(qb + 1) * w, :]
            v2 = vc_ref[(qb - 1) * w:(qb + 1) * w, :]
            first = 0
        mask = jnp.logical_and(col > jnp.maximum(row, first - 1), col <= row + w)
        keys, vals = [], []
        for j in range(_A_KV_HEADS):
            own_k = jnp.where(low, k2, 0.0) if j == 0 else jnp.where(low, 0.0, k2)
            own_v = jnp.where(low, v2, 0.0) if j == 0 else jnp.where(low, 0.0, v2)
            oth_k = pltpu.roll(own_k, hd, axis=1)
            oth_v = pltpu.roll(own_v, hd, axis=1)
            lo_k, hi_k = (own_k, oth_k) if j == 0 else (oth_k, own_k)
            lo_v, hi_v = (own_v, oth_v) if j == 0 else (oth_v, own_v)
            keys.append(jnp.concatenate([lo_k, hi_k], axis=0).astype(_BF16))
            vals.append(jnp.concatenate([jnp.concatenate([lo_v.astype(_BF16), ones_lo], axis=1),
                                         jnp.concatenate([hi_v.astype(_BF16), ones_hi], axis=1)], axis=0))
        scores = [_dot_nt(q_ref[rows, p * _LANES:(p + 1) * _LANES] * scale, keys[kv_of[p]]) for p in pairs]
        return scores, mask, vals

    def softmax_stage(scores, mask):
        probs, sink_terms = [], []
        for p in pairs:
            s = jnp.where(mask, scores[p], -jnp.inf)
            halves = []
            for i in range(2):
                sh = s[:, i * 2 * w:(i + 1) * 2 * w]
                sink = sink_ref[2 * p + i] * log2e
                m = jnp.maximum(jnp.max(sh, axis=-1, keepdims=True), sink)
                halves.append((jnp.exp2(sh - m), jnp.exp2(sink - m)))
            probs.append(jnp.concatenate([halves[0][0], halves[1][0]], axis=1).astype(_BF16))
            sink_terms.append(jnp.where(low_q, halves[0][1], halves[1][1]))
        return probs, sink_terms

    def value_stage(qb, probs, sink_terms, vals):
        rows = slice(qb * w, (qb + 1) * w)
        for p in pairs:
            r = _dot(probs[p], vals[kv_of[p]])
            o_ref[rows, p * _LANES:(p + 1) * _LANES] = r[:, :_LANES] / (r[:, _LANES:] + sink_terms[p])

    n_qb = q_ref.shape[0] // w
    scored, soft = {}, {}
    for step in range(n_qb + 2):
        if 0 <= step - 1 < n_qb:
            qb = step - 1
            soft[qb] = softmax_stage(*scored[qb][:2])
        if step < n_qb:
            scored[step] = score_stage(step)
        if 0 <= step - 2 < n_qb:
            qb = step - 2
            value_stage(qb, *soft.pop(qb), scored.pop(qb)[2])


def _swa_prompt_call(q, k, v, sinks, batch):
    n, nq = q.shape
    nk = k.shape[1]
    w = _WINDOW
    qb = _SWA_QBLOCKS
    nb = n // batch // (w * qb)
    cur = lambda b, i: (b * nb + i, 0)
    prev = lambda b, i: ((b * nb + i) * qb - jnp.minimum(i, 1), 0)
    return pl.pallas_call(
        _swa_prompt_kernel,
        grid=(batch, nb),
        in_specs=[
            pl.BlockSpec(memory_space=pltpu.SMEM),
            pl.BlockSpec((qb * w, nq), cur),
            pl.BlockSpec((w, nk), prev), pl.BlockSpec((qb * w, nk), cur),
            pl.BlockSpec((w, nk), prev), pl.BlockSpec((qb * w, nk), cur),
        ],
        out_specs=pl.BlockSpec((qb * w, nq), cur),
        out_shape=jax.ShapeDtypeStruct((n, nq), _F32),
        compiler_params=_params("parallel", "parallel"),
        name="swa_prompt",
    )(sinks, q, k, k, v, v)


def _swa_sample_kernel(sink_ref, q_ref, kn_ref, vn_ref, ck_ref, cv_ref, o_ref, kw_ref, vw_ref, *, t_new):
    p = ck_ref.shape[1]
    scale = np.float32(1.0 / np.sqrt(_A_HEAD_DIM))
    rows = _A_GROUP * _SUBLANES
    t = lax.broadcasted_iota(jnp.int32, (rows, p + _SUBLANES), 0) & (_SUBLANES - 1)
    c = lax.broadcasted_iota(jnp.int32, (rows, p + _SUBLANES), 1)
    mask = jnp.logical_and(c <= t + p, c > t + p - _WINDOW)
    g_of_row = lax.broadcasted_iota(jnp.int32, (rows, 1), 0) >> (_SUBLANES.bit_length() - 1)
    nseq = ck_ref.shape[0]
    q_t = [q_ref[t] for t in range(t_new)]
    kn_t = [kn_ref[t] for t in range(t_new)]
    vn_t = [vn_ref[t] for t in range(t_new)]
    sinks = []
    for j in range(_A_KV_HEADS):
        sink = jnp.zeros((rows, 1), _F32)
        for g in range(_A_GROUP):
            sink = jnp.where(g_of_row == g, sink_ref[j * _A_GROUP + g], sink)
        sinks.append(sink)
    new_k, new_v, qs, keys, vals = [], [], [], [], []
    for s in range(nseq):
        q8, kn8, vn8 = _seq_tile(q_t, s), _seq_tile(kn_t, s), _seq_tile(vn_t, s)
        ck, cv = ck_ref[s], cv_ref[s]
        kw_ref[s, 0:p - t_new, :] = ck[t_new:, :]
        kw_ref[s, p - t_new:p, :] = kn8[0:t_new, :]
        vw_ref[s, 0:p - t_new, :] = cv[t_new:, :]
        vw_ref[s, p - t_new:p, :] = vn8[0:t_new, :]
        for j in range(_A_KV_HEADS):
            ks = slice(j * _A_HEAD_DIM, (j + 1) * _A_HEAD_DIM)
            keys.append(jnp.concatenate([ck[:, ks], kn8[:, ks]], axis=0))
            vals.append(jnp.concatenate([cv[:, ks], vn8[:, ks]], axis=0))
            qs.append(jnp.concatenate(
                [q8[:, (j * _A_GROUP + g) * _A_HEAD_DIM:(j * _A_GROUP + g + 1) * _A_HEAD_DIM]
                 for g in range(_A_GROUP)], axis=0))
    scores = [_dot_nt(qj, k) * scale for qj, k in zip(qs, keys)]
    outs = [_sink_softmax_pv(sc, mask, sinks[i % _A_KV_HEADS], v) for i, (sc, v) in enumerate(zip(scores, vals))]
    per_seq = []
    for s in range(nseq):
        heads = []
        for j in range(_A_KV_HEADS):
            o = outs[s * _A_KV_HEADS + j]
            heads.extend(o[g * _SUBLANES:(g + 1) * _SUBLANES, :] for g in range(_A_GROUP))
        per_seq.append(jnp.concatenate(heads, axis=1))
    for t, tile in enumerate(_time_tiles(per_seq, t_new)):
        o_ref[t] = tile


def _swa_sample_call(q, k_new, v_new, cache_k, cache_v, sinks):
    t_new, nseq, nq = q.shape
    nk = k_new.shape[2]
    p = cache_k.shape[1]
    bs = _SAMPLE_SEQS
    toks = lambda i: (0, i, 0)
    seqs = lambda i: (i, 0, 0)
    return pl.pallas_call(
        functools.partial(_swa_sample_kernel, t_new=t_new),
        grid=(nseq // bs,),
        in_specs=[
            pl.BlockSpec(memory_space=pltpu.SMEM),
            pl.BlockSpec((t_new, bs, nq), toks),
            pl.BlockSpec((t_new, bs, nk), toks), pl.BlockSpec((t_new, bs, nk), toks),
            pl.BlockSpec((bs, p, nk), seqs), pl.BlockSpec((bs, p, nk), seqs),
        ],
        out_specs=[pl.BlockSpec((t_new, bs, nq), toks),
                   pl.BlockSpec((bs, p, nk), seqs), pl.BlockSpec((bs, p, nk), seqs)],
        out_shape=[jax.ShapeDtypeStruct((t_new, nseq, nq), _F32),
                   jax.ShapeDtypeStruct(cache_k.shape, _F32), jax.ShapeDtypeStruct(cache_v.shape, _F32)],
        compiler_params=_params("parallel"),
        name="swa_sample",
    )(sinks, q, k_new, v_new, cache_k, cache_v)


def _softplus(z):
    return jnp.maximum(z, 0.0) + jnp.log1p(jnp.exp(-jnp.abs(z)))


def _gelu_tanh(x):
    return 0.5 * x * (1.0 + jnp.tanh(np.float32(np.sqrt(2.0 / np.pi)) * (x + 0.044715 * (x * x * x))))


def _rg_gates(xc, wa_ref, ba_ref, wx_ref, bx_ref, sp_neg_lam):
    r = _sigmoid(_dot(xc, wa_ref[...]) + ba_ref[...])
    i = _sigmoid(_dot(xc, wx_ref[...]) + bx_ref[...])
    a = jnp.exp((-_RG_C) * r * sp_neg_lam)
    gap = jnp.maximum(1.0 - a * a, 0.0)
    mult = jnp.where(gap > 0.0, gap * lax.rsqrt(gap), 0.0)
    return a, mult * (i * xc)


def _rg_prompt_kernel(xr_ref, xg_ref, cw_ref, cb_ref, wa_ref, ba_ref, wx_ref, bx_ref, lam_ref,
                      o_ref, conv_ref, h_ref, xpad, a_s, b_s):
    t_len, w = xr_ref.shape
    cw = cw_ref.shape[0]
    xpad[0:_SUBLANES, :] = jnp.zeros((_SUBLANES, w), _F32)
    xpad[_SUBLANES:, :] = xr_ref[...]
    sp = _softplus(-lam_ref[...])
    for c in range(t_len // _RG_CHUNK):
        r0 = c * _RG_CHUNK
        xc = cb_ref[...]
        for j in range(cw):
            lo = _SUBLANES + r0 - (cw - 1) + j
            xc = xc + xpad[lo:lo + _RG_CHUNK, :] * cw_ref[j:j + 1, :]
        a, b = _rg_gates(xc, wa_ref, ba_ref, wx_ref, bx_ref, sp)
        a_s[r0:r0 + _RG_CHUNK, :] = a
        b_s[r0:r0 + _RG_CHUNK, :] = b

    row = lax.broadcasted_iota(jnp.int32, (_SUBLANES, w), 0)

    def group(g, h):
        r0 = pl.multiple_of(g * _SUBLANES, _SUBLANES)
        a = a_s[pl.ds(r0, _SUBLANES), :]
        b = b_s[pl.ds(r0, _SUBLANES), :]
        sh = 1
        while sh < _SUBLANES:
            a_prev = jnp.where(row >= sh, pltpu.roll(a, sh, axis=0), 1.0)
            b_prev = jnp.where(row >= sh, pltpu.roll(b, sh, axis=0), 0.0)
            b = a * b_prev + b
            a = a * a_prev
            sh *= 2
        hs = a * h + b
        o_ref[pl.ds(r0, _SUBLANES), :] = _gelu_tanh(xg_ref[pl.ds(r0, _SUBLANES), :]) * hs
        return hs[_SUBLANES - 1:_SUBLANES, :]

    h_last = lax.fori_loop(0, t_len // _SUBLANES, group, jnp.zeros((1, w), _F32), unroll=8)
    h_ref[0] = h_last
    conv_ref[0] = xr_ref[t_len - (cw - 1):t_len, :]


def _rg_prompt_call(xr, xg, conv_w, conv_b, wa_bd, ba, wx_bd, bx, lam, batch):
    n, w = xr.shape
    t_len = n // batch
    cw = conv_w.shape[0]
    seq = lambda b: (b, 0)
    const = lambda b: (0, 0)
    vec = pl.BlockSpec((1, w), const)
    mat = pl.BlockSpec((w, w), const)
    return pl.pallas_call(
        _rg_prompt_kernel,
        grid=(batch,),
        in_specs=[pl.BlockSpec((t_len, w), seq), pl.BlockSpec((t_len, w), seq),
                  pl.BlockSpec((cw, w), const), vec, mat, vec, mat, vec, vec],
        out_specs=[pl.BlockSpec((t_len, w), seq),
                   pl.BlockSpec((1, cw - 1, w), lambda b: (b, 0, 0)),
                   pl.BlockSpec((1, 1, w), lambda b: (b, 0, 0))],
        out_shape=[jax.ShapeDtypeStruct((n, w), _F32),
                   jax.ShapeDtypeStruct((batch, cw - 1, w), _F32),
                   jax.ShapeDtypeStruct((batch, 1, w), _F32)],
        scratch_shapes=[pltpu.VMEM((t_len + _SUBLANES, w), _F32),
                        pltpu.VMEM((t_len, w), _F32), pltpu.VMEM((t_len, w), _F32)],
        compiler_params=_params("parallel"),
        name="rglru_prompt",
    )(xr, xg, conv_w, conv_b, wa_bd, ba, wx_bd, bx, lam)


def _rg_sample_kernel(xr_ref, xg_ref, conv0_ref, h0_ref, cw_ref, cb_ref, wa_ref, ba_ref, wx_ref, bx_ref, lam_ref,
                      o_ref, conv_ref, h_ref):
    cw = cw_ref.shape[0]
    t_new = xr_ref.shape[0]
    sp = _softplus(-lam_ref[...])
    hist = [conv0_ref[j] for j in range(cw - 1)] + [xr_ref[t] for t in range(t_new)]
    h = h0_ref[...]
    for t in range(t_new):
        xc = cb_ref[...]
        for j in range(cw):
            xc = xc + hist[t + j] * cw_ref[j:j + 1, :]
        a, b = _rg_gates(xc, wa_ref, ba_ref, wx_ref, bx_ref, sp)
        h = a * h + b
        o_ref[t] = _gelu_tanh(xg_ref[t]) * h
    h_ref[...] = h
    for j in range(cw - 1):
        conv_ref[j] = hist[t_new + j]


def _rg_sample_call(xr_t, xg_t, conv0_t, h0, conv_w, conv_b, wa_bd, ba, wx_bd, bx, lam):
    t_new, nseq, w = xg_t.shape
    return pl.pallas_call(
        _rg_sample_kernel,
        out_shape=[jax.ShapeDtypeStruct((t_new, nseq, w), _F32), jax.ShapeDtypeStruct(conv0_t.shape, _F32),
                   jax.ShapeDtypeStruct((nseq, w), _F32)],
        compiler_params=pltpu.CompilerParams(vmem_limit_bytes=_VMEM_LIMIT),
        name="rglru_sample",
    )(xr_t, xg_t, conv0_t, h0, conv_w, conv_b, wa_bd, ba, wx_bd, bx, lam)


def _row_bcast(x, r, n):
    return jnp.broadcast_to(x[r:r + 1, :], (n, x.shape[1]))


def _chunk_cumsum(x):
    n_tiles = x.shape[0] // _SUBLANES
    row = lax.broadcasted_iota(jnp.int32, (_SUBLANES, x.shape[1]), 0)
    tiles = []
    carry = None
    for i in range(n_tiles):
        t = x[i * _SUBLANES:(i + 1) * _SUBLANES, :]
        sh = 1
        while sh < _SUBLANES:
            t = t + jnp.where(row >= sh, pltpu.roll(t, sh, axis=0), 0.0)
            sh *= 2
        if carry is not None:
            t = t + carry
        carry = _row_bcast(t, _SUBLANES - 1, _SUBLANES)
        tiles.append(t)
    return tiles[0] if n_tiles == 1 else jnp.concatenate(tiles, axis=0)


def _level_reference(b, m):
    n = b.shape[0]
    if 2 * m >= _SUBLANES:
        pieces = [_row_bcast(b, lo + m - 1, 2 * m) for lo in range(0, n, 2 * m)]
        return pieces[0] if len(pieces) == 1 else jnp.concatenate(pieces, axis=0)
    row = lax.broadcasted_iota(jnp.int32, (_SUBLANES, b.shape[1]), 0)
    tiles = []
    for i in range(n // _SUBLANES):
        t = b[i * _SUBLANES:(i + 1) * _SUBLANES, :]
        ref = None
        for lo in range(0, _SUBLANES, 2 * m):
            piece = _row_bcast(t, lo + m - 1, _SUBLANES)
            ref = piece if ref is None else jnp.where(row >= lo, piece, ref)
        tiles.append(ref)
    return tiles[0] if len(tiles) == 1 else jnp.concatenate(tiles, axis=0)


def _hgrn_gates(fz, lb):
    f = lb + (1.0 - lb) * _sigmoid(fz)
    return jnp.log(f), 1.0 - f


def _hgrn_chunk(q, fz, v, lb, state):
    n, kd = q.shape
    log_f, k = _hgrn_gates(fz, lb)
    b = _chunk_cumsum(log_f)
    b_last = _row_bcast(b, n - 1, n)

    o = _dot_nt(q * jnp.exp(b), state)
    new_state = jnp.exp(b_last[0:1, :]) * state + _dot_tn(v, k * jnp.exp(b_last - b))

    row = lax.broadcasted_iota(jnp.int32, (n, n), 0)
    col = lax.broadcasted_iota(jnp.int32, (n, n), 1)
    upper = lax.broadcasted_iota(jnp.int32, (n, kd), 0)
    scores = jnp.where(row == col, jnp.sum(q * k, axis=-1, keepdims=True), 0.0)
    m = 1
    while m < n:
        e = jnp.exp(-jnp.abs(b - _level_reference(b, m)))
        z = jnp.where((upper & m) != 0, q, k) * e
        pair = jnp.logical_and((row & m) != 0, (row ^ m) >> (m.bit_length() - 1) == col >> (m.bit_length() - 1))
        scores = scores + jnp.where(pair, _dot_nt(z, z), 0.0)
        m *= 2
    return o + _dot(scores, v), new_state


def _lower_bound(logits, layer):
    m = jnp.max(logits, axis=0, keepdims=True)
    e = jnp.exp(logits - m)
    return jnp.sum(e[1:layer + 1, :], axis=0, keepdims=True) / jnp.sum(e, axis=0, keepdims=True)


def _hgrn_out(o, g, gnorm):
    return _rms(o, gnorm) * _silu(g)


def _hgrn_prompt_kernel(q_ref, f_ref, v_ref, g_ref, lbl_ref, gn_ref, o_ref, s_ref,
                        st_scr, st0_scr, *, layer):
    rows_blk, width = q_ref.shape
    kd = width // _HG_HEADS
    n_chunks = rows_blk // _HG_CHUNK
    heads = [slice(h * kd, (h + 1) * kd) for h in range(_HG_HEADS)]
    lb = _lower_bound(lbl_ref[...], layer)
    gn = gn_ref[...]

    @pl.when(pl.program_id(2) == 0)
    def _():
        st_scr[...] = jnp.zeros(st_scr.shape, _F32)

    st0_scr[...] = st_scr[...]

    def chunk_rows(c):
        return pl.ds(pl.multiple_of(c * _HG_CHUNK, _HG_CHUNK), _HG_CHUNK)

    def prepare(c):
        rows = chunk_rows(c)
        log_f, k = _hgrn_gates(f_ref[rows, :], lb)
        b = _chunk_cumsum(log_f)
        b_end = b[_HG_CHUNK - 1:_HG_CHUNK, :]
        dec = jnp.exp(b_end)
        e_b = jnp.exp(b)
        k_start = k / e_b
        q, v = q_ref[rows, :], v_ref[rows, :]
        qk = q * k
        own = jnp.concatenate([jnp.broadcast_to(jnp.sum(qk[:, l], axis=-1, keepdims=True), (_HG_CHUNK, kd))
                               for l in heads], axis=1) * v
        ops = ((q * e_b).astype(_BF16), k_start.astype(_BF16), (k_start * dec).astype(_BF16),
               v.astype(_BF16), dec, own)
        return ops, b_end

    row = lax.broadcasted_iota(jnp.int32, (_HG_CHUNK, _HG_CHUNK), 0)
    col = lax.broadcasted_iota(jnp.int32, (_HG_CHUNK, _HG_CHUNK), 1)

    def finish(c, h, o):
        rows = chunk_rows(c)
        o_ref[rows, heads[h]] = _hgrn_out(o, g_ref[rows, heads[h]], gn)

    def contract(c, ops):
        qs, ks, ke, vb, dec, own = ops
        states = [st_scr[h] for h in range(_HG_HEADS)]
        scores = [_dot_nt(qs[:, l], ks[:, l]) for l in heads]
        carried = [_dot_nt(qs[:, l], st) for l, st in zip(heads, states)]
        incs = [_dot_tn(vb[:, l], ke[:, l]) for l in heads]
        outs = [_dot(jnp.where(row > col, s, 0.0), vb[:, l]) for s, l in zip(scores, heads)]
        for h, l in enumerate(heads):
            st_scr[h] = states[h] * dec[:, l] + incs[h]
        for h, l in enumerate(heads):
            finish(c, h, outs[h] + carried[h] + own[:, l])

    def step(c, carry):
        ops, min_b = carry
        contract(c, ops)
        ops, b_end = prepare(c + 1)
        return ops, jnp.minimum(min_b, b_end)

    last_ops, min_b = lax.fori_loop(0, n_chunks - 1, step, prepare(0), unroll=True)
    contract(n_chunks - 1, last_ops)

    @pl.when(jnp.min(min_b) <= -_HG_SAFE_LOG_DECAY)
    def _():
        st_scr[...] = st0_scr[...]

        def chunk(c, carry):
            rows = chunk_rows(c)
            for h, l in enumerate(heads):
                o, st_scr[h] = _hgrn_chunk(q_ref[rows, l], f_ref[rows, l], v_ref[rows, l], lb[:, l], st_scr[h])
                finish(c, h, o)
            return carry
        lax.fori_loop(0, n_chunks, chunk, 0)

    @pl.when(pl.program_id(2) == pl.num_programs(2) - 1)
    def _():
        for h in range(_HG_HEADS):
            s_ref[0, h] = st_scr[h].T


def _hgrn_prompt_call(q, fz, v, g, lb_logits, gnorm, batch, layer):
    n, width = q.shape
    t_len = n // batch
    kd = width // _C_HEADS
    gw = _HG_HEADS * kd
    rows_blk = min(_HG_ROWS, t_len)
    nt = t_len // rows_blk
    blk = pl.BlockSpec((rows_blk, gw), lambda b, h, t: (b * nt + t, h))
    return pl.pallas_call(
        functools.partial(_hgrn_prompt_kernel, layer=layer),
        grid=(batch, _C_HEADS // _HG_HEADS, nt),
        in_specs=[blk, blk, blk, blk,
                  pl.BlockSpec((lb_logits.shape[0], gw), lambda b, h, t: (0, h)),
                  pl.BlockSpec((1, kd), lambda b, h, t: (0, 0))],
        out_specs=[blk, pl.BlockSpec((1, _HG_HEADS, kd, kd), lambda b, h, t: (b, h, 0, 0))],
        out_shape=[jax.ShapeDtypeStruct((n, width), _F32),
                   jax.ShapeDtypeStruct((batch, _C_HEADS, kd, kd), _F32)],
        scratch_shapes=[pltpu.VMEM((_HG_HEADS, kd, kd), _F32)] * 2,
        compiler_params=_params("parallel", "parallel", "arbitrary"),
        name="hgrn2_prompt",
    )(q, fz, v, g, lb_logits, gnorm)


def _hgrn_sample_kernel(q_ref, f_ref, v_ref, g_ref, lbl_ref, gn_ref, s0_hbm, o_ref, s_ref, ring, sems, *, layer):
    t_new, nseq, width = q_ref.shape
    n_heads = s_ref.shape[1]
    kd = width // n_heads
    n_hg = pl.num_programs(1)
    step = pl.program_id(0) * n_hg + pl.program_id(1)
    n_steps = pl.num_programs(0) * n_hg

    def state_copy(k):
        slot = lax.rem(k, _STATE_RING)
        src = s0_hbm.at[pl.ds(lax.div(k, n_hg) * nseq, nseq), pl.ds(lax.rem(k, n_hg) * n_heads, n_heads)]
        return pltpu.make_async_copy(src, ring.at[slot], sems.at[slot])

    @pl.when(step == 0)
    def _():
        for k in range(_STATE_RING - 1):
            state_copy(jnp.int32(k)).start()

    @pl.when(step + _STATE_RING - 1 < n_steps)
    def _():
        state_copy(step + _STATE_RING - 1).start()

    state_copy(step).wait()
    s0_ref = ring.at[lax.rem(step, _STATE_RING)]
    lanes = [slice(h * kd, (h + 1) * kd) for h in range(n_heads)]
    lb = _lower_bound(lbl_ref[...], layer)
    gn = gn_ref[...]
    q = [q_ref[t] for t in range(t_new)]
    v = [v_ref[t] for t in range(t_new)]
    keys, b = [], []
    for t in range(t_new):
        log_f, k = _hgrn_gates(f_ref[t], lb)
        keys.append(k)
        b.append(log_f if t == 0 else b[-1] + log_f)

    def per_head_sum(w):
        return jnp.concatenate([jnp.broadcast_to(jnp.sum(w[:, l], axis=-1, keepdims=True), (nseq, kd))
                                for l in lanes], axis=1)

    within = []
    for t in range(t_new):
        acc = per_head_sum(q[t] * keys[t]) * v[t]
        for s in range(t):
            acc = acc + per_head_sum(q[t] * keys[s] * jnp.exp(b[t] - b[s])) * v[s]
        within.append(acc)

    q_in = [q[t] * jnp.exp(b[t]) for t in range(t_new)]
    k_end = [keys[t] * jnp.exp(b[-1] - b[t]) for t in range(t_new)]
    decay = jnp.exp(b[-1])
    pairs = [(s, h) for h in range(n_heads) for s in range(nseq)]
    lhs = [_seq_tile([x[:, lanes[h]] for x in q_in], s) for s, h in pairs]
    k_seq = [_seq_tile([x[:, lanes[h]] for x in k_end], s) for s, h in pairs]
    v_seq = [_seq_tile([x[:, lanes[h]] for x in v], s) for s, h in pairs]
    carried = [_dot(x, s0_ref[s, h]) for x, (s, h) in zip(lhs, pairs)]
    incs = [_dot_tn(ks, vs) for ks, vs in zip(k_seq, v_seq)]
    carried_t = []
    for h in range(n_heads):
        dec_cols = jnp.concatenate([decay[:, lanes[h]], jnp.zeros((kd - nseq, kd), _F32)], axis=0).T
        for s in range(nseq):
            i = h * nseq + s
            s_ref[s, h] = jnp.broadcast_to(dec_cols[:, s:s + 1], (kd, kd)) * s0_ref[s, h] + incs[i]
        carried_t.append(_time_tiles(carried[h * nseq:(h + 1) * nseq], t_new))
    for t in range(t_new):
        o = within[t] + jnp.concatenate([carried_t[h][t] for h in range(n_heads)], axis=1)
        g = g_ref[t]
        o_ref[t] = jnp.concatenate([_hgrn_out(o[:, l], g[:, l], gn) for l in lanes], axis=1)


def _hgrn_sample_call(q, fz, v, g, lb_logits, gnorm, s0, layer):
    t_new, nseq, width = q.shape
    kd = width // _C_HEADS
    gw = _HG_HEADS * kd
    bs = _SAMPLE_SEQS
    blk = pl.BlockSpec((t_new, bs, gw), lambda i, h: (0, i, h))
    st = pl.BlockSpec((bs, _HG_HEADS, kd, kd), lambda i, h: (i, h, 0, 0))
    return pl.pallas_call(
        functools.partial(_hgrn_sample_kernel, layer=layer),
        grid=(nseq // bs, _C_HEADS // _HG_HEADS),
        in_specs=[blk, blk, blk, blk,
                  pl.BlockSpec((lb_logits.shape[0], gw), lambda i, h: (0, h)),
                  pl.BlockSpec((1, kd), lambda i, h: (0, 0)),
                  pl.BlockSpec(memory_space=pl.ANY)],
        out_specs=[blk, st],
        scratch_shapes=[pltpu.VMEM((_STATE_RING, bs, _HG_HEADS, kd, kd), _F32),
                        pltpu.SemaphoreType.DMA((_STATE_RING,))],
        out_shape=[jax.ShapeDtypeStruct((t_new, nseq, width), _F32), jax.ShapeDtypeStruct(s0.shape, _F32)],
        compiler_params=_params("arbitrary", "arbitrary"),
        name="hgrn2_sample",
    )(q, fz, v, g, lb_logits, gnorm, s0)


def _rope_tables(pos):
    half = _ROT_DIM // 2
    inv_freq = _ROPE_THETA ** (-jnp.arange(0, _ROT_DIM, 2, dtype=_F32) / _ROT_DIM)
    ang = pos.astype(_F32)[:, None] * inv_freq[None, :]
    cos, sin = jnp.cos(ang), jnp.sin(ang)
    ones = jnp.ones((pos.shape[0], _A_HEAD_DIM - _ROT_DIM), _F32)
    zeros = jnp.zeros((pos.shape[0], _A_HEAD_DIM - half), _F32)
    zeros_h = jnp.zeros((pos.shape[0], half), _F32)
    reps = _LANES // _A_HEAD_DIM
    cos_t = jnp.tile(jnp.concatenate([cos, cos, ones], axis=1), (1, reps))
    sin_lo = jnp.tile(jnp.concatenate([-sin, zeros], axis=1), (1, reps))
    sin_hi = jnp.tile(jnp.concatenate([zeros_h, sin, ones * 0.0], axis=1), (1, reps))
    return cos_t, sin_lo, sin_hi


def _block_diag(w):
    nb, bd, _ = w.shape
    eye = jnp.eye(nb, dtype=w.dtype)
    return (w[:, :, None, :] * eye[:, None, :, None]).reshape(nb * bd, nb * bd)


def kernel(x_prompt, x_sample, c_prompt, c_sample, cache_k_win, cache_v_win, state_conv_rglru,
           state_h_rglru, state_s_hgrn, norm_pre, norm_post, ada_w, ada_b, ffn1_w_in, ffn1_w_out,
           ffn2_w_in, ffn2_w_out, even_w_in, even_w_out, attn_sinks, rg_conv_w, rg_conv_b, rg_wa,
           rg_ba, rg_wx, rg_bx, rg_lambda, odd_w_in, odd_w_out, hgrn_lb_logits, hgrn_gnorm):
    bp, tp, d = x_prompt.shape
    bs, ts, _ = x_sample.shape
    depth = norm_pre.shape[0]
    n_sub = depth * _N_SUB
    nk = _A_KV_HEADS * _A_HEAD_DIM
    win = cache_k_win.shape[2]
    cw = rg_conv_w.shape[1]
    bw = rg_conv_w.shape[2]
    kd = state_s_hgrn.shape[3]

    cast = lambda w: w.astype(_BF16)
    ffn1_in, ffn1_out, ffn2_in, ffn2_out = ffn1_w_in, ffn1_w_out, ffn2_w_in, ffn2_w_out
    ev_in, ev_out, od_in, od_out = even_w_in, even_w_out, odd_w_in, odd_w_out
    gpre = norm_pre.reshape(n_sub, 1, d)
    gpost = norm_post.reshape(n_sub, 1, d)

    mod_p, mod_s = _ada_call(jnp.concatenate([c_prompt, c_sample], axis=0), bp,
                             ada_w.reshape(n_sub, d, 3 * d), ada_b.reshape(n_sub, 1, 3 * d))
    mod_p = mod_p.reshape(n_sub, bp, 1, 3 * d)
    mod_s = mod_s.reshape(n_sub, 1, bs, 3 * d)

    tabs_p = _rope_tables(jnp.arange(tp, dtype=jnp.int32))
    tabs_s = tuple(jnp.repeat(t, bs, axis=0) for t in _rope_tables(_PAST_LEN + jnp.arange(ts, dtype=jnp.int32)))

    time_major = lambda a: a.reshape(ts, bs, a.shape[-1])
    xp = x_prompt.reshape(bp * tp, d)
    xs = x_sample.transpose(1, 0, 2).reshape(ts * bs, d)
    groups = {"p": dict(mod=mod_p, seq_rows=tp, tabs=tabs_p), "s": dict(mod=mod_s, seq_rows=ts * bs, tabs=tabs_s)}
    outs = {g: dict(k=[], v=[], conv=[], h=[], s=[]) for g in groups}

    for l in range(depth):
        s0, s1, s2 = l * _N_SUB, l * _N_SUB + 1, l * _N_SUB + 2
        xp, xs = _ffn_call(xp, xs, mod_p, mod_s, s0, l, gpre, ffn1_in, ffn1_out, gpost, 0.5, tp)
        acts = {}
        for name, x in (("p", xp), ("s", xs)):
            grp = groups[name]
            mod4, tps = grp["mod"], grp["seq_rows"]
            if l % 2 == 0:
                e = l // 2
                q, k, v, xg, xr = _even_in_call(x, mod4, s1, e, gpre, ev_in, grp["tabs"], tps)
                wa_bd, wx_bd = cast(_block_diag(rg_wa[e])), cast(_block_diag(rg_wx[e]))
                vecs = [a[e].reshape(1, bw) for a in (rg_conv_b, rg_ba, rg_bx, rg_lambda)]
                if name == "p":
                    o_a = _swa_prompt_call(q, k, v, attn_sinks[e], bp)
                    o_b, conv, h_last = _rg_prompt_call(xr, xg, rg_conv_w[e], vecs[0], wa_bd, vecs[1], wx_bd,
                                                        vecs[2], vecs[3], bp)
                    last = lambda a: a.reshape(bp, tp, nk)[:, tp - win:].reshape(bp, win, _A_KV_HEADS, _A_HEAD_DIM)
                    outs[name]["k"].append(last(k))
                    outs[name]["v"].append(last(v))
                    outs[name]["h"].append(h_last.reshape(bp, bw))
                else:
                    o_a, kw, vw = _swa_sample_call(time_major(q), time_major(k), time_major(v),
                                                   cache_k_win[e].reshape(bs, win, nk),
                                                   cache_v_win[e].reshape(bs, win, nk), attn_sinks[e])
                    o_b, conv_t, h_last = _rg_sample_call(time_major(xr), time_major(xg),
                                                          state_conv_rglru[e].transpose(1, 0, 2), state_h_rglru[e],
                                                          rg_conv_w[e], vecs[0], wa_bd, vecs[1], wx_bd, vecs[2], vecs[3])
                    o_a, o_b = o_a.reshape(ts * bs, -1), o_b.reshape(ts * bs, bw)
                    conv = conv_t.transpose(1, 0, 2)
                    outs[name]["k"].append(kw.reshape(bs, win, _A_KV_HEADS, _A_HEAD_DIM))
                    outs[name]["v"].append(vw.reshape(bs, win, _A_KV_HEADS, _A_HEAD_DIM))
                    outs[name]["h"].append(h_last)
                outs[name]["conv"].append(conv)
                acts[name] = [o_a, o_b]
                widx, w_mix = e, ev_out
            else:
                o = l // 2
                q, fz, v, g = _odd_in_call(x, mod4, s1, o, gpre, od_in, tps)
                gn = hgrn_gnorm[o].reshape(1, kd)
                if name == "p":
                    y, s_last = _hgrn_prompt_call(q, fz, v, g, hgrn_lb_logits, gn, bp, l)
                else:
                    y, s_last = _hgrn_sample_call(time_major(q), time_major(fz), time_major(v), time_major(g),
                                                  hgrn_lb_logits, gn, state_s_hgrn[o], l)
                    y = y.reshape(ts * bs, -1)
                outs[name]["s"].append(s_last)
                acts[name] = [y]
                widx, w_mix = o, od_out
        xp, xs = _ffn_call(xp, xs, mod_p, mod_s, s2, l, gpre, ffn2_in, ffn2_out, gpost, 0.5, tp,
                           (s1, widx, w_mix, acts["p"], acts["s"]))

    ys = {"p": xp.reshape(bp, tp, d), "s": xs.reshape(ts, bs, d).transpose(1, 0, 2)}
    res = []
    for name in ("p", "s"):
        o = outs[name]
        res.append((jnp.stack(o["k"]), jnp.stack(o["v"]), jnp.stack(o["conv"]), jnp.stack(o["h"]), jnp.stack(o["s"])))
    return (ys["p"], ys["s"]) + res[0] + res[1]
```
